```python
import jax, jax.numpy as jnp
from jax import lax
import numpy as np

D_MODEL = 2048
BATCH = 4
SEQ = 4096
DEPTH = 1

CHUNK = 64
LEFT_CHUNKS = 8
BAND = LEFT_CHUNKS + 1
D_MIX = D_MODEL
D_ATTN = D_MIX // 2
D_CONV = D_MIX - D_ATTN
HEAD_DIM = 64
N_HEADS = D_ATTN // HEAD_DIM
REL_CLIP = 256
N_REL = 2 * REL_CLIP + 1
CONV_WIDTH = 3
D_IN_PROJ = 3 * D_ATTN + 3 * D_CONV

N_GROUPS = 4
EXPERTS_PER_GROUP = 8
N_EXPERTS = N_GROUPS * EXPERTS_PER_GROUP
TOP_K_INNER = 2
D_EXPERT = 512
MOE_BLOCK = 128

EPS = 1e-6
NEG_INF = -1e30

kernel_name = "hymba_chunk_attn_shortconv_hiermoe"


def rmsnorm(x, g):
    xf = x.astype(jnp.float32)
    y = xf * lax.rsqrt(jnp.mean(xf * xf, axis=-1, keepdims=True) + EPS)
    return (y * g.astype(jnp.float32)).astype(x.dtype)


def chunk_attention(q, k, v, rel_bias):
    b, s = q.shape[0], q.shape[1]
    nc = s // CHUNK
    qc = q.reshape(b, nc, CHUNK, N_HEADS, HEAD_DIM)
    pad = ((0, 0), (LEFT_CHUNKS, 0), (0, 0), (0, 0), (0, 0))
    kp = jnp.pad(k.reshape(b, nc, CHUNK, N_HEADS, HEAD_DIM), pad)
    vp = jnp.pad(v.reshape(b, nc, CHUNK, N_HEADS, HEAD_DIM), pad)
    k_band = jnp.concatenate([kp[:, o:o + nc] for o in range(BAND)], axis=2)
    v_band = jnp.concatenate([vp[:, o:o + nc] for o in range(BAND)], axis=2)
    qi = jnp.arange(CHUNK)[:, None]
    kj = jnp.arange(BAND * CHUNK)[None, :]
    dist = LEFT_CHUNKS * CHUNK + qi - kj
    idx = jnp.clip(dist, -REL_CLIP, REL_CLIP) + REL_CLIP
    bias = rel_bias[:, idx].astype(jnp.float32)
    key_chunk = jnp.arange(nc)[:, None] - LEFT_CHUNKS + (jnp.arange(BAND * CHUNK) // CHUNK)[None, :]
    valid = key_chunk >= 0
    scale = HEAD_DIM ** -0.5
    scores = jnp.einsum('bnqhd,bnkhd->bnhqk', qc, k_band).astype(jnp.float32) * scale + bias
    scores = jnp.where(valid[None, :, None, None, :], scores, jnp.float32(NEG_INF))
    p = jax.nn.softmax(scores, axis=-1).astype(v.dtype)
    o = jnp.einsum('bnhqk,bnkhd->bnqhd', p, v_band)
    return o.reshape(b, s, N_HEADS * HEAD_DIM)


def short_conv(bg, cg, u, w):
    z = cg * u
    y = lax.conv_general_dilated(
        z, w[:, None, :], window_strides=(1,), padding=[(CONV_WIDTH - 1, 0)],
        dimension_numbers=('NWC', 'WIO', 'NWC'), feature_group_count=D_CONV)
    return bg * y


def hier_moe(h, w_rg, b_rg, w_re, b_re, w_gate, w_up, w_down):
    bsz, s, d = h.shape
    t = bsz * s
    xt = h.reshape(t, d)
    tok = jnp.arange(t)
    glog = (xt @ w_rg + b_rg).astype(jnp.float32)
    gprob = jax.nn.softmax(glog, axis=-1)
    g_sel = jnp.argmax(glog, axis=-1).astype(jnp.int32)
    p_g = gprob[tok, g_sel]
    elog = (xt @ w_re + b_re).astype(jnp.float32).reshape(t, N_GROUPS, EXPERTS_PER_GROUP)
    elog_g = elog[tok, g_sel]
    top_v, top_i = lax.top_k(elog_g, TOP_K_INNER)
    w_tok = p_g[:, None] * jax.nn.softmax(top_v, axis=-1)
    eid = g_sel[:, None] * EXPERTS_PER_GROUP + top_i.astype(jnp.int32)

    n = t * TOP_K_INNER
    flat_e = eid.reshape(-1)
    flat_w = w_tok.reshape(-1)
    flat_tok = (jnp.arange(n) // TOP_K_INNER).astype(jnp.int32)
    order = jnp.argsort(flat_e, stable=True)
    se, stok, sw = flat_e[order], flat_tok[order], flat_w[order]
    counts = jnp.bincount(flat_e, length=N_EXPERTS).astype(jnp.int32)
    starts = jnp.cumsum(counts) - counts
    pcounts = (counts + MOE_BLOCK - 1) // MOE_BLOCK * MOE_BLOCK
    pends = jnp.cumsum(pcounts)
    pstarts = pends - pcounts
    dest = pstarts[se] + jnp.arange(n, dtype=jnp.int32) - starts[se]
    n_blocks = -(-n // MOE_BLOCK) + N_EXPERTS
    npad = n_blocks * MOE_BLOCK
    buf_tok = jnp.full((npad,), t, dtype=jnp.int32).at[dest].set(stok)
    buf_w = jnp.zeros((npad,), jnp.float32).at[dest].set(sw)
    block_e = jnp.minimum(
        jnp.searchsorted(pends, jnp.arange(n_blocks, dtype=jnp.int32) * MOE_BLOCK, side='right'),
        N_EXPERTS - 1).astype(jnp.int32)
    xpad = jnp.concatenate([xt, jnp.zeros((1, d), xt.dtype)], axis=0)
    xb = xpad[buf_tok].reshape(n_blocks, MOE_BLOCK, d)

    def expert_block(args):
        xblk, e = args
        a = jax.nn.silu(xblk @ w_gate[e]) * (xblk @ w_up[e])
        return a @ w_down[e]

    yb = lax.map(expert_block, (xb, block_e))
    y = yb.reshape(npad, d) * buf_w[:, None].astype(xt.dtype)
    out = jnp.zeros((t + 1, d), xt.dtype).at[buf_tok].add(y)[:t]
    return out.reshape(bsz, s, d)


def setup_inputs(seed: int = 0) -> dict:
    key = jax.random.key(seed)
    ks = jax.random.split(key, 18)
    f32 = jnp.float32
    nrm = lambda k, shape, sc: (jax.random.normal(k, shape, f32) * sc).astype(f32)
    return {
        "x": nrm(ks[0], (BATCH, SEQ, D_MODEL), 1.0),
        "norm1": 1.0 + nrm(ks[1], (DEPTH, D_MODEL), 0.02),
        "w_in": nrm(ks[2], (DEPTH, D_MODEL, D_IN_PROJ), D_MODEL ** -0.5),
        "rel_bias": nrm(ks[3], (DEPTH, N_HEADS, N_REL), 0.1),
        "conv_w": nrm(ks[4], (DEPTH, CONV_WIDTH, D_CONV), CONV_WIDTH ** -0.5),
        "g_out_attn": 1.0 + nrm(ks[5], (DEPTH, D_ATTN), 0.02),
        "g_out_conv": 1.0 + nrm(ks[6], (DEPTH, D_CONV), 0.02),
        "w_out": nrm(ks[7], (DEPTH, D_MIX, D_MODEL), D_MIX ** -0.5),
        "norm2": 1.0 + nrm(ks[8], (DEPTH, D_MODEL), 0.02),
        "w_router_group": nrm(ks[9], (DEPTH, D_MODEL, N_GROUPS), D_MODEL ** -0.5),
        "b_router_group": nrm(ks[10], (DEPTH, N_GROUPS), 0.01),
        "w_router_expert": nrm(ks[11], (DEPTH, D_MODEL, N_EXPERTS), D_MODEL ** -0.5),
        "b_router_expert": nrm(ks[12], (DEPTH, N_EXPERTS), 0.01),
        "w_gate": nrm(ks[13], (DEPTH, N_EXPERTS, D_MODEL, D_EXPERT), D_MODEL ** -0.5),
        "w_up": nrm(ks[14], (DEPTH, N_EXPERTS, D_MODEL, D_EXPERT), D_MODEL ** -0.5),
        "w_down": nrm(ks[15], (DEPTH, N_EXPERTS, D_EXPERT, D_MODEL), D_EXPERT ** -0.5),
        "norm_final": 1.0 + nrm(ks[16], (D_MODEL,), 0.02),
    }


def reference(x, norm1, w_in, rel_bias, conv_w, g_out_attn, g_out_conv, w_out, norm2,
              w_router_group, b_router_group, w_router_expert, b_router_expert,
              w_gate, w_up, w_down, norm_final):
    b, s, _ = x.shape
    for l in range(DEPTH):
        h = rmsnorm(x, norm1[l])
        proj = h @ w_in[l]
        q, k, v, bg, cg, u = jnp.split(proj, 6, axis=-1)
        q = q.reshape(b, s, N_HEADS, HEAD_DIM)
        k = k.reshape(b, s, N_HEADS, HEAD_DIM)
        v = v.reshape(b, s, N_HEADS, HEAD_DIM)
        a_out = chunk_attention(q, k, v, rel_bias[l])
        c_out = short_conv(bg, cg, u, conv_w[l])
        mix = jnp.concatenate([rmsnorm(a_out, g_out_attn[l]), rmsnorm(c_out, g_out_conv[l])], axis=-1)
        x = x + mix @ w_out[l]
        h2 = rmsnorm(x, norm2[l])
        x = x + hier_moe(h2, w_router_group[l], b_router_group[l], w_router_expert[l],
                         b_router_expert[l], w_gate[l], w_up[l], w_down[l])
    return rmsnorm(x, norm_final)
```

```python
import functools

import numpy as np
import jax
import jax.numpy as jnp
from jax import lax
from jax.experimental import pallas as pl
from jax.experimental.pallas import tpu as pltpu

F32 = jnp.float32
BF16 = jnp.bfloat16
I32 = jnp.int32
U32 = jnp.uint32

D_MODEL = 2048
BATCH = 4
SEQ = 4096
T = BATCH * SEQ
CHUNK = 64
LEFT_CHUNKS = 8
BAND = LEFT_CHUNKS + 1
KEYS = BAND * CHUNK
D_ATTN = 1024
D_CONV = 1024
HEAD_DIM = 64
N_HEADS = 16
N_PAIRS = N_HEADS // 2
REL_CLIP = 256
D_IN_PROJ = 6 * 1024
N_GROUPS = 4
EPG = 8
N_EXPERTS = 32
D_EXPERT = 512
MOE_BLOCK = 128
N_ROUTED = 2 * T
N_BLOCKS = N_ROUTED // MOE_BLOCK + N_EXPERTS
NPAD = N_BLOCKS * MOE_BLOCK
EPS = 1e-6
NEG_INF = -1e30
LANES = 128
VMEM_LIMIT = 52 * 1024 * 1024

ATTN_TILE = 512
ATTN_TILES_PER_SEQ = SEQ // ATTN_TILE


def _params(sem):
    return pltpu.CompilerParams(dimension_semantics=sem, vmem_limit_bytes=VMEM_LIMIT)


def _rms(x, g):
    ms = jnp.mean(x * x, axis=-1, keepdims=True)
    return x * lax.rsqrt(ms + EPS) * g


INPROJ_TM = 1024
INPROJ_TN = 512


def _inproj_kernel(x_ref, g_ref, w_ref, o_ref, hn_ref):
    @pl.when(pl.program_id(1) == 0)
    def _():
        hn_ref[...] = _rms(x_ref[...], g_ref[...]).astype(BF16)

    o_ref[...] = jnp.dot(hn_ref[...], w_ref[...], preferred_element_type=F32).astype(BF16)


def _inproj(x2, g, w_bf):
    tm, tn = INPROJ_TM, INPROJ_TN
    return pl.pallas_call(
        _inproj_kernel,
        grid=(T // tm, D_IN_PROJ // tn),
        in_specs=[
            pl.BlockSpec((tm, D_MODEL), lambda i, j: (i, 0)),
            pl.BlockSpec((1, D_MODEL), lambda i, j: (0, 0)),
            pl.BlockSpec((D_MODEL, tn), lambda i, j: (0, j)),
        ],
        out_specs=pl.BlockSpec((tm, tn), lambda i, j: (i, j)),
        out_shape=jax.ShapeDtypeStruct((T, D_IN_PROJ), BF16),
        scratch_shapes=[pltpu.VMEM((tm, D_MODEL), BF16)],
        compiler_params=_params(("arbitrary", "arbitrary")),
        name="inproj",
    )(x2, g, w_bf)


def _attn_kernel(q_ref, kp_ref, kc_ref, vp_ref, vc_ref, bias_ref, g_ref, o_ref,
                 kw_ref, vw_ref, a_ref):
    i = pl.program_id(1)
    kw_ref[0:ATTN_TILE, :] = kp_ref[...]
    kw_ref[ATTN_TILE:2 * ATTN_TILE, :] = kc_ref[...]
    vw_ref[0:ATTN_TILE, :] = vp_ref[...]
    vw_ref[ATTN_TILE:2 * ATTN_TILE, :] = vc_ref[...]

    lane_q = lax.broadcasted_iota(I32, (CHUNK, LANES), 1)
    first_head = lane_q < HEAD_DIM
    key_row = lax.broadcasted_iota(I32, (KEYS, LANES), 0)
    not_first_tile = i > 0

    def chunk_body(ci, carry):
        r0 = pl.multiple_of(ci * CHUNK, CHUNK)
        valid = jnp.logical_or(key_row + r0 >= ATTN_TILE, not_first_tile)
        for hp in range(N_PAIRS):
            c0 = hp * LANES
            qp = q_ref[pl.ds(r0, CHUNK), c0:c0 + LANES] * (HEAD_DIM ** -0.5)
            zero = jnp.zeros_like(qp)
            qbd = jnp.concatenate(
                [jnp.where(first_head, qp, zero), jnp.where(first_head, zero, qp)], axis=0)
            kp = kw_ref[pl.ds(r0, KEYS), c0:c0 + LANES]
            st = lax.dot_general(kp, qbd, (((1,), (1,)), ((), ())),
                                 preferred_element_type=F32)
            st = st + bias_ref[hp]
            st = jnp.where(valid, st, NEG_INF)
            m = jnp.max(st, axis=0, keepdims=True)
            p = jnp.exp(st - m)
            l = jnp.sum(p, axis=0, keepdims=True)
            pn = (p * (1.0 / l)).astype(BF16)
            vp = vw_ref[pl.ds(r0, KEYS), c0:c0 + LANES]
            o = lax.dot_general(pn, vp, (((0,), (0,)), ((), ())),
                                preferred_element_type=F32)
            a_ref[pl.ds(r0, CHUNK), c0:c0 + LANES] = jnp.where(
                first_head, o[0:CHUNK], o[CHUNK:2 * CHUNK])
        return carry

    lax.fori_loop(0, ATTN_TILE // CHUNK, chunk_body, 0)
    o_ref[...] = _rms(a_ref[...], g_ref[...]).astype(BF16)


def _attention(proj, bias_t, g):
    n = ATTN_TILES_PER_SEQ
    cur = lambda col: (lambda b, i: (b * n + i, col))
    prev = lambda col: (lambda b, i: (b * n + jnp.maximum(i - 1, 0), col))
    blk = (ATTN_TILE, D_ATTN)
    return pl.pallas_call(
        _attn_kernel,
        grid=(BATCH, n),
        in_specs=[
            pl.BlockSpec(blk, cur(0)),
            pl.BlockSpec(blk, prev(1)),
            pl.BlockSpec(blk, cur(1)),
            pl.BlockSpec(blk, prev(2)),
            pl.BlockSpec(blk, cur(2)),
            pl.BlockSpec((N_PAIRS, KEYS, LANES), lambda b, i: (0, 0, 0)),
            pl.BlockSpec((1, D_ATTN), lambda b, i: (0, 0)),
        ],
        out_specs=pl.BlockSpec(blk, lambda b, i: (b * n + i, 0)),
        out_shape=jax.ShapeDtypeStruct((T, D_ATTN), BF16),
        scratch_shapes=[
            pltpu.VMEM((2 * ATTN_TILE, D_ATTN), BF16),
            pltpu.VMEM((2 * ATTN_TILE, D_ATTN), BF16),
            pltpu.VMEM((ATTN_TILE, D_ATTN), F32),
        ],
        compiler_params=_params(("arbitrary", "arbitrary")),
        name="attn",
    )(proj, proj, proj, proj, proj, bias_t, g)


CONV_TM = 512
CONV_PREV = 16


def _conv_kernel(b_ref, c_ref, u_ref, cp_ref, up_ref, w_ref, g_ref, o_ref):
    i = pl.program_id(0)
    z = c_ref[...].astype(F32) * u_ref[...].astype(F32)
    zp = cp_ref[...].astype(F32) * up_ref[...].astype(F32)
    seq_start = (i % (SEQ // CONV_TM)) == 0
    zp = jnp.where(seq_start, 0.0, zp)
    zm1 = zp[CONV_PREV - 1:CONV_PREV]
    zm2 = zp[CONV_PREV - 2:CONV_PREV - 1]
    row = lax.broadcasted_iota(I32, z.shape, 0)
    z1 = jnp.where(row == 0, zm1, pltpu.roll(z, 1, axis=0))
    z2 = jnp.where(row == 0, zm2, jnp.where(row == 1, zm1, pltpu.roll(z, 2, axis=0)))
    w = w_ref[...]
    y = w[0:1] * z2 + w[1:2] * z1 + w[2:3] * z
    o_ref[...] = _rms(b_ref[...].astype(F32) * y, g_ref[...]).astype(BF16)


def _short_conv(proj, w, g):
    tm = CONV_TM
    per = tm // CONV_PREV
    blk = (tm, D_CONV)
    col = lambda c: (lambda i: (i, c))
    prev = lambda c: (lambda i: (jnp.maximum(i * per - 1, 0), c))
    return pl.pallas_call(
        _conv_kernel,
        grid=(T // tm,),
        in_specs=[
            pl.BlockSpec(blk, col(3)),
            pl.BlockSpec(blk, col(4)),
            pl.BlockSpec(blk, col(5)),
            pl.BlockSpec((CONV_PREV, D_CONV), prev(4)),
            pl.BlockSpec((CONV_PREV, D_CONV), prev(5)),
            pl.BlockSpec((8, D_CONV), lambda i: (0, 0)),
            pl.BlockSpec((1, D_CONV), lambda i: (0, 0)),
        ],
        out_specs=pl.BlockSpec(blk, lambda i: (i, 0)),
        out_shape=jax.ShapeDtypeStruct((T, D_CONV), BF16),
        compiler_params=_params(("arbitrary",)),
        name="shortconv",
    )(proj, proj, proj, proj, proj, w, g)


OUT_TM = 512


def _pack_bf16_pairs(h):
    half = D_MODEL // 2
    lo = h[:, :half].astype(BF16).astype(F32)
    hi = h[:, half:].astype(BF16).astype(F32)
    return (lax.bitcast_convert_type(lo, U32) >> 16) | lax.bitcast_convert_type(hi, U32)


def _unpack_bf16_pairs(u):
    lo = lax.bitcast_convert_type(u << 16, F32).astype(BF16)
    hi = lax.bitcast_convert_type(u & jnp.uint32(0xFFFF0000), F32).astype(BF16)
    return lo, hi


def _outproj_kernel(a_ref, c_ref, x_ref, wo_ref, g2_ref, wr_ref, br_ref,
                    x1_ref, hp_ref, lg_ref):
    acc = jnp.dot(a_ref[...], wo_ref[0:D_ATTN, :], preferred_element_type=F32)
    acc = acc + jnp.dot(c_ref[...], wo_ref[D_ATTN:, :], preferred_element_type=F32)
    x1 = x_ref[...] + acc
    x1_ref[...] = x1
    h2 = _rms(x1, g2_ref[...])
    lg_ref[...] = jnp.dot(h2.astype(BF16), wr_ref[...],
                          preferred_element_type=F32) + br_ref[...]
    hp_ref[...] = _pack_bf16_pairs(h2)


def _outproj(a_n, c_n, x2, wo_bf, g2, wr_bf, br):
    tm = OUT_TM
    row = lambda i: (i, 0)
    fixed = lambda i: (0, 0)
    return pl.pallas_call(
        _outproj_kernel,
        grid=(T // tm,),
        in_specs=[
            pl.BlockSpec((tm, D_ATTN), row),
            pl.BlockSpec((tm, D_CONV), row),
            pl.BlockSpec((tm, D_MODEL), row),
            pl.BlockSpec((D_MODEL, D_MODEL), fixed),
            pl.BlockSpec((1, D_MODEL), fixed),
            pl.BlockSpec((D_MODEL, LANES), fixed),
            pl.BlockSpec((1, LANES), fixed),
        ],
        out_specs=[
            pl.BlockSpec((tm, D_MODEL), row),
            pl.BlockSpec((tm, D_MODEL // 2), row),
            pl.BlockSpec((tm, LANES), row),
        ],
        out_shape=[
            jax.ShapeDtypeStruct((T, D_MODEL), F32),
            jax.ShapeDtypeStruct((T, D_MODEL // 2), U32),
            jax.ShapeDtypeStruct((T, LANES), F32),
        ],
        compiler_params=_params(("arbitrary",)),
        name="outproj",
    )(a_n, c_n, x2, wo_bf, g2, wr_bf, br)


ROUTE_TM = 512
FIRST_EXPERT_LANE = N_GROUPS


def _route_kernel(lg_ref, ri_ref, rw_ref, cnt_ref, base_ref):
    tm = ROUTE_TM

    @pl.when(pl.program_id(0) == 0)
    def _():
        base_ref[...] = jnp.zeros_like(base_ref)

    lg = lg_ref[...]
    lane = lax.broadcasted_iota(I32, (tm, LANES), 1)
    lane_f = lane.astype(F32)
    big = jnp.float32(LANES)
    ninf = jnp.float32(-jnp.inf)

    gmask = lane < N_GROUPS
    gmax = jnp.max(jnp.where(gmask, lg, ninf), axis=1, keepdims=True)
    gsum = jnp.sum(jnp.where(gmask, jnp.exp(lg - gmax), 0.0), axis=1, keepdims=True)
    p_g = 1.0 / gsum
    g_sel = jnp.min(jnp.where(jnp.logical_and(gmask, lg == gmax), lane_f, big),
                    axis=1, keepdims=True)
    lo = FIRST_EXPERT_LANE + EPG * g_sel
    emask = jnp.logical_and(lane_f >= lo, lane_f < lo + EPG)
    v1 = jnp.max(jnp.where(emask, lg, ninf), axis=1, keepdims=True)
    i1 = jnp.min(jnp.where(jnp.logical_and(emask, lg == v1), lane_f, big),
                 axis=1, keepdims=True)
    emask2 = jnp.logical_and(emask, lane_f != i1)
    v2 = jnp.max(jnp.where(emask2, lg, ninf), axis=1, keepdims=True)
    i2 = jnp.min(jnp.where(jnp.logical_and(emask2, lg == v2), lane_f, big),
                 axis=1, keepdims=True)
    t = jnp.exp(v2 - v1)
    w1 = p_g * (1.0 / (1.0 + t))
    w2 = p_g * (t / (1.0 + t))
    e1 = i1 - FIRST_EXPERT_LANE
    e2 = i2 - FIRST_EXPERT_LANE

    hit1 = lane_f == e1
    hit2 = lane_f == e2
    onehot = jnp.where(jnp.logical_or(hit1, hit2), 1.0, 0.0)
    r = lax.broadcasted_iota(I32, (tm, tm), 0)
    c = lax.broadcasted_iota(I32, (tm, tm), 1)
    tri = jnp.where(r > c, 1.0, 0.0).astype(BF16)
    before = jnp.dot(tri, onehot.astype(BF16), preferred_element_type=F32) + base_ref[...]
    r1 = jnp.sum(jnp.where(hit1, before, 0.0), axis=1, keepdims=True)
    r2 = jnp.sum(jnp.where(hit2, before, 0.0), axis=1, keepdims=True)
    base_ref[...] = base_ref[...] + jnp.sum(onehot, axis=0, keepdims=True)
    cnt_ref[...] = base_ref[...]

    ri = jnp.where(lane == 0, e1, jnp.where(lane == 1, e2, jnp.where(lane == 2, r1, r2)))
    ri_ref[...] = ri.astype(I32)
    rw_ref[...] = jnp.where(lane == 0, w1, w2)


def _route(logits):
    tm = ROUTE_TM
    row = lambda i: (i, 0)
    return pl.pallas_call(
        _route_kernel,
        grid=(T // tm,),
        in_specs=[pl.BlockSpec((tm, LANES), row)],
        out_specs=[
            pl.BlockSpec((tm, LANES), row),
            pl.BlockSpec((tm, LANES), row),
            pl.BlockSpec((1, LANES), lambda i: (0, 0)),
        ],
        out_shape=[
            jax.ShapeDtypeStruct((T, LANES), I32),
            jax.ShapeDtypeStruct((T, LANES), F32),
            jax.ShapeDtypeStruct((1, LANES), F32),
        ],
        scratch_shapes=[pltpu.VMEM((1, LANES), F32)],
        compiler_params=_params(("arbitrary",)),
        name="route",
    )(logits)


BLOCK_LANES = 384


def _dest_kernel(ri_ref, cnt_ref, dest_ref, be_ref):
    tm = ROUTE_TM
    lane8 = lax.broadcasted_iota(I32, (8, LANES), 1)
    cnt = jnp.broadcast_to(cnt_ref[...], (8, LANES)).astype(I32)
    pcnt = ((cnt + (MOE_BLOCK - 1)) >> 7) << 7
    pend = pcnt
    for s in (1, 2, 4, 8, 16, 32, 64):
        pend = pend + jnp.where(lane8 >= s, pltpu.roll(pend, s, axis=1), 0)
    pstart = (pend - pcnt)[0:1].astype(F32)

    ri = ri_ref[...].astype(F32)
    lane = lax.broadcasted_iota(I32, (tm, LANES), 1)
    lane_f = lane.astype(F32)
    e1, e2, r1, r2 = ri[:, 0:1], ri[:, 1:2], ri[:, 2:3], ri[:, 3:4]
    d1 = jnp.sum(jnp.where(lane_f == e1, pstart, 0.0), axis=1, keepdims=True) + r1
    d2 = jnp.sum(jnp.where(lane_f == e2, pstart, 0.0), axis=1, keepdims=True) + r2
    dest_ref[...] = jnp.where(lane == 0, d1, d2).astype(I32)

    rr = lax.broadcasted_iota(I32, (LANES, LANES), 0)
    cc = lax.broadcasted_iota(I32, (LANES, LANES), 1)
    pend_f = jnp.broadcast_to(pend[0:1].astype(F32), (LANES, LANES))
    pend_col = jnp.sum(jnp.where(rr == cc, pend_f, 0.0), axis=1, keepdims=True)
    blk0 = (lax.broadcasted_iota(I32, (LANES, BLOCK_LANES), 1) * MOE_BLOCK).astype(F32)
    erow = lax.broadcasted_iota(I32, (LANES, BLOCK_LANES), 0)
    le = jnp.where(jnp.logical_and(erow < N_EXPERTS, pend_col <= blk0), 1.0, 0.0)
    be = jnp.minimum(jnp.sum(le, axis=0, keepdims=True), float(N_EXPERTS - 1))
    be_ref[...] = jnp.broadcast_to(be, (8, BLOCK_LANES)).astype(I32)


def _dest(ri, cnt):
    tm = ROUTE_TM
    row = lambda i: (i, 0)
    return pl.pallas_call(
        _dest_kernel,
        grid=(T // tm,),
        in_specs=[pl.BlockSpec((tm, LANES), row), pl.BlockSpec((1, LANES), lambda i: (0, 0))],
        out_specs=[pl.BlockSpec((tm, LANES), row),
                   pl.BlockSpec((8, BLOCK_LANES), lambda i: (0, 0))],
        out_shape=[jax.ShapeDtypeStruct((T, LANES), I32),
                   jax.ShapeDtypeStruct((8, BLOCK_LANES), I32)],
        compiler_params=_params(("arbitrary",)),
        name="dest",
    )(ri, cnt)


SCAT_TM = 256


def _scatter_kernel(dest_ref, h_ref, xin_ref, xb_ref, sem):
    del xin_ref
    tm = SCAT_TM
    t0 = pl.program_id(0) * tm

    def row_copy(r, d):
        return pltpu.make_async_copy(h_ref.at[pl.ds(r, 1), :], xb_ref.at[pl.ds(d, 1), :], sem)

    def issue(r, carry):
        s = 2 * (t0 + r)
        row_copy(r, dest_ref[s]).start()
        row_copy(r, dest_ref[s + 1]).start()
        return carry

    lax.fori_loop(0, tm, issue, 0)

    def drain(r, carry):
        row_copy(0, 0).wait()
        row_copy(0, 0).wait()
        return carry

    lax.fori_loop(0, tm, drain, 0)


def _scatter(dest_flat, hp, xb_zero):
    tm = SCAT_TM
    half = D_MODEL // 2
    return pl.pallas_call(
        _scatter_kernel,
        grid_spec=pltpu.PrefetchScalarGridSpec(
            num_scalar_prefetch=1,
            grid=(T // tm,),
            in_specs=[pl.BlockSpec((tm, half), lambda i, d: (i, 0)),
                      pl.BlockSpec(memory_space=pl.ANY)],
            out_specs=pl.BlockSpec(memory_space=pl.ANY),
            scratch_shapes=[pltpu.SemaphoreType.DMA(())],
        ),
        out_shape=jax.ShapeDtypeStruct((NPAD, half), U32),
        input_output_aliases={2: 0},
        compiler_params=_params(("arbitrary",)),
        name="scatter",
    )(dest_flat, hp, xb_zero)


def _expert_kernel(be_ref, xb_ref, wg_ref, wu_ref, wd_ref, y_ref, wg_b, wu_b, wd_b):
    b = pl.program_id(0)
    prev = be_ref[jnp.maximum(b - 1, 0)]
    new_expert = jnp.logical_or(b == 0, be_ref[b] != prev)

    @pl.when(new_expert)
    def _():
        wg_b[...] = wg_ref[...].astype(BF16)
        wu_b[...] = wu_ref[...].astype(BF16)
        wd_b[...] = wd_ref[...].astype(BF16)

    half = D_MODEL // 2
    lo, hi = _unpack_bf16_pairs(xb_ref[...])
    g = (jnp.dot(lo, wg_b[0:half, :], preferred_element_type=F32)
         + jnp.dot(hi, wg_b[half:, :], preferred_element_type=F32))
    u = (jnp.dot(lo, wu_b[0:half, :], preferred_element_type=F32)
         + jnp.dot(hi, wu_b[half:, :], preferred_element_type=F32))
    a = g * (1.0 / (1.0 + jnp.exp(-g))) * u
    y_ref[...] = jnp.dot(a.astype(BF16), wd_b[...], preferred_element_type=F32)


def _experts(block_e, xb, w_gate, w_up, w_down):
    half = D_MODEL // 2
    return pl.pallas_call(
        _expert_kernel,
        grid_spec=pltpu.PrefetchScalarGridSpec(
            num_scalar_prefetch=1,
            grid=(N_BLOCKS,),
            in_specs=[
                pl.BlockSpec((MOE_BLOCK, half), lambda b, be: (b, 0)),
                pl.BlockSpec((None, D_MODEL, D_EXPERT), lambda b, be: (be[b], 0, 0)),
                pl.BlockSpec((None, D_MODEL, D_EXPERT), lambda b, be: (be[b], 0, 0)),
                pl.BlockSpec((None, D_EXPERT, D_MODEL), lambda b, be: (be[b], 0, 0)),
            ],
            out_specs=pl.BlockSpec((MOE_BLOCK, D_MODEL), lambda b, be: (b, 0)),
            scratch_shapes=[
                pltpu.VMEM((D_MODEL, D_EXPERT), BF16),
                pltpu.VMEM((D_MODEL, D_EXPERT), BF16),
                pltpu.VMEM((D_EXPERT, D_MODEL), BF16),
            ],
        ),
        out_shape=jax.ShapeDtypeStruct((NPAD, D_MODEL), F32),
        compiler_params=_params(("arbitrary",)),
        name="experts",
    )(block_e, xb, w_gate, w_up, w_down)


COMB_TM = 128


def _combine_kernel(dest_ref, x1_ref, rw_ref, gf_ref, yb_ref, o_ref, ybuf, sem):
    tm = COMB_TM
    t0 = pl.program_id(0) * tm

    def row_copy(d, k, r):
        return pltpu.make_async_copy(yb_ref.at[pl.ds(d, 1), :], ybuf.at[k, pl.ds(r, 1), :], sem)

    def issue(r, carry):
        s = 2 * (t0 + r)
        row_copy(dest_ref[s], 0, r).start()
        row_copy(dest_ref[s + 1], 1, r).start()
        return carry

    lax.fori_loop(0, tm, issue, 0)

    def drain(r, carry):
        row_copy(0, 0, 0).wait()
        row_copy(0, 1, 0).wait()
        return carry

    lax.fori_loop(0, tm, drain, 0)

    w = rw_ref[...]
    moe = ybuf[0] * w[:, 0:1] + ybuf[1] * w[:, 1:2]
    o_ref[...] = _rms(x1_ref[...] + moe, gf_ref[...])


def _combine(dest_flat, x1, rw, gf, yb):
    tm = COMB_TM
    return pl.pallas_call(
        _combine_kernel,
        grid_spec=pltpu.PrefetchScalarGridSpec(
            num_scalar_prefetch=1,
            grid=(T // tm,),
            in_specs=[
                pl.BlockSpec((tm, D_MODEL), lambda i, d: (i, 0)),
                pl.BlockSpec((tm, LANES), lambda i, d: (i, 0)),
                pl.BlockSpec((1, D_MODEL), lambda i, d: (0, 0)),
                pl.BlockSpec(memory_space=pl.ANY),
            ],
            out_specs=pl.BlockSpec((tm, D_MODEL), lambda i, d: (i, 0)),
            scratch_shapes=[pltpu.VMEM((2, tm, D_MODEL), F32), pltpu.SemaphoreType.DMA(())],
        ),
        out_shape=jax.ShapeDtypeStruct((T, D_MODEL), F32),
        compiler_params=_params(("arbitrary",)),
        name="combine",
    )(dest_flat, x1, rw, gf, yb)


def _bias_table(rel_bias):
    qi = np.arange(CHUNK)[:, None]
    kj = np.arange(KEYS)[None, :]
    idx = np.clip(LEFT_CHUNKS * CHUNK + qi - kj, -REL_CLIP, REL_CLIP) + REL_CLIP
    bias = rel_bias[:, idx.T]
    bias = bias.reshape(N_PAIRS, 2, KEYS, CHUNK).transpose(0, 2, 1, 3)
    return bias.reshape(N_PAIRS, KEYS, LANES).astype(F32)


def _layer(x2, norm1, w_in, rel_bias, conv_w, g_out_attn, g_out_conv, w_out, norm2,
           w_rg, b_rg, w_re, b_re, w_gate, w_up, w_down):
    proj = _inproj(x2, norm1[None, :], w_in.astype(BF16))
    a_n = _attention(proj, _bias_table(rel_bias), g_out_attn[None, :])
    conv_w8 = jnp.zeros((8, D_CONV), F32).at[0:3].set(conv_w)
    c_n = _short_conv(proj, conv_w8, g_out_conv[None, :])

    w_r = jnp.zeros((D_MODEL, LANES), F32)
    w_r = w_r.at[:, 0:N_GROUPS].set(w_rg).at[:, N_GROUPS:N_GROUPS + N_EXPERTS].set(w_re)
    b_r = jnp.zeros((1, LANES), F32)
    b_r = b_r.at[0, 0:N_GROUPS].set(b_rg).at[0, N_GROUPS:N_GROUPS + N_EXPERTS].set(b_re)
    x1, hp, logits = _outproj(a_n, c_n, x2, w_out.astype(BF16), norm2[None, :],
                              w_r.astype(BF16), b_r)

    ri, rw, cnt = _route(logits)
    dest, be = _dest(ri, cnt)
    dest_flat = dest[:, 0:2].reshape(N_ROUTED)
    block_e = be[0, 0:N_BLOCKS]
    xb = _scatter(dest_flat, hp, jnp.zeros((NPAD, D_MODEL // 2), U32))
    yb = _experts(block_e, xb, w_gate, w_up, w_down)
    return dest_flat, x1, rw, yb


def kernel(x, norm1, w_in, rel_bias, conv_w, g_out_attn, g_out_conv, w_out, norm2,
           w_router_group, b_router_group, w_router_expert, b_router_expert,
           w_gate, w_up, w_down, norm_final):
    assert x.shape == (BATCH, SEQ, D_MODEL) and norm1.shape[0] == 1
    x2 = x.reshape(T, D_MODEL)
    dest_flat, x1, rw, yb = _layer(
        x2, norm1[0], w_in[0], rel_bias[0], conv_w[0], g_out_attn[0], g_out_conv[0],
        w_out[0], norm2[0], w_router_group[0], b_router_group[0], w_router_expert[0],
        b_router_expert[0], w_gate[0], w_up[0], w_down[0])
    out = _combine(dest_flat, x1, rw, norm_final[None, :], yb)
    return out.reshape(BATCH, SEQ, D_MODEL)
```

```python
import functools

import numpy as np
import jax
import jax.numpy as jnp
from jax import lax
from jax.experimental import pallas as pl
from jax.experimental.pallas import tpu as pltpu

F32 = jnp.float32
BF16 = jnp.bfloat16
I32 = jnp.int32
U32 = jnp.uint32

D_MODEL = 2048
BATCH = 4
SEQ = 4096
T = BATCH * SEQ
CHUNK = 64
LEFT_CHUNKS = 8
BAND = LEFT_CHUNKS + 1
KEYS = BAND * CHUNK
D_ATTN = 1024
D_CONV = 1024
HEAD_DIM = 64
N_HEADS = 16
N_PAIRS = N_HEADS // 2
REL_CLIP = 256
D_IN_PROJ = 6 * 1024
N_GROUPS = 4
EPG = 8
N_EXPERTS = 32
D_EXPERT = 512
MOE_BLOCK = 128
N_ROUTED = 2 * T
N_BLOCKS = N_ROUTED // MOE_BLOCK + N_EXPERTS
NPAD = N_BLOCKS * MOE_BLOCK
EPS = 1e-6
NEG_INF = -1e30
LANES = 128
VMEM_LIMIT = 52 * 1024 * 1024

ATTN_TILE = 512
ATTN_TILES_PER_SEQ = SEQ // ATTN_TILE


def _params(sem):
    return pltpu.CompilerParams(dimension_semantics=sem, vmem_limit_bytes=VMEM_LIMIT)


def _rms(x, g):
    ms = jnp.mean(x * x, axis=-1, keepdims=True)
    return x * lax.rsqrt(ms + EPS) * g


INPROJ_TM = 1024
INPROJ_TN = 512


def _inproj_kernel(x_ref, g_ref, w_ref, o_ref, hn_ref):
    @pl.when(pl.program_id(1) == 0)
    def _():
        hn_ref[...] = _rms(x_ref[...], g_ref[...]).astype(BF16)

    o_ref[...] = jnp.dot(hn_ref[...], w_ref[...], preferred_element_type=F32).astype(BF16)


def _inproj(x2, g, w_bf):
    tm, tn = INPROJ_TM, INPROJ_TN
    return pl.pallas_call(
        _inproj_kernel,
        grid=(T // tm, D_IN_PROJ // tn),
        in_specs=[
            pl.BlockSpec((tm, D_MODEL), lambda i, j: (i, 0)),
            pl.BlockSpec((1, D_MODEL), lambda i, j: (0, 0)),
            pl.BlockSpec((D_MODEL, tn), lambda i, j: (0, j)),
        ],
        out_specs=pl.BlockSpec((tm, tn), lambda i, j: (i, j)),
        out_shape=jax.ShapeDtypeStruct((T, D_IN_PROJ), BF16),
        scratch_shapes=[pltpu.VMEM((tm, D_MODEL), BF16)],
        compiler_params=_params(("arbitrary", "arbitrary")),
        name="inproj",
    )(x2, g, w_bf)


N_QUADS = N_HEADS // 4
QUAD = 4 * HEAD_DIM
BIAS_W = 640
BIAS_WIN = 512


def _bias_kernel(b_ref, o_ref):
    first_head = lax.broadcasted_iota(I32, (KEYS, LANES), 1) < HEAD_DIM
    for hq in range(N_QUADS):
        for col in range(2):
            h0 = 4 * hq + 2 * col
            pieces = []
            for hh in range(2):
                x = jnp.broadcast_to(b_ref[h0 + hh:h0 + hh + 1, :], (KEYS, BIAS_W))
                rolled = pltpu.roll(x, 0, 1, stride=1, stride_axis=0)
                pieces.append(rolled[:, BIAS_WIN:BIAS_WIN + LANES])
            o_ref[hq, :, col * LANES:(col + 1) * LANES] = jnp.where(
                first_head, pieces[0], pieces[1])


def _bias_table(rel_bias):
    r = rel_bias.astype(F32)
    edge = jnp.broadcast_to(r[:, 2 * REL_CLIP:], (N_HEADS, KEYS - REL_CLIP + 1))
    b = jnp.concatenate([r[:, REL_CLIP:2 * REL_CLIP], edge,
                         r[:, REL_CLIP - (BIAS_W - KEYS - 1):REL_CLIP]], axis=1)
    odd = jnp.concatenate([b[:, BIAS_W - HEAD_DIM:], b[:, :BIAS_W - HEAD_DIM]], axis=1)
    is_odd = (jnp.arange(N_HEADS) % 2 == 1)[:, None]
    bvec = jnp.where(is_odd, odd, b)
    return pl.pallas_call(
        _bias_kernel,
        out_shape=jax.ShapeDtypeStruct((N_QUADS, KEYS, QUAD), F32),
        compiler_params=pltpu.CompilerParams(vmem_limit_bytes=VMEM_LIMIT),
        name="biastab",
    )(bvec)


def _attn_kernel(q_ref, kp_ref, kc_ref, vp_ref, vc_ref, bias_ref, g_ref, o_ref,
                 kw_ref, vw_ref, st_ref, p_ref, a_ref):
    i = pl.program_id(1)
    kw_ref[0:ATTN_TILE, :] = kp_ref[...]
    kw_ref[ATTN_TILE:2 * ATTN_TILE, :] = kc_ref[...]
    vw_ref[0:ATTN_TILE, :] = vp_ref[...]
    vw_ref[ATTN_TILE:2 * ATTN_TILE, :] = vc_ref[...]

    head_of_lane = lax.broadcasted_iota(I32, (CHUNK, QUAD), 1) // HEAD_DIM
    first_head = lax.broadcasted_iota(I32, (CHUNK, LANES), 1) < HEAD_DIM
    key_row = lax.broadcasted_iota(I32, (KEYS, QUAD), 0)
    ones = jnp.ones((KEYS, LANES), BF16)
    n_rb = KEYS // CHUNK

    def chunk_body(masked, ci, carry):
        r0 = pl.multiple_of(ci * CHUNK, CHUNK)
        for hq in range(N_QUADS):
            c0 = hq * QUAD
            q4 = q_ref[pl.ds(r0, CHUNK), c0:c0 + QUAD] * (HEAD_DIM ** -0.5)
            zero = jnp.zeros_like(q4)
            qbd = jnp.concatenate(
                [jnp.where(head_of_lane == h, q4, zero) for h in range(4)], axis=0)
            k4 = kw_ref[pl.ds(r0, KEYS), c0:c0 + QUAD]
            st = lax.dot_general(k4, qbd, (((1,), (1,)), ((), ())),
                                 preferred_element_type=F32)
            st = st + bias_ref[hq]
            if masked:
                st = jnp.where(key_row + r0 >= ATTN_TILE, st, NEG_INF)
            st_ref[hq] = st
        for hq in range(N_QUADS):
            m = st_ref[hq, 0:CHUNK, :]
            for rb in range(1, n_rb):
                m = jnp.maximum(m, st_ref[hq, rb * CHUNK:(rb + 1) * CHUNK, :])
            m = jnp.max(m, axis=0, keepdims=True)
            for rb in range(n_rb):
                rows = slice(rb * CHUNK, (rb + 1) * CHUNK)
                p_ref[hq, rows, :] = jnp.exp(st_ref[hq, rows, :] - m).astype(BF16)
        for hp in range(N_PAIRS):
            c0 = hp * LANES
            half = (hp % 2) * LANES
            pt = p_ref[hp // 2, :, half:half + LANES]
            vext = jnp.concatenate([vw_ref[pl.ds(r0, KEYS), c0:c0 + LANES], ones], axis=1)
            oe = lax.dot_general(pt, vext, (((0,), (0,)), ((), ())),
                                 preferred_element_type=F32)
            o = oe[:, :LANES] * (1.0 / oe[:, LANES:])
            a_ref[pl.ds(r0, CHUNK), c0:c0 + LANES] = jnp.where(
                first_head, o[0:CHUNK], o[CHUNK:2 * CHUNK])
        return carry

    n_chunks = ATTN_TILE // CHUNK

    @pl.when(i == 0)
    def _():
        lax.fori_loop(0, n_chunks, functools.partial(chunk_body, True), 0)

    @pl.when(i > 0)
    def _():
        lax.fori_loop(0, n_chunks, functools.partial(chunk_body, False), 0)

    o_ref[...] = _rms(a_ref[...], g_ref[...]).astype(BF16)


def _attention(proj, bias_t, g):
    n = ATTN_TILES_PER_SEQ
    cur = lambda col: (lambda b, i: (b * n + i, col))
    prev = lambda col: (lambda b, i: (b * n + jnp.maximum(i - 1, 0), col))
    blk = (ATTN_TILE, D_ATTN)
    return pl.pallas_call(
        _attn_kernel,
        grid=(BATCH, n),
        in_specs=[
            pl.BlockSpec(blk, cur(0)),
            pl.BlockSpec(blk, prev(1)),
            pl.BlockSpec(blk, cur(1)),
            pl.BlockSpec(blk, prev(2)),
            pl.BlockSpec(blk, cur(2)),
            pl.BlockSpec((N_QUADS, KEYS, QUAD), lambda b, i: (0, 0, 0)),
            pl.BlockSpec((1, D_ATTN), lambda b, i: (0, 0)),
        ],
        out_specs=pl.BlockSpec(blk, lambda b, i: (b * n + i, 0)),
        out_shape=jax.ShapeDtypeStruct((T, D_ATTN), BF16),
        scratch_shapes=[
            pltpu.VMEM((2 * ATTN_TILE, D_ATTN), BF16),
            pltpu.VMEM((2 * ATTN_TILE, D_ATTN), BF16),
            pltpu.VMEM((N_QUADS, KEYS, QUAD), F32),
            pltpu.VMEM((N_QUADS, KEYS, QUAD), BF16),
            pltpu.VMEM((ATTN_TILE, D_ATTN), F32),
        ],
        compiler_params=_params(("arbitrary", "arbitrary")),
        name="attn",
    )(proj, proj, proj, proj, proj, bias_t, g)


CONV_TM = 512
CONV_PREV = 16


def _conv_kernel(b_ref, c_ref, u_ref, cp_ref, up_ref, w_ref, g_ref, o_ref):
    i = pl.program_id(0)
    z = c_ref[...].astype(F32) * u_ref[...].astype(F32)
    zp = cp_ref[...].astype(F32) * up_ref[...].astype(F32)
    seq_start = (i % (SEQ // CONV_TM)) == 0
    zp = jnp.where(seq_start, 0.0, zp)
    zm1 = zp[CONV_PREV - 1:CONV_PREV]
    zm2 = zp[CONV_PREV - 2:CONV_PREV - 1]
    row = lax.broadcasted_iota(I32, z.shape, 0)
    z1 = jnp.where(row == 0, zm1, pltpu.roll(z, 1, axis=0))
    z2 = jnp.where(row == 0, zm2, jnp.where(row == 1, zm1, pltpu.roll(z, 2, axis=0)))
    w = w_ref[...]
    y = w[0:1] * z2 + w[1:2] * z1 + w[2:3] * z
    o_ref[...] = _rms(b_ref[...].astype(F32) * y, g_ref[...]).astype(BF16)


def _short_conv(proj, w, g):
    tm = CONV_TM
    per = tm // CONV_PREV
    blk = (tm, D_CONV)
    col = lambda c: (lambda i: (i, c))
    prev = lambda c: (lambda i: (jnp.maximum(i * per - 1, 0), c))
    return pl.pallas_call(
        _conv_kernel,
        grid=(T // tm,),
        in_specs=[
            pl.BlockSpec(blk, col(3)),
            pl.BlockSpec(blk, col(4)),
            pl.BlockSpec(blk, col(5)),
            pl.BlockSpec((CONV_PREV, D_CONV), prev(4)),
            pl.BlockSpec((CONV_PREV, D_CONV), prev(5)),
            pl.BlockSpec((8, D_CONV), lambda i: (0, 0)),
            pl.BlockSpec((1, D_CONV), lambda i: (0, 0)),
        ],
        out_specs=pl.BlockSpec(blk, lambda i: (i, 0)),
        out_shape=jax.ShapeDtypeStruct((T, D_CONV), BF16),
        compiler_params=_params(("arbitrary",)),
        name="shortconv",
    )(proj, proj, proj, proj, proj, w, g)


OUT_TM = 512


def _pack_bf16_pairs(h):
    half = D_MODEL // 2
    lo = h[:, :half].astype(BF16).astype(F32)
    hi = h[:, half:].astype(BF16).astype(F32)
    return (lax.bitcast_convert_type(lo, U32) >> 16) | lax.bitcast_convert_type(hi, U32)


def _unpack_bf16_pairs(u):
    lo = lax.bitcast_convert_type(u << 16, F32).astype(BF16)
    hi = lax.bitcast_convert_type(u & jnp.uint32(0xFFFF0000), F32).astype(BF16)
    return lo, hi


def _outproj_kernel(a_ref, c_ref, x_ref, wo_ref, g2_ref, wr_ref, br_ref,
                    x1_ref, hp_ref, lg_ref):
    acc = jnp.dot(a_ref[...], wo_ref[0:D_ATTN, :], preferred_element_type=F32)
    acc = acc + jnp.dot(c_ref[...], wo_ref[D_ATTN:, :], preferred_element_type=F32)
    x1 = x_ref[...] + acc
    x1_ref[...] = x1
    h2 = _rms(x1, g2_ref[...])
    lg_ref[...] = jnp.dot(h2.astype(BF16), wr_ref[...],
                          preferred_element_type=F32) + br_ref[...]
    hp_ref[...] = _pack_bf16_pairs(h2)


def _outproj(a_n, c_n, x2, wo_bf, g2, wr_bf, br):
    tm = OUT_TM
    row = lambda i: (i, 0)
    fixed = lambda i: (0, 0)
    return pl.pallas_call(
        _outproj_kernel,
        grid=(T // tm,),
        in_specs=[
            pl.BlockSpec((tm, D_ATTN), row),
            pl.BlockSpec((tm, D_CONV), row),
            pl.BlockSpec((tm, D_MODEL), row),
            pl.BlockSpec((D_MODEL, D_MODEL), fixed),
            pl.BlockSpec((1, D_MODEL), fixed),
            pl.BlockSpec((D_MODEL, LANES), fixed),
            pl.BlockSpec((1, LANES), fixed),
        ],
        out_specs=[
            pl.BlockSpec((tm, D_MODEL), row),
            pl.BlockSpec((tm, D_MODEL // 2), row),
            pl.BlockSpec((tm, LANES), row),
        ],
        out_shape=[
            jax.ShapeDtypeStruct((T, D_MODEL), F32),
            jax.ShapeDtypeStruct((T, D_MODEL // 2), U32),
            jax.ShapeDtypeStruct((T, LANES), F32),
        ],
        compiler_params=_params(("arbitrary",)),
        name="outproj",
    )(a_n, c_n, x2, wo_bf, g2, wr_bf, br)


ROUTE_TM = 512
FIRST_EXPERT_LANE = N_GROUPS


def _route_kernel(lg_ref, ri_ref, rw_ref, cnt_ref, base_ref):
    tm = ROUTE_TM

    @pl.when(pl.program_id(0) == 0)
    def _():
        base_ref[...] = jnp.zeros_like(base_ref)

    lg = lg_ref[...]
    lane = lax.broadcasted_iota(I32, (tm, LANES), 1)
    lane_f = lane.astype(F32)
    big = jnp.float32(LANES)
    ninf = jnp.float32(-jnp.inf)

    gmask = lane < N_GROUPS
    gmax = jnp.max(jnp.where(gmask, lg, ninf), axis=1, keepdims=True)
    gsum = jnp.sum(jnp.where(gmask, jnp.exp(lg - gmax), 0.0), axis=1, keepdims=True)
    p_g = 1.0 / gsum
    g_sel = jnp.min(jnp.where(jnp.logical_and(gmask, lg == gmax), lane_f, big),
                    axis=1, keepdims=True)
    lo = FIRST_EXPERT_LANE + EPG * g_sel
    emask = jnp.logical_and(lane_f >= lo, lane_f < lo + EPG)
    v1 = jnp.max(jnp.where(emask, lg, ninf), axis=1, keepdims=True)
    i1 = jnp.min(jnp.where(jnp.logical_and(emask, lg == v1), lane_f, big),
                 axis=1, keepdims=True)
    emask2 = jnp.logical_and(emask, lane_f != i1)
    v2 = jnp.max(jnp.where(emask2, lg, ninf), axis=1, keepdims=True)
    i2 = jnp.min(jnp.where(jnp.logical_and(emask2, lg == v2), lane_f, big),
                 axis=1, keepdims=True)
    t = jnp.exp(v2 - v1)
    w1 = p_g * (1.0 / (1.0 + t))
    w2 = p_g * (t / (1.0 + t))
    e1 = i1 - FIRST_EXPERT_LANE
    e2 = i2 - FIRST_EXPERT_LANE

    hit1 = lane_f == e1
    hit2 = lane_f == e2
    onehot = jnp.where(jnp.logical_or(hit1, hit2), 1.0, 0.0)
    r = lax.broadcasted_iota(I32, (tm, tm), 0)
    c = lax.broadcasted_iota(I32, (tm, tm), 1)
    tri = jnp.where(r > c, 1.0, 0.0).astype(BF16)
    before = jnp.dot(tri, onehot.astype(BF16), preferred_element_type=F32) + base_ref[...]
    r1 = jnp.sum(jnp.where(hit1, before, 0.0), axis=1, keepdims=True)
    r2 = jnp.sum(jnp.where(hit2, before, 0.0), axis=1, keepdims=True)
    base_ref[...] = base_ref[...] + jnp.sum(onehot, axis=0, keepdims=True)
    cnt_ref[...] = base_ref[...]

    ri = jnp.where(lane == 0, e1, jnp.where(lane == 1, e2, jnp.where(lane == 2, r1, r2)))
    ri_ref[...] = ri.astype(I32)
    rw_ref[...] = jnp.where(lane == 0, w1, w2)


def _route(logits):
    tm = ROUTE_TM
    row = lambda i: (i, 0)
    return pl.pallas_call(
        _route_kernel,
        grid=(T // tm,),
        in_specs=[pl.BlockSpec((tm, LANES), row)],
        out_specs=[
            pl.BlockSpec((tm, LANES), row),
            pl.BlockSpec((tm, LANES), row),
            pl.BlockSpec((1, LANES), lambda i: (0, 0)),
        ],
        out_shape=[
            jax.ShapeDtypeStruct((T, LANES), I32),
            jax.ShapeDtypeStruct((T, LANES), F32),
            jax.ShapeDtypeStruct((1, LANES), F32),
        ],
        scratch_shapes=[pltpu.VMEM((1, LANES), F32)],
        compiler_params=_params(("arbitrary",)),
        name="route",
    )(logits)


BLOCK_LANES = 384


def _dest_kernel(ri_ref, cnt_ref, dest_ref, be_ref):
    tm = ROUTE_TM
    lane8 = lax.broadcasted_iota(I32, (8, LANES), 1)
    cnt = jnp.broadcast_to(cnt_ref[...], (8, LANES)).astype(I32)
    pcnt = ((cnt + (MOE_BLOCK - 1)) >> 7) << 7
    pend = pcnt
    for s in (1, 2, 4, 8, 16, 32, 64):
        pend = pend + jnp.where(lane8 >= s, pltpu.roll(pend, s, axis=1), 0)
    pstart = (pend - pcnt)[0:1].astype(F32)

    ri = ri_ref[...].astype(F32)
    lane = lax.broadcasted_iota(I32, (tm, LANES), 1)
    lane_f = lane.astype(F32)
    e1, e2, r1, r2 = ri[:, 0:1], ri[:, 1:2], ri[:, 2:3], ri[:, 3:4]
    d1 = jnp.sum(jnp.where(lane_f == e1, pstart, 0.0), axis=1, keepdims=True) + r1
    d2 = jnp.sum(jnp.where(lane_f == e2, pstart, 0.0), axis=1, keepdims=True) + r2
    dest_ref[...] = jnp.where(lane == 0, d1, d2).astype(I32)

    rr = lax.broadcasted_iota(I32, (LANES, LANES), 0)
    cc = lax.broadcasted_iota(I32, (LANES, LANES), 1)
    pend_f = jnp.broadcast_to(pend[0:1].astype(F32), (LANES, LANES))
    pend_col = jnp.sum(jnp.where(rr == cc, pend_f, 0.0), axis=1, keepdims=True)
    blk0 = (lax.broadcasted_iota(I32, (LANES, BLOCK_LANES), 1) * MOE_BLOCK).astype(F32)
    erow = lax.broadcasted_iota(I32, (LANES, BLOCK_LANES), 0)
    le = jnp.where(jnp.logical_and(erow < N_EXPERTS, pend_col <= blk0), 1.0, 0.0)
    be = jnp.minimum(jnp.sum(le, axis=0, keepdims=True), float(N_EXPERTS - 1))
    be_ref[...] = jnp.broadcast_to(be, (8, BLOCK_LANES)).astype(I32)


def _dest(ri, cnt):
    tm = ROUTE_TM
    row = lambda i: (i, 0)
    return pl.pallas_call(
        _dest_kernel,
        grid=(T // tm,),
        in_specs=[pl.BlockSpec((tm, LANES), row), pl.BlockSpec((1, LANES), lambda i: (0, 0))],
        out_specs=[pl.BlockSpec((tm, LANES), row),
                   pl.BlockSpec((8, BLOCK_LANES), lambda i: (0, 0))],
        out_shape=[jax.ShapeDtypeStruct((T, LANES), I32),
                   jax.ShapeDtypeStruct((8, BLOCK_LANES), I32)],
        compiler_params=_params(("arbitrary",)),
        name="dest",
    )(ri, cnt)


SCAT_TM = 256


def _scatter_kernel(dest_ref, h_ref, xin_ref, xb_ref, sem):
    del xin_ref
    tm = SCAT_TM
    t0 = pl.program_id(0) * tm

    def row_copy(r, d):
        return pltpu.make_async_copy(h_ref.at[pl.ds(r, 1), :], xb_ref.at[pl.ds(d, 1), :], sem)

    def issue(r, carry):
        s = 2 * (t0 + r)
        row_copy(r, dest_ref[s]).start()
        row_copy(r, dest_ref[s + 1]).start()
        return carry

    lax.fori_loop(0, tm, issue, 0)

    def drain(r, carry):
        row_copy(0, 0).wait()
        row_copy(0, 0).wait()
        return carry

    lax.fori_loop(0, tm, drain, 0)


def _scatter(dest_flat, hp, xb_zero):
    tm = SCAT_TM
    half = D_MODEL // 2
    return pl.pallas_call(
        _scatter_kernel,
        grid_spec=pltpu.PrefetchScalarGridSpec(
            num_scalar_prefetch=1,
            grid=(T // tm,),
            in_specs=[pl.BlockSpec((tm, half), lambda i, d: (i, 0)),
                      pl.BlockSpec(memory_space=pl.ANY)],
            out_specs=pl.BlockSpec(memory_space=pl.ANY),
            scratch_shapes=[pltpu.SemaphoreType.DMA(())],
        ),
        out_shape=jax.ShapeDtypeStruct((NPAD, half), U32),
        input_output_aliases={2: 0},
        compiler_params=_params(("arbitrary",)),
        name="scatter",
    )(dest_flat, hp, xb_zero)


def _expert_kernel(be_ref, xb_ref, wg_ref, wu_ref, wd_ref, y_ref, wg_b, wu_b, wd_b):
    b = pl.program_id(0)
    prev = be_ref[jnp.maximum(b - 1, 0)]
    new_expert = jnp.logical_or(b == 0, be_ref[b] != prev)

    @pl.when(new_expert)
    def _():
        wg_b[...] = wg_ref[...].astype(BF16)
        wu_b[...] = wu_ref[...].astype(BF16)
        wd_b[...] = wd_ref[...].astype(BF16)

    half = D_MODEL // 2
    lo, hi = _unpack_bf16_pairs(xb_ref[...])
    g = (jnp.dot(lo, wg_b[0:half, :], preferred_element_type=F32)
         + jnp.dot(hi, wg_b[half:, :], preferred_element_type=F32))
    u = (jnp.dot(lo, wu_b[0:half, :], preferred_element_type=F32)
         + jnp.dot(hi, wu_b[half:, :], preferred_element_type=F32))
    a = g * (1.0 / (1.0 + jnp.exp(-g))) * u
    y_ref[...] = jnp.dot(a.astype(BF16), wd_b[...], preferred_element_type=F32)


def _experts(block_e, xb, w_gate, w_up, w_down):
    half = D_MODEL // 2
    return pl.pallas_call(
        _expert_kernel,
        grid_spec=pltpu.PrefetchScalarGridSpec(
            num_scalar_prefetch=1,
            grid=(N_BLOCKS,),
            in_specs=[
                pl.BlockSpec((MOE_BLOCK, half), lambda b, be: (b, 0)),
                pl.BlockSpec((None, D_MODEL, D_EXPERT), lambda b, be: (be[b], 0, 0)),
                pl.BlockSpec((None, D_MODEL, D_EXPERT), lambda b, be: (be[b], 0, 0)),
                pl.BlockSpec((None, D_EXPERT, D_MODEL), lambda b, be: (be[b], 0, 0)),
            ],
            out_specs=pl.BlockSpec((MOE_BLOCK, D_MODEL), lambda b, be: (b, 0)),
            scratch_shapes=[
                pltpu.VMEM((D_MODEL, D_EXPERT), BF16),
                pltpu.VMEM((D_MODEL, D_EXPERT), BF16),
                pltpu.VMEM((D_EXPERT, D_MODEL), BF16),
            ],
        ),
        out_shape=jax.ShapeDtypeStruct((NPAD, D_MODEL), F32),
        compiler_params=_params(("arbitrary",)),
        name="experts",
    )(block_e, xb, w_gate, w_up, w_down)


COMB_TM = 128


def _combine_kernel(dest_ref, x1_ref, rw_ref, gf_ref, yb_ref, o_ref, ybuf, sem):
    tm = COMB_TM
    t0 = pl.program_id(0) * tm

    def row_copy(d, k, r):
        return pltpu.make_async_copy(yb_ref.at[pl.ds(d, 1), :], ybuf.at[k, pl.ds(r, 1), :], sem)

    def issue(r, carry):
        s = 2 * (t0 + r)
        row_copy(dest_ref[s], 0, r).start()
        row_copy(dest_ref[s + 1], 1, r).start()
        return carry

    lax.fori_loop(0, tm, issue, 0)

    def drain(r, carry):
        row_copy(0, 0, 0).wait()
        row_copy(0, 1, 0).wait()
        return carry

    lax.fori_loop(0, tm, drain, 0)

    w = rw_ref[...]
    moe = ybuf[0] * w[:, 0:1] + ybuf[1] * w[:, 1:2]
    o_ref[...] = _rms(x1_ref[...] + moe, gf_ref[...])


def _combine(dest_flat, x1, rw, gf, yb):
    tm = COMB_TM
    return pl.pallas_call(
        _combine_kernel,
        grid_spec=pltpu.PrefetchScalarGridSpec(
            num_scalar_prefetch=1,
            grid=(T // tm,),
            in_specs=[
                pl.BlockSpec((tm, D_MODEL), lambda i, d: (i, 0)),
                pl.BlockSpec((tm, LANES), lambda i, d: (i, 0)),
                pl.BlockSpec((1, D_MODEL), lambda i, d: (0, 0)),
                pl.BlockSpec(memory_space=pl.ANY),
            ],
            out_specs=pl.BlockSpec((tm, D_MODEL), lambda i, d: (i, 0)),
            scratch_shapes=[pltpu.VMEM((2, tm, D_MODEL), F32), pltpu.SemaphoreType.DMA(())],
        ),
        out_shape=jax.ShapeDtypeStruct((T, D_MODEL), F32),
        compiler_params=_params(("arbitrary",)),
        name="combine",
    )(dest_flat, x1, rw, gf, yb)


def _layer(x2, norm1, w_in, rel_bias, conv_w, g_out_attn, g_out_conv, w_out, norm2,
           w_rg, b_rg, w_re, b_re, w_gate, w_up, w_down):
    proj = _inproj(x2, norm1[None, :], w_in.astype(BF16))
    a_n = _attention(proj, _bias_table(rel_bias), g_out_attn[None, :])
    conv_w8 = jnp.zeros((8, D_CONV), F32).at[0:3].set(conv_w)
    c_n = _short_conv(proj, conv_w8, g_out_conv[None, :])

    w_r = jnp.zeros((D_MODEL, LANES), F32)
    w_r = w_r.at[:, 0:N_GROUPS].set(w_rg).at[:, N_GROUPS:N_GROUPS + N_EXPERTS].set(w_re)
    b_r = jnp.zeros((1, LANES), F32)
    b_r = b_r.at[0, 0:N_GROUPS].set(b_rg).at[0, N_GROUPS:N_GROUPS + N_EXPERTS].set(b_re)
    x1, hp, logits = _outproj(a_n, c_n, x2, w_out.astype(BF16), norm2[None, :],
                              w_r.astype(BF16), b_r)

    ri, rw, cnt = _route(logits)
    dest, be = _dest(ri, cnt)
    dest_flat = dest[:, 0:2].reshape(N_ROUTED)
    block_e = be[0, 0:N_BLOCKS]
    xb = _scatter(dest_flat, hp, jnp.zeros((NPAD, D_MODEL // 2), U32))
    yb = _experts(block_e, xb, w_gate, w_up, w_down)
    return dest_flat, x1, rw, yb


def kernel(x, norm1, w_in, rel_bias, conv_w, g_out_attn, g_out_conv, w_out, norm2,
           w_router_group, b_router_group, w_router_expert, b_router_expert,
           w_gate, w_up, w_down, norm_final):
    assert x.shape == (BATCH, SEQ, D_MODEL) and norm1.shape[0] == 1
    x2 = x.reshape(T, D_MODEL)
    dest_flat, x1, rw, yb = _layer(
        x2, norm1[0], w_in[0], rel_bias[0], conv_w[0], g_out_attn[0], g_out_conv[0],
        w_out[0], norm2[0], w_router_group[0], b_router_group[0], w_router_expert[0],
        b_router_expert[0], w_gate[0], w_up[0], w_down[0])
    out = _combine(dest_flat, x1, rw, norm_final[None, :], yb)
    return out.reshape(BATCH, SEQ, D_MODEL)
```

```python
import functools

import numpy as np
import jax
import jax.numpy as jnp
from jax import lax
from jax.experimental import pallas as pl
from jax.experimental.pallas import tpu as pltpu

F32 = jnp.float32
BF16 = jnp.bfloat16
I32 = jnp.int32
U32 = jnp.uint32

D_MODEL = 2048
BATCH = 4
SEQ = 4096
T = BATCH * SEQ
CHUNK = 64
LEFT_CHUNKS = 8
BAND = LEFT_CHUNKS + 1
KEYS = BAND * CHUNK
D_ATTN = 1024
D_CONV = 1024
HEAD_DIM = 64
N_HEADS = 16
N_PAIRS = N_HEADS // 2
REL_CLIP = 256
D_IN_PROJ = 6 * 1024
N_GROUPS = 4
EPG = 8
N_EXPERTS = 32
D_EXPERT = 512
MOE_BLOCK = 128
N_ROUTED = 2 * T
N_BLOCKS = N_ROUTED // MOE_BLOCK + N_EXPERTS
NPAD = N_BLOCKS * MOE_BLOCK
EPS = 1e-6
NEG_INF = -1e30
LANES = 128
VMEM_LIMIT = 52 * 1024 * 1024

ATTN_TILE = 512
ATTN_TILES_PER_SEQ = SEQ // ATTN_TILE


def _params(sem):
    return pltpu.CompilerParams(dimension_semantics=sem, vmem_limit_bytes=VMEM_LIMIT)


def _rms(x, g):
    ms = jnp.mean(x * x, axis=-1, keepdims=True)
    return x * lax.rsqrt(ms + EPS) * g


INPROJ_TM = 1024
INPROJ_TN = 512


def _inproj_kernel(x_ref, g_ref, w_ref, o_ref, hn_ref):
    @pl.when(pl.program_id(1) == 0)
    def _():
        hn_ref[...] = _rms(x_ref[...], g_ref[...]).astype(BF16)

    o_ref[...] = jnp.dot(hn_ref[...], w_ref[...], preferred_element_type=F32).astype(BF16)


def _inproj(x2, g, w_bf):
    tm, tn = INPROJ_TM, INPROJ_TN
    return pl.pallas_call(
        _inproj_kernel,
        grid=(T // tm, D_IN_PROJ // tn),
        in_specs=[
            pl.BlockSpec((tm, D_MODEL), lambda i, j: (i, 0)),
            pl.BlockSpec((1, D_MODEL), lambda i, j: (0, 0)),
            pl.BlockSpec((D_MODEL, tn), lambda i, j: (0, j)),
        ],
        out_specs=pl.BlockSpec((tm, tn), lambda i, j: (i, j)),
        out_shape=jax.ShapeDtypeStruct((T, D_IN_PROJ), BF16),
        scratch_shapes=[pltpu.VMEM((tm, D_MODEL), BF16)],
        compiler_params=_params(("arbitrary", "arbitrary")),
        name="inproj",
    )(x2, g, w_bf)


N_QUADS = N_HEADS // 4
QUAD = 4 * HEAD_DIM
BIAS_W = 640
BIAS_WIN = 512


def _bias_kernel(b_ref, o_ref):
    first_head = lax.broadcasted_iota(I32, (KEYS, LANES), 1) < HEAD_DIM
    for hq in range(N_QUADS):
        for col in range(2):
            h0 = 4 * hq + 2 * col
            pieces = []
            for hh in range(2):
                x = jnp.broadcast_to(b_ref[h0 + hh:h0 + hh + 1, :], (KEYS, BIAS_W))
                rolled = pltpu.roll(x, 0, 1, stride=1, stride_axis=0)
                pieces.append(rolled[:, BIAS_WIN:BIAS_WIN + LANES])
            o_ref[hq, :, col * LANES:(col + 1) * LANES] = jnp.where(
                first_head, pieces[0], pieces[1])


def _bias_table(rel_bias):
    r = rel_bias.astype(F32)
    edge = jnp.broadcast_to(r[:, 2 * REL_CLIP:], (N_HEADS, KEYS - REL_CLIP + 1))
    b = jnp.concatenate([r[:, REL_CLIP:2 * REL_CLIP], edge,
                         r[:, REL_CLIP - (BIAS_W - KEYS - 1):REL_CLIP]], axis=1)
    odd = jnp.concatenate([b[:, BIAS_W - HEAD_DIM:], b[:, :BIAS_W - HEAD_DIM]], axis=1)
    is_odd = (jnp.arange(N_HEADS) % 2 == 1)[:, None]
    bvec = jnp.where(is_odd, odd, b)
    return pl.pallas_call(
        _bias_kernel,
        out_shape=jax.ShapeDtypeStruct((N_QUADS, KEYS, QUAD), F32),
        compiler_params=pltpu.CompilerParams(vmem_limit_bytes=VMEM_LIMIT),
        name="biastab",
    )(bvec)


def _attn_kernel(q_ref, kp_ref, kc_ref, vp_ref, vc_ref, bias_ref, g_ref, o_ref,
                 kw_ref, vw_ref, st_ref, p_ref, a_ref):
    i = pl.program_id(1)
    kw_ref[0:ATTN_TILE, :] = kp_ref[...]
    kw_ref[ATTN_TILE:2 * ATTN_TILE, :] = kc_ref[...]
    vw_ref[0:ATTN_TILE, :] = vp_ref[...]
    vw_ref[ATTN_TILE:2 * ATTN_TILE, :] = vc_ref[...]

    head_of_lane = lax.broadcasted_iota(I32, (CHUNK, QUAD), 1) // HEAD_DIM
    first_head = lax.broadcasted_iota(I32, (CHUNK, LANES), 1) < HEAD_DIM
    key_row = lax.broadcasted_iota(I32, (KEYS, QUAD), 0)
    ones = jnp.ones((KEYS, LANES), BF16)
    n_rb = KEYS // CHUNK

    def chunk_body(masked, ci, carry):
        r0 = pl.multiple_of(ci * CHUNK, CHUNK)
        for hq in range(N_QUADS):
            c0 = hq * QUAD
            q4 = q_ref[pl.ds(r0, CHUNK), c0:c0 + QUAD] * (HEAD_DIM ** -0.5)
            zero = jnp.zeros_like(q4)
            qbd = jnp.concatenate(
                [jnp.where(head_of_lane == h, q4, zero) for h in range(4)], axis=0)
            k4 = kw_ref[pl.ds(r0, KEYS), c0:c0 + QUAD]
            st = lax.dot_general(k4, qbd, (((1,), (1,)), ((), ())),
                                 preferred_element_type=F32)
            st = st + bias_ref[hq]
            if masked:
                st = jnp.where(key_row + r0 >= ATTN_TILE, st, NEG_INF)
            st_ref[hq] = st
        for hq in range(N_QUADS):
            m = st_ref[hq, 0:CHUNK, :]
            for rb in range(1, n_rb):
                m = jnp.maximum(m, st_ref[hq, rb * CHUNK:(rb + 1) * CHUNK, :])
            m = jnp.max(m, axis=0, keepdims=True)
            for rb in range(n_rb):
                rows = slice(rb * CHUNK, (rb + 1) * CHUNK)
                p_ref[hq, rows, :] = jnp.exp(st_ref[hq, rows, :] - m).astype(BF16)
        for hp in range(N_PAIRS):
            c0 = hp * LANES
            half = (hp % 2) * LANES
            pt = p_ref[hp // 2, :, half:half + LANES]
            vext = jnp.concatenate([vw_ref[pl.ds(r0, KEYS), c0:c0 + LANES], ones], axis=1)
            oe = lax.dot_general(pt, vext, (((0,), (0,)), ((), ())),
                                 preferred_element_type=F32)
            o = oe[:, :LANES] * (1.0 / oe[:, LANES:])
            a_ref[pl.ds(r0, CHUNK), c0:c0 + LANES] = jnp.where(
                first_head, o[0:CHUNK], o[CHUNK:2 * CHUNK])
        return carry

    n_chunks = ATTN_TILE // CHUNK

    @pl.when(i == 0)
    def _():
        lax.fori_loop(0, n_chunks, functools.partial(chunk_body, True), 0)

    @pl.when(i > 0)
    def _():
        lax.fori_loop(0, n_chunks, functools.partial(chunk_body, False), 0)

    o_ref[...] = _rms(a_ref[...], g_ref[...]).astype(BF16)


def _attention(proj, bias_t, g):
    n = ATTN_TILES_PER_SEQ
    cur = lambda col: (lambda b, i: (b * n + i, col))
    prev = lambda col: (lambda b, i: (b * n + jnp.maximum(i - 1, 0), col))
    blk = (ATTN_TILE, D_ATTN)
    return pl.pallas_call(
        _attn_kernel,
        grid=(BATCH, n),
        in_specs=[
            pl.BlockSpec(blk, cur(0)),
            pl.BlockSpec(blk, prev(1)),
            pl.BlockSpec(blk, cur(1)),
            pl.BlockSpec(blk, prev(2)),
            pl.BlockSpec(blk, cur(2)),
            pl.BlockSpec((N_QUADS, KEYS, QUAD), lambda b, i: (0, 0, 0)),
            pl.BlockSpec((1, D_ATTN), lambda b, i: (0, 0)),
        ],
        out_specs=pl.BlockSpec(blk, lambda b, i: (b * n + i, 0)),
        out_shape=jax.ShapeDtypeStruct((T, D_ATTN), BF16),
        scratch_shapes=[
            pltpu.VMEM((2 * ATTN_TILE, D_ATTN), BF16),
            pltpu.VMEM((2 * ATTN_TILE, D_ATTN), BF16),
            pltpu.VMEM((N_QUADS, KEYS, QUAD), F32),
            pltpu.VMEM((N_QUADS, KEYS, QUAD), BF16),
            pltpu.VMEM((ATTN_TILE, D_ATTN), F32),
        ],
        compiler_params=_params(("arbitrary", "arbitrary")),
        name="attn",
    )(proj, proj, proj, proj, proj, bias_t, g)


CONV_TM = 512
CONV_PREV = 16


def _conv_kernel(b_ref, c_ref, u_ref, cp_ref, up_ref, w_ref, g_ref, o_ref):
    i = pl.program_id(0)
    z = c_ref[...].astype(F32) * u_ref[...].astype(F32)
    zp = cp_ref[...].astype(F32) * up_ref[...].astype(F32)
    seq_start = (i % (SEQ // CONV_TM)) == 0
    zp = jnp.where(seq_start, 0.0, zp)
    zm1 = zp[CONV_PREV - 1:CONV_PREV]
    zm2 = zp[CONV_PREV - 2:CONV_PREV - 1]
    row = lax.broadcasted_iota(I32, z.shape, 0)
    z1 = jnp.where(row == 0, zm1, pltpu.roll(z, 1, axis=0))
    z2 = jnp.where(row == 0, zm2, jnp.where(row == 1, zm1, pltpu.roll(z, 2, axis=0)))
    w = w_ref[...]
    y = w[0:1] * z2 + w[1:2] * z1 + w[2:3] * z
    o_ref[...] = _rms(b_ref[...].astype(F32) * y, g_ref[...]).astype(BF16)


def _short_conv(proj, w, g):
    tm = CONV_TM
    per = tm // CONV_PREV
    blk = (tm, D_CONV)
    col = lambda c: (lambda i: (i, c))
    prev = lambda c: (lambda i: (jnp.maximum(i * per - 1, 0), c))
    return pl.pallas_call(
        _conv_kernel,
        grid=(T // tm,),
        in_specs=[
            pl.BlockSpec(blk, col(3)),
            pl.BlockSpec(blk, col(4)),
            pl.BlockSpec(blk, col(5)),
            pl.BlockSpec((CONV_PREV, D_CONV), prev(4)),
            pl.BlockSpec((CONV_PREV, D_CONV), prev(5)),
            pl.BlockSpec((8, D_CONV), lambda i: (0, 0)),
            pl.BlockSpec((1, D_CONV), lambda i: (0, 0)),
        ],
        out_specs=pl.BlockSpec(blk, lambda i: (i, 0)),
        out_shape=jax.ShapeDtypeStruct((T, D_CONV), BF16),
        compiler_params=_params(("arbitrary",)),
        name="shortconv",
    )(proj, proj, proj, proj, proj, w, g)


OUT_TM = 512


def _pack_bf16_pairs(h):
    half = D_MODEL // 2
    lo = h[:, :half].astype(BF16).astype(F32)
    hi = h[:, half:].astype(BF16).astype(F32)
    return (lax.bitcast_convert_type(lo, U32) >> 16) | lax.bitcast_convert_type(hi, U32)


def _unpack_pairs_f32(u):
    lo = lax.bitcast_convert_type(u << 16, F32)
    hi = lax.bitcast_convert_type(u & jnp.uint32(0xFFFF0000), F32)
    return lo, hi


ROW_TILE = 8


def _store_row_tiles(ref, packed):
    m = packed.shape[0]
    for s in range(ROW_TILE):
        ref[pl.ds(s, m, stride=ROW_TILE), :] = packed[:, s * LANES:(s + 1) * LANES]


def _load_row_tiles(ref, m, first=0, stride=ROW_TILE):
    return jnp.concatenate(
        [ref[pl.ds(first + s, m, stride=stride), :] for s in range(ROW_TILE)], axis=1)


def _outproj_kernel(a_ref, c_ref, x_ref, wo_ref, g2_ref, wr_ref, br_ref,
                    x1_ref, hp_ref, lg_ref):
    acc = jnp.dot(a_ref[...], wo_ref[0:D_ATTN, :], preferred_element_type=F32)
    acc = acc + jnp.dot(c_ref[...], wo_ref[D_ATTN:, :], preferred_element_type=F32)
    x1 = x_ref[...] + acc
    x1_ref[...] = x1
    h2 = _rms(x1, g2_ref[...])
    lg_ref[...] = jnp.dot(h2.astype(BF16), wr_ref[...],
                          preferred_element_type=F32) + br_ref[...]
    _store_row_tiles(hp_ref, _pack_bf16_pairs(h2))


def _outproj(a_n, c_n, x2, wo_bf, g2, wr_bf, br):
    tm = OUT_TM
    row = lambda i: (i, 0)
    fixed = lambda i: (0, 0)
    return pl.pallas_call(
        _outproj_kernel,
        grid=(T // tm,),
        in_specs=[
            pl.BlockSpec((tm, D_ATTN), row),
            pl.BlockSpec((tm, D_CONV), row),
            pl.BlockSpec((tm, D_MODEL), row),
            pl.BlockSpec((D_MODEL, D_MODEL), fixed),
            pl.BlockSpec((1, D_MODEL), fixed),
            pl.BlockSpec((D_MODEL, LANES), fixed),
            pl.BlockSpec((1, LANES), fixed),
        ],
        out_specs=[
            pl.BlockSpec((tm, D_MODEL), row),
            pl.BlockSpec((tm * ROW_TILE, LANES), row),
            pl.BlockSpec((tm, LANES), row),
        ],
        out_shape=[
            jax.ShapeDtypeStruct((T, D_MODEL), F32),
            jax.ShapeDtypeStruct((T * ROW_TILE, LANES), U32),
            jax.ShapeDtypeStruct((T, LANES), F32),
        ],
        compiler_params=_params(("arbitrary",)),
        name="outproj",
    )(a_n, c_n, x2, wo_bf, g2, wr_bf, br)


ROUTE_TM = 512
FIRST_EXPERT_LANE = N_GROUPS


def _route_kernel(lg_ref, ri_ref, rw_ref, cnt_ref, base_ref):
    tm = ROUTE_TM

    @pl.when(pl.program_id(0) == 0)
    def _():
        base_ref[...] = jnp.zeros_like(base_ref)

    lg = lg_ref[...]
    lane = lax.broadcasted_iota(I32, (tm, LANES), 1)
    lane_f = lane.astype(F32)
    big = jnp.float32(LANES)
    ninf = jnp.float32(-jnp.inf)

    gmask = lane < N_GROUPS
    gmax = jnp.max(jnp.where(gmask, lg, ninf), axis=1, keepdims=True)
    gsum = jnp.sum(jnp.where(gmask, jnp.exp(lg - gmax), 0.0), axis=1, keepdims=True)
    p_g = 1.0 / gsum
    g_sel = jnp.min(jnp.where(jnp.logical_and(gmask, lg == gmax), lane_f, big),
                    axis=1, keepdims=True)
    lo = FIRST_EXPERT_LANE + EPG * g_sel
    emask = jnp.logical_and(lane_f >= lo, lane_f < lo + EPG)
    v1 = jnp.max(jnp.where(emask, lg, ninf), axis=1, keepdims=True)
    i1 = jnp.min(jnp.where(jnp.logical_and(emask, lg == v1), lane_f, big),
                 axis=1, keepdims=True)
    emask2 = jnp.logical_and(emask, lane_f != i1)
    v2 = jnp.max(jnp.where(emask2, lg, ninf), axis=1, keepdims=True)
    i2 = jnp.min(jnp.where(jnp.logical_and(emask2, lg == v2), lane_f, big),
                 axis=1, keepdims=True)
    t = jnp.exp(v2 - v1)
    w1 = p_g * (1.0 / (1.0 + t))
    w2 = p_g * (t / (1.0 + t))
    e1 = i1 - FIRST_EXPERT_LANE
    e2 = i2 - FIRST_EXPERT_LANE

    hit1 = lane_f == e1
    hit2 = lane_f == e2
    onehot = jnp.where(jnp.logical_or(hit1, hit2), 1.0, 0.0)
    r = lax.broadcasted_iota(I32, (tm, tm), 0)
    c = lax.broadcasted_iota(I32, (tm, tm), 1)
    tri = jnp.where(r > c, 1.0, 0.0).astype(BF16)
    before = jnp.dot(tri, onehot.astype(BF16), preferred_element_type=F32) + base_ref[...]
    r1 = jnp.sum(jnp.where(hit1, before, 0.0), axis=1, keepdims=True)
    r2 = jnp.sum(jnp.where(hit2, before, 0.0), axis=1, keepdims=True)
    base_ref[...] = base_ref[...] + jnp.sum(onehot, axis=0, keepdims=True)
    cnt_ref[...] = base_ref[...]

    ri = jnp.where(lane == 0, e1, jnp.where(lane == 1, e2, jnp.where(lane == 2, r1, r2)))
    ri_ref[...] = ri.astype(I32)
    rw_ref[...] = jnp.where(lane == 0, w1, w2)


def _route(logits):
    tm = ROUTE_TM
    row = lambda i: (i, 0)
    return pl.pallas_call(
        _route_kernel,
        grid=(T // tm,),
        in_specs=[pl.BlockSpec((tm, LANES), row)],
        out_specs=[
            pl.BlockSpec((tm, LANES), row),
            pl.BlockSpec((tm, LANES), row),
            pl.BlockSpec((1, LANES), lambda i: (0, 0)),
        ],
        out_shape=[
            jax.ShapeDtypeStruct((T, LANES), I32),
            jax.ShapeDtypeStruct((T, LANES), F32),
            jax.ShapeDtypeStruct((1, LANES), F32),
        ],
        scratch_shapes=[pltpu.VMEM((1, LANES), F32)],
        compiler_params=_params(("arbitrary",)),
        name="route",
    )(logits)


BLOCK_LANES = 384


def _dest_kernel(ri_ref, cnt_ref, dest_ref, be_ref, seg_ref):
    tm = ROUTE_TM
    lane8 = lax.broadcasted_iota(I32, (8, LANES), 1)
    cnt = jnp.broadcast_to(cnt_ref[...], (8, LANES)).astype(I32)
    pcnt = ((cnt + (MOE_BLOCK - 1)) >> 7) << 7
    pend = pcnt
    for s in (1, 2, 4, 8, 16, 32, 64):
        pend = pend + jnp.where(lane8 >= s, pltpu.roll(pend, s, axis=1), 0)
    pstart = (pend - pcnt)[0:1].astype(F32)

    ri = ri_ref[...].astype(F32)
    lane = lax.broadcasted_iota(I32, (tm, LANES), 1)
    lane_f = lane.astype(F32)
    e1, e2, r1, r2 = ri[:, 0:1], ri[:, 1:2], ri[:, 2:3], ri[:, 3:4]
    d1 = jnp.sum(jnp.where(lane_f == e1, pstart, 0.0), axis=1, keepdims=True) + r1
    d2 = jnp.sum(jnp.where(lane_f == e2, pstart, 0.0), axis=1, keepdims=True) + r2
    dest_ref[...] = jnp.where(lane == 0, d1, d2).astype(I32)

    rr = lax.broadcasted_iota(I32, (LANES, LANES), 0)
    cc = lax.broadcasted_iota(I32, (LANES, LANES), 1)
    pend_f = jnp.broadcast_to(pend[0:1].astype(F32), (LANES, LANES))
    pend_col = jnp.sum(jnp.where(rr == cc, pend_f, 0.0), axis=1, keepdims=True)
    blk0 = (lax.broadcasted_iota(I32, (LANES, BLOCK_LANES), 1) * MOE_BLOCK).astype(F32)
    erow = lax.broadcasted_iota(I32, (LANES, BLOCK_LANES), 0)
    le = jnp.where(jnp.logical_and(erow < N_EXPERTS, pend_col <= blk0), 1.0, 0.0)
    be = jnp.minimum(jnp.sum(le, axis=0, keepdims=True), float(N_EXPERTS - 1))
    total = jnp.sum(jnp.where(lane8[0:1] == N_EXPERTS - 1, pend[0:1].astype(F32), 0.0),
                    axis=1, keepdims=True)
    blk_lane = lax.broadcasted_iota(I32, (1, BLOCK_LANES), 1)
    be = jnp.where(blk_lane == N_BLOCKS, total * (1.0 / MOE_BLOCK), be)
    be_ref[...] = jnp.broadcast_to(be, (8, BLOCK_LANES)).astype(I32)
    row8 = lax.broadcasted_iota(I32, (8, LANES), 0)
    seg_ref[...] = jnp.where(row8 == 0, pend - pcnt + cnt, pend)


def _dest(ri, cnt):
    tm = ROUTE_TM
    row = lambda i: (i, 0)
    fixed = lambda i: (0, 0)
    return pl.pallas_call(
        _dest_kernel,
        grid=(T // tm,),
        in_specs=[pl.BlockSpec((tm, LANES), row), pl.BlockSpec((1, LANES), fixed)],
        out_specs=[pl.BlockSpec((tm, LANES), row),
                   pl.BlockSpec((8, BLOCK_LANES), fixed),
                   pl.BlockSpec((8, LANES), fixed)],
        out_shape=[jax.ShapeDtypeStruct((T, LANES), I32),
                   jax.ShapeDtypeStruct((8, BLOCK_LANES), I32),
                   jax.ShapeDtypeStruct((8, LANES), I32)],
        compiler_params=_params(("arbitrary",)),
        name="dest",
    )(ri, cnt)


GROUP = 3
INV_LEN = (N_BLOCKS + GROUP - 1) * MOE_BLOCK
DUMP_ROWS = 4 * MOE_BLOCK


def _invperm_kernel(dest_ref, vend_ref, pend_ref, inv_ref):
    def pad(p, carry):
        inv_ref[p] = N_ROUTED + (p & (DUMP_ROWS - 1))
        return carry

    def pad_expert(e, carry):
        lax.fori_loop(vend_ref[e], pend_ref[e], pad, 0)
        return carry

    lax.fori_loop(0, N_EXPERTS, pad_expert, 0)
    lax.fori_loop(pend_ref[N_EXPERTS - 1], INV_LEN, pad, 0)

    def place(s, carry):
        inv_ref[dest_ref[s]] = s
        return carry

    lax.fori_loop(0, N_ROUTED, place, 0, unroll=8)


def _invperm(dest_flat, vend, pend):
    return pl.pallas_call(
        _invperm_kernel,
        grid_spec=pltpu.PrefetchScalarGridSpec(
            num_scalar_prefetch=3,
            grid=(1,),
            in_specs=[],
            out_specs=pl.BlockSpec(memory_space=pltpu.SMEM),
        ),
        out_shape=jax.ShapeDtypeStruct((INV_LEN,), I32),
        compiler_params=_params(("arbitrary",)),
        name="invperm",
    )(dest_flat, vend, pend)


BLOCK_TILE_ROWS = MOE_BLOCK * ROW_TILE
Y_ROWS = (N_ROUTED + DUMP_ROWS) * ROW_TILE


def _moe_kernel(meta_ref, inv_ref, h_ref, wg_hbm, wu_hbm, wd_hbm, y_ref,
                xb0, xb1, xb2, yb0, yb1, yb2, wg_s, wu_s, wd_s, wg_b, wu_b, wd_b,
                sem_x, sem_y, sem_w):
    xbufs = (xb0, xb1, xb2)
    ybufs = (yb0, yb1, yb2)
    n_used = meta_ref[N_BLOCKS]
    n_groups = (n_used + (GROUP - 1)) // GROUP
    half = D_MODEL // 2

    def issue_gather(b, v):
        for r in range(MOE_BLOCK):
            tok = (inv_ref[b * MOE_BLOCK + r] >> 1) & (T - 1)
            src = h_ref.at[pl.ds(pl.multiple_of(tok * ROW_TILE, ROW_TILE), ROW_TILE), :]
            pltpu.make_async_copy(src, xbufs[v].at[pl.ds(r * ROW_TILE, ROW_TILE), :],
                                  sem_x.at[v]).start()

    def wait_gather(v):
        pltpu.make_async_copy(h_ref.at[pl.ds(0, BLOCK_TILE_ROWS), :], xbufs[v],
                              sem_x.at[v]).wait()

    def issue_scatter(b, v):
        for r in range(MOE_BLOCK):
            slot = inv_ref[b * MOE_BLOCK + r]
            dst = y_ref.at[pl.ds(pl.multiple_of(slot * ROW_TILE, ROW_TILE), ROW_TILE), :]
            pltpu.make_async_copy(ybufs[v].at[pl.ds(r * ROW_TILE, ROW_TILE), :], dst,
                                  sem_y.at[v]).start()

    def wait_scatter(v):
        pltpu.make_async_copy(ybufs[v], y_ref.at[pl.ds(0, BLOCK_TILE_ROWS), :],
                              sem_y.at[v]).wait()

    def weight_copies(e):
        return (pltpu.make_async_copy(wg_hbm.at[e], wg_s, sem_w),
                pltpu.make_async_copy(wu_hbm.at[e], wu_s, sem_w),
                pltpu.make_async_copy(wd_hbm.at[e], wd_s, sem_w))

    def load_expert(b):
        e = meta_ref[b]
        prev = meta_ref[jnp.maximum(b - 1, 0)]
        first = jnp.logical_and(b < n_used, jnp.logical_or(b == 0, e != prev))

        @pl.when(first)
        def _():
            for c in weight_copies(e):
                c.wait()
            wg_b[...] = wg_s[...].astype(BF16)
            wu_b[...] = wu_s[...].astype(BF16)
            wd_b[...] = wd_s[...].astype(BF16)
            nxt = lax.while_loop(
                lambda j: jnp.logical_and(j < n_used, meta_ref[jnp.minimum(j, N_BLOCKS - 1)] == e),
                lambda j: j + 1, b + 1)

            @pl.when(nxt < n_used)
            def _():
                for c in weight_copies(meta_ref[nxt]):
                    c.start()

    def compute(v):
        lo, hi = _unpack_pairs_f32(_load_row_tiles(xbufs[v], MOE_BLOCK))
        lo, hi = lo.astype(BF16), hi.astype(BF16)
        g = (jnp.dot(lo, wg_b[0:half, :], preferred_element_type=F32)
             + jnp.dot(hi, wg_b[half:, :], preferred_element_type=F32))
        u = (jnp.dot(lo, wu_b[0:half, :], preferred_element_type=F32)
             + jnp.dot(hi, wu_b[half:, :], preferred_element_type=F32))
        a = g * (1.0 / (1.0 + jnp.exp(-g))) * u
        y = jnp.dot(a.astype(BF16), wd_b[...], preferred_element_type=F32)
        _store_row_tiles(ybufs[v], _pack_bf16_pairs(y))

    def block(b, v, scatter_prev, wait_y):
        wait_gather(v)
        if wait_y:
            wait_scatter(v)
        load_expert(b)
        issue_gather(b + 2, (v + 2) % GROUP)
        if scatter_prev:
            issue_scatter(b - 1, (v + 2) % GROUP)
        compute(v)

    for c in weight_copies(meta_ref[0]):
        c.start()
    yb2[...] = jnp.zeros_like(yb2)
    dump_fills = [
        pltpu.make_async_copy(
            yb2, y_ref.at[pl.ds((N_ROUTED + k * MOE_BLOCK) * ROW_TILE, BLOCK_TILE_ROWS), :],
            sem_y.at[GROUP - 1])
        for k in range(DUMP_ROWS // MOE_BLOCK)]
    for c in dump_fills:
        c.start()
    for c in dump_fills:
        c.wait()
    issue_gather(0, 0)
    issue_gather(1, 1)
    for v in range(GROUP):
        block(v, v, scatter_prev=v > 0, wait_y=False)

    def group(gi, carry):
        for v in range(GROUP):
            block(gi * GROUP + v, v, scatter_prev=True, wait_y=True)
        return carry

    lax.fori_loop(1, n_groups, group, 0)

    last = n_groups * GROUP
    issue_scatter(last - 1, GROUP - 1)
    wait_gather(0)
    wait_gather(1)
    for v in range(GROUP):
        wait_scatter(v)


def _moe(meta, inv, h_tiles, w_gate, w_up, w_down):
    any_spec = pl.BlockSpec(memory_space=pl.ANY)
    xy = pltpu.VMEM((BLOCK_TILE_ROWS, LANES), U32)
    return pl.pallas_call(
        _moe_kernel,
        grid_spec=pltpu.PrefetchScalarGridSpec(
            num_scalar_prefetch=2,
            grid=(1,),
            in_specs=[any_spec, any_spec, any_spec, any_spec],
            out_specs=any_spec,
            scratch_shapes=[
                xy, xy, xy, xy, xy, xy,
                pltpu.VMEM((D_MODEL, D_EXPERT), F32),
                pltpu.VMEM((D_MODEL, D_EXPERT), F32),
                pltpu.VMEM((D_EXPERT, D_MODEL), F32),
                pltpu.VMEM((D_MODEL, D_EXPERT), BF16),
                pltpu.VMEM((D_MODEL, D_EXPERT), BF16),
                pltpu.VMEM((D_EXPERT, D_MODEL), BF16),
                pltpu.SemaphoreType.DMA((GROUP,)),
                pltpu.SemaphoreType.DMA((GROUP,)),
                pltpu.SemaphoreType.DMA(()),
            ],
        ),
        out_shape=jax.ShapeDtypeStruct((Y_ROWS, LANES), U32),
        compiler_params=_params(("arbitrary",)),
        name="moe",
    )(meta, inv, h_tiles, w_gate, w_up, w_down)


COMB_TM = 256


def _combine_kernel(x1_ref, rw_ref, gf_ref, y_ref, o_ref):
    tm = COMB_TM
    w = rw_ref[...]
    w0, w1 = w[:, 0:1], w[:, 1:2]
    lo0, hi0 = _unpack_pairs_f32(_load_row_tiles(y_ref, tm, 0, 2 * ROW_TILE))
    lo1, hi1 = _unpack_pairs_f32(_load_row_tiles(y_ref, tm, ROW_TILE, 2 * ROW_TILE))
    moe = jnp.concatenate([lo0 * w0 + lo1 * w1, hi0 * w0 + hi1 * w1], axis=1)
    o_ref[...] = _rms(x1_ref[...] + moe, gf_ref[...])


def _combine(x1, rw, gf, y_tiles):
    tm = COMB_TM
    row = lambda i: (i, 0)
    return pl.pallas_call(
        _combine_kernel,
        grid=(T // tm,),
        in_specs=[
            pl.BlockSpec((tm, D_MODEL), row),
            pl.BlockSpec((tm, LANES), row),
            pl.BlockSpec((1, D_MODEL), lambda i: (0, 0)),
            pl.BlockSpec((tm * 2 * ROW_TILE, LANES), row),
        ],
        out_specs=pl.BlockSpec((tm, D_MODEL), row),
        out_shape=jax.ShapeDtypeStruct((T, D_MODEL), F32),
        compiler_params=_params(("arbitrary",)),
        name="combine",
    )(x1, rw, gf, y_tiles)


def _layer(x2, norm1, w_in, rel_bias, conv_w, g_out_attn, g_out_conv, w_out, norm2,
           w_rg, b_rg, w_re, b_re, w_gate, w_up, w_down):
    proj = _inproj(x2, norm1[None, :], w_in.astype(BF16))
    a_n = _attention(proj, _bias_table(rel_bias), g_out_attn[None, :])
    conv_w8 = jnp.zeros((8, D_CONV), F32).at[0:3].set(conv_w)
    c_n = _short_conv(proj, conv_w8, g_out_conv[None, :])

    w_r = jnp.zeros((D_MODEL, LANES), F32)
    w_r = w_r.at[:, 0:N_GROUPS].set(w_rg).at[:, N_GROUPS:N_GROUPS + N_EXPERTS].set(w_re)
    b_r = jnp.zeros((1, LANES), F32)
    b_r = b_r.at[0, 0:N_GROUPS].set(b_rg).at[0, N_GROUPS:N_GROUPS + N_EXPERTS].set(b_re)
    x1, hp, logits = _outproj(a_n, c_n, x2, w_out.astype(BF16), norm2[None, :],
                              w_r.astype(BF16), b_r)

    ri, rw, cnt = _route(logits)
    dest, be, seg = _dest(ri, cnt)
    dest_flat = dest[:, 0:2].reshape(N_ROUTED)
    inv = _invperm(dest_flat, seg[0, 0:N_EXPERTS], seg[1, 0:N_EXPERTS])
    y_tiles = _moe(be[0], inv, hp, w_gate, w_up, w_down)
    return x1, rw, y_tiles


def kernel(x, norm1, w_in, rel_bias, conv_w, g_out_attn, g_out_conv, w_out, norm2,
           w_router_group, b_router_group, w_router_expert, b_router_expert,
           w_gate, w_up, w_down, norm_final):
    assert x.shape == (BATCH, SEQ, D_MODEL) and norm1.shape[0] == 1
    x2 = x.reshape(T, D_MODEL)
    x1, rw, y_tiles = _layer(
        x2, norm1[0], w_in[0], rel_bias[0], conv_w[0], g_out_attn[0], g_out_conv[0],
        w_out[0], norm2[0], w_router_group[0], b_router_group[0], w_router_expert[0],
        b_router_expert[0], w_gate[0], w_up[0], w_down[0])
    out = _combine(x1, rw, norm_final[None, :], y_tiles)
    return out.reshape(BATCH, SEQ, D_MODEL)
```

```python
import functools

import numpy as np
import jax
import jax.numpy as jnp
from jax import lax
from jax.experimental import pallas as pl
from jax.experimental.pallas import tpu as pltpu

F32 = jnp.float32
BF16 = jnp.bfloat16
I32 = jnp.int32
U32 = jnp.uint32

D_MODEL = 2048
BATCH = 4
SEQ = 4096
T = BATCH * SEQ
CHUNK = 64
LEFT_CHUNKS = 8
BAND = LEFT_CHUNKS + 1
KEYS = BAND * CHUNK
D_ATTN = 1024
D_CONV = 1024
HEAD_DIM = 64
N_HEADS = 16
N_PAIRS = N_HEADS // 2
REL_CLIP = 256
D_IN_PROJ = 6 * 1024
N_GROUPS = 4
EPG = 8
N_EXPERTS = 32
D_EXPERT = 512
MOE_BLOCK = 128
N_ROUTED = 2 * T
N_BLOCKS = N_ROUTED // MOE_BLOCK + N_EXPERTS
NPAD = N_BLOCKS * MOE_BLOCK
EPS = 1e-6
NEG_INF = -1e30
LANES = 128
VMEM_LIMIT = 52 * 1024 * 1024

ATTN_TILE = 512
ATTN_TILES_PER_SEQ = SEQ // ATTN_TILE


def _params(sem):
    return pltpu.CompilerParams(dimension_semantics=sem, vmem_limit_bytes=VMEM_LIMIT)


def _rms(x, g):
    ms = jnp.mean(x * x, axis=-1, keepdims=True)
    return x * lax.rsqrt(ms + EPS) * g


INPROJ_TM = 1024
INPROJ_TN = 512


def _inproj_kernel(x_ref, g_ref, w_ref, o_ref, hn_ref):
    @pl.when(pl.program_id(1) == 0)
    def _():
        hn_ref[...] = _rms(x_ref[...], g_ref[...]).astype(BF16)

    o_ref[...] = jnp.dot(hn_ref[...], w_ref[...], preferred_element_type=F32).astype(BF16)


def _inproj(x2, g, w_bf):
    tm, tn = INPROJ_TM, INPROJ_TN
    return pl.pallas_call(
        _inproj_kernel,
        grid=(T // tm, D_IN_PROJ // tn),
        in_specs=[
            pl.BlockSpec((tm, D_MODEL), lambda i, j: (i, 0)),
            pl.BlockSpec((1, D_MODEL), lambda i, j: (0, 0)),
            pl.BlockSpec((D_MODEL, tn), lambda i, j: (0, j)),
        ],
        out_specs=pl.BlockSpec((tm, tn), lambda i, j: (i, j)),
        out_shape=jax.ShapeDtypeStruct((T, D_IN_PROJ), BF16),
        scratch_shapes=[pltpu.VMEM((tm, D_MODEL), BF16)],
        compiler_params=_params(("arbitrary", "arbitrary")),
        name="inproj",
    )(x2, g, w_bf)


N_QUADS = N_HEADS // 4
QUAD = 4 * HEAD_DIM
BIAS_W = 640
BIAS_WIN = 512


def _bias_kernel(b_ref, o_ref):
    first_head = lax.broadcasted_iota(I32, (KEYS, LANES), 1) < HEAD_DIM
    for hq in range(N_QUADS):
        for col in range(2):
            h0 = 4 * hq + 2 * col
            pieces = []
            for hh in range(2):
                x = jnp.broadcast_to(b_ref[h0 + hh:h0 + hh + 1, :], (KEYS, BIAS_W))
                rolled = pltpu.roll(x, 0, 1, stride=1, stride_axis=0)
                pieces.append(rolled[:, BIAS_WIN:BIAS_WIN + LANES])
            o_ref[hq, :, col * LANES:(col + 1) * LANES] = jnp.where(
                first_head, pieces[0], pieces[1])


def _bias_table(rel_bias):
    r = rel_bias.astype(F32)
    edge = jnp.broadcast_to(r[:, 2 * REL_CLIP:], (N_HEADS, KEYS - REL_CLIP + 1))
    b = jnp.concatenate([r[:, REL_CLIP:2 * REL_CLIP], edge,
                         r[:, REL_CLIP - (BIAS_W - KEYS - 1):REL_CLIP]], axis=1)
    odd = jnp.concatenate([b[:, BIAS_W - HEAD_DIM:], b[:, :BIAS_W - HEAD_DIM]], axis=1)
    is_odd = (jnp.arange(N_HEADS) % 2 == 1)[:, None]
    bvec = jnp.where(is_odd, odd, b)
    return pl.pallas_call(
        _bias_kernel,
        out_shape=jax.ShapeDtypeStruct((N_QUADS, KEYS, QUAD), F32),
        compiler_params=pltpu.CompilerParams(vmem_limit_bytes=VMEM_LIMIT),
        name="biastab",
    )(bvec)


def _attn_kernel(q_ref, kp_ref, kc_ref, vp_ref, vc_ref, bias_ref, g_ref, o_ref,
                 kw_ref, vw_ref, st_ref, p_ref, a_ref):
    i = pl.program_id(1)
    kw_ref[0:ATTN_TILE, :] = kp_ref[...]
    kw_ref[ATTN_TILE:2 * ATTN_TILE, :] = kc_ref[...]
    vw_ref[0:ATTN_TILE, :] = vp_ref[...]
    vw_ref[ATTN_TILE:2 * ATTN_TILE, :] = vc_ref[...]

    head_of_lane = lax.broadcasted_iota(I32, (CHUNK, QUAD), 1) // HEAD_DIM
    first_head = lax.broadcasted_iota(I32, (CHUNK, LANES), 1) < HEAD_DIM
    key_row = lax.broadcasted_iota(I32, (KEYS, QUAD), 0)
    ones = jnp.ones((KEYS, LANES), BF16)
    n_rb = KEYS // CHUNK

    def scores(r0, buf, masked):
        for hq in range(N_QUADS):
            c0 = hq * QUAD
            q4 = q_ref[pl.ds(r0, CHUNK), c0:c0 + QUAD] * (HEAD_DIM ** -0.5)
            zero = jnp.zeros_like(q4)
            qbd = jnp.concatenate(
                [jnp.where(head_of_lane == h, q4, zero) for h in range(4)], axis=0)
            k4 = kw_ref[pl.ds(r0, KEYS), c0:c0 + QUAD]
            st = lax.dot_general(k4, qbd, (((1,), (1,)), ((), ())),
                                 preferred_element_type=F32)
            st = st + bias_ref[hq]
            if masked:
                st = jnp.where(key_row + r0 >= ATTN_TILE, st, NEG_INF)
            st_ref[buf, hq] = st

    def exps(buf):
        for hq in range(N_QUADS):
            m = st_ref[buf, hq, 0:CHUNK, :]
            for rb in range(1, n_rb):
                m = jnp.maximum(m, st_ref[buf, hq, rb * CHUNK:(rb + 1) * CHUNK, :])
            m = jnp.max(m, axis=0, keepdims=True)
            for rb in range(n_rb):
                rows = slice(rb * CHUNK, (rb + 1) * CHUNK)
                p_ref[buf, hq, rows, :] = jnp.exp(st_ref[buf, hq, rows, :] - m).astype(BF16)

    def values(r0, buf):
        for hp in range(N_PAIRS):
            c0 = hp * LANES
            half = (hp % 2) * LANES
            pt = p_ref[buf, hp // 2, :, half:half + LANES]
            vext = jnp.concatenate([vw_ref[pl.ds(r0, KEYS), c0:c0 + LANES], ones], axis=1)
            oe = lax.dot_general(pt, vext, (((0,), (0,)), ((), ())),
                                 preferred_element_type=F32)
            o = oe[:, :LANES] * (1.0 / oe[:, LANES:])
            a_ref[pl.ds(r0, CHUNK), c0:c0 + LANES] = jnp.where(
                first_head, o[0:CHUNK], o[CHUNK:2 * CHUNK])

    n_chunks = ATTN_TILE // CHUNK

    @pl.when(i == 0)
    def _():
        def chunk_body(ci, carry):
            r0 = pl.multiple_of(ci * CHUNK, CHUNK)
            scores(r0, 0, True)
            exps(0)
            values(r0, 0)
            return carry

        lax.fori_loop(0, n_chunks, chunk_body, 0)

    @pl.when(i > 0)
    def _():
        scores(0, 0, False)
        for ci in range(n_chunks):
            if ci + 1 < n_chunks:
                scores((ci + 1) * CHUNK, (ci + 1) % 2, False)
            if ci > 0:
                values((ci - 1) * CHUNK, (ci - 1) % 2)
            exps(ci % 2)
        values((n_chunks - 1) * CHUNK, (n_chunks - 1) % 2)

    o_ref[...] = _rms(a_ref[...], g_ref[...]).astype(BF16)


def _attention(proj, bias_t, g):
    n = ATTN_TILES_PER_SEQ
    cur = lambda col: (lambda b, i: (b * n + i, col))
    prev = lambda col: (lambda b, i: (b * n + jnp.maximum(i - 1, 0), col))
    blk = (ATTN_TILE, D_ATTN)
    return pl.pallas_call(
        _attn_kernel,
        grid=(BATCH, n),
        in_specs=[
            pl.BlockSpec(blk, cur(0)),
            pl.BlockSpec(blk, prev(1)),
            pl.BlockSpec(blk, cur(1)),
            pl.BlockSpec(blk, prev(2)),
            pl.BlockSpec(blk, cur(2)),
            pl.BlockSpec((N_QUADS, KEYS, QUAD), lambda b, i: (0, 0, 0)),
            pl.BlockSpec((1, D_ATTN), lambda b, i: (0, 0)),
        ],
        out_specs=pl.BlockSpec(blk, lambda b, i: (b * n + i, 0)),
        out_shape=jax.ShapeDtypeStruct((T, D_ATTN), BF16),
        scratch_shapes=[
            pltpu.VMEM((2 * ATTN_TILE, D_ATTN), BF16),
            pltpu.VMEM((2 * ATTN_TILE, D_ATTN), BF16),
            pltpu.VMEM((2, N_QUADS, KEYS, QUAD), F32),
            pltpu.VMEM((2, N_QUADS, KEYS, QUAD), BF16),
            pltpu.VMEM((ATTN_TILE, D_ATTN), F32),
        ],
        compiler_params=_params(("arbitrary", "arbitrary")),
        name="attn",
    )(proj, proj, proj, proj, proj, bias_t, g)


CONV_TM = 512
CONV_PREV = 16


def _conv_kernel(b_ref, c_ref, u_ref, cp_ref, up_ref, w_ref, g_ref, o_ref):
    i = pl.program_id(0)
    z = c_ref[...].astype(F32) * u_ref[...].astype(F32)
    zp = cp_ref[...].astype(F32) * up_ref[...].astype(F32)
    seq_start = (i % (SEQ // CONV_TM)) == 0
    zp = jnp.where(seq_start, 0.0, zp)
    zm1 = zp[CONV_PREV - 1:CONV_PREV]
    zm2 = zp[CONV_PREV - 2:CONV_PREV - 1]
    row = lax.broadcasted_iota(I32, z.shape, 0)
    z1 = jnp.where(row == 0, zm1, pltpu.roll(z, 1, axis=0))
    z2 = jnp.where(row == 0, zm2, jnp.where(row == 1, zm1, pltpu.roll(z, 2, axis=0)))
    w = w_ref[...]
    y = w[0:1] * z2 + w[1:2] * z1 + w[2:3] * z
    o_ref[...] = _rms(b_ref[...].astype(F32) * y, g_ref[...]).astype(BF16)


def _short_conv(proj, w, g):
    tm = CONV_TM
    per = tm // CONV_PREV
    blk = (tm, D_CONV)
    col = lambda c: (lambda i: (i, c))
    prev = lambda c: (lambda i: (jnp.maximum(i * per - 1, 0), c))
    return pl.pallas_call(
        _conv_kernel,
        grid=(T // tm,),
        in_specs=[
            pl.BlockSpec(blk, col(3)),
            pl.BlockSpec(blk, col(4)),
            pl.BlockSpec(blk, col(5)),
            pl.BlockSpec((CONV_PREV, D_CONV), prev(4)),
            pl.BlockSpec((CONV_PREV, D_CONV), prev(5)),
            pl.BlockSpec((8, D_CONV), lambda i: (0, 0)),
            pl.BlockSpec((1, D_CONV), lambda i: (0, 0)),
        ],
        out_specs=pl.BlockSpec(blk, lambda i: (i, 0)),
        out_shape=jax.ShapeDtypeStruct((T, D_CONV), BF16),
        compiler_params=_params(("arbitrary",)),
        name="shortconv",
    )(proj, proj, proj, proj, proj, w, g)


OUT_TM = 512


def _pack_bf16_pairs(h):
    half = D_MODEL // 2
    lo = h[:, :half].astype(BF16).astype(F32)
    hi = h[:, half:].astype(BF16).astype(F32)
    return (lax.bitcast_convert_type(lo, U32) >> 16) | lax.bitcast_convert_type(hi, U32)


def _unpack_pairs_f32(u):
    lo = lax.bitcast_convert_type(u << 16, F32)
    hi = lax.bitcast_convert_type(u & jnp.uint32(0xFFFF0000), F32)
    return lo, hi


ROW_TILE = 8


def _store_row_tiles(ref, packed):
    m = packed.shape[0]
    for s in range(ROW_TILE):
        ref[pl.ds(s, m, stride=ROW_TILE), :] = packed[:, s * LANES:(s + 1) * LANES]


def _load_row_tiles(ref, m, first=0, stride=ROW_TILE):
    return jnp.concatenate(
        [ref[pl.ds(first + s, m, stride=stride), :] for s in range(ROW_TILE)], axis=1)


def _outproj_kernel(a_ref, c_ref, x_ref, wo_ref, g2_ref, wr_ref, br_ref,
                    x1_ref, hp_ref, lg_ref):
    acc = jnp.dot(a_ref[...], wo_ref[0:D_ATTN, :], preferred_element_type=F32)
    acc = acc + jnp.dot(c_ref[...], wo_ref[D_ATTN:, :], preferred_element_type=F32)
    x1 = x_ref[...] + acc
    x1_ref[...] = x1
    h2 = _rms(x1, g2_ref[...])
    lg_ref[...] = jnp.dot(h2.astype(BF16), wr_ref[...],
                          preferred_element_type=F32) + br_ref[...]
    _store_row_tiles(hp_ref, _pack_bf16_pairs(h2))


def _outproj(a_n, c_n, x2, wo_bf, g2, wr_bf, br):
    tm = OUT_TM
    row = lambda i: (i, 0)
    fixed = lambda i: (0, 0)
    return pl.pallas_call(
        _outproj_kernel,
        grid=(T // tm,),
        in_specs=[
            pl.BlockSpec((tm, D_ATTN), row),
            pl.BlockSpec((tm, D_CONV), row),
            pl.BlockSpec((tm, D_MODEL), row),
            pl.BlockSpec((D_MODEL, D_MODEL), fixed),
            pl.BlockSpec((1, D_MODEL), fixed),
            pl.BlockSpec((D_MODEL, LANES), fixed),
            pl.BlockSpec((1, LANES), fixed),
        ],
        out_specs=[
            pl.BlockSpec((tm, D_MODEL), row),
            pl.BlockSpec((tm * ROW_TILE, LANES), row),
            pl.BlockSpec((tm, LANES), row),
        ],
        out_shape=[
            jax.ShapeDtypeStruct((T, D_MODEL), F32),
            jax.ShapeDtypeStruct((T * ROW_TILE, LANES), U32),
            jax.ShapeDtypeStruct((T, LANES), F32),
        ],
        compiler_params=_params(("arbitrary",)),
        name="outproj",
    )(a_n, c_n, x2, wo_bf, g2, wr_bf, br)


ROUTE_TM = 512
FIRST_EXPERT_LANE = N_GROUPS


def _route_kernel(lg_ref, ri_ref, rw_ref, cnt_ref, base_ref):
    tm = ROUTE_TM

    @pl.when(pl.program_id(0) == 0)
    def _():
        base_ref[...] = jnp.zeros_like(base_ref)

    lg = lg_ref[...]
    lane = lax.broadcasted_iota(I32, (tm, LANES), 1)
    lane_f = lane.astype(F32)
    big = jnp.float32(LANES)
    ninf = jnp.float32(-jnp.inf)

    gmask = lane < N_GROUPS
    gmax = jnp.max(jnp.where(gmask, lg, ninf), axis=1, keepdims=True)
    gsum = jnp.sum(jnp.where(gmask, jnp.exp(lg - gmax), 0.0), axis=1, keepdims=True)
    p_g = 1.0 / gsum
    g_sel = jnp.min(jnp.where(jnp.logical_and(gmask, lg == gmax), lane_f, big),
                    axis=1, keepdims=True)
    lo = FIRST_EXPERT_LANE + EPG * g_sel
    emask = jnp.logical_and(lane_f >= lo, lane_f < lo + EPG)
    v1 = jnp.max(jnp.where(emask, lg, ninf), axis=1, keepdims=True)
    i1 = jnp.min(jnp.where(jnp.logical_and(emask, lg == v1), lane_f, big),
                 axis=1, keepdims=True)
    emask2 = jnp.logical_and(emask, lane_f != i1)
    v2 = jnp.max(jnp.where(emask2, lg, ninf), axis=1, keepdims=True)
    i2 = jnp.min(jnp.where(jnp.logical_and(emask2, lg == v2), lane_f, big),
                 axis=1, keepdims=True)
    t = jnp.exp(v2 - v1)
    w1 = p_g * (1.0 / (1.0 + t))
    w2 = p_g * (t / (1.0 + t))
    e1 = i1 - FIRST_EXPERT_LANE
    e2 = i2 - FIRST_EXPERT_LANE

    hit1 = lane_f == e1
    hit2 = lane_f == e2
    onehot = jnp.where(jnp.logical_or(hit1, hit2), 1.0, 0.0)
    r = lax.broadcasted_iota(I32, (tm, tm), 0)
    c = lax.broadcasted_iota(I32, (tm, tm), 1)
    tri = jnp.where(r > c, 1.0, 0.0).astype(BF16)
    before = jnp.dot(tri, onehot.astype(BF16), preferred_element_type=F32) + base_ref[...]
    r1 = jnp.sum(jnp.where(hit1, before, 0.0), axis=1, keepdims=True)
    r2 = jnp.sum(jnp.where(hit2, before, 0.0), axis=1, keepdims=True)
    base_ref[...] = base_ref[...] + jnp.sum(onehot, axis=0, keepdims=True)
    cnt_ref[...] = base_ref[...]

    ri = jnp.where(lane == 0, e1, jnp.where(lane == 1, e2, jnp.where(lane == 2, r1, r2)))
    ri_ref[...] = ri.astype(I32)
    rw_ref[...] = jnp.where(lane == 0, w1, w2)


def _route(logits):
    tm = ROUTE_TM
    row = lambda i: (i, 0)
    return pl.pallas_call(
        _route_kernel,
        grid=(T // tm,),
        in_specs=[pl.BlockSpec((tm, LANES), row)],
        out_specs=[
            pl.BlockSpec((tm, LANES), row),
            pl.BlockSpec((tm, LANES), row),
            pl.BlockSpec((1, LANES), lambda i: (0, 0)),
        ],
        out_shape=[
            jax.ShapeDtypeStruct((T, LANES), I32),
            jax.ShapeDtypeStruct((T, LANES), F32),
            jax.ShapeDtypeStruct((1, LANES), F32),
        ],
        scratch_shapes=[pltpu.VMEM((1, LANES), F32)],
        compiler_params=_params(("arbitrary",)),
        name="route",
    )(logits)


BLOCK_LANES = 384
DEST_TM = 2048


def _dest_kernel(ri_ref, cnt_ref, dest_ref, be_ref, seg_ref):
    tm = DEST_TM
    lane8 = lax.broadcasted_iota(I32, (8, LANES), 1)
    cnt = jnp.broadcast_to(cnt_ref[...], (8, LANES)).astype(I32)
    pcnt = ((cnt + (MOE_BLOCK - 1)) >> 7) << 7
    pend = pcnt
    for s in (1, 2, 4, 8, 16, 32, 64):
        pend = pend + jnp.where(lane8 >= s, pltpu.roll(pend, s, axis=1), 0)
    pstart = (pend - pcnt)[0:1].astype(F32)

    ri = ri_ref[...].astype(F32)
    lane = lax.broadcasted_iota(I32, (tm, LANES), 1)
    lane_f = lane.astype(F32)
    e1, e2, r1, r2 = ri[:, 0:1], ri[:, 1:2], ri[:, 2:3], ri[:, 3:4]
    d1 = jnp.sum(jnp.where(lane_f == e1, pstart, 0.0), axis=1, keepdims=True) + r1
    d2 = jnp.sum(jnp.where(lane_f == e2, pstart, 0.0), axis=1, keepdims=True) + r2
    dest_ref[...] = jnp.where(lane == 0, d1, d2).astype(I32)

    rr = lax.broadcasted_iota(I32, (LANES, LANES), 0)
    cc = lax.broadcasted_iota(I32, (LANES, LANES), 1)
    pend_f = jnp.broadcast_to(pend[0:1].astype(F32), (LANES, LANES))
    pend_col = jnp.sum(jnp.where(rr == cc, pend_f, 0.0), axis=1, keepdims=True)
    blk0 = (lax.broadcasted_iota(I32, (LANES, BLOCK_LANES), 1) * MOE_BLOCK).astype(F32)
    erow = lax.broadcasted_iota(I32, (LANES, BLOCK_LANES), 0)
    le = jnp.where(jnp.logical_and(erow < N_EXPERTS, pend_col <= blk0), 1.0, 0.0)
    be = jnp.minimum(jnp.sum(le, axis=0, keepdims=True), float(N_EXPERTS - 1))
    total = jnp.sum(jnp.where(lane8[0:1] == N_EXPERTS - 1, pend[0:1].astype(F32), 0.0),
                    axis=1, keepdims=True)
    blk_lane = lax.broadcasted_iota(I32, (1, BLOCK_LANES), 1)
    be = jnp.where(blk_lane == N_BLOCKS, total * (1.0 / MOE_BLOCK), be)
    be_ref[...] = jnp.broadcast_to(be, (8, BLOCK_LANES)).astype(I32)
    row8 = lax.broadcasted_iota(I32, (8, LANES), 0)
    seg_ref[...] = jnp.where(row8 == 0, pend - pcnt + cnt, pend)


def _dest(ri, cnt):
    tm = DEST_TM
    row = lambda i: (i, 0)
    fixed = lambda i: (0, 0)
    return pl.pallas_call(
        _dest_kernel,
        grid=(T // tm,),
        in_specs=[pl.BlockSpec((tm, LANES), row), pl.BlockSpec((1, LANES), fixed)],
        out_specs=[pl.BlockSpec((tm, LANES), row),
                   pl.BlockSpec((8, BLOCK_LANES), fixed),
                   pl.BlockSpec((8, LANES), fixed)],
        out_shape=[jax.ShapeDtypeStruct((T, LANES), I32),
                   jax.ShapeDtypeStruct((8, BLOCK_LANES), I32),
                   jax.ShapeDtypeStruct((8, LANES), I32)],
        compiler_params=_params(("arbitrary",)),
        name="dest",
    )(ri, cnt)


GROUP = 3
INV_LEN = (N_BLOCKS + GROUP - 1) * MOE_BLOCK
DUMP_ROWS = 4 * MOE_BLOCK


def _invperm_kernel(dest_ref, vend_ref, pend_ref, inv_ref):
    def pad(p, carry):
        inv_ref[p] = N_ROUTED + (p & (DUMP_ROWS - 1))
        return carry

    def pad_expert(e, carry):
        lax.fori_loop(vend_ref[e], pend_ref[e], pad, 0)
        return carry

    lax.fori_loop(0, N_EXPERTS, pad_expert, 0)
    lax.fori_loop(pend_ref[N_EXPERTS - 1], INV_LEN, pad, 0)

    def place(s, carry):
        inv_ref[dest_ref[s]] = s
        return carry

    lax.fori_loop(0, N_ROUTED, place, 0, unroll=8)


def _invperm(dest_flat, vend, pend):
    return pl.pallas_call(
        _invperm_kernel,
        grid_spec=pltpu.PrefetchScalarGridSpec(
            num_scalar_prefetch=3,
            grid=(1,),
            in_specs=[],
            out_specs=pl.BlockSpec(memory_space=pltpu.SMEM),
        ),
        out_shape=jax.ShapeDtypeStruct((INV_LEN,), I32),
        compiler_params=_params(("arbitrary",)),
        name="invperm",
    )(dest_flat, vend, pend)


BLOCK_TILE_ROWS = MOE_BLOCK * ROW_TILE
Y_ROWS = (N_ROUTED + DUMP_ROWS) * ROW_TILE


def _moe_kernel(meta_ref, inv_ref, h_ref, wg_hbm, wu_hbm, wd_hbm, y_ref,
                xb0, xb1, xb2, yb0, yb1, yb2, wg_s, wu_s, wd_s, wg_b, wu_b, wd_b,
                sem_x, sem_y, sem_w):
    xbufs = (xb0, xb1, xb2)
    ybufs = (yb0, yb1, yb2)
    n_used = meta_ref[N_BLOCKS]
    n_groups = (n_used + (GROUP - 1)) // GROUP
    half = D_MODEL // 2

    def issue_gather(b, v):
        for r in range(MOE_BLOCK):
            tok = (inv_ref[b * MOE_BLOCK + r] >> 1) & (T - 1)
            src = h_ref.at[pl.ds(pl.multiple_of(tok * ROW_TILE, ROW_TILE), ROW_TILE), :]
            pltpu.make_async_copy(src, xbufs[v].at[pl.ds(r * ROW_TILE, ROW_TILE), :],
                                  sem_x.at[v]).start()

    def wait_gather(v):
        pltpu.make_async_copy(h_ref.at[pl.ds(0, BLOCK_TILE_ROWS), :], xbufs[v],
                              sem_x.at[v]).wait()

    def issue_scatter(b, v):
        for r in range(MOE_BLOCK):
            slot = inv_ref[b * MOE_BLOCK + r]
            dst = y_ref.at[pl.ds(pl.multiple_of(slot * ROW_TILE, ROW_TILE), ROW_TILE), :]
            pltpu.make_async_copy(ybufs[v].at[pl.ds(r * ROW_TILE, ROW_TILE), :], dst,
                                  sem_y.at[v]).start()

    def wait_scatter(v):
        pltpu.make_async_copy(ybufs[v], y_ref.at[pl.ds(0, BLOCK_TILE_ROWS), :],
                              sem_y.at[v]).wait()

    def weight_copies(e):
        return (pltpu.make_async_copy(wg_hbm.at[e], wg_s, sem_w),
                pltpu.make_async_copy(wu_hbm.at[e], wu_s, sem_w),
                pltpu.make_async_copy(wd_hbm.at[e], wd_s, sem_w))

    def load_expert(b):
        e = meta_ref[b]
        prev = meta_ref[jnp.maximum(b - 1, 0)]
        first = jnp.logical_and(b < n_used, jnp.logical_or(b == 0, e != prev))

        @pl.when(first)
        def _():
            for c in weight_copies(e):
                c.wait()
            wg_b[...] = wg_s[...].astype(BF16)
            wu_b[...] = wu_s[...].astype(BF16)
            wd_b[...] = wd_s[...].astype(BF16)
            nxt = lax.while_loop(
                lambda j: jnp.logical_and(j < n_used, meta_ref[jnp.minimum(j, N_BLOCKS - 1)] == e),
                lambda j: j + 1, b + 1)

            @pl.when(nxt < n_used)
            def _():
                for c in weight_copies(meta_ref[nxt]):
                    c.start(priority=1)

    def compute(v):
        lo, hi = _unpack_pairs_f32(_load_row_tiles(xbufs[v], MOE_BLOCK))
        lo, hi = lo.astype(BF16), hi.astype(BF16)
        g = (jnp.dot(lo, wg_b[0:half, :], preferred_element_type=F32)
             + jnp.dot(hi, wg_b[half:, :], preferred_element_type=F32))
        u = (jnp.dot(lo, wu_b[0:half, :], preferred_element_type=F32)
             + jnp.dot(hi, wu_b[half:, :], preferred_element_type=F32))
        a = g * (1.0 / (1.0 + jnp.exp(-g))) * u
        y = jnp.dot(a.astype(BF16), wd_b[...], preferred_element_type=F32)
        _store_row_tiles(ybufs[v], _pack_bf16_pairs(y))

    def block(b, v, scatter_prev, wait_y):
        wait_gather(v)
        if wait_y:
            wait_scatter(v)
        load_expert(b)
        issue_gather(b + 2, (v + 2) % GROUP)
        if scatter_prev:
            issue_scatter(b - 1, (v + 2) % GROUP)
        compute(v)

    for c in weight_copies(meta_ref[0]):
        c.start()
    yb2[...] = jnp.zeros_like(yb2)
    dump_fills = [
        pltpu.make_async_copy(
            yb2, y_ref.at[pl.ds((N_ROUTED + k * MOE_BLOCK) * ROW_TILE, BLOCK_TILE_ROWS), :],
            sem_y.at[GROUP - 1])
        for k in range(DUMP_ROWS // MOE_BLOCK)]
    for c in dump_fills:
        c.start()
    for c in dump_fills:
        c.wait()
    issue_gather(0, 0)
    issue_gather(1, 1)
    for v in range(GROUP):
        block(v, v, scatter_prev=v > 0, wait_y=False)

    def group(gi, carry):
        for v in range(GROUP):
            block(gi * GROUP + v, v, scatter_prev=True, wait_y=True)
        return carry

    lax.fori_loop(1, n_groups, group, 0)

    last = n_groups * GROUP
    issue_scatter(last - 1, GROUP - 1)
    wait_gather(0)
    wait_gather(1)
    for v in range(GROUP):
        wait_scatter(v)


def _moe(meta, inv, h_tiles, w_gate, w_up, w_down):
    any_spec = pl.BlockSpec(memory_space=pl.ANY)
    xy = pltpu.VMEM((BLOCK_TILE_ROWS, LANES), U32)
    return pl.pallas_call(
        _moe_kernel,
        grid_spec=pltpu.PrefetchScalarGridSpec(
            num_scalar_prefetch=2,
            grid=(1,),
            in_specs=[any_spec, any_spec, any_spec, any_spec],
            out_specs=any_spec,
            scratch_shapes=[
                xy, xy, xy, xy, xy, xy,
                pltpu.VMEM((D_MODEL, D_EXPERT), F32),
                pltpu.VMEM((D_MODEL, D_EXPERT), F32),
                pltpu.VMEM((D_EXPERT, D_MODEL), F32),
                pltpu.VMEM((D_MODEL, D_EXPERT), BF16),
                pltpu.VMEM((D_MODEL, D_EXPERT), BF16),
                pltpu.VMEM((D_EXPERT, D_MODEL), BF16),
                pltpu.SemaphoreType.DMA((GROUP,)),
                pltpu.SemaphoreType.DMA((GROUP,)),
                pltpu.SemaphoreType.DMA(()),
            ],
        ),
        out_shape=jax.ShapeDtypeStruct((Y_ROWS, LANES), U32),
        compiler_params=_params(("arbitrary",)),
        name="moe",
    )(meta, inv, h_tiles, w_gate, w_up, w_down)


COMB_TM = 256


def _combine_kernel(x1_ref, rw_ref, gf_ref, y_ref, o_ref):
    tm = COMB_TM
    w = rw_ref[...]
    w0, w1 = w[:, 0:1], w[:, 1:2]
    lo0, hi0 = _unpack_pairs_f32(_load_row_tiles(y_ref, tm, 0, 2 * ROW_TILE))
    lo1, hi1 = _unpack_pairs_f32(_load_row_tiles(y_ref, tm, ROW_TILE, 2 * ROW_TILE))
    moe = jnp.concatenate([lo0 * w0 + lo1 * w1, hi0 * w0 + hi1 * w1], axis=1)
    o_ref[...] = _rms(x1_ref[...] + moe, gf_ref[...])


def _combine(x1, rw, gf, y_tiles):
    tm = COMB_TM
    row = lambda i: (i, 0)
    return pl.pallas_call(
        _combine_kernel,
        grid=(T // tm,),
        in_specs=[
            pl.BlockSpec((tm, D_MODEL), row),
            pl.BlockSpec((tm, LANES), row),
            pl.BlockSpec((1, D_MODEL), lambda i: (0, 0)),
            pl.BlockSpec((tm * 2 * ROW_TILE, LANES), row),
        ],
        out_specs=pl.BlockSpec((tm, D_MODEL), row),
        out_shape=jax.ShapeDtypeStruct((T, D_MODEL), F32),
        compiler_params=_params(("arbitrary",)),
        name="combine",
    )(x1, rw, gf, y_tiles)


def _layer(x2, norm1, w_in, rel_bias, conv_w, g_out_attn, g_out_conv, w_out, norm2,
           w_rg, b_rg, w_re, b_re, w_gate, w_up, w_down):
    proj = _inproj(x2, norm1[None, :], w_in.astype(BF16))
    a_n = _attention(proj, _bias_table(rel_bias), g_out_attn[None, :])
    conv_w8 = jnp.zeros((8, D_CONV), F32).at[0:3].set(conv_w)
    c_n = _short_conv(proj, conv_w8, g_out_conv[None, :])

    w_r = jnp.zeros((D_MODEL, LANES), F32)
    w_r = w_r.at[:, 0:N_GROUPS].set(w_rg).at[:, N_GROUPS:N_GROUPS + N_EXPERTS].set(w_re)
    b_r = jnp.zeros((1, LANES), F32)
    b_r = b_r.at[0, 0:N_GROUPS].set(b_rg).at[0, N_GROUPS:N_GROUPS + N_EXPERTS].set(b_re)
    x1, hp, logits = _outproj(a_n, c_n, x2, w_out.astype(BF16), norm2[None, :],
                              w_r.astype(BF16), b_r)

    ri, rw, cnt = _route(logits)
    dest, be, seg = _dest(ri, cnt)
    dest_flat = dest[:, 0:2].reshape(N_ROUTED)
    inv = _invperm(dest_flat, seg[0, 0:N_EXPERTS], seg[1, 0:N_EXPERTS])
    y_tiles = _moe(be[0], inv, hp, w_gate, w_up, w_down)
    return x1, rw, y_tiles


def kernel(x, norm1, w_in, rel_bias, conv_w, g_out_attn, g_out_conv, w_out, norm2,
           w_router_group, b_router_group, w_router_expert, b_router_expert,
           w_gate, w_up, w_down, norm_final):
    assert x.shape == (BATCH, SEQ, D_MODEL) and norm1.shape[0] == 1
    x2 = x.reshape(T, D_MODEL)
    x1, rw, y_tiles = _layer(
        x2, norm1[0], w_in[0], rel_bias[0], conv_w[0], g_out_attn[0], g_out_conv[0],
        w_out[0], norm2[0], w_router_group[0], b_router_group[0], w_router_expert[0],
        b_router_expert[0], w_gate[0], w_up[0], w_down[0])
    out = _combine(x1, rw, norm_final[None, :], y_tiles)
    return out.reshape(BATCH, SEQ, D_MODEL)
```

```python
import functools

import numpy as np
import jax
import jax.numpy as jnp
from jax import lax
from jax.experimental import pallas as pl
from jax.experimental.pallas import tpu as pltpu

F32 = jnp.float32
BF16 = jnp.bfloat16
I32 = jnp.int32
U32 = jnp.uint32

D_MODEL = 2048
BATCH = 4
SEQ = 4096
T = BATCH * SEQ
CHUNK = 64
LEFT_CHUNKS = 8
BAND = LEFT_CHUNKS + 1
KEYS = BAND * CHUNK
D_ATTN = 1024
D_CONV = 1024
HEAD_DIM = 64
N_HEADS = 16
N_PAIRS = N_HEADS // 2
REL_CLIP = 256
D_IN_PROJ = 6 * 1024
N_GROUPS = 4
EPG = 8
N_EXPERTS = 32
D_EXPERT = 512
MOE_BLOCK = 128
N_ROUTED = 2 * T
N_BLOCKS = N_ROUTED // MOE_BLOCK + N_EXPERTS
NPAD = N_BLOCKS * MOE_BLOCK
EPS = 1e-6
NEG_INF = -1e30
LANES = 128
VMEM_LIMIT = 52 * 1024 * 1024

ATTN_TILE = 512
ATTN_TILES_PER_SEQ = SEQ // ATTN_TILE


def _params(sem):
    return pltpu.CompilerParams(dimension_semantics=sem, vmem_limit_bytes=VMEM_LIMIT)


def _rms(x, g):
    ms = jnp.mean(x * x, axis=-1, keepdims=True)
    return x * lax.rsqrt(ms + EPS) * g


INPROJ_TM = 1024
INPROJ_TN = 1536


def _inproj_kernel(x_ref, g_ref, w_ref, o_ref, hn_ref):
    @pl.when(pl.program_id(1) == 0)
    def _():
        hn_ref[...] = _rms(x_ref[...], g_ref[...]).astype(BF16)

    o_ref[...] = jnp.dot(hn_ref[...], w_ref[...], preferred_element_type=F32).astype(BF16)


def _inproj(x2, g, w_bf):
    tm, tn = INPROJ_TM, INPROJ_TN
    return pl.pallas_call(
        _inproj_kernel,
        grid=(T // tm, D_IN_PROJ // tn),
        in_specs=[
            pl.BlockSpec((tm, D_MODEL), lambda i, j: (i, 0)),
            pl.BlockSpec((1, D_MODEL), lambda i, j: (0, 0)),
            pl.BlockSpec((D_MODEL, tn), lambda i, j: (0, j)),
        ],
        out_specs=pl.BlockSpec((tm, tn), lambda i, j: (i, j)),
        out_shape=jax.ShapeDtypeStruct((T, D_IN_PROJ), BF16),
        scratch_shapes=[pltpu.VMEM((tm, D_MODEL), BF16)],
        compiler_params=_params(("arbitrary", "arbitrary")),
        name="inproj",
    )(x2, g, w_bf)


N_QUADS = N_HEADS // 4
QUAD = 4 * HEAD_DIM
BIAS_W = 640
BIAS_WIN = 512


def _bias_kernel(b_ref, o_ref):
    first_head = lax.broadcasted_iota(I32, (KEYS, LANES), 1) < HEAD_DIM
    for hq in range(N_QUADS):
        for col in range(2):
            h0 = 4 * hq + 2 * col
            pieces = []
            for hh in range(2):
                x = jnp.broadcast_to(b_ref[h0 + hh:h0 + hh + 1, :], (KEYS, BIAS_W))
                rolled = pltpu.roll(x, 0, 1, stride=1, stride_axis=0)
                pieces.append(rolled[:, BIAS_WIN:BIAS_WIN + LANES])
            o_ref[hq, :, col * LANES:(col + 1) * LANES] = jnp.where(
                first_head, pieces[0], pieces[1])


def _bias_table(rel_bias):
    r = rel_bias.astype(F32)
    edge = jnp.broadcast_to(r[:, 2 * REL_CLIP:], (N_HEADS, KEYS - REL_CLIP + 1))
    b = jnp.concatenate([r[:, REL_CLIP:2 * REL_CLIP], edge,
                         r[:, REL_CLIP - (BIAS_W - KEYS - 1):REL_CLIP]], axis=1)
    odd = jnp.concatenate([b[:, BIAS_W - HEAD_DIM:], b[:, :BIAS_W - HEAD_DIM]], axis=1)
    is_odd = (jnp.arange(N_HEADS) % 2 == 1)[:, None]
    bvec = jnp.where(is_odd, odd, b)
    return pl.pallas_call(
        _bias_kernel,
        out_shape=jax.ShapeDtypeStruct((N_QUADS, KEYS, QUAD), F32),
        compiler_params=pltpu.CompilerParams(vmem_limit_bytes=VMEM_LIMIT),
        name="biastab",
    )(bvec)


def _attn_kernel(q_ref, kp_ref, kc_ref, vp_ref, vc_ref, bias_ref, g_ref, o_ref,
                 kw_ref, vw_ref, st_ref, p_ref, a_ref):
    i = pl.program_id(1)
    kw_ref[0:ATTN_TILE, :] = kp_ref[...]
    kw_ref[ATTN_TILE:2 * ATTN_TILE, :] = kc_ref[...]
    vw_ref[0:ATTN_TILE, :] = vp_ref[...]
    vw_ref[ATTN_TILE:2 * ATTN_TILE, :] = vc_ref[...]

    head_of_lane = lax.broadcasted_iota(I32, (CHUNK, QUAD), 1) // HEAD_DIM
    first_head = lax.broadcasted_iota(I32, (CHUNK, LANES), 1) < HEAD_DIM
    key_row = lax.broadcasted_iota(I32, (KEYS, QUAD), 0)
    ones = jnp.ones((KEYS, LANES), BF16)
    n_rb = KEYS // CHUNK

    def scores(r0, buf, masked):
        for hq in range(N_QUADS):
            c0 = hq * QUAD
            q4 = q_ref[pl.ds(r0, CHUNK), c0:c0 + QUAD] * (HEAD_DIM ** -0.5)
            zero = jnp.zeros_like(q4)
            qbd = jnp.concatenate(
                [jnp.where(head_of_lane == h, q4, zero) for h in range(4)], axis=0)
            k4 = kw_ref[pl.ds(r0, KEYS), c0:c0 + QUAD]
            st = lax.dot_general(k4, qbd, (((1,), (1,)), ((), ())),
                                 preferred_element_type=F32)
            st = st + bias_ref[hq]
            if masked:
                st = jnp.where(key_row + r0 >= ATTN_TILE, st, NEG_INF)
            st_ref[buf, hq] = st

    def exps(buf):
        for hq in range(N_QUADS):
            m = st_ref[buf, hq, 0:CHUNK, :]
            for rb in range(1, n_rb):
                m = jnp.maximum(m, st_ref[buf, hq, rb * CHUNK:(rb + 1) * CHUNK, :])
            m = jnp.max(m, axis=0, keepdims=True)
            for rb in range(n_rb):
                rows = slice(rb * CHUNK, (rb + 1) * CHUNK)
                p_ref[buf, hq, rows, :] = jnp.exp(st_ref[buf, hq, rows, :] - m).astype(BF16)

    def values(r0, buf):
        for hp in range(N_PAIRS):
            c0 = hp * LANES
            half = (hp % 2) * LANES
            pt = p_ref[buf, hp // 2, :, half:half + LANES]
            vext = jnp.concatenate([vw_ref[pl.ds(r0, KEYS), c0:c0 + LANES], ones], axis=1)
            oe = lax.dot_general(pt, vext, (((0,), (0,)), ((), ())),
                                 preferred_element_type=F32)
            o = oe[:, :LANES] * (1.0 / oe[:, LANES:])
            a_ref[pl.ds(r0, CHUNK), c0:c0 + LANES] = jnp.where(
                first_head, o[0:CHUNK], o[CHUNK:2 * CHUNK])

    n_chunks = ATTN_TILE // CHUNK

    @pl.when(i == 0)
    def _():
        def chunk_body(ci, carry):
            r0 = pl.multiple_of(ci * CHUNK, CHUNK)
            scores(r0, 0, True)
            exps(0)
            values(r0, 0)
            return carry

        lax.fori_loop(0, n_chunks, chunk_body, 0)

    @pl.when(i > 0)
    def _():
        scores(0, 0, False)
        for ci in range(n_chunks):
            if ci + 1 < n_chunks:
                scores((ci + 1) * CHUNK, (ci + 1) % 2, False)
            if ci > 0:
                values((ci - 1) * CHUNK, (ci - 1) % 2)
            exps(ci % 2)
        values((n_chunks - 1) * CHUNK, (n_chunks - 1) % 2)

    o_ref[...] = _rms(a_ref[...], g_ref[...]).astype(BF16)


def _attention(proj, bias_t, g):
    n = ATTN_TILES_PER_SEQ
    cur = lambda col: (lambda b, i: (b * n + i, col))
    prev = lambda col: (lambda b, i: (b * n + jnp.maximum(i - 1, 0), col))
    blk = (ATTN_TILE, D_ATTN)
    return pl.pallas_call(
        _attn_kernel,
        grid=(BATCH, n),
        in_specs=[
            pl.BlockSpec(blk, cur(0)),
            pl.BlockSpec(blk, prev(1)),
            pl.BlockSpec(blk, cur(1)),
            pl.BlockSpec(blk, prev(2)),
            pl.BlockSpec(blk, cur(2)),
            pl.BlockSpec((N_QUADS, KEYS, QUAD), lambda b, i: (0, 0, 0)),
            pl.BlockSpec((1, D_ATTN), lambda b, i: (0, 0)),
        ],
        out_specs=pl.BlockSpec(blk, lambda b, i: (b * n + i, 0)),
        out_shape=jax.ShapeDtypeStruct((T, D_ATTN), BF16),
        scratch_shapes=[
            pltpu.VMEM((2 * ATTN_TILE, D_ATTN), BF16),
            pltpu.VMEM((2 * ATTN_TILE, D_ATTN), BF16),
            pltpu.VMEM((2, N_QUADS, KEYS, QUAD), F32),
            pltpu.VMEM((2, N_QUADS, KEYS, QUAD), BF16),
            pltpu.VMEM((ATTN_TILE, D_ATTN), F32),
        ],
        compiler_params=_params(("arbitrary", "arbitrary")),
        name="attn",
    )(proj, proj, proj, proj, proj, bias_t, g)


CONV_PREV = 16


def _short_conv_rows(seq_start, b, c, u, cp, up, w, g):
    z = c.astype(F32) * u.astype(F32)
    zp = cp.astype(F32) * up.astype(F32)
    zp = jnp.where(seq_start, 0.0, zp)
    zm1 = zp[CONV_PREV - 1:CONV_PREV]
    zm2 = zp[CONV_PREV - 2:CONV_PREV - 1]
    row = lax.broadcasted_iota(I32, z.shape, 0)
    z1 = jnp.where(row == 0, zm1, pltpu.roll(z, 1, axis=0))
    z2 = jnp.where(row == 0, zm2, jnp.where(row == 1, zm1, pltpu.roll(z, 2, axis=0)))
    y = w[0:1] * z2 + w[1:2] * z1 + w[2:3] * z
    return _rms(b.astype(F32) * y, g).astype(BF16)


OUT_TM = 512
OUT_SUB = 2


def _pack_bf16_pairs(h):
    half = D_MODEL // 2
    lo = h[:, :half].astype(BF16).astype(F32)
    hi = h[:, half:].astype(BF16).astype(F32)
    return (lax.bitcast_convert_type(lo, U32) >> 16) | lax.bitcast_convert_type(hi, U32)


def _unpack_pairs_f32(u):
    lo = lax.bitcast_convert_type(u << 16, F32)
    hi = lax.bitcast_convert_type(u & jnp.uint32(0xFFFF0000), F32)
    return lo, hi


ROW_TILE = 8


def _store_row_tiles(ref, packed):
    m = packed.shape[0]
    for s in range(ROW_TILE):
        ref[pl.ds(s, m, stride=ROW_TILE), :] = packed[:, s * LANES:(s + 1) * LANES]


def _load_row_tiles(ref, m, first=0, stride=ROW_TILE):
    return jnp.concatenate(
        [ref[pl.ds(first + s, m, stride=stride), :] for s in range(ROW_TILE)], axis=1)


FIRST_EXPERT_LANE = N_GROUPS


def _route_rows(lg, base_ref):
    tm = lg.shape[0]
    lane = lax.broadcasted_iota(I32, (tm, LANES), 1)
    lane_f = lane.astype(F32)
    big = jnp.float32(LANES)
    ninf = jnp.float32(-jnp.inf)

    gmask = lane < N_GROUPS
    gmax = jnp.max(jnp.where(gmask, lg, ninf), axis=1, keepdims=True)
    gsum = jnp.sum(jnp.where(gmask, jnp.exp(lg - gmax), 0.0), axis=1, keepdims=True)
    p_g = 1.0 / gsum
    g_sel = jnp.min(jnp.where(jnp.logical_and(gmask, lg == gmax), lane_f, big),
                    axis=1, keepdims=True)
    lo = FIRST_EXPERT_LANE + EPG * g_sel
    emask = jnp.logical_and(lane_f >= lo, lane_f < lo + EPG)
    v1 = jnp.max(jnp.where(emask, lg, ninf), axis=1, keepdims=True)
    i1 = jnp.min(jnp.where(jnp.logical_and(emask, lg == v1), lane_f, big),
                 axis=1, keepdims=True)
    emask2 = jnp.logical_and(emask, lane_f != i1)
    v2 = jnp.max(jnp.where(emask2, lg, ninf), axis=1, keepdims=True)
    i2 = jnp.min(jnp.where(jnp.logical_and(emask2, lg == v2), lane_f, big),
                 axis=1, keepdims=True)
    t = jnp.exp(v2 - v1)
    w1 = p_g * (1.0 / (1.0 + t))
    w2 = p_g * (t / (1.0 + t))
    e1 = i1 - FIRST_EXPERT_LANE
    e2 = i2 - FIRST_EXPERT_LANE

    hit1 = lane_f == e1
    hit2 = lane_f == e2
    onehot = jnp.where(jnp.logical_or(hit1, hit2), 1.0, 0.0)
    r = lax.broadcasted_iota(I32, (tm, tm), 0)
    c = lax.broadcasted_iota(I32, (tm, tm), 1)
    tri = jnp.where(r > c, 1.0, 0.0).astype(BF16)
    before = jnp.dot(tri, onehot.astype(BF16), preferred_element_type=F32) + base_ref[...]
    r1 = jnp.sum(jnp.where(hit1, before, 0.0), axis=1, keepdims=True)
    r2 = jnp.sum(jnp.where(hit2, before, 0.0), axis=1, keepdims=True)
    base_ref[...] = base_ref[...] + jnp.sum(onehot, axis=0, keepdims=True)

    ri = jnp.where(lane == 0, e1, jnp.where(lane == 1, e2, jnp.where(lane == 2, r1, r2)))
    return ri.astype(I32), jnp.where(lane == 0, w1, w2)


def _outproj_kernel(a_ref, b_ref, c_ref, u_ref, cp_ref, up_ref, cw_ref, gc_ref,
                    x_ref, wo_ref, g2_ref, wr_ref, br_ref,
                    x1_ref, hp_ref, ri_ref, rw_ref, cnt_ref, base_ref):
    i = pl.program_id(0)

    @pl.when(i == 0)
    def _():
        base_ref[...] = jnp.zeros_like(base_ref)

    sub = OUT_TM // OUT_SUB
    for s in range(OUT_SUB):
        rows = slice(s * sub, (s + 1) * sub)
        if s == 0:
            seq_start = (i % (SEQ // OUT_TM)) == 0
            cp, up = cp_ref[...], up_ref[...]
        else:
            seq_start = False
            before = slice(s * sub - CONV_PREV, s * sub)
            cp, up = c_ref[before, :], u_ref[before, :]
        c_n = _short_conv_rows(seq_start, b_ref[rows, :], c_ref[rows, :], u_ref[rows, :],
                               cp, up, cw_ref[...], gc_ref[...])
        acc = jnp.dot(a_ref[rows, :], wo_ref[0:D_ATTN, :], preferred_element_type=F32)
        acc = acc + jnp.dot(c_n, wo_ref[D_ATTN:, :], preferred_element_type=F32)
        x1 = x_ref[rows, :] + acc
        x1_ref[rows, :] = x1
        h2 = _rms(x1, g2_ref[...])
        logits = jnp.dot(h2.astype(BF16), wr_ref[...],
                         preferred_element_type=F32) + br_ref[...]
        _store_row_tiles(hp_ref.at[pl.ds(s * sub * ROW_TILE, sub * ROW_TILE), :],
                         _pack_bf16_pairs(h2))
        ri, rw = _route_rows(logits, base_ref)
        ri_ref[rows, :] = ri
        rw_ref[rows, :] = rw
    cnt_ref[...] = base_ref[...]


def _outproj(a_n, proj, conv_w, g_conv, x2, wo_bf, g2, wr_bf, br):
    tm = OUT_TM
    per = tm // CONV_PREV
    row = lambda i: (i, 0)
    fixed = lambda i: (0, 0)
    col = lambda c: (lambda i: (i, c))
    prev = lambda c: (lambda i: (jnp.maximum(i * per - 1, 0), c))
    return pl.pallas_call(
        _outproj_kernel,
        grid=(T // tm,),
        in_specs=[
            pl.BlockSpec((tm, D_ATTN), row),
            pl.BlockSpec((tm, D_CONV), col(3)),
            pl.BlockSpec((tm, D_CONV), col(4)),
            pl.BlockSpec((tm, D_CONV), col(5)),
            pl.BlockSpec((CONV_PREV, D_CONV), prev(4)),
            pl.BlockSpec((CONV_PREV, D_CONV), prev(5)),
            pl.BlockSpec((8, D_CONV), fixed),
            pl.BlockSpec((1, D_CONV), fixed),
            pl.BlockSpec((tm, D_MODEL), row),
            pl.BlockSpec((D_MODEL, D_MODEL), fixed),
            pl.BlockSpec((1, D_MODEL), fixed),
            pl.BlockSpec((D_MODEL, LANES), fixed),
            pl.BlockSpec((1, LANES), fixed),
        ],
        out_specs=[
            pl.BlockSpec((tm, D_MODEL), row),
            pl.BlockSpec((tm * ROW_TILE, LANES), row),
            pl.BlockSpec((tm, LANES), row),
            pl.BlockSpec((tm, LANES), row),
            pl.BlockSpec((1, LANES), fixed),
        ],
        out_shape=[
            jax.ShapeDtypeStruct((T, D_MODEL), F32),
            jax.ShapeDtypeStruct((T * ROW_TILE, LANES), U32),
            jax.ShapeDtypeStruct((T, LANES), I32),
            jax.ShapeDtypeStruct((T, LANES), F32),
            jax.ShapeDtypeStruct((1, LANES), F32),
        ],
        scratch_shapes=[pltpu.VMEM((1, LANES), F32)],
        compiler_params=_params(("arbitrary",)),
        name="outproj",
    )(a_n, proj, proj, proj, proj, proj, conv_w, g_conv, x2, wo_bf, g2, wr_bf, br)


BLOCK_LANES = 384
DEST_TM = 2048


def _dest_kernel(ri_ref, cnt_ref, dest_ref, be_ref, seg_ref):
    tm = DEST_TM
    lane8 = lax.broadcasted_iota(I32, (8, LANES), 1)
    cnt = jnp.broadcast_to(cnt_ref[...], (8, LANES)).astype(I32)
    pcnt = ((cnt + (MOE_BLOCK - 1)) >> 7) << 7
    pend = pcnt
    for s in (1, 2, 4, 8, 16, 32, 64):
        pend = pend + jnp.where(lane8 >= s, pltpu.roll(pend, s, axis=1), 0)
    pstart = (pend - pcnt)[0:1].astype(F32)

    ri = ri_ref[...].astype(F32)
    lane = lax.broadcasted_iota(I32, (tm, LANES), 1)
    lane_f = lane.astype(F32)
    e1, e2, r1, r2 = ri[:, 0:1], ri[:, 1:2], ri[:, 2:3], ri[:, 3:4]
    d1 = jnp.sum(jnp.where(lane_f == e1, pstart, 0.0), axis=1, keepdims=True) + r1
    d2 = jnp.sum(jnp.where(lane_f == e2, pstart, 0.0), axis=1, keepdims=True) + r2
    dest_ref[...] = jnp.where(lane == 0, d1, d2).astype(I32)

    rr = lax.broadcasted_iota(I32, (LANES, LANES), 0)
    cc = lax.broadcasted_iota(I32, (LANES, LANES), 1)
    pend_f = jnp.broadcast_to(pend[0:1].astype(F32), (LANES, LANES))
    pend_col = jnp.sum(jnp.where(rr == cc, pend_f, 0.0), axis=1, keepdims=True)
    blk0 = (lax.broadcasted_iota(I32, (LANES, BLOCK_LANES), 1) * MOE_BLOCK).astype(F32)
    erow = lax.broadcasted_iota(I32, (LANES, BLOCK_LANES), 0)
    le = jnp.where(jnp.logical_and(erow < N_EXPERTS, pend_col <= blk0), 1.0, 0.0)
    be = jnp.minimum(jnp.sum(le, axis=0, keepdims=True), float(N_EXPERTS - 1))
    total = jnp.sum(jnp.where(lane8[0:1] == N_EXPERTS - 1, pend[0:1].astype(F32), 0.0),
                    axis=1, keepdims=True)
    blk_lane = lax.broadcasted_iota(I32, (1, BLOCK_LANES), 1)
    be = jnp.where(blk_lane == N_BLOCKS, total * (1.0 / MOE_BLOCK), be)
    be_ref[...] = jnp.broadcast_to(be, (8, BLOCK_LANES)).astype(I32)
    row8 = lax.broadcasted_iota(I32, (8, LANES), 0)
    seg_ref[...] = jnp.where(row8 == 0, pend - pcnt + cnt, pend)


def _dest(ri, cnt):
    tm = DEST_TM
    row = lambda i: (i, 0)
    fixed = lambda i: (0, 0)
    return pl.pallas_call(
        _dest_kernel,
        grid=(T // tm,),
        in_specs=[pl.BlockSpec((tm, LANES), row), pl.BlockSpec((1, LANES), fixed)],
        out_specs=[pl.BlockSpec((tm, LANES), row),
                   pl.BlockSpec((8, BLOCK_LANES), fixed),
                   pl.BlockSpec((8, LANES), fixed)],
        out_shape=[jax.ShapeDtypeStruct((T, LANES), I32),
                   jax.ShapeDtypeStruct((8, BLOCK_LANES), I32),
                   jax.ShapeDtypeStruct((8, LANES), I32)],
        compiler_params=_params(("arbitrary",)),
        name="dest",
    )(ri, cnt)


GROUP = 3
INV_LEN = (N_BLOCKS + GROUP - 1) * MOE_BLOCK
DUMP_ROWS = 4 * MOE_BLOCK


def _invperm_kernel(dest_ref, vend_ref, pend_ref, inv_ref):
    def pad(p, carry):
        inv_ref[p] = N_ROUTED + (p & (DUMP_ROWS - 1))
        return carry

    def pad_expert(e, carry):
        lax.fori_loop(vend_ref[e], pend_ref[e], pad, 0)
        return carry

    lax.fori_loop(0, N_EXPERTS, pad_expert, 0)
    lax.fori_loop(pend_ref[N_EXPERTS - 1], INV_LEN, pad, 0)

    def place(s, carry):
        inv_ref[dest_ref[s]] = s
        return carry

    lax.fori_loop(0, N_ROUTED, place, 0, unroll=8)


def _invperm(dest_flat, vend, pend):
    return pl.pallas_call(
        _invperm_kernel,
        grid_spec=pltpu.PrefetchScalarGridSpec(
            num_scalar_prefetch=3,
            grid=(1,),
            in_specs=[],
            out_specs=pl.BlockSpec(memory_space=pltpu.SMEM),
        ),
        out_shape=jax.ShapeDtypeStruct((INV_LEN,), I32),
        compiler_params=_params(("arbitrary",)),
        name="invperm",
    )(dest_flat, vend, pend)


BLOCK_TILE_ROWS = MOE_BLOCK * ROW_TILE
Y_ROWS = (N_ROUTED + DUMP_ROWS) * ROW_TILE


def _moe_kernel(meta_ref, inv_ref, h_ref, wg_hbm, wu_hbm, wd_hbm, y_ref,
                xb0, xb1, xb2, yb0, yb1, yb2, wg_s, wu_s, wd_s, wg_b, wu_b, wd_b,
                sem_x, sem_y, sem_w):
    xbufs = (xb0, xb1, xb2)
    ybufs = (yb0, yb1, yb2)
    n_used = meta_ref[N_BLOCKS]
    n_groups = (n_used + (GROUP - 1)) // GROUP
    half = D_MODEL // 2

    def issue_gather(b, v):
        for r in range(MOE_BLOCK):
            tok = (inv_ref[b * MOE_BLOCK + r] >> 1) & (T - 1)
            src = h_ref.at[pl.ds(pl.multiple_of(tok * ROW_TILE, ROW_TILE), ROW_TILE), :]
            pltpu.make_async_copy(src, xbufs[v].at[pl.ds(r * ROW_TILE, ROW_TILE), :],
                                  sem_x.at[v]).start()

    def wait_gather(v):
        pltpu.make_async_copy(h_ref.at[pl.ds(0, BLOCK_TILE_ROWS), :], xbufs[v],
                              sem_x.at[v]).wait()

    def issue_scatter(b, v):
        for r in range(MOE_BLOCK):
            slot = inv_ref[b * MOE_BLOCK + r]
            dst = y_ref.at[pl.ds(pl.multiple_of(slot * ROW_TILE, ROW_TILE), ROW_TILE), :]
            pltpu.make_async_copy(ybufs[v].at[pl.ds(r * ROW_TILE, ROW_TILE), :], dst,
                                  sem_y.at[v]).start()

    def wait_scatter(v):
        pltpu.make_async_copy(ybufs[v], y_ref.at[pl.ds(0, BLOCK_TILE_ROWS), :],
                              sem_y.at[v]).wait()

    def weight_copies(e):
        return (pltpu.make_async_copy(wg_hbm.at[e], wg_s, sem_w),
                pltpu.make_async_copy(wu_hbm.at[e], wu_s, sem_w),
                pltpu.make_async_copy(wd_hbm.at[e], wd_s, sem_w))

    def load_expert(b):
        e = meta_ref[b]
        prev = meta_ref[jnp.maximum(b - 1, 0)]
        first = jnp.logical_and(b < n_used, jnp.logical_or(b == 0, e != prev))

        @pl.when(first)
        def _():
            for c in weight_copies(e):
                c.wait()
            wg_b[...] = wg_s[...].astype(BF16)
            wu_b[...] = wu_s[...].astype(BF16)
            wd_b[...] = wd_s[...].astype(BF16)
            nxt = lax.while_loop(
                lambda j: jnp.logical_and(j < n_used, meta_ref[jnp.minimum(j, N_BLOCKS - 1)] == e),
                lambda j: j + 1, b + 1)

            @pl.when(nxt < n_used)
            def _():
                for c in weight_copies(meta_ref[nxt]):
                    c.start(priority=1)

    def compute(v):
        lo, hi = _unpack_pairs_f32(_load_row_tiles(xbufs[v], MOE_BLOCK))
        lo, hi = lo.astype(BF16), hi.astype(BF16)
        g = (jnp.dot(lo, wg_b[0:half, :], preferred_element_type=F32)
             + jnp.dot(hi, wg_b[half:, :], preferred_element_type=F32))
        u = (jnp.dot(lo, wu_b[0:half, :], preferred_element_type=F32)
             + jnp.dot(hi, wu_b[half:, :], preferred_element_type=F32))
        a = g * (1.0 / (1.0 + jnp.exp(-g))) * u
        y = jnp.dot(a.astype(BF16), wd_b[...], preferred_element_type=F32)
        _store_row_tiles(ybufs[v], _pack_bf16_pairs(y))

    def block(b, v, scatter_prev, wait_y):
        wait_gather(v)
        if wait_y:
            wait_scatter(v)
        load_expert(b)
        issue_gather(b + 2, (v + 2) % GROUP)
        if scatter_prev:
            issue_scatter(b - 1, (v + 2) % GROUP)
        compute(v)

    for c in weight_copies(meta_ref[0]):
        c.start()
    yb2[...] = jnp.zeros_like(yb2)
    dump_fills = [
        pltpu.make_async_copy(
            yb2, y_ref.at[pl.ds((N_ROUTED + k * MOE_BLOCK) * ROW_TILE, BLOCK_TILE_ROWS), :],
            sem_y.at[GROUP - 1])
        for k in range(DUMP_ROWS // MOE_BLOCK)]
    for c in dump_fills:
        c.start()
    for c in dump_fills:
        c.wait()
    issue_gather(0, 0)
    issue_gather(1, 1)
    for v in range(GROUP):
        block(v, v, scatter_prev=v > 0, wait_y=False)

    def group(gi, carry):
        for v in range(GROUP):
            block(gi * GROUP + v, v, scatter_prev=True, wait_y=True)
        return carry

    lax.fori_loop(1, n_groups, group, 0)

    last = n_groups * GROUP
    issue_scatter(last - 1, GROUP - 1)
    wait_gather(0)
    wait_gather(1)
    for v in range(GROUP):
        wait_scatter(v)


def _moe(meta, inv, h_tiles, w_gate, w_up, w_down):
    any_spec = pl.BlockSpec(memory_space=pl.ANY)
    xy = pltpu.VMEM((BLOCK_TILE_ROWS, LANES), U32)
    return pl.pallas_call(
        _moe_kernel,
        grid_spec=pltpu.PrefetchScalarGridSpec(
            num_scalar_prefetch=2,
            grid=(1,),
            in_specs=[any_spec, any_spec, any_spec, any_spec],
            out_specs=any_spec,
            scratch_shapes=[
                xy, xy, xy, xy, xy, xy,
                pltpu.VMEM((D_MODEL, D_EXPERT), F32),
                pltpu.VMEM((D_MODEL, D_EXPERT), F32),
                pltpu.VMEM((D_EXPERT, D_MODEL), F32),
                pltpu.VMEM((D_MODEL, D_EXPERT), BF16),
                pltpu.VMEM((D_MODEL, D_EXPERT), BF16),
                pltpu.VMEM((D_EXPERT, D_MODEL), BF16),
                pltpu.SemaphoreType.DMA((GROUP,)),
                pltpu.SemaphoreType.DMA((GROUP,)),
                pltpu.SemaphoreType.DMA(()),
            ],
        ),
        out_shape=jax.ShapeDtypeStruct((Y_ROWS, LANES), U32),
        compiler_params=_params(("arbitrary",)),
        name="moe",
    )(meta, inv, h_tiles, w_gate, w_up, w_down)


COMB_TM = 256


def _combine_kernel(x1_ref, rw_ref, gf_ref, y_ref, o_ref):
    tm = COMB_TM
    w = rw_ref[...]
    w0, w1 = w[:, 0:1], w[:, 1:2]
    lo0, hi0 = _unpack_pairs_f32(_load_row_tiles(y_ref, tm, 0, 2 * ROW_TILE))
    lo1, hi1 = _unpack_pairs_f32(_load_row_tiles(y_ref, tm, ROW_TILE, 2 * ROW_TILE))
    moe = jnp.concatenate([lo0 * w0 + lo1 * w1, hi0 * w0 + hi1 * w1], axis=1)
    o_ref[...] = _rms(x1_ref[...] + moe, gf_ref[...])


def _combine(x1, rw, gf, y_tiles):
    tm = COMB_TM
    row = lambda i: (i, 0)
    return pl.pallas_call(
        _combine_kernel,
        grid=(T // tm,),
        in_specs=[
            pl.BlockSpec((tm, D_MODEL), row),
            pl.BlockSpec((tm, LANES), row),
            pl.BlockSpec((1, D_MODEL), lambda i: (0, 0)),
            pl.BlockSpec((tm * 2 * ROW_TILE, LANES), row),
        ],
        out_specs=pl.BlockSpec((tm, D_MODEL), row),
        out_shape=jax.ShapeDtypeStruct((T, D_MODEL), F32),
        compiler_params=_params(("arbitrary",)),
        name="combine",
    )(x1, rw, gf, y_tiles)


def _layer(x2, norm1, w_in, rel_bias, conv_w, g_out_attn, g_out_conv, w_out, norm2,
           w_rg, b_rg, w_re, b_re, w_gate, w_up, w_down):
    proj = _inproj(x2, norm1[None, :], w_in.astype(BF16))
    a_n = _attention(proj, _bias_table(rel_bias), g_out_attn[None, :])
    conv_w8 = jnp.zeros((8, D_CONV), F32).at[0:3].set(conv_w)

    w_r = jnp.zeros((D_MODEL, LANES), F32)
    w_r = w_r.at[:, 0:N_GROUPS].set(w_rg).at[:, N_GROUPS:N_GROUPS + N_EXPERTS].set(w_re)
    b_r = jnp.zeros((1, LANES), F32)
    b_r = b_r.at[0, 0:N_GROUPS].set(b_rg).at[0, N_GROUPS:N_GROUPS + N_EXPERTS].set(b_re)
    x1, hp, ri, rw, cnt = _outproj(a_n, proj, conv_w8, g_out_conv[None, :], x2,
                                   w_out.astype(BF16), norm2[None, :], w_r.astype(BF16), b_r)

    dest, be, seg = _dest(ri, cnt)
    dest_flat = dest[:, 0:2].reshape(N_ROUTED)
    inv = _invperm(dest_flat, seg[0, 0:N_EXPERTS], seg[1, 0:N_EXPERTS])
    y_tiles = _moe(be[0], inv, hp, w_gate, w_up, w_down)
    return x1, rw, y_tiles


def kernel(x, norm1, w_in, rel_bias, conv_w, g_out_attn, g_out_conv, w_out, norm2,
           w_router_group, b_router_group, w_router_expert, b_router_expert,
           w_gate, w_up, w_down, norm_final):
    assert x.shape == (BATCH, SEQ, D_MODEL) and norm1.shape[0] == 1
    x2 = x.reshape(T, D_MODEL)
    x1, rw, y_tiles = _layer(
        x2, norm1[0], w_in[0], rel_bias[0], conv_w[0], g_out_attn[0], g_out_conv[0],
        w_out[0], norm2[0], w_router_group[0], b_router_group[0], w_router_expert[0],
        b_router_expert[0], w_gate[0], w_up[0], w_down[0])
    out = _combine(x1, rw, norm_final[None, :], y_tiles)
    return out.reshape(BATCH, SEQ, D_MODEL)
```

```python
import functools

import numpy as np
import jax
import jax.numpy as jnp
from jax import lax
from jax.experimental import pallas as pl
from jax.experimental.pallas import tpu as pltpu

F32 = jnp.float32
BF16 = jnp.bfloat16
I32 = jnp.int32

D_MODEL = 2048
BATCH = 4
SEQ = 4096
T = BATCH * SEQ
CHUNK = 64
LEFT_CHUNKS = 8
BAND = LEFT_CHUNKS + 1
KEYS = BAND * CHUNK
D_ATTN = 1024
D_CONV = 1024
HEAD_DIM = 64
N_HEADS = 16
N_PAIRS = N_HEADS // 2
REL_CLIP = 256
D_IN_PROJ = 6 * 1024
N_GROUPS = 4
EPG = 8
N_EXPERTS = 32
D_EXPERT = 512
MOE_BLOCK = 256
MOE_BLOCK_SHIFT = 8
N_ROUTED = 2 * T
N_BLOCKS = N_ROUTED // MOE_BLOCK + N_EXPERTS
EPS = 1e-6
NEG_INF = -1e30
LANES = 128
VMEM_LIMIT = 52 * 1024 * 1024

ATTN_TILE = 512
ATTN_TILES_PER_SEQ = SEQ // ATTN_TILE


def _params(sem):
    return pltpu.CompilerParams(dimension_semantics=sem, vmem_limit_bytes=VMEM_LIMIT)


def _rms(x, g):
    ms = jnp.mean(x * x, axis=-1, keepdims=True)
    return x * lax.rsqrt(ms + EPS) * g


INPROJ_TM = 1024
INPROJ_TN = 1536


def _inproj_kernel(x_ref, g_ref, w_ref, o_ref, hn_ref):
    @pl.when(pl.program_id(1) == 0)
    def _():
        hn_ref[...] = _rms(x_ref[...], g_ref[...]).astype(BF16)

    o_ref[...] = jnp.dot(hn_ref[...], w_ref[...], preferred_element_type=F32).astype(BF16)


def _inproj(x2, g, w_bf):
    tm, tn = INPROJ_TM, INPROJ_TN
    return pl.pallas_call(
        _inproj_kernel,
        grid=(T // tm, D_IN_PROJ // tn),
        in_specs=[
            pl.BlockSpec((tm, D_MODEL), lambda i, j: (i, 0)),
            pl.BlockSpec((1, D_MODEL), lambda i, j: (0, 0)),
            pl.BlockSpec((D_MODEL, tn), lambda i, j: (0, j)),
        ],
        out_specs=pl.BlockSpec((tm, tn), lambda i, j: (i, j)),
        out_shape=jax.ShapeDtypeStruct((T, D_IN_PROJ), BF16),
        scratch_shapes=[pltpu.VMEM((tm, D_MODEL), BF16)],
        compiler_params=_params(("arbitrary", "arbitrary")),
        name="inproj",
    )(x2, g, w_bf)


N_QUADS = N_HEADS // 4
QUAD = 4 * HEAD_DIM
BIAS_W = 640
BIAS_WIN = 512


def _bias_kernel(b_ref, o_ref):
    first_head = lax.broadcasted_iota(I32, (KEYS, LANES), 1) < HEAD_DIM
    for hq in range(N_QUADS):
        for col in range(2):
            h0 = 4 * hq + 2 * col
            pieces = []
            for hh in range(2):
                x = jnp.broadcast_to(b_ref[h0 + hh:h0 + hh + 1, :], (KEYS, BIAS_W))
                rolled = pltpu.roll(x, 0, 1, stride=1, stride_axis=0)
                pieces.append(rolled[:, BIAS_WIN:BIAS_WIN + LANES])
            o_ref[hq, :, col * LANES:(col + 1) * LANES] = jnp.where(
                first_head, pieces[0], pieces[1])


def _bias_table(rel_bias):
    r = rel_bias.astype(F32)
    edge = jnp.broadcast_to(r[:, 2 * REL_CLIP:], (N_HEADS, KEYS - REL_CLIP + 1))
    b = jnp.concatenate([r[:, REL_CLIP:2 * REL_CLIP], edge,
                         r[:, REL_CLIP - (BIAS_W - KEYS - 1):REL_CLIP]], axis=1)
    odd = jnp.concatenate([b[:, BIAS_W - HEAD_DIM:], b[:, :BIAS_W - HEAD_DIM]], axis=1)
    is_odd = (jnp.arange(N_HEADS) % 2 == 1)[:, None]
    bvec = jnp.where(is_odd, odd, b)
    return pl.pallas_call(
        _bias_kernel,
        out_shape=jax.ShapeDtypeStruct((N_QUADS, KEYS, QUAD), F32),
        compiler_params=pltpu.CompilerParams(vmem_limit_bytes=VMEM_LIMIT),
        name="biastab",
    )(bvec)


def _attn_kernel(q_ref, kp_ref, kc_ref, vp_ref, vc_ref, bias_ref, g_ref, o_ref,
                 kw_ref, vw_ref, st_ref, p_ref, a_ref):
    i = pl.program_id(1)
    kw_ref[0:ATTN_TILE, :] = kp_ref[...]
    kw_ref[ATTN_TILE:2 * ATTN_TILE, :] = kc_ref[...]
    vw_ref[0:ATTN_TILE, :] = vp_ref[...]
    vw_ref[ATTN_TILE:2 * ATTN_TILE, :] = vc_ref[...]

    head_of_lane = lax.broadcasted_iota(I32, (CHUNK, QUAD), 1) // HEAD_DIM
    first_head = lax.broadcasted_iota(I32, (CHUNK, LANES), 1) < HEAD_DIM
    key_row = lax.broadcasted_iota(I32, (KEYS, QUAD), 0)
    ones = jnp.ones((KEYS, LANES), BF16)
    n_rb = KEYS // CHUNK

    def scores(r0, buf, masked):
        for hq in range(N_QUADS):
            c0 = hq * QUAD
            q4 = q_ref[pl.ds(r0, CHUNK), c0:c0 + QUAD] * (HEAD_DIM ** -0.5)
            zero = jnp.zeros_like(q4)
            qbd = jnp.concatenate(
                [jnp.where(head_of_lane == h, q4, zero) for h in range(4)], axis=0)
            k4 = kw_ref[pl.ds(r0, KEYS), c0:c0 + QUAD]
            st = lax.dot_general(k4, qbd, (((1,), (1,)), ((), ())),
                                 preferred_element_type=F32)
            st = st + bias_ref[hq]
            if masked:
                st = jnp.where(key_row + r0 >= ATTN_TILE, st, NEG_INF)
            st_ref[buf, hq] = st

    def exps(buf):
        for hq in range(N_QUADS):
            m = st_ref[buf, hq, 0:CHUNK, :]
            for rb in range(1, n_rb):
                m = jnp.maximum(m, st_ref[buf, hq, rb * CHUNK:(rb + 1) * CHUNK, :])
            m = jnp.max(m, axis=0, keepdims=True)
            for rb in range(n_rb):
                rows = slice(rb * CHUNK, (rb + 1) * CHUNK)
                p_ref[buf, hq, rows, :] = jnp.exp(st_ref[buf, hq, rows, :] - m).astype(BF16)

    def values(r0, buf):
        for hp in range(N_PAIRS):
            c0 = hp * LANES
            half = (hp % 2) * LANES
            pt = p_ref[buf, hp // 2, :, half:half + LANES]
            vext = jnp.concatenate([vw_ref[pl.ds(r0, KEYS), c0:c0 + LANES], ones], axis=1)
            oe = lax.dot_general(pt, vext, (((0,), (0,)), ((), ())),
                                 preferred_element_type=F32)
            o = oe[:, :LANES] * (1.0 / oe[:, LANES:])
            a_ref[pl.ds(r0, CHUNK), c0:c0 + LANES] = jnp.where(
                first_head, o[0:CHUNK], o[CHUNK:2 * CHUNK])

    n_chunks = ATTN_TILE // CHUNK

    @pl.when(i == 0)
    def _():
        def chunk_body(ci, carry):
            r0 = pl.multiple_of(ci * CHUNK, CHUNK)
            scores(r0, 0, True)
            exps(0)
            values(r0, 0)
            return carry

        lax.fori_loop(0, n_chunks, chunk_body, 0)

    @pl.when(i > 0)
    def _():
        scores(0, 0, False)
        for ci in range(n_chunks):
            if ci + 1 < n_chunks:
                scores((ci + 1) * CHUNK, (ci + 1) % 2, False)
            if ci > 0:
                values((ci - 1) * CHUNK, (ci - 1) % 2)
            exps(ci % 2)
        values((n_chunks - 1) * CHUNK, (n_chunks - 1) % 2)

    o_ref[...] = _rms(a_ref[...], g_ref[...]).astype(BF16)


def _attention(proj, bias_t, g):
    n = ATTN_TILES_PER_SEQ
    cur = lambda col: (lambda b, i: (b * n + i, col))
    prev = lambda col: (lambda b, i: (b * n + jnp.maximum(i - 1, 0), col))
    blk = (ATTN_TILE, D_ATTN)
    return pl.pallas_call(
        _attn_kernel,
        grid=(BATCH, n),
        in_specs=[
            pl.BlockSpec(blk, cur(0)),
            pl.BlockSpec(blk, prev(1)),
            pl.BlockSpec(blk, cur(1)),
            pl.BlockSpec(blk, prev(2)),
            pl.BlockSpec(blk, cur(2)),
            pl.BlockSpec((N_QUADS, KEYS, QUAD), lambda b, i: (0, 0, 0)),
            pl.BlockSpec((1, D_ATTN), lambda b, i: (0, 0)),
        ],
        out_specs=pl.BlockSpec(blk, lambda b, i: (b * n + i, 0)),
        out_shape=jax.ShapeDtypeStruct((T, D_ATTN), BF16),
        scratch_shapes=[
            pltpu.VMEM((2 * ATTN_TILE, D_ATTN), BF16),
            pltpu.VMEM((2 * ATTN_TILE, D_ATTN), BF16),
            pltpu.VMEM((2, N_QUADS, KEYS, QUAD), F32),
            pltpu.VMEM((2, N_QUADS, KEYS, QUAD), BF16),
            pltpu.VMEM((ATTN_TILE, D_ATTN), F32),
        ],
        compiler_params=_params(("arbitrary", "arbitrary")),
        name="attn",
    )(proj, proj, proj, proj, proj, bias_t, g)


CONV_PREV = 16


def _short_conv_rows(seq_start, b, c, u, cp, up, w, g):
    z = c.astype(F32) * u.astype(F32)
    zp = cp.astype(F32) * up.astype(F32)
    zp = jnp.where(seq_start, 0.0, zp)
    zm1 = zp[CONV_PREV - 1:CONV_PREV]
    zm2 = zp[CONV_PREV - 2:CONV_PREV - 1]
    row = lax.broadcasted_iota(I32, z.shape, 0)
    z1 = jnp.where(row == 0, zm1, pltpu.roll(z, 1, axis=0))
    z2 = jnp.where(row == 0, zm2, jnp.where(row == 1, zm1, pltpu.roll(z, 2, axis=0)))
    y = w[0:1] * z2 + w[1:2] * z1 + w[2:3] * z
    return _rms(b.astype(F32) * y, g).astype(BF16)


OUT_TM = 512
OUT_SUB = 2


def _pack_bf16_pairs(h):
    half = D_MODEL // 2
    return pltpu.pack_elementwise([h[:, :half], h[:, half:]], packed_dtype=BF16)


def _unpack_pairs_f32(u):
    lo = pltpu.unpack_elementwise(u, index=0, packed_dtype=BF16, unpacked_dtype=F32)
    hi = pltpu.unpack_elementwise(u, index=1, packed_dtype=BF16, unpacked_dtype=F32)
    return lo, hi


ROW_TILE = 8


def _store_row_tiles(ref, packed):
    m = packed.shape[0]
    for s in range(ROW_TILE):
        ref[pl.ds(s, m, stride=ROW_TILE), :] = packed[:, s * LANES:(s + 1) * LANES]


def _load_row_tiles(ref, m, first=0, stride=ROW_TILE):
    return jnp.concatenate(
        [ref[pl.ds(first + s, m, stride=stride), :] for s in range(ROW_TILE)], axis=1)


FIRST_EXPERT_LANE = N_GROUPS


def _route_rows(lg, base_ref):
    tm = lg.shape[0]
    lane = lax.broadcasted_iota(I32, (tm, LANES), 1)
    lane_f = lane.astype(F32)
    big = jnp.float32(LANES)
    ninf = jnp.float32(-jnp.inf)

    gmask = lane < N_GROUPS
    gmax = jnp.max(jnp.where(gmask, lg, ninf), axis=1, keepdims=True)
    gsum = jnp.sum(jnp.where(gmask, jnp.exp(lg - gmax), 0.0), axis=1, keepdims=True)
    p_g = 1.0 / gsum
    g_sel = jnp.min(jnp.where(jnp.logical_and(gmask, lg == gmax), lane_f, big),
                    axis=1, keepdims=True)
    lo = FIRST_EXPERT_LANE + EPG * g_sel
    emask = jnp.logical_and(lane_f >= lo, lane_f < lo + EPG)
    v1 = jnp.max(jnp.where(emask, lg, ninf), axis=1, keepdims=True)
    i1 = jnp.min(jnp.where(jnp.logical_and(emask, lg == v1), lane_f, big),
                 axis=1, keepdims=True)
    emask2 = jnp.logical_and(emask, lane_f != i1)
    v2 = jnp.max(jnp.where(emask2, lg, ninf), axis=1, keepdims=True)
    i2 = jnp.min(jnp.where(jnp.logical_and(emask2, lg == v2), lane_f, big),
                 axis=1, keepdims=True)
    t = jnp.exp(v2 - v1)
    w1 = p_g * (1.0 / (1.0 + t))
    w2 = p_g * (t / (1.0 + t))
    e1 = i1 - FIRST_EXPERT_LANE
    e2 = i2 - FIRST_EXPERT_LANE

    hit1 = lane_f == e1
    hit2 = lane_f == e2
    onehot = jnp.where(jnp.logical_or(hit1, hit2), 1.0, 0.0)
    r = lax.broadcasted_iota(I32, (tm, tm), 0)
    c = lax.broadcasted_iota(I32, (tm, tm), 1)
    tri = jnp.where(r > c, 1.0, 0.0).astype(BF16)
    before = jnp.dot(tri, onehot.astype(BF16), preferred_element_type=F32) + base_ref[...]
    r1 = jnp.sum(jnp.where(hit1, before, 0.0), axis=1, keepdims=True)
    r2 = jnp.sum(jnp.where(hit2, before, 0.0), axis=1, keepdims=True)
    base_ref[...] = base_ref[...] + jnp.sum(onehot, axis=0, keepdims=True)

    ri = jnp.where(lane == 0, e1, jnp.where(lane == 1, e2, jnp.where(lane == 2, r1, r2)))
    return ri.astype(I32), jnp.where(lane == 0, w1, w2)


def _outproj_kernel(a_ref, b_ref, c_ref, u_ref, cp_ref, up_ref, cw_ref, gc_ref,
                    x_ref, wo_ref, g2_ref, wr_ref, br_ref,
                    x1_ref, hp_ref, ri_ref, rw_ref, cnt_ref, base_ref, cn_ref):
    i = pl.program_id(0)

    @pl.when(i == 0)
    def _():
        base_ref[...] = jnp.zeros_like(base_ref)

    sub = OUT_TM // OUT_SUB
    rows_of = lambda s: slice(s * sub, (s + 1) * sub)

    def conv(s):
        rows = rows_of(s)
        if s == 0:
            seq_start = (i % (SEQ // OUT_TM)) == 0
            cp, up = cp_ref[...], up_ref[...]
        else:
            seq_start = False
            before = slice(s * sub - CONV_PREV, s * sub)
            cp, up = c_ref[before, :], u_ref[before, :]
        cn_ref[rows, :] = _short_conv_rows(
            seq_start, b_ref[rows, :], c_ref[rows, :], u_ref[rows, :],
            cp, up, cw_ref[...], gc_ref[...])

    def project(s):
        rows = rows_of(s)
        acc = jnp.dot(a_ref[rows, :], wo_ref[0:D_ATTN, :], preferred_element_type=F32)
        acc = acc + jnp.dot(cn_ref[rows, :], wo_ref[D_ATTN:, :], preferred_element_type=F32)
        x1_ref[rows, :] = x_ref[rows, :] + acc

    def route(s):
        rows = rows_of(s)
        h2 = _rms(x1_ref[rows, :], g2_ref[...])
        logits = jnp.dot(h2.astype(BF16), wr_ref[...],
                         preferred_element_type=F32) + br_ref[...]
        _store_row_tiles(hp_ref.at[pl.ds(s * sub * ROW_TILE, sub * ROW_TILE), :],
                         _pack_bf16_pairs(h2))
        ri, rw = _route_rows(logits, base_ref)
        ri_ref[rows, :] = ri
        rw_ref[rows, :] = rw

    conv(0)
    for s in range(OUT_SUB):
        project(s)
        if s + 1 < OUT_SUB:
            conv(s + 1)
        if s > 0:
            route(s - 1)
    route(OUT_SUB - 1)
    cnt_ref[...] = base_ref[...]


def _outproj(a_n, proj, conv_w, g_conv, x2, wo_bf, g2, wr_bf, br):
    tm = OUT_TM
    per = tm // CONV_PREV
    row = lambda i: (i, 0)
    fixed = lambda i: (0, 0)
    col = lambda c: (lambda i: (i, c))
    prev = lambda c: (lambda i: (jnp.maximum(i * per - 1, 0), c))
    return pl.pallas_call(
        _outproj_kernel,
        grid=(T // tm,),
        in_specs=[
            pl.BlockSpec((tm, D_ATTN), row),
            pl.BlockSpec((tm, D_CONV), col(3)),
            pl.BlockSpec((tm, D_CONV), col(4)),
            pl.BlockSpec((tm, D_CONV), col(5)),
            pl.BlockSpec((CONV_PREV, D_CONV), prev(4)),
            pl.BlockSpec((CONV_PREV, D_CONV), prev(5)),
            pl.BlockSpec((8, D_CONV), fixed),
            pl.BlockSpec((1, D_CONV), fixed),
            pl.BlockSpec((tm, D_MODEL), row),
            pl.BlockSpec((D_MODEL, D_MODEL), fixed),
            pl.BlockSpec((1, D_MODEL), fixed),
            pl.BlockSpec((D_MODEL, LANES), fixed),
            pl.BlockSpec((1, LANES), fixed),
        ],
        out_specs=[
            pl.BlockSpec((tm, D_MODEL), row),
            pl.BlockSpec((tm * ROW_TILE, LANES), row),
            pl.BlockSpec((tm, LANES), row),
            pl.BlockSpec((tm, LANES), row),
            pl.BlockSpec((1, LANES), fixed),
        ],
        out_shape=[
            jax.ShapeDtypeStruct((T, D_MODEL), F32),
            jax.ShapeDtypeStruct((T * ROW_TILE, LANES), I32),
            jax.ShapeDtypeStruct((T, LANES), I32),
            jax.ShapeDtypeStruct((T, LANES), F32),
            jax.ShapeDtypeStruct((1, LANES), F32),
        ],
        scratch_shapes=[pltpu.VMEM((1, LANES), F32), pltpu.VMEM((tm, D_CONV), BF16)],
        compiler_params=_params(("arbitrary",)),
        name="outproj",
    )(a_n, proj, proj, proj, proj, proj, conv_w, g_conv, x2, wo_bf, g2, wr_bf, br)


BLOCK_LANES = 384
DEST_TM = 2048


def _dest_kernel(ri_ref, cnt_ref, dest_ref, be_ref, seg_ref):
    tm = DEST_TM
    lane8 = lax.broadcasted_iota(I32, (8, LANES), 1)
    cnt = jnp.broadcast_to(cnt_ref[...], (8, LANES)).astype(I32)
    pcnt = ((cnt + (MOE_BLOCK - 1)) >> MOE_BLOCK_SHIFT) << MOE_BLOCK_SHIFT
    pend = pcnt
    for s in (1, 2, 4, 8, 16, 32, 64):
        pend = pend + jnp.where(lane8 >= s, pltpu.roll(pend, s, axis=1), 0)
    pstart = (pend - pcnt)[0:1].astype(F32)

    ri = ri_ref[...].astype(F32)
    lane = lax.broadcasted_iota(I32, (tm, LANES), 1)
    lane_f = lane.astype(F32)
    e1, e2, r1, r2 = ri[:, 0:1], ri[:, 1:2], ri[:, 2:3], ri[:, 3:4]
    d1 = jnp.sum(jnp.where(lane_f == e1, pstart, 0.0), axis=1, keepdims=True) + r1
    d2 = jnp.sum(jnp.where(lane_f == e2, pstart, 0.0), axis=1, keepdims=True) + r2
    dest_ref[...] = jnp.where(lane == 0, d1, d2).astype(I32)

    rr = lax.broadcasted_iota(I32, (LANES, LANES), 0)
    cc = lax.broadcasted_iota(I32, (LANES, LANES), 1)
    pend_f = jnp.broadcast_to(pend[0:1].astype(F32), (LANES, LANES))
    pend_col = jnp.sum(jnp.where(rr == cc, pend_f, 0.0), axis=1, keepdims=True)
    blk0 = (lax.broadcasted_iota(I32, (LANES, BLOCK_LANES), 1) * MOE_BLOCK).astype(F32)
    erow = lax.broadcasted_iota(I32, (LANES, BLOCK_LANES), 0)
    le = jnp.where(jnp.logical_and(erow < N_EXPERTS, pend_col <= blk0), 1.0, 0.0)
    be = jnp.minimum(jnp.sum(le, axis=0, keepdims=True), float(N_EXPERTS - 1))
    total = jnp.sum(jnp.where(lane8[0:1] == N_EXPERTS - 1, pend[0:1].astype(F32), 0.0),
                    axis=1, keepdims=True)
    blk_lane = lax.broadcasted_iota(I32, (1, BLOCK_LANES), 1)
    be = jnp.where(blk_lane == N_BLOCKS, total * (1.0 / MOE_BLOCK), be)
    be_ref[...] = jnp.broadcast_to(be, (8, BLOCK_LANES)).astype(I32)
    row8 = lax.broadcasted_iota(I32, (8, LANES), 0)
    seg_ref[...] = jnp.where(row8 == 0, pend - pcnt + cnt, pend)


def _dest(ri, cnt):
    tm = DEST_TM
    row = lambda i: (i, 0)
    fixed = lambda i: (0, 0)
    return pl.pallas_call(
        _dest_kernel,
        grid=(T // tm,),
        in_specs=[pl.BlockSpec((tm, LANES), row), pl.BlockSpec((1, LANES), fixed)],
        out_specs=[pl.BlockSpec((tm, LANES), row),
                   pl.BlockSpec((8, BLOCK_LANES), fixed),
                   pl.BlockSpec((8, LANES), fixed)],
        out_shape=[jax.ShapeDtypeStruct((T, LANES), I32),
                   jax.ShapeDtypeStruct((8, BLOCK_LANES), I32),
                   jax.ShapeDtypeStruct((8, LANES), I32)],
        compiler_params=_params(("arbitrary",)),
        name="dest",
    )(ri, cnt)


GROUP = 3
INV_LEN = (-(-N_BLOCKS // GROUP) * GROUP + GROUP - 1) * MOE_BLOCK
DUMP_ROWS = 4 * MOE_BLOCK


def _invperm_kernel(dest_ref, vend_ref, pend_ref, inv_ref):
    def pad(p, carry):
        inv_ref[p] = N_ROUTED + (p & (DUMP_ROWS - 1))
        return carry

    def pad_expert(e, carry):
        lax.fori_loop(vend_ref[e], pend_ref[e], pad, 0)
        return carry

    lax.fori_loop(0, N_EXPERTS, pad_expert, 0)
    lax.fori_loop(pend_ref[N_EXPERTS - 1], INV_LEN, pad, 0)

    def place(s, carry):
        inv_ref[dest_ref[s]] = s
        return carry

    lax.fori_loop(0, N_ROUTED, place, 0, unroll=32)


def _invperm(dest_flat, vend, pend):
    return pl.pallas_call(
        _invperm_kernel,
        grid_spec=pltpu.PrefetchScalarGridSpec(
            num_scalar_prefetch=3,
            grid=(1,),
            in_specs=[],
            out_specs=pl.BlockSpec(memory_space=pltpu.SMEM),
        ),
        out_shape=jax.ShapeDtypeStruct((INV_LEN,), I32),
        compiler_params=_params(("arbitrary",)),
        name="invperm",
    )(dest_flat, vend, pend)


BLOCK_TILE_ROWS = MOE_BLOCK * ROW_TILE
Y_ROWS = (N_ROUTED + DUMP_ROWS) * ROW_TILE


def _moe_kernel(meta_ref, inv_ref, h_ref, wg_hbm, wu_hbm, wd_hbm, y_ref,
                xb0, xb1, xb2, yb0, yb1, yb2, wg_s, wu_s, wd_s, wg_b, wu_b, wd_b,
                sem_x, sem_y, sem_w):
    xbufs = (xb0, xb1, xb2)
    ybufs = (yb0, yb1, yb2)
    n_used = meta_ref[N_BLOCKS]
    n_groups = (n_used + (GROUP - 1)) // GROUP
    half = D_MODEL // 2

    def issue_gather(b, v):
        for r in range(MOE_BLOCK):
            tok = (inv_ref[b * MOE_BLOCK + r] >> 1) & (T - 1)
            src = h_ref.at[pl.ds(pl.multiple_of(tok * ROW_TILE, ROW_TILE), ROW_TILE), :]
            pltpu.make_async_copy(src, xbufs[v].at[pl.ds(r * ROW_TILE, ROW_TILE), :],
                                  sem_x.at[v]).start()

    def wait_gather(v):
        pltpu.make_async_copy(h_ref.at[pl.ds(0, BLOCK_TILE_ROWS), :], xbufs[v],
                              sem_x.at[v]).wait()

    def issue_scatter(b, v):
        for r in range(MOE_BLOCK):
            slot = inv_ref[b * MOE_BLOCK + r]
            dst = y_ref.at[pl.ds(pl.multiple_of(slot * ROW_TILE, ROW_TILE), ROW_TILE), :]
            pltpu.make_async_copy(ybufs[v].at[pl.ds(r * ROW_TILE, ROW_TILE), :], dst,
                                  sem_y.at[v]).start()

    def wait_scatter(v):
        pltpu.make_async_copy(ybufs[v], y_ref.at[pl.ds(0, BLOCK_TILE_ROWS), :],
                              sem_y.at[v]).wait()

    def weight_copies(e):
        return (pltpu.make_async_copy(wg_hbm.at[e], wg_s, sem_w),
                pltpu.make_async_copy(wu_hbm.at[e], wu_s, sem_w),
                pltpu.make_async_copy(wd_hbm.at[e], wd_s, sem_w))

    def load_expert(b):
        e = meta_ref[b]
        prev = meta_ref[jnp.maximum(b - 1, 0)]
        first = jnp.logical_and(b < n_used, jnp.logical_or(b == 0, e != prev))

        @pl.when(first)
        def _():
            for c in weight_copies(e):
                c.wait()
            wg_b[...] = wg_s[...].astype(BF16)
            wu_b[...] = wu_s[...].astype(BF16)
            wd_b[...] = wd_s[...].astype(BF16)
            nxt = lax.while_loop(
                lambda j: jnp.logical_and(j < n_used, meta_ref[jnp.minimum(j, N_BLOCKS - 1)] == e),
                lambda j: j + 1, b + 1)

            @pl.when(nxt < n_used)
            def _():
                for c in weight_copies(meta_ref[nxt]):
                    c.start(priority=1)

    def compute(v):
        lo, hi = _unpack_pairs_f32(_load_row_tiles(xbufs[v], MOE_BLOCK))
        lo, hi = lo.astype(BF16), hi.astype(BF16)
        g = (jnp.dot(lo, wg_b[0:half, :], preferred_element_type=F32)
             + jnp.dot(hi, wg_b[half:, :], preferred_element_type=F32))
        u = (jnp.dot(lo, wu_b[0:half, :], preferred_element_type=F32)
             + jnp.dot(hi, wu_b[half:, :], preferred_element_type=F32))
        a = g * (1.0 / (1.0 + jnp.exp(-g))) * u
        y = jnp.dot(a.astype(BF16), wd_b[...], preferred_element_type=F32)
        _store_row_tiles(ybufs[v], _pack_bf16_pairs(y))

    def block(b, v, scatter_prev, wait_y):
        wait_gather(v)
        if wait_y:
            wait_scatter(v)
        load_expert(b)
        issue_gather(b + 2, (v + 2) % GROUP)
        if scatter_prev:
            issue_scatter(b - 1, (v + 2) % GROUP)
        compute(v)

    for c in weight_copies(meta_ref[0]):
        c.start()
    yb2[...] = jnp.zeros_like(yb2)
    dump_fills = [
        pltpu.make_async_copy(
            yb2, y_ref.at[pl.ds((N_ROUTED + k * MOE_BLOCK) * ROW_TILE, BLOCK_TILE_ROWS), :],
            sem_y.at[GROUP - 1])
        for k in range(DUMP_ROWS // MOE_BLOCK)]
    for c in dump_fills:
        c.start()
    for c in dump_fills:
        c.wait()
    issue_gather(0, 0)
    issue_gather(1, 1)
    for v in range(GROUP):
        block(v, v, scatter_prev=v > 0, wait_y=False)

    def group(gi, carry):
        for v in range(GROUP):
            block(gi * GROUP + v, v, scatter_prev=True, wait_y=True)
        return carry

    lax.fori_loop(1, n_groups, group, 0)

    last = n_groups * GROUP
    issue_scatter(last - 1, GROUP - 1)
    wait_gather(0)
    wait_gather(1)
    for v in range(GROUP):
        wait_scatter(v)


def _moe(meta, inv, h_tiles, w_gate, w_up, w_down):
    any_spec = pl.BlockSpec(memory_space=pl.ANY)
    xy = pltpu.VMEM((BLOCK_TILE_ROWS, LANES), I32)
    return pl.pallas_call(
        _moe_kernel,
        grid_spec=pltpu.PrefetchScalarGridSpec(
            num_scalar_prefetch=2,
            grid=(1,),
            in_specs=[any_spec, any_spec, any_spec, any_spec],
            out_specs=any_spec,
            scratch_shapes=[
                xy, xy, xy, xy, xy, xy,
                pltpu.VMEM((D_MODEL, D_EXPERT), F32),
                pltpu.VMEM((D_MODEL, D_EXPERT), F32),
                pltpu.VMEM((D_EXPERT, D_MODEL), F32),
                pltpu.VMEM((D_MODEL, D_EXPERT), BF16),
                pltpu.VMEM((D_MODEL, D_EXPERT), BF16),
                pltpu.VMEM((D_EXPERT, D_MODEL), BF16),
                pltpu.SemaphoreType.DMA((GROUP,)),
                pltpu.SemaphoreType.DMA((GROUP,)),
                pltpu.SemaphoreType.DMA(()),
            ],
        ),
        out_shape=jax.ShapeDtypeStruct((Y_ROWS, LANES), I32),
        compiler_params=_params(("arbitrary",)),
        name="moe",
    )(meta, inv, h_tiles, w_gate, w_up, w_down)


COMB_TM = 512


def _combine_kernel(x1_ref, rw_ref, gf_ref, y_ref, o_ref):
    tm = COMB_TM
    w = rw_ref[...]
    w0, w1 = w[:, 0:1], w[:, 1:2]
    lo0, hi0 = _unpack_pairs_f32(_load_row_tiles(y_ref, tm, 0, 2 * ROW_TILE))
    lo1, hi1 = _unpack_pairs_f32(_load_row_tiles(y_ref, tm, ROW_TILE, 2 * ROW_TILE))
    moe = jnp.concatenate([lo0 * w0 + lo1 * w1, hi0 * w0 + hi1 * w1], axis=1)
    o_ref[...] = _rms(x1_ref[...] + moe, gf_ref[...])


def _combine(x1, rw, gf, y_tiles):
    tm = COMB_TM
    row = lambda i: (i, 0)
    return pl.pallas_call(
        _combine_kernel,
        grid=(T // tm,),
        in_specs=[
            pl.BlockSpec((tm, D_MODEL), row),
            pl.BlockSpec((tm, LANES), row),
            pl.BlockSpec((1, D_MODEL), lambda i: (0, 0)),
            pl.BlockSpec((tm * 2 * ROW_TILE, LANES), row),
        ],
        out_specs=pl.BlockSpec((tm, D_MODEL), row),
        out_shape=jax.ShapeDtypeStruct((T, D_MODEL), F32),
        compiler_params=_params(("arbitrary",)),
        name="combine",
    )(x1, rw, gf, y_tiles)


def _layer(x2, norm1, w_in, rel_bias, conv_w, g_out_attn, g_out_conv, w_out, norm2,
           w_rg, b_rg, w_re, b_re, w_gate, w_up, w_down):
    proj = _inproj(x2, norm1[None, :], w_in.astype(BF16))
    a_n = _attention(proj, _bias_table(rel_bias), g_out_attn[None, :])
    conv_w8 = jnp.zeros((8, D_CONV), F32).at[0:3].set(conv_w)

    w_r = jnp.zeros((D_MODEL, LANES), F32)
    w_r = w_r.at[:, 0:N_GROUPS].set(w_rg).at[:, N_GROUPS:N_GROUPS + N_EXPERTS].set(w_re)
    b_r = jnp.zeros((1, LANES), F32)
    b_r = b_r.at[0, 0:N_GROUPS].set(b_rg).at[0, N_GROUPS:N_GROUPS + N_EXPERTS].set(b_re)
    x1, hp, ri, rw, cnt = _outproj(a_n, proj, conv_w8, g_out_conv[None, :], x2,
                                   w_out.astype(BF16), norm2[None, :], w_r.astype(BF16), b_r)

    dest, be, seg = _dest(ri, cnt)
    dest_flat = dest[:, 0:2].reshape(N_ROUTED)
    inv = _invperm(dest_flat, seg[0, 0:N_EXPERTS], seg[1, 0:N_EXPERTS])
    y_tiles = _moe(be[0], inv, hp, w_gate, w_up, w_down)
    return x1, rw, y_tiles


def kernel(x, norm1, w_in, rel_bias, conv_w, g_out_attn, g_out_conv, w_out, norm2,
           w_router_group, b_router_group, w_router_expert, b_router_expert,
           w_gate, w_up, w_down, norm_final):
    assert x.shape == (BATCH, SEQ, D_MODEL) and norm1.shape[0] == 1
    x2 = x.reshape(T, D_MODEL)
    x1, rw, y_tiles = _layer(
        x2, norm1[0], w_in[0], rel_bias[0], conv_w[0], g_out_attn[0], g_out_conv[0],
        w_out[0], norm2[0], w_router_group[0], b_router_group[0], w_router_expert[0],
        b_router_expert[0], w_gate[0], w_up[0], w_down[0])
    out = _combine(x1, rw, norm_final[None, :], y_tiles)
    return out.reshape(BATCH, SEQ, D_MODEL)
```

```python
import functools

import numpy as np
import jax
import jax.numpy as jnp
from jax import lax
from jax.experimental import pallas as pl
from jax.experimental.pallas import tpu as pltpu

F32 = jnp.float32
BF16 = jnp.bfloat16
I32 = jnp.int32

D_MODEL = 2048
BATCH = 4
SEQ = 4096
T = BATCH * SEQ
CHUNK = 64
LEFT_CHUNKS = 8
BAND = LEFT_CHUNKS + 1
KEYS = BAND * CHUNK
D_ATTN = 1024
D_CONV = 1024
HEAD_DIM = 64
N_HEADS = 16
N_PAIRS = N_HEADS // 2
REL_CLIP = 256
D_IN_PROJ = 6 * 1024
N_GROUPS = 4
EPG = 8
N_EXPERTS = 32
D_EXPERT = 512
MOE_BLOCK = 256
MOE_BLOCK_SHIFT = 8
N_ROUTED = 2 * T
N_BLOCKS = N_ROUTED // MOE_BLOCK + N_EXPERTS
EPS = 1e-6
NEG_INF = -1e30
LANES = 128
VMEM_LIMIT = 52 * 1024 * 1024

ATTN_TILE = 512
ATTN_TILES_PER_SEQ = SEQ // ATTN_TILE


def _params(sem):
    return pltpu.CompilerParams(dimension_semantics=sem, vmem_limit_bytes=VMEM_LIMIT)


def _rms(x, g):
    ms = jnp.mean(x * x, axis=-1, keepdims=True)
    return x * lax.rsqrt(ms + EPS) * g


INPROJ_TM = 1024
INPROJ_TN = 1536


def _inproj_kernel(x_ref, g_ref, w_ref, o_ref, hn_ref):
    @pl.when(pl.program_id(1) == 0)
    def _():
        hn_ref[...] = _rms(x_ref[...], g_ref[...]).astype(BF16)

    o_ref[...] = jnp.dot(hn_ref[...], w_ref[...], preferred_element_type=F32).astype(BF16)


def _inproj(x2, g, w_bf):
    tm, tn = INPROJ_TM, INPROJ_TN
    return pl.pallas_call(
        _inproj_kernel,
        grid=(T // tm, D_IN_PROJ // tn),
        in_specs=[
            pl.BlockSpec((tm, D_MODEL), lambda i, j: (i, 0)),
            pl.BlockSpec((1, D_MODEL), lambda i, j: (0, 0)),
            pl.BlockSpec((D_MODEL, tn), lambda i, j: (0, j)),
        ],
        out_specs=pl.BlockSpec((tm, tn), lambda i, j: (i, j)),
        out_shape=jax.ShapeDtypeStruct((T, D_IN_PROJ), BF16),
        scratch_shapes=[pltpu.VMEM((tm, D_MODEL), BF16)],
        compiler_params=_params(("arbitrary", "arbitrary")),
        name="inproj",
    )(x2, g, w_bf)


N_QUADS = N_HEADS // 4
QUAD = 4 * HEAD_DIM
BIAS_W = 640
BIAS_WIN = 512
P_BUFS = 4


def _bias_kernel(b_ref, o_ref):
    first_head = lax.broadcasted_iota(I32, (KEYS, LANES), 1) < HEAD_DIM
    for hq in range(N_QUADS):
        for col in range(2):
            h0 = 4 * hq + 2 * col
            pieces = []
            for hh in range(2):
                x = jnp.broadcast_to(b_ref[h0 + hh:h0 + hh + 1, :], (KEYS, BIAS_W))
                rolled = pltpu.roll(x, 0, 1, stride=1, stride_axis=0)
                pieces.append(rolled[:, BIAS_WIN:BIAS_WIN + LANES])
            o_ref[hq, :, col * LANES:(col + 1) * LANES] = jnp.where(
                first_head, pieces[0], pieces[1])


def _bias_table(rel_bias):
    r = rel_bias.astype(F32)
    edge = jnp.broadcast_to(r[:, 2 * REL_CLIP:], (N_HEADS, KEYS - REL_CLIP + 1))
    b = jnp.concatenate([r[:, REL_CLIP:2 * REL_CLIP], edge,
                         r[:, REL_CLIP - (BIAS_W - KEYS - 1):REL_CLIP]], axis=1)
    odd = jnp.concatenate([b[:, BIAS_W - HEAD_DIM:], b[:, :BIAS_W - HEAD_DIM]], axis=1)
    is_odd = (jnp.arange(N_HEADS) % 2 == 1)[:, None]
    bvec = jnp.where(is_odd, odd, b)
    return pl.pallas_call(
        _bias_kernel,
        out_shape=jax.ShapeDtypeStruct((N_QUADS, KEYS, QUAD), F32),
        compiler_params=pltpu.CompilerParams(vmem_limit_bytes=VMEM_LIMIT),
        name="biastab",
    )(bvec)


def _attn_kernel(q_ref, kp_ref, kc_ref, vp_ref, vc_ref, bias_ref, g_ref, o_ref,
                 kw_ref, vw_ref, st_ref, p_ref, a_ref):
    i = pl.program_id(1)
    kw_ref[0:ATTN_TILE, :] = kp_ref[...]
    kw_ref[ATTN_TILE:2 * ATTN_TILE, :] = kc_ref[...]
    vw_ref[0:ATTN_TILE, :] = vp_ref[...]
    vw_ref[ATTN_TILE:2 * ATTN_TILE, :] = vc_ref[...]

    head_of_lane = lax.broadcasted_iota(I32, (CHUNK, QUAD), 1) // HEAD_DIM
    first_head = lax.broadcasted_iota(I32, (CHUNK, LANES), 1) < HEAD_DIM
    key_row = lax.broadcasted_iota(I32, (KEYS, QUAD), 0)
    n_rb = KEYS // CHUNK

    def scores(r0, buf, masked):
        for hq in range(N_QUADS):
            c0 = hq * QUAD
            q4 = q_ref[pl.ds(r0, CHUNK), c0:c0 + QUAD] * (HEAD_DIM ** -0.5)
            zero = jnp.zeros_like(q4)
            qbd = jnp.concatenate(
                [jnp.where(head_of_lane == h, q4, zero) for h in range(4)], axis=0)
            k4 = kw_ref[pl.ds(r0, KEYS), c0:c0 + QUAD]
            st = lax.dot_general(k4, qbd, (((1,), (1,)), ((), ())),
                                 preferred_element_type=F32)
            st = st + bias_ref[hq]
            if masked:
                st = jnp.where(key_row + r0 >= ATTN_TILE, st, NEG_INF)
            st_ref[buf, hq] = st

    def exps(buf, pbuf, off):
        for hq in range(N_QUADS):
            m = st_ref[buf, hq, 0:CHUNK, :]
            for rb in range(1, n_rb):
                m = jnp.maximum(m, st_ref[buf, hq, rb * CHUNK:(rb + 1) * CHUNK, :])
            m = jnp.max(m, axis=0, keepdims=True)
            for rb in range(n_rb):
                rows = slice(rb * CHUNK, (rb + 1) * CHUNK)
                prow = slice(off + rb * CHUNK, off + (rb + 1) * CHUNK)
                p_ref[pbuf, hq, prow, :] = jnp.exp(st_ref[buf, hq, rows, :] - m).astype(BF16)

    def values(r0, pbufs):
        nq = len(pbufs)
        span = KEYS + (nq - 1) * CHUNK
        ones = jnp.ones((span, LANES), BF16)
        for hp in range(N_PAIRS):
            c0 = hp * LANES
            half = (hp % 2) * LANES
            pt = jnp.concatenate(
                [p_ref[b, hp // 2, 0:span, half:half + LANES] for b in pbufs], axis=1)
            vext = jnp.concatenate([vw_ref[pl.ds(r0, span), c0:c0 + LANES], ones], axis=1)
            oe = lax.dot_general(pt, vext, (((0,), (0,)), ((), ())),
                                 preferred_element_type=F32)
            o = oe[:, :LANES] * (1.0 / oe[:, LANES:])
            for j in range(nq):
                oj = o[2 * j * CHUNK:2 * (j + 1) * CHUNK]
                a_ref[pl.ds(r0 + j * CHUNK, CHUNK), c0:c0 + LANES] = jnp.where(
                    first_head, oj[0:CHUNK], oj[CHUNK:2 * CHUNK])

    n_chunks = ATTN_TILE // CHUNK

    @pl.when(i == 0)
    def _():
        def chunk_body(ci, carry):
            r0 = pl.multiple_of(ci * CHUNK, CHUNK)
            scores(r0, 0, True)
            exps(0, 0, 0)
            values(r0, (0,))
            return carry

        lax.fori_loop(0, n_chunks, chunk_body, 0)

    @pl.when(i > 0)
    def _():
        for b in range(P_BUFS):
            z = slice(KEYS, KEYS + CHUNK) if b % 2 == 0 else slice(0, CHUNK)
            p_ref[b, :, z, :] = jnp.zeros((N_QUADS, CHUNK, QUAD), BF16)
        scores(0, 0, False)
        for ci in range(n_chunks):
            if ci + 1 < n_chunks:
                scores((ci + 1) * CHUNK, (ci + 1) % 2, False)
            if ci % 2 == 0 and ci >= 2:
                values((ci - 2) * CHUNK, ((ci - 2) % P_BUFS, (ci - 1) % P_BUFS))
            exps(ci % 2, ci % P_BUFS, (ci % 2) * CHUNK)
        values((n_chunks - 2) * CHUNK, ((n_chunks - 2) % P_BUFS, (n_chunks - 1) % P_BUFS))

    o_ref[...] = _rms(a_ref[...], g_ref[...]).astype(BF16)


def _attention(proj, bias_t, g):
    n = ATTN_TILES_PER_SEQ
    cur = lambda col: (lambda b, i: (b * n + i, col))
    prev = lambda col: (lambda b, i: (b * n + jnp.maximum(i - 1, 0), col))
    blk = (ATTN_TILE, D_ATTN)
    return pl.pallas_call(
        _attn_kernel,
        grid=(BATCH, n),
        in_specs=[
            pl.BlockSpec(blk, cur(0)),
            pl.BlockSpec(blk, prev(1)),
            pl.BlockSpec(blk, cur(1)),
            pl.BlockSpec(blk, prev(2)),
            pl.BlockSpec(blk, cur(2)),
            pl.BlockSpec((N_QUADS, KEYS, QUAD), lambda b, i: (0, 0, 0)),
            pl.BlockSpec((1, D_ATTN), lambda b, i: (0, 0)),
        ],
        out_specs=pl.BlockSpec(blk, lambda b, i: (b * n + i, 0)),
        out_shape=jax.ShapeDtypeStruct((T, D_ATTN), BF16),
        scratch_shapes=[
            pltpu.VMEM((2 * ATTN_TILE, D_ATTN), BF16),
            pltpu.VMEM((2 * ATTN_TILE, D_ATTN), BF16),
            pltpu.VMEM((2, N_QUADS, KEYS, QUAD), F32),
            pltpu.VMEM((P_BUFS, N_QUADS, KEYS + CHUNK, QUAD), BF16),
            pltpu.VMEM((ATTN_TILE, D_ATTN), F32),
        ],
        compiler_params=_params(("arbitrary", "arbitrary")),
        name="attn",
    )(proj, proj, proj, proj, proj, bias_t, g)


CONV_PREV = 16


def _short_conv_rows(seq_start, b, c, u, cp, up, w, g):
    z = c.astype(F32) * u.astype(F32)
    zp = cp.astype(F32) * up.astype(F32)
    zp = jnp.where(seq_start, 0.0, zp)
    zm1 = zp[CONV_PREV - 1:CONV_PREV]
    zm2 = zp[CONV_PREV - 2:CONV_PREV - 1]
    row = lax.broadcasted_iota(I32, z.shape, 0)
    z1 = jnp.where(row == 0, zm1, pltpu.roll(z, 1, axis=0))
    z2 = jnp.where(row == 0, zm2, jnp.where(row == 1, zm1, pltpu.roll(z, 2, axis=0)))
    y = w[0:1] * z2 + w[1:2] * z1 + w[2:3] * z
    return _rms(b.astype(F32) * y, g).astype(BF16)


OUT_TM = 512
OUT_SUB = 2


def _pack_bf16_pairs(h):
    half = D_MODEL // 2
    return pltpu.pack_elementwise([h[:, :half], h[:, half:]], packed_dtype=BF16)


def _unpack_pairs_f32(u):
    lo = pltpu.unpack_elementwise(u, index=0, packed_dtype=BF16, unpacked_dtype=F32)
    hi = pltpu.unpack_elementwise(u, index=1, packed_dtype=BF16, unpacked_dtype=F32)
    return lo, hi


ROW_TILE = 8


def _store_row_tiles(ref, packed):
    m = packed.shape[0]
    for s in range(ROW_TILE):
        ref[pl.ds(s, m, stride=ROW_TILE), :] = packed[:, s * LANES:(s + 1) * LANES]


def _load_row_tiles(ref, m, first=0, stride=ROW_TILE):
    return jnp.concatenate(
        [ref[pl.ds(first + s, m, stride=stride), :] for s in range(ROW_TILE)], axis=1)


FIRST_EXPERT_LANE = N_GROUPS


def _route_rows(lg, base_ref):
    tm = lg.shape[0]
    lane = lax.broadcasted_iota(I32, (tm, LANES), 1)
    lane_f = lane.astype(F32)
    big = jnp.float32(LANES)
    ninf = jnp.float32(-jnp.inf)

    gmask = lane < N_GROUPS
    gmax = jnp.max(jnp.where(gmask, lg, ninf), axis=1, keepdims=True)
    gsum = jnp.sum(jnp.where(gmask, jnp.exp(lg - gmax), 0.0), axis=1, keepdims=True)
    p_g = 1.0 / gsum
    g_sel = jnp.min(jnp.where(jnp.logical_and(gmask, lg == gmax), lane_f, big),
                    axis=1, keepdims=True)
    lo = FIRST_EXPERT_LANE + EPG * g_sel
    emask = jnp.logical_and(lane_f >= lo, lane_f < lo + EPG)
    v1 = jnp.max(jnp.where(emask, lg, ninf), axis=1, keepdims=True)
    i1 = jnp.min(jnp.where(jnp.logical_and(emask, lg == v1), lane_f, big),
                 axis=1, keepdims=True)
    emask2 = jnp.logical_and(emask, lane_f != i1)
    v2 = jnp.max(jnp.where(emask2, lg, ninf), axis=1, keepdims=True)
    i2 = jnp.min(jnp.where(jnp.logical_and(emask2, lg == v2), lane_f, big),
                 axis=1, keepdims=True)
    t = jnp.exp(v2 - v1)
    w1 = p_g * (1.0 / (1.0 + t))
    w2 = p_g * (t / (1.0 + t))
    e1 = i1 - FIRST_EXPERT_LANE
    e2 = i2 - FIRST_EXPERT_LANE

    hit1 = lane_f == e1
    hit2 = lane_f == e2
    onehot = jnp.where(jnp.logical_or(hit1, hit2), 1.0, 0.0)
    r = lax.broadcasted_iota(I32, (tm, tm), 0)
    c = lax.broadcasted_iota(I32, (tm, tm), 1)
    tri = jnp.where(r > c, 1.0, 0.0).astype(BF16)
    before = jnp.dot(tri, onehot.astype(BF16), preferred_element_type=F32) + base_ref[...]
    r1 = jnp.sum(jnp.where(hit1, before, 0.0), axis=1, keepdims=True)
    r2 = jnp.sum(jnp.where(hit2, before, 0.0), axis=1, keepdims=True)
    base_ref[...] = base_ref[...] + jnp.sum(onehot, axis=0, keepdims=True)

    ri = jnp.where(lane == 0, e1, jnp.where(lane == 1, e2, jnp.where(lane == 2, r1, r2)))
    return ri.astype(I32), jnp.where(lane == 0, w1, w2)


def _outproj_kernel(a_ref, b_ref, c_ref, u_ref, cp_ref, up_ref, cw_ref, gc_ref,
                    x_ref, wo_ref, g2_ref, wr_ref, br_ref,
                    x1_ref, hp_ref, ri_ref, rw_ref, cnt_ref, base_ref, cn_ref):
    i = pl.program_id(0)

    @pl.when(i == 0)
    def _():
        base_ref[...] = jnp.zeros_like(base_ref)

    sub = OUT_TM // OUT_SUB
    rows_of = lambda s: slice(s * sub, (s + 1) * sub)

    def conv(s):
        rows = rows_of(s)
        if s == 0:
            seq_start = (i % (SEQ // OUT_TM)) == 0
            cp, up = cp_ref[...], up_ref[...]
        else:
            seq_start = False
            before = slice(s * sub - CONV_PREV, s * sub)
            cp, up = c_ref[before, :], u_ref[before, :]
        cn_ref[rows, :] = _short_conv_rows(
            seq_start, b_ref[rows, :], c_ref[rows, :], u_ref[rows, :],
            cp, up, cw_ref[...], gc_ref[...])

    def project(s):
        rows = rows_of(s)
        acc = jnp.dot(a_ref[rows, :], wo_ref[0:D_ATTN, :], preferred_element_type=F32)
        acc = acc + jnp.dot(cn_ref[rows, :], wo_ref[D_ATTN:, :], preferred_element_type=F32)
        x1_ref[rows, :] = x_ref[rows, :] + acc

    def route(s):
        rows = rows_of(s)
        h2 = _rms(x1_ref[rows, :], g2_ref[...])
        logits = jnp.dot(h2.astype(BF16), wr_ref[...],
                         preferred_element_type=F32) + br_ref[...]
        _store_row_tiles(hp_ref.at[pl.ds(s * sub * ROW_TILE, sub * ROW_TILE), :],
                         _pack_bf16_pairs(h2))
        ri, rw = _route_rows(logits, base_ref)
        ri_ref[rows, :] = ri
        rw_ref[rows, :] = rw

    conv(0)
    for s in range(OUT_SUB):
        project(s)
        if s + 1 < OUT_SUB:
            conv(s + 1)
        if s > 0:
            route(s - 1)
    route(OUT_SUB - 1)
    cnt_ref[...] = base_ref[...]


def _outproj(a_n, proj, conv_w, g_conv, x2, wo_bf, g2, wr_bf, br):
    tm = OUT_TM
    per = tm // CONV_PREV
    row = lambda i: (i, 0)
    fixed = lambda i: (0, 0)
    col = lambda c: (lambda i: (i, c))
    prev = lambda c: (lambda i: (jnp.maximum(i * per - 1, 0), c))
    return pl.pallas_call(
        _outproj_kernel,
        grid=(T // tm,),
        in_specs=[
            pl.BlockSpec((tm, D_ATTN), row),
            pl.BlockSpec((tm, D_CONV), col(3)),
            pl.BlockSpec((tm, D_CONV), col(4)),
            pl.BlockSpec((tm, D_CONV), col(5)),
            pl.BlockSpec((CONV_PREV, D_CONV), prev(4)),
            pl.BlockSpec((CONV_PREV, D_CONV), prev(5)),
            pl.BlockSpec((8, D_CONV), fixed),
            pl.BlockSpec((1, D_CONV), fixed),
            pl.BlockSpec((tm, D_MODEL), row),
            pl.BlockSpec((D_MODEL, D_MODEL), fixed),
            pl.BlockSpec((1, D_MODEL), fixed),
            pl.BlockSpec((D_MODEL, LANES), fixed),
            pl.BlockSpec((1, LANES), fixed),
        ],
        out_specs=[
            pl.BlockSpec((tm, D_MODEL), row),
            pl.BlockSpec((tm * ROW_TILE, LANES), row),
            pl.BlockSpec((tm, LANES), row),
            pl.BlockSpec((tm, LANES), row),
            pl.BlockSpec((1, LANES), fixed),
        ],
        out_shape=[
            jax.ShapeDtypeStruct((T, D_MODEL), F32),
            jax.ShapeDtypeStruct((T * ROW_TILE, LANES), I32),
            jax.ShapeDtypeStruct((T, LANES), I32),
            jax.ShapeDtypeStruct((T, LANES), F32),
            jax.ShapeDtypeStruct((1, LANES), F32),
        ],
        scratch_shapes=[pltpu.VMEM((1, LANES), F32), pltpu.VMEM((tm, D_CONV), BF16)],
        compiler_params=_params(("arbitrary",)),
        name="outproj",
    )(a_n, proj, proj, proj, proj, proj, conv_w, g_conv, x2, wo_bf, g2, wr_bf, br)


BLOCK_LANES = 384
DEST_TM = 2048


def _dest_kernel(ri_ref, cnt_ref, dest_ref, be_ref, seg_ref):
    tm = DEST_TM
    lane8 = lax.broadcasted_iota(I32, (8, LANES), 1)
    cnt = jnp.broadcast_to(cnt_ref[...], (8, LANES)).astype(I32)
    pcnt = ((cnt + (MOE_BLOCK - 1)) >> MOE_BLOCK_SHIFT) << MOE_BLOCK_SHIFT
    pend = pcnt
    for s in (1, 2, 4, 8, 16, 32, 64):
        pend = pend + jnp.where(lane8 >= s, pltpu.roll(pend, s, axis=1), 0)
    pstart = (pend - pcnt)[0:1].astype(F32)

    ri = ri_ref[...].astype(F32)
    lane = lax.broadcasted_iota(I32, (tm, LANES), 1)
    lane_f = lane.astype(F32)
    e1, e2, r1, r2 = ri[:, 0:1], ri[:, 1:2], ri[:, 2:3], ri[:, 3:4]
    d1 = jnp.sum(jnp.where(lane_f == e1, pstart, 0.0), axis=1, keepdims=True) + r1
    d2 = jnp.sum(jnp.where(lane_f == e2, pstart, 0.0), axis=1, keepdims=True) + r2
    dest_ref[...] = jnp.where(lane == 0, d1, d2).astype(I32)

    rr = lax.broadcasted_iota(I32, (LANES, LANES), 0)
    cc = lax.broadcasted_iota(I32, (LANES, LANES), 1)
    pend_f = jnp.broadcast_to(pend[0:1].astype(F32), (LANES, LANES))
    pend_col = jnp.sum(jnp.where(rr == cc, pend_f, 0.0), axis=1, keepdims=True)
    blk0 = (lax.broadcasted_iota(I32, (LANES, BLOCK_LANES), 1) * MOE_BLOCK).astype(F32)
    erow = lax.broadcasted_iota(I32, (LANES, BLOCK_LANES), 0)
    le = jnp.where(jnp.logical_and(erow < N_EXPERTS, pend_col <= blk0), 1.0, 0.0)
    be = jnp.minimum(jnp.sum(le, axis=0, keepdims=True), float(N_EXPERTS - 1))
    total = jnp.sum(jnp.where(lane8[0:1] == N_EXPERTS - 1, pend[0:1].astype(F32), 0.0),
                    axis=1, keepdims=True)
    blk_lane = lax.broadcasted_iota(I32, (1, BLOCK_LANES), 1)
    be = jnp.where(blk_lane == N_BLOCKS, total * (1.0 / MOE_BLOCK), be)
    be_ref[...] = jnp.broadcast_to(be, (8, BLOCK_LANES)).astype(I32)
    row8 = lax.broadcasted_iota(I32, (8, LANES), 0)
    seg_ref[...] = jnp.where(row8 == 0, pend - pcnt + cnt, pend)


def _dest(ri, cnt):
    tm = DEST_TM
    row = lambda i: (i, 0)
    fixed = lambda i: (0, 0)
    return pl.pallas_call(
        _dest_kernel,
        grid=(T // tm,),
        in_specs=[pl.BlockSpec((tm, LANES), row), pl.BlockSpec((1, LANES), fixed)],
        out_specs=[pl.BlockSpec((tm, LANES), row),
                   pl.BlockSpec((8, BLOCK_LANES), fixed),
                   pl.BlockSpec((8, LANES), fixed)],
        out_shape=[jax.ShapeDtypeStruct((T, LANES), I32),
                   jax.ShapeDtypeStruct((8, BLOCK_LANES), I32),
                   jax.ShapeDtypeStruct((8, LANES), I32)],
        compiler_params=_params(("arbitrary",)),
        name="dest",
    )(ri, cnt)


GROUP = 3
INV_LEN = (-(-N_BLOCKS // GROUP) * GROUP + GROUP - 1) * MOE_BLOCK
DUMP_ROWS = 4 * MOE_BLOCK


def _invert_permutation(dest_ref, vend_ref, pend_ref, inv_ref):
    def pad(p, carry):
        inv_ref[p] = N_ROUTED + (p & (DUMP_ROWS - 1))
        return carry

    def pad_expert(e, carry):
        lax.fori_loop(vend_ref[e], pend_ref[e], pad, 0)
        return carry

    lax.fori_loop(0, N_EXPERTS, pad_expert, 0)
    lax.fori_loop(pend_ref[N_EXPERTS - 1], INV_LEN, pad, 0)

    def place(s, carry):
        inv_ref[dest_ref[s]] = s
        return carry

    lax.fori_loop(0, N_ROUTED, place, 0, unroll=32)


BLOCK_TILE_ROWS = MOE_BLOCK * ROW_TILE
Y_ROWS = (N_ROUTED + DUMP_ROWS) * ROW_TILE


def _moe_kernel(meta_ref, dest_ref, vend_ref, pend_ref, h_ref, wg_hbm, wu_hbm, wd_hbm, y_ref,
                xb0, xb1, xb2, yb0, yb1, yb2, wg_s, wu_s, wd_s, wg_b, wu_b, wd_b, inv_ref,
                sem_x, sem_y, sem_w):
    xbufs = (xb0, xb1, xb2)
    ybufs = (yb0, yb1, yb2)
    n_used = meta_ref[N_BLOCKS]
    n_groups = (n_used + (GROUP - 1)) // GROUP
    half = D_MODEL // 2

    def issue_gather(b, v):
        for r in range(MOE_BLOCK):
            tok = (inv_ref[b * MOE_BLOCK + r] >> 1) & (T - 1)
            src = h_ref.at[pl.ds(pl.multiple_of(tok * ROW_TILE, ROW_TILE), ROW_TILE), :]
            pltpu.make_async_copy(src, xbufs[v].at[pl.ds(r * ROW_TILE, ROW_TILE), :],
                                  sem_x.at[v]).start()

    def wait_gather(v):
        pltpu.make_async_copy(h_ref.at[pl.ds(0, BLOCK_TILE_ROWS), :], xbufs[v],
                              sem_x.at[v]).wait()

    def issue_scatter(b, v):
        for r in range(MOE_BLOCK):
            slot = inv_ref[b * MOE_BLOCK + r]
            dst = y_ref.at[pl.ds(pl.multiple_of(slot * ROW_TILE, ROW_TILE), ROW_TILE), :]
            pltpu.make_async_copy(ybufs[v].at[pl.ds(r * ROW_TILE, ROW_TILE), :], dst,
                                  sem_y.at[v]).start()

    def wait_scatter(v):
        pltpu.make_async_copy(ybufs[v], y_ref.at[pl.ds(0, BLOCK_TILE_ROWS), :],
                              sem_y.at[v]).wait()

    def weight_copies(e):
        return (pltpu.make_async_copy(wg_hbm.at[e], wg_s, sem_w),
                pltpu.make_async_copy(wu_hbm.at[e], wu_s, sem_w),
                pltpu.make_async_copy(wd_hbm.at[e], wd_s, sem_w))

    def load_expert(b):
        e = meta_ref[b]
        prev = meta_ref[jnp.maximum(b - 1, 0)]
        first = jnp.logical_and(b < n_used, jnp.logical_or(b == 0, e != prev))

        @pl.when(first)
        def _():
            for c in weight_copies(e):
                c.wait()
            wg_b[...] = wg_s[...].astype(BF16)
            wu_b[...] = wu_s[...].astype(BF16)
            wd_b[...] = wd_s[...].astype(BF16)
            nxt = lax.while_loop(
                lambda j: jnp.logical_and(j < n_used, meta_ref[jnp.minimum(j, N_BLOCKS - 1)] == e),
                lambda j: j + 1, b + 1)

            @pl.when(nxt < n_used)
            def _():
                for c in weight_copies(meta_ref[nxt]):
                    c.start(priority=1)

    def compute(v):
        lo, hi = _unpack_pairs_f32(_load_row_tiles(xbufs[v], MOE_BLOCK))
        lo, hi = lo.astype(BF16), hi.astype(BF16)
        g = (jnp.dot(lo, wg_b[0:half, :], preferred_element_type=F32)
             + jnp.dot(hi, wg_b[half:, :], preferred_element_type=F32))
        u = (jnp.dot(lo, wu_b[0:half, :], preferred_element_type=F32)
             + jnp.dot(hi, wu_b[half:, :], preferred_element_type=F32))
        a = g * (1.0 / (1.0 + jnp.exp(-g))) * u
        y = jnp.dot(a.astype(BF16), wd_b[...], preferred_element_type=F32)
        _store_row_tiles(ybufs[v], _pack_bf16_pairs(y))

    def block(b, v, scatter_prev, wait_y):
        wait_gather(v)
        if wait_y:
            wait_scatter(v)
        load_expert(b)
        issue_gather(b + 2, (v + 2) % GROUP)
        if scatter_prev:
            issue_scatter(b - 1, (v + 2) % GROUP)
        compute(v)

    for c in weight_copies(meta_ref[0]):
        c.start()
    yb2[...] = jnp.zeros_like(yb2)
    dump_fills = [
        pltpu.make_async_copy(
            yb2, y_ref.at[pl.ds((N_ROUTED + k * MOE_BLOCK) * ROW_TILE, BLOCK_TILE_ROWS), :],
            sem_y.at[GROUP - 1])
        for k in range(DUMP_ROWS // MOE_BLOCK)]
    for c in dump_fills:
        c.start()
    _invert_permutation(dest_ref, vend_ref, pend_ref, inv_ref)
    for c in dump_fills:
        c.wait()
    issue_gather(0, 0)
    issue_gather(1, 1)
    for v in range(GROUP):
        block(v, v, scatter_prev=v > 0, wait_y=False)

    def group(gi, carry):
        for v in range(GROUP):
            block(gi * GROUP + v, v, scatter_prev=True, wait_y=True)
        return carry

    lax.fori_loop(1, n_groups, group, 0)

    last = n_groups * GROUP
    issue_scatter(last - 1, GROUP - 1)
    wait_gather(0)
    wait_gather(1)
    for v in range(GROUP):
        wait_scatter(v)


def _moe(meta, dest_flat, vend, pend, h_tiles, w_gate, w_up, w_down):
    any_spec = pl.BlockSpec(memory_space=pl.ANY)
    xy = pltpu.VMEM((BLOCK_TILE_ROWS, LANES), I32)
    return pl.pallas_call(
        _moe_kernel,
        grid_spec=pltpu.PrefetchScalarGridSpec(
            num_scalar_prefetch=4,
            grid=(1,),
            in_specs=[any_spec, any_spec, any_spec, any_spec],
            out_specs=any_spec,
            scratch_shapes=[
                xy, xy, xy, xy, xy, xy,
                pltpu.VMEM((D_MODEL, D_EXPERT), F32),
                pltpu.VMEM((D_MODEL, D_EXPERT), F32),
                pltpu.VMEM((D_EXPERT, D_MODEL), F32),
                pltpu.VMEM((D_MODEL, D_EXPERT), BF16),
                pltpu.VMEM((D_MODEL, D_EXPERT), BF16),
                pltpu.VMEM((D_EXPERT, D_MODEL), BF16),
                pltpu.SMEM((INV_LEN,), I32),
                pltpu.SemaphoreType.DMA((GROUP,)),
                pltpu.SemaphoreType.DMA((GROUP,)),
                pltpu.SemaphoreType.DMA(()),
            ],
        ),
        out_shape=jax.ShapeDtypeStruct((Y_ROWS, LANES), I32),
        compiler_params=_params(("arbitrary",)),
        name="moe",
    )(meta, dest_flat, vend, pend, h_tiles, w_gate, w_up, w_down)


COMB_TM = 512


def _combine_kernel(x1_ref, rw_ref, gf_ref, y_ref, o_ref):
    tm = COMB_TM
    w = rw_ref[...]
    w0, w1 = w[:, 0:1], w[:, 1:2]
    lo0, hi0 = _unpack_pairs_f32(_load_row_tiles(y_ref, tm, 0, 2 * ROW_TILE))
    lo1, hi1 = _unpack_pairs_f32(_load_row_tiles(y_ref, tm, ROW_TILE, 2 * ROW_TILE))
    moe = jnp.concatenate([lo0 * w0 + lo1 * w1, hi0 * w0 + hi1 * w1], axis=1)
    o_ref[...] = _rms(x1_ref[...] + moe, gf_ref[...])


def _combine(x1, rw, gf, y_tiles):
    tm = COMB_TM
    row = lambda i: (i, 0)
    return pl.pallas_call(
        _combine_kernel,
        grid=(T // tm,),
        in_specs=[
            pl.BlockSpec((tm, D_MODEL), row),
            pl.BlockSpec((tm, LANES), row),
            pl.BlockSpec((1, D_MODEL), lambda i: (0, 0)),
            pl.BlockSpec((tm * 2 * ROW_TILE, LANES), row),
        ],
        out_specs=pl.BlockSpec((tm, D_MODEL), row),
        out_shape=jax.ShapeDtypeStruct((T, D_MODEL), F32),
        compiler_params=_params(("arbitrary",)),
        name="combine",
    )(x1, rw, gf, y_tiles)


def _layer(x2, norm1, w_in, rel_bias, conv_w, g_out_attn, g_out_conv, w_out, norm2,
           w_rg, b_rg, w_re, b_re, w_gate, w_up, w_down):
    proj = _inproj(x2, norm1[None, :], w_in.astype(BF16))
    a_n = _attention(proj, _bias_table(rel_bias), g_out_attn[None, :])
    conv_w8 = jnp.zeros((8, D_CONV), F32).at[0:3].set(conv_w)

    w_r = jnp.zeros((D_MODEL, LANES), F32)
    w_r = w_r.at[:, 0:N_GROUPS].set(w_rg).at[:, N_GROUPS:N_GROUPS + N_EXPERTS].set(w_re)
    b_r = jnp.zeros((1, LANES), F32)
    b_r = b_r.at[0, 0:N_GROUPS].set(b_rg).at[0, N_GROUPS:N_GROUPS + N_EXPERTS].set(b_re)
    x1, hp, ri, rw, cnt = _outproj(a_n, proj, conv_w8, g_out_conv[None, :], x2,
                                   w_out.astype(BF16), norm2[None, :], w_r.astype(BF16), b_r)

    dest, be, seg = _dest(ri, cnt)
    dest_flat = dest[:, 0:2].reshape(N_ROUTED)
    y_tiles = _moe(be[0], dest_flat, seg[0, 0:N_EXPERTS], seg[1, 0:N_EXPERTS], hp,
                   w_gate, w_up, w_down)
    return x1, rw, y_tiles


def kernel(x, norm1, w_in, rel_bias, conv_w, g_out_attn, g_out_conv, w_out, norm2,
           w_router_group, b_router_group, w_router_expert, b_router_expert,
           w_gate, w_up, w_down, norm_final):
    assert x.shape == (BATCH, SEQ, D_MODEL) and norm1.shape[0] == 1
    x2 = x.reshape(T, D_MODEL)
    x1, rw, y_tiles = _layer(
        x2, norm1[0], w_in[0], rel_bias[0], conv_w[0], g_out_attn[0], g_out_conv[0],
        w_out[0], norm2[0], w_router_group[0], b_router_group[0], w_router_expert[0],
        b_router_expert[0], w_gate[0], w_up[0], w_down[0])
    out = _combine(x1, rw, norm_final[None, :], y_tiles)
    return out.reshape(BATCH, SEQ, D_MODEL)
```

```python
import functools

import numpy as np
import jax
import jax.numpy as jnp
from jax import lax
from jax.experimental import pallas as pl
from jax.experimental.pallas import tpu as pltpu

F32 = jnp.float32
BF16 = jnp.bfloat16
I32 = jnp.int32

D_MODEL = 2048
BATCH = 4
SEQ = 4096
T = BATCH * SEQ
CHUNK = 64
LEFT_CHUNKS = 8
BAND = LEFT_CHUNKS + 1
KEYS = BAND * CHUNK
D_ATTN = 1024
D_CONV = 1024
HEAD_DIM = 64
N_HEADS = 16
N_PAIRS = N_HEADS // 2
REL_CLIP = 256
D_IN_PROJ = 6 * 1024
N_GROUPS = 4
EPG = 8
N_EXPERTS = 32
D_EXPERT = 512
MOE_BLOCK = 256
MOE_BLOCK_SHIFT = 8
N_ROUTED = 2 * T
N_BLOCKS = N_ROUTED // MOE_BLOCK + N_EXPERTS
EPS = 1e-6
NEG_INF = -1e30
LANES = 128
VMEM_LIMIT = 52 * 1024 * 1024

ATTN_TILE = 512
ATTN_TILES_PER_SEQ = SEQ // ATTN_TILE


def _params(sem):
    return pltpu.CompilerParams(dimension_semantics=sem, vmem_limit_bytes=VMEM_LIMIT)


def _rms(x, g):
    ms = jnp.mean(x * x, axis=-1, keepdims=True)
    return x * lax.rsqrt(ms + EPS) * g


INPROJ_TM = 1024
INPROJ_TN = 1536


def _inproj_kernel(x_ref, g_ref, w_ref, o_ref, hn_ref):
    @pl.when(pl.program_id(1) == 0)
    def _():
        hn_ref[...] = _rms(x_ref[...], g_ref[...]).astype(BF16)

    o_ref[...] = jnp.dot(hn_ref[...], w_ref[...], preferred_element_type=F32).astype(BF16)


def _inproj(x2, g, w_bf):
    tm, tn = INPROJ_TM, INPROJ_TN
    return pl.pallas_call(
        _inproj_kernel,
        grid=(T // tm, D_IN_PROJ // tn),
        in_specs=[
            pl.BlockSpec((tm, D_MODEL), lambda i, j: (i, 0)),
            pl.BlockSpec((1, D_MODEL), lambda i, j: (0, 0)),
            pl.BlockSpec((D_MODEL, tn), lambda i, j: (0, j)),
        ],
        out_specs=pl.BlockSpec((tm, tn), lambda i, j: (i, j)),
        out_shape=jax.ShapeDtypeStruct((T, D_IN_PROJ), BF16),
        scratch_shapes=[pltpu.VMEM((tm, D_MODEL), BF16)],
        compiler_params=_params(("arbitrary", "arbitrary")),
        name="inproj",
    )(x2, g, w_bf)


N_QUADS = N_HEADS // 4
QUAD = 4 * HEAD_DIM
BIAS_W = 640
BIAS_WIN = 512
P_BUFS = 4


def _bias_kernel(b_ref, o_ref):
    first_head = lax.broadcasted_iota(I32, (KEYS, LANES), 1) < HEAD_DIM
    for hq in range(N_QUADS):
        for col in range(2):
            h0 = 4 * hq + 2 * col
            pieces = []
            for hh in range(2):
                x = jnp.broadcast_to(b_ref[h0 + hh:h0 + hh + 1, :], (KEYS, BIAS_W))
                rolled = pltpu.roll(x, 0, 1, stride=1, stride_axis=0)
                pieces.append(rolled[:, BIAS_WIN:BIAS_WIN + LANES])
            o_ref[hq, :, col * LANES:(col + 1) * LANES] = jnp.where(
                first_head, pieces[0], pieces[1])


def _bias_table(rel_bias):
    r = rel_bias.astype(F32)
    edge = jnp.broadcast_to(r[:, 2 * REL_CLIP:], (N_HEADS, KEYS - REL_CLIP + 1))
    b = jnp.concatenate([r[:, REL_CLIP:2 * REL_CLIP], edge,
                         r[:, REL_CLIP - (BIAS_W - KEYS - 1):REL_CLIP]], axis=1)
    odd = jnp.concatenate([b[:, BIAS_W - HEAD_DIM:], b[:, :BIAS_W - HEAD_DIM]], axis=1)
    is_odd = (jnp.arange(N_HEADS) % 2 == 1)[:, None]
    bvec = jnp.where(is_odd, odd, b)
    return pl.pallas_call(
        _bias_kernel,
        out_shape=jax.ShapeDtypeStruct((N_QUADS, KEYS, QUAD), F32),
        compiler_params=pltpu.CompilerParams(vmem_limit_bytes=VMEM_LIMIT),
        name="biastab",
    )(bvec)


def _attn_kernel(q_ref, kp_ref, kc_ref, vp_ref, vc_ref, bias_ref, g_ref, o_ref,
                 kw_ref, vw_ref, st_ref, p_ref, a_ref):
    i = pl.program_id(1)
    kw_ref[0:ATTN_TILE, :] = kp_ref[...]
    kw_ref[ATTN_TILE:2 * ATTN_TILE, :] = kc_ref[...]
    vw_ref[0:ATTN_TILE, :] = vp_ref[...]
    vw_ref[ATTN_TILE:2 * ATTN_TILE, :] = vc_ref[...]

    head_of_lane = lax.broadcasted_iota(I32, (CHUNK, QUAD), 1) // HEAD_DIM
    first_head = lax.broadcasted_iota(I32, (CHUNK, LANES), 1) < HEAD_DIM
    key_row = lax.broadcasted_iota(I32, (KEYS, QUAD), 0)
    n_rb = KEYS // CHUNK

    def scores(r0, buf, masked):
        for hq in range(N_QUADS):
            c0 = hq * QUAD
            q4 = q_ref[pl.ds(r0, CHUNK), c0:c0 + QUAD] * (HEAD_DIM ** -0.5)
            zero = jnp.zeros_like(q4)
            qbd = jnp.concatenate(
                [jnp.where(head_of_lane == h, q4, zero) for h in range(4)], axis=0)
            k4 = kw_ref[pl.ds(r0, KEYS), c0:c0 + QUAD]
            st = lax.dot_general(k4, qbd, (((1,), (1,)), ((), ())),
                                 preferred_element_type=F32)
            st = st + bias_ref[hq]
            if masked:
                st = jnp.where(key_row + r0 >= ATTN_TILE, st, NEG_INF)
            st_ref[buf, hq] = st

    def exps(buf, pbuf, off):
        for hq in range(N_QUADS):
            m = st_ref[buf, hq, 0:CHUNK, :]
            for rb in range(1, n_rb):
                m = jnp.maximum(m, st_ref[buf, hq, rb * CHUNK:(rb + 1) * CHUNK, :])
            m = jnp.max(m, axis=0, keepdims=True)
            for rb in range(n_rb):
                rows = slice(rb * CHUNK, (rb + 1) * CHUNK)
                prow = slice(off + rb * CHUNK, off + (rb + 1) * CHUNK)
                p_ref[pbuf, hq, prow, :] = jnp.exp(st_ref[buf, hq, rows, :] - m).astype(BF16)

    def values(r0, pbufs):
        nq = len(pbufs)
        span = KEYS + (nq - 1) * CHUNK
        ones = jnp.ones((span, LANES), BF16)
        for hp in range(N_PAIRS):
            c0 = hp * LANES
            half = (hp % 2) * LANES
            pt = jnp.concatenate(
                [p_ref[b, hp // 2, 0:span, half:half + LANES] for b in pbufs], axis=1)
            vext = jnp.concatenate([vw_ref[pl.ds(r0, span), c0:c0 + LANES], ones], axis=1)
            oe = lax.dot_general(pt, vext, (((0,), (0,)), ((), ())),
                                 preferred_element_type=F32)
            o = oe[:, :LANES] * (1.0 / oe[:, LANES:])
            for j in range(nq):
                oj = o[2 * j * CHUNK:2 * (j + 1) * CHUNK]
                a_ref[pl.ds(r0 + j * CHUNK, CHUNK), c0:c0 + LANES] = jnp.where(
                    first_head, oj[0:CHUNK], oj[CHUNK:2 * CHUNK])

    n_chunks = ATTN_TILE // CHUNK

    @pl.when(i == 0)
    def _():
        def chunk_body(ci, carry):
            r0 = pl.multiple_of(ci * CHUNK, CHUNK)
            scores(r0, 0, True)
            exps(0, 0, 0)
            values(r0, (0,))
            return carry

        lax.fori_loop(0, n_chunks, chunk_body, 0)

    @pl.when(i > 0)
    def _():
        for b in range(P_BUFS):
            z = slice(KEYS, KEYS + CHUNK) if b % 2 == 0 else slice(0, CHUNK)
            p_ref[b, :, z, :] = jnp.zeros((N_QUADS, CHUNK, QUAD), BF16)
        scores(0, 0, False)
        for ci in range(n_chunks):
            if ci + 1 < n_chunks:
                scores((ci + 1) * CHUNK, (ci + 1) % 2, False)
            if ci % 2 == 0 and ci >= 2:
                values((ci - 2) * CHUNK, ((ci - 2) % P_BUFS, (ci - 1) % P_BUFS))
            exps(ci % 2, ci % P_BUFS, (ci % 2) * CHUNK)
        values((n_chunks - 2) * CHUNK, ((n_chunks - 2) % P_BUFS, (n_chunks - 1) % P_BUFS))

    o_ref[...] = _rms(a_ref[...], g_ref[...]).astype(BF16)


def _attention(proj, bias_t, g):
    n = ATTN_TILES_PER_SEQ
    cur = lambda col: (lambda b, i: (b * n + i, col))
    prev = lambda col: (lambda b, i: (b * n + jnp.maximum(i - 1, 0), col))
    blk = (ATTN_TILE, D_ATTN)
    return pl.pallas_call(
        _attn_kernel,
        grid=(BATCH, n),
        in_specs=[
            pl.BlockSpec(blk, cur(0)),
            pl.BlockSpec(blk, prev(1)),
            pl.BlockSpec(blk, cur(1)),
            pl.BlockSpec(blk, prev(2)),
            pl.BlockSpec(blk, cur(2)),
            pl.BlockSpec((N_QUADS, KEYS, QUAD), lambda b, i: (0, 0, 0)),
            pl.BlockSpec((1, D_ATTN), lambda b, i: (0, 0)),
        ],
        out_specs=pl.BlockSpec(blk, lambda b, i: (b * n + i, 0)),
        out_shape=jax.ShapeDtypeStruct((T, D_ATTN), BF16),
        scratch_shapes=[
            pltpu.VMEM((2 * ATTN_TILE, D_ATTN), BF16),
            pltpu.VMEM((2 * ATTN_TILE, D_ATTN), BF16),
            pltpu.VMEM((2, N_QUADS, KEYS, QUAD), F32),
            pltpu.VMEM((P_BUFS, N_QUADS, KEYS + CHUNK, QUAD), BF16),
            pltpu.VMEM((ATTN_TILE, D_ATTN), F32),
        ],
        compiler_params=_params(("arbitrary", "arbitrary")),
        name="attn",
    )(proj, proj, proj, proj, proj, bias_t, g)


CONV_PREV = 16


def _short_conv_rows(seq_start, b, c, u, cp, up, w, g):
    z = c.astype(F32) * u.astype(F32)
    zp = cp.astype(F32) * up.astype(F32)
    zp = jnp.where(seq_start, 0.0, zp)
    zm1 = zp[CONV_PREV - 1:CONV_PREV]
    zm2 = zp[CONV_PREV - 2:CONV_PREV - 1]
    row = lax.broadcasted_iota(I32, z.shape, 0)
    z1 = jnp.where(row == 0, zm1, pltpu.roll(z, 1, axis=0))
    z2 = jnp.where(row == 0, zm2, jnp.where(row == 1, zm1, pltpu.roll(z, 2, axis=0)))
    y = w[0:1] * z2 + w[1:2] * z1 + w[2:3] * z
    return _rms(b.astype(F32) * y, g).astype(BF16)


OUT_TM = 512
OUT_SUB = 2


def _pack_bf16_pairs(h):
    half = D_MODEL // 2
    return pltpu.pack_elementwise([h[:, :half], h[:, half:]], packed_dtype=BF16)


def _unpack_pairs_f32(u):
    lo = pltpu.unpack_elementwise(u, index=0, packed_dtype=BF16, unpacked_dtype=F32)
    hi = pltpu.unpack_elementwise(u, index=1, packed_dtype=BF16, unpacked_dtype=F32)
    return lo, hi


ROW_TILE = 8


def _store_row_tiles(ref, packed):
    m = packed.shape[0]
    for s in range(ROW_TILE):
        ref[pl.ds(s, m, stride=ROW_TILE), :] = packed[:, s * LANES:(s + 1) * LANES]


def _load_row_tiles(ref, m, first=0, stride=ROW_TILE):
    return jnp.concatenate(
        [ref[pl.ds(first + s, m, stride=stride), :] for s in range(ROW_TILE)], axis=1)


GROUP_LOGIT_ROW = N_EXPERTS
SUBLANES = 8


def _route_cols(lg, base_ref):
    tm = lg.shape[0]
    lt = lg.T
    sub = lax.broadcasted_iota(I32, (SUBLANES, tm), 0)
    sub_f = sub.astype(F32)
    big = jnp.float32(SUBLANES)
    ninf = jnp.float32(-jnp.inf)

    gl = lt[GROUP_LOGIT_ROW:GROUP_LOGIT_ROW + SUBLANES, :]
    gvalid = sub < N_GROUPS
    gmax = jnp.max(jnp.where(gvalid, gl, ninf), axis=0, keepdims=True)
    gsum = jnp.sum(jnp.where(gvalid, jnp.exp(gl - gmax), 0.0), axis=0, keepdims=True)
    p_g = 1.0 / gsum
    g_sel = jnp.min(jnp.where(jnp.logical_and(gvalid, gl == gmax), sub_f, big),
                    axis=0, keepdims=True)
    el = lt[0:EPG, :]
    for g in range(1, N_GROUPS):
        el = jnp.where(g_sel == g, lt[g * EPG:(g + 1) * EPG, :], el)
    v1 = jnp.max(el, axis=0, keepdims=True)
    i1 = jnp.min(jnp.where(el == v1, sub_f, big), axis=0, keepdims=True)
    rest = sub_f != i1
    v2 = jnp.max(jnp.where(rest, el, ninf), axis=0, keepdims=True)
    i2 = jnp.min(jnp.where(jnp.logical_and(rest, el == v2), sub_f, big),
                 axis=0, keepdims=True)
    t = jnp.exp(v2 - v1)
    w1 = p_g * (1.0 / (1.0 + t))
    w2 = p_g * (t / (1.0 + t))
    e1 = g_sel * EPG + i1
    e2 = g_sel * EPG + i2

    erow = lax.broadcasted_iota(I32, (N_EXPERTS, tm), 0).astype(F32)
    hit1 = erow == e1
    hit2 = erow == e2
    onehot = jnp.where(jnp.logical_or(hit1, hit2), 1.0, 0.0)
    r = lax.broadcasted_iota(I32, (tm, tm), 0)
    c = lax.broadcasted_iota(I32, (tm, tm), 1)
    tri = jnp.where(r < c, 1.0, 0.0).astype(BF16)
    base = base_ref[...]
    before = (jnp.dot(onehot.astype(BF16), tri, preferred_element_type=F32)
              + jnp.concatenate([base] * (tm // LANES), axis=1))
    r1 = jnp.sum(jnp.where(hit1, before, 0.0), axis=0, keepdims=True)
    r2 = jnp.sum(jnp.where(hit2, before, 0.0), axis=0, keepdims=True)
    base_ref[...] = base + jnp.broadcast_to(
        jnp.sum(onehot, axis=1, keepdims=True), (N_EXPERTS, LANES))

    ri_t = jnp.where(sub == 0, e1, jnp.where(sub == 1, e2, jnp.where(sub == 2, r1, r2)))
    w_t = jnp.where(sub == 0, w1, jnp.where(sub == 1, w2, 0.0))
    w_t = jnp.concatenate([w_t, jnp.zeros((LANES - SUBLANES, tm), F32)], axis=0)
    return ri_t.astype(I32), w_t.T


def _outproj_kernel(a_ref, b_ref, c_ref, u_ref, cp_ref, up_ref, cw_ref, gc_ref,
                    x_ref, wo_ref, g2_ref, wr_ref, br_ref,
                    x1_ref, hp_ref, ri_ref, rw_ref, cnt_ref, base_ref, cn_ref):
    i = pl.program_id(0)

    @pl.when(i == 0)
    def _():
        base_ref[...] = jnp.zeros_like(base_ref)

    sub = OUT_TM // OUT_SUB
    rows_of = lambda s: slice(s * sub, (s + 1) * sub)

    def conv(s):
        rows = rows_of(s)
        if s == 0:
            seq_start = (i % (SEQ // OUT_TM)) == 0
            cp, up = cp_ref[...], up_ref[...]
        else:
            seq_start = False
            before = slice(s * sub - CONV_PREV, s * sub)
            cp, up = c_ref[before, :], u_ref[before, :]
        cn_ref[rows, :] = _short_conv_rows(
            seq_start, b_ref[rows, :], c_ref[rows, :], u_ref[rows, :],
            cp, up, cw_ref[...], gc_ref[...])

    def project(s):
        rows = rows_of(s)
        acc = jnp.dot(a_ref[rows, :], wo_ref[0:D_ATTN, :], preferred_element_type=F32)
        acc = acc + jnp.dot(cn_ref[rows, :], wo_ref[D_ATTN:, :], preferred_element_type=F32)
        x1_ref[rows, :] = x_ref[rows, :] + acc

    def route(s):
        rows = rows_of(s)
        h2 = _rms(x1_ref[rows, :], g2_ref[...])
        logits = jnp.dot(h2.astype(BF16), wr_ref[...],
                         preferred_element_type=F32) + br_ref[...]
        _store_row_tiles(hp_ref.at[pl.ds(s * sub * ROW_TILE, sub * ROW_TILE), :],
                         _pack_bf16_pairs(h2))
        ri_t, rw = _route_cols(logits, base_ref)
        ri_ref[:, rows] = ri_t
        rw_ref[rows, :] = rw

    conv(0)
    for s in range(OUT_SUB):
        project(s)
        if s + 1 < OUT_SUB:
            conv(s + 1)
        if s > 0:
            route(s - 1)
    route(OUT_SUB - 1)
    cnt_ref[...] = base_ref[...]


def _outproj(a_n, proj, conv_w, g_conv, x2, wo_bf, g2, wr_bf, br):
    tm = OUT_TM
    per = tm // CONV_PREV
    row = lambda i: (i, 0)
    fixed = lambda i: (0, 0)
    col = lambda c: (lambda i: (i, c))
    prev = lambda c: (lambda i: (jnp.maximum(i * per - 1, 0), c))
    return pl.pallas_call(
        _outproj_kernel,
        grid=(T // tm,),
        in_specs=[
            pl.BlockSpec((tm, D_ATTN), row),
            pl.BlockSpec((tm, D_CONV), col(3)),
            pl.BlockSpec((tm, D_CONV), col(4)),
            pl.BlockSpec((tm, D_CONV), col(5)),
            pl.BlockSpec((CONV_PREV, D_CONV), prev(4)),
            pl.BlockSpec((CONV_PREV, D_CONV), prev(5)),
            pl.BlockSpec((8, D_CONV), fixed),
            pl.BlockSpec((1, D_CONV), fixed),
            pl.BlockSpec((tm, D_MODEL), row),
            pl.BlockSpec((D_MODEL, D_MODEL), fixed),
            pl.BlockSpec((1, D_MODEL), fixed),
            pl.BlockSpec((D_MODEL, LANES), fixed),
            pl.BlockSpec((1, LANES), fixed),
        ],
        out_specs=[
            pl.BlockSpec((tm, D_MODEL), row),
            pl.BlockSpec((tm * ROW_TILE, LANES), row),
            pl.BlockSpec((SUBLANES, tm), lambda i: (0, i)),
            pl.BlockSpec((tm, LANES), row),
            pl.BlockSpec((N_EXPERTS, LANES), fixed),
        ],
        out_shape=[
            jax.ShapeDtypeStruct((T, D_MODEL), F32),
            jax.ShapeDtypeStruct((T * ROW_TILE, LANES), I32),
            jax.ShapeDtypeStruct((SUBLANES, T), I32),
            jax.ShapeDtypeStruct((T, LANES), F32),
            jax.ShapeDtypeStruct((N_EXPERTS, LANES), F32),
        ],
        scratch_shapes=[pltpu.VMEM((N_EXPERTS, LANES), F32), pltpu.VMEM((tm, D_CONV), BF16)],
        compiler_params=_params(("arbitrary",)),
        name="outproj",
    )(a_n, proj, proj, proj, proj, proj, conv_w, g_conv, x2, wo_bf, g2, wr_bf, br)


BLOCK_LANES = 384
DEST_TM = 2048


def _dest_kernel(ri_ref, cnt_ref, dest_ref, be_ref, seg_ref):
    tm = DEST_TM
    erow = lax.broadcasted_iota(I32, (N_EXPERTS, LANES), 0)
    cnt = cnt_ref[...].astype(I32)
    pcnt = ((cnt + (MOE_BLOCK - 1)) >> MOE_BLOCK_SHIFT) << MOE_BLOCK_SHIFT
    pend = pcnt
    for s in (1, 2, 4, 8, 16):
        pend = pend + jnp.where(erow >= s, pltpu.roll(pend, s, axis=0), 0)
    pstart = (pend - pcnt).astype(F32)

    ri = ri_ref[...].astype(F32)
    etok = lax.broadcasted_iota(I32, (N_EXPERTS, tm), 0).astype(F32)
    pstart_t = jnp.concatenate([pstart] * (tm // LANES), axis=1)
    d1 = jnp.sum(jnp.where(etok == ri[0:1], pstart_t, 0.0), axis=0, keepdims=True) + ri[2:3]
    d2 = jnp.sum(jnp.where(etok == ri[1:2], pstart_t, 0.0), axis=0, keepdims=True) + ri[3:4]
    sub = lax.broadcasted_iota(I32, (SUBLANES, tm), 0)
    dest_ref[...] = jnp.where(sub == 0, d1, d2).astype(I32)

    pend_f = pend.astype(F32)
    pend_b = jnp.concatenate([pend_f] * (BLOCK_LANES // LANES), axis=1)
    blk_lane = lax.broadcasted_iota(I32, (1, BLOCK_LANES), 1)
    le = jnp.where(pend_b <= (blk_lane * MOE_BLOCK).astype(F32), 1.0, 0.0)
    be = jnp.minimum(jnp.sum(le, axis=0, keepdims=True), float(N_EXPERTS - 1))
    total = pend_b[N_EXPERTS - 1:N_EXPERTS, :]
    be = jnp.where(blk_lane == N_BLOCKS, total * (1.0 / MOE_BLOCK), be)
    be_ref[...] = jnp.broadcast_to(be, (SUBLANES, BLOCK_LANES)).astype(I32)
    seg_ref[0:N_EXPERTS, :] = pend - pcnt + cnt
    seg_ref[N_EXPERTS:2 * N_EXPERTS, :] = pend


def _dest(ri_t, cnt):
    tm = DEST_TM
    fixed = lambda i: (0, 0)
    tok = lambda i: (0, i)
    return pl.pallas_call(
        _dest_kernel,
        grid=(T // tm,),
        in_specs=[pl.BlockSpec((SUBLANES, tm), tok), pl.BlockSpec((N_EXPERTS, LANES), fixed)],
        out_specs=[pl.BlockSpec((SUBLANES, tm), tok),
                   pl.BlockSpec((SUBLANES, BLOCK_LANES), fixed),
                   pl.BlockSpec((2 * N_EXPERTS, LANES), fixed)],
        out_shape=[jax.ShapeDtypeStruct((SUBLANES, T), I32),
                   jax.ShapeDtypeStruct((SUBLANES, BLOCK_LANES), I32),
                   jax.ShapeDtypeStruct((2 * N_EXPERTS, LANES), I32)],
        compiler_params=_params(("arbitrary",)),
        name="dest",
    )(ri_t, cnt)


GROUP = 3
INV_LEN = (-(-N_BLOCKS // GROUP) * GROUP + GROUP - 1) * MOE_BLOCK
DUMP_ROWS = 4 * MOE_BLOCK


PAD_STEP = 8


def _invert_permutation(d1_ref, d2_ref, vend_ref, pend_ref, inv_ref, pos_end):
    def pad_range(lo, hi):
        def body(k, carry):
            p0 = lo + k * PAD_STEP
            for j in range(PAD_STEP):
                inv_ref[p0 + j] = N_ROUTED + ((p0 + j) & (DUMP_ROWS - 1))
            return carry

        lax.fori_loop(0, (hi - lo + (PAD_STEP - 1)) // PAD_STEP, body, 0)

    def pad_expert(e, carry):
        pad_range(vend_ref[e], pend_ref[e])
        return carry

    lax.fori_loop(0, N_EXPERTS, pad_expert, 0)
    pad_range(pend_ref[N_EXPERTS - 1], pos_end)

    def place(t, carry):
        inv_ref[d1_ref[t]] = 2 * t
        inv_ref[d2_ref[t]] = 2 * t + 1
        return carry

    lax.fori_loop(0, T, place, 0, unroll=16)


BLOCK_TILE_ROWS = MOE_BLOCK * ROW_TILE
Y_ROWS = (N_ROUTED + DUMP_ROWS) * ROW_TILE


def _moe_kernel(meta_ref, d1_ref, d2_ref, vend_ref, pend_ref, h_ref, wg_hbm, wu_hbm, wd_hbm, y_ref,
                xb0, xb1, xb2, yb0, yb1, yb2, wg_s, wu_s, wd_s, wg_b, wu_b, wd_b, inv_ref,
                sem_x, sem_y, sem_w):
    xbufs = (xb0, xb1, xb2)
    ybufs = (yb0, yb1, yb2)
    n_used = meta_ref[N_BLOCKS]
    n_groups = (n_used + (GROUP - 1)) // GROUP
    half = D_MODEL // 2

    def issue_gather(b, v):
        for r in range(MOE_BLOCK):
            tok = (inv_ref[b * MOE_BLOCK + r] >> 1) & (T - 1)
            src = h_ref.at[pl.ds(pl.multiple_of(tok * ROW_TILE, ROW_TILE), ROW_TILE), :]
            pltpu.make_async_copy(src, xbufs[v].at[pl.ds(r * ROW_TILE, ROW_TILE), :],
                                  sem_x.at[v]).start()

    def wait_gather(v):
        pltpu.make_async_copy(h_ref.at[pl.ds(0, BLOCK_TILE_ROWS), :], xbufs[v],
                              sem_x.at[v]).wait()

    def issue_scatter(b, v):
        for r in range(MOE_BLOCK):
            slot = inv_ref[b * MOE_BLOCK + r]
            dst = y_ref.at[pl.ds(pl.multiple_of(slot * ROW_TILE, ROW_TILE), ROW_TILE), :]
            pltpu.make_async_copy(ybufs[v].at[pl.ds(r * ROW_TILE, ROW_TILE), :], dst,
                                  sem_y.at[v]).start()

    def wait_scatter(v):
        pltpu.make_async_copy(ybufs[v], y_ref.at[pl.ds(0, BLOCK_TILE_ROWS), :],
                              sem_y.at[v]).wait()

    def weight_copies(e):
        return (pltpu.make_async_copy(wg_hbm.at[e], wg_s, sem_w),
                pltpu.make_async_copy(wu_hbm.at[e], wu_s, sem_w),
                pltpu.make_async_copy(wd_hbm.at[e], wd_s, sem_w))

    def load_expert(b):
        e = meta_ref[b]
        prev = meta_ref[jnp.maximum(b - 1, 0)]
        first = jnp.logical_and(b < n_used, jnp.logical_or(b == 0, e != prev))

        @pl.when(first)
        def _():
            for c in weight_copies(e):
                c.wait()
            wg_b[...] = wg_s[...].astype(BF16)
            wu_b[...] = wu_s[...].astype(BF16)
            wd_b[...] = wd_s[...].astype(BF16)
            nxt = lax.while_loop(
                lambda j: jnp.logical_and(j < n_used, meta_ref[jnp.minimum(j, N_BLOCKS - 1)] == e),
                lambda j: j + 1, b + 1)

            @pl.when(nxt < n_used)
            def _():
                for c in weight_copies(meta_ref[nxt]):
                    c.start(priority=1)

    def compute(v):
        lo, hi = _unpack_pairs_f32(_load_row_tiles(xbufs[v], MOE_BLOCK))
        lo, hi = lo.astype(BF16), hi.astype(BF16)
        g = (jnp.dot(lo, wg_b[0:half, :], preferred_element_type=F32)
             + jnp.dot(hi, wg_b[half:, :], preferred_element_type=F32))
        u = (jnp.dot(lo, wu_b[0:half, :], preferred_element_type=F32)
             + jnp.dot(hi, wu_b[half:, :], preferred_element_type=F32))
        a = g * (1.0 / (1.0 + jnp.exp(-g))) * u
        y = jnp.dot(a.astype(BF16), wd_b[...], preferred_element_type=F32)
        _store_row_tiles(ybufs[v], _pack_bf16_pairs(y))

    def block(b, v, scatter_prev, wait_y):
        wait_gather(v)
        if wait_y:
            wait_scatter(v)
        load_expert(b)
        issue_gather(b + 2, (v + 2) % GROUP)
        if scatter_prev:
            issue_scatter(b - 1, (v + 2) % GROUP)
        compute(v)

    for c in weight_copies(meta_ref[0]):
        c.start()
    yb2[...] = jnp.zeros_like(yb2)
    dump_fills = [
        pltpu.make_async_copy(
            yb2, y_ref.at[pl.ds((N_ROUTED + k * MOE_BLOCK) * ROW_TILE, BLOCK_TILE_ROWS), :],
            sem_y.at[GROUP - 1])
        for k in range(DUMP_ROWS // MOE_BLOCK)]
    for c in dump_fills:
        c.start()
    _invert_permutation(d1_ref, d2_ref, vend_ref, pend_ref, inv_ref,
                        (n_groups * GROUP + GROUP - 1) * MOE_BLOCK)
    for c in dump_fills:
        c.wait()
    issue_gather(0, 0)
    issue_gather(1, 1)
    for v in range(GROUP):
        block(v, v, scatter_prev=v > 0, wait_y=False)

    def group(gi, carry):
        for v in range(GROUP):
            block(gi * GROUP + v, v, scatter_prev=True, wait_y=True)
        return carry

    lax.fori_loop(1, n_groups, group, 0)

    last = n_groups * GROUP
    issue_scatter(last - 1, GROUP - 1)
    wait_gather(0)
    wait_gather(1)
    for v in range(GROUP):
        wait_scatter(v)


def _moe(meta, d1, d2, vend, pend, h_tiles, w_gate, w_up, w_down):
    any_spec = pl.BlockSpec(memory_space=pl.ANY)
    xy = pltpu.VMEM((BLOCK_TILE_ROWS, LANES), I32)
    return pl.pallas_call(
        _moe_kernel,
        grid_spec=pltpu.PrefetchScalarGridSpec(
            num_scalar_prefetch=5,
            grid=(1,),
            in_specs=[any_spec, any_spec, any_spec, any_spec],
            out_specs=any_spec,
            scratch_shapes=[
                xy, xy, xy, xy, xy, xy,
                pltpu.VMEM((D_MODEL, D_EXPERT), F32),
                pltpu.VMEM((D_MODEL, D_EXPERT), F32),
                pltpu.VMEM((D_EXPERT, D_MODEL), F32),
                pltpu.VMEM((D_MODEL, D_EXPERT), BF16),
                pltpu.VMEM((D_MODEL, D_EXPERT), BF16),
                pltpu.VMEM((D_EXPERT, D_MODEL), BF16),
                pltpu.SMEM((INV_LEN,), I32),
                pltpu.SemaphoreType.DMA((GROUP,)),
                pltpu.SemaphoreType.DMA((GROUP,)),
                pltpu.SemaphoreType.DMA(()),
            ],
        ),
        out_shape=jax.ShapeDtypeStruct((Y_ROWS, LANES), I32),
        compiler_params=_params(("arbitrary",)),
        name="moe",
    )(meta, d1, d2, vend, pend, h_tiles, w_gate, w_up, w_down)


COMB_TM = 512


def _combine_kernel(x1_ref, rw_ref, gf_ref, y_ref, o_ref):
    tm = COMB_TM
    w = rw_ref[...]
    w0, w1 = w[:, 0:1], w[:, 1:2]
    lo0, hi0 = _unpack_pairs_f32(_load_row_tiles(y_ref, tm, 0, 2 * ROW_TILE))
    lo1, hi1 = _unpack_pairs_f32(_load_row_tiles(y_ref, tm, ROW_TILE, 2 * ROW_TILE))
    moe = jnp.concatenate([lo0 * w0 + lo1 * w1, hi0 * w0 + hi1 * w1], axis=1)
    o_ref[...] = _rms(x1_ref[...] + moe, gf_ref[...])


def _combine(x1, rw, gf, y_tiles):
    tm = COMB_TM
    row = lambda i: (i, 0)
    return pl.pallas_call(
        _combine_kernel,
        grid=(T // tm,),
        in_specs=[
            pl.BlockSpec((tm, D_MODEL), row),
            pl.BlockSpec((tm, LANES), row),
            pl.BlockSpec((1, D_MODEL), lambda i: (0, 0)),
            pl.BlockSpec((tm * 2 * ROW_TILE, LANES), row),
        ],
        out_specs=pl.BlockSpec((tm, D_MODEL), row),
        out_shape=jax.ShapeDtypeStruct((T, D_MODEL), F32),
        compiler_params=_params(("arbitrary",)),
        name="combine",
    )(x1, rw, gf, y_tiles)


def _layer(x2, norm1, w_in, rel_bias, conv_w, g_out_attn, g_out_conv, w_out, norm2,
           w_rg, b_rg, w_re, b_re, w_gate, w_up, w_down):
    proj = _inproj(x2, norm1[None, :], w_in.astype(BF16))
    a_n = _attention(proj, _bias_table(rel_bias), g_out_attn[None, :])
    conv_w8 = jnp.zeros((8, D_CONV), F32).at[0:3].set(conv_w)

    w_r = jnp.zeros((D_MODEL, LANES), F32)
    g0 = GROUP_LOGIT_ROW
    w_r = w_r.at[:, 0:N_EXPERTS].set(w_re).at[:, g0:g0 + N_GROUPS].set(w_rg)
    b_r = jnp.zeros((1, LANES), F32)
    b_r = b_r.at[0, 0:N_EXPERTS].set(b_re).at[0, g0:g0 + N_GROUPS].set(b_rg)
    x1, hp, ri_t, rw, cnt = _outproj(a_n, proj, conv_w8, g_out_conv[None, :], x2,
                                     w_out.astype(BF16), norm2[None, :], w_r.astype(BF16), b_r)

    dest_t, be, seg = _dest(ri_t, cnt)
    y_tiles = _moe(be[0], dest_t[0], dest_t[1], seg[0:N_EXPERTS, 0],
                   seg[N_EXPERTS:2 * N_EXPERTS, 0], hp, w_gate, w_up, w_down)
    return x1, rw, y_tiles


def kernel(x, norm1, w_in, rel_bias, conv_w, g_out_attn, g_out_conv, w_out, norm2,
           w_router_group, b_router_group, w_router_expert, b_router_expert,
           w_gate, w_up, w_down, norm_final):
    assert x.shape == (BATCH, SEQ, D_MODEL) and norm1.shape[0] == 1
    x2 = x.reshape(T, D_MODEL)
    x1, rw, y_tiles = _layer(
        x2, norm1[0], w_in[0], rel_bias[0], conv_w[0], g_out_attn[0], g_out_conv[0],
        w_out[0], norm2[0], w_router_group[0], b_router_group[0], w_router_expert[0],
        b_router_expert[0], w_gate[0], w_up[0], w_down[0])
    out = _combine(x1, rw, norm_final[None, :], y_tiles)
    return out.reshape(BATCH, SEQ, D_MODEL)
```

```python
import functools

import numpy as np
import jax
import jax.numpy as jnp
from jax import lax
from jax.experimental import pallas as pl
from jax.experimental.pallas import tpu as pltpu

F32 = jnp.float32
BF16 = jnp.bfloat16
I32 = jnp.int32

D_MODEL = 2048
BATCH = 4
SEQ = 4096
T = BATCH * SEQ
CHUNK = 64
LEFT_CHUNKS = 8
BAND = LEFT_CHUNKS + 1
KEYS = BAND * CHUNK
D_ATTN = 1024
D_CONV = 1024
HEAD_DIM = 64
N_HEADS = 16
N_PAIRS = N_HEADS // 2
REL_CLIP = 256
D_IN_PROJ = 6 * 1024
N_GROUPS = 4
EPG = 8
N_EXPERTS = 32
D_EXPERT = 512
MOE_BLOCK = 256
MOE_BLOCK_SHIFT = 8
N_ROUTED = 2 * T
N_BLOCKS = N_ROUTED // MOE_BLOCK + N_EXPERTS
EPS = 1e-6
NEG_INF = -1e30
LANES = 128
VMEM_LIMIT = 52 * 1024 * 1024

ATTN_TILE = 512
ATTN_TILES_PER_SEQ = SEQ // ATTN_TILE


def _params(sem):
    return pltpu.CompilerParams(dimension_semantics=sem, vmem_limit_bytes=VMEM_LIMIT)


def _rms(x, g):
    ms = jnp.mean(x * x, axis=-1, keepdims=True)
    return x * lax.rsqrt(ms + EPS) * g


INPROJ_TM = 1024
INPROJ_TN = 1536


def _inproj_kernel(x_ref, g_ref, w_ref, o_ref, hn_ref):
    @pl.when(pl.program_id(1) == 0)
    def _():
        hn_ref[...] = _rms(x_ref[...], g_ref[...]).astype(BF16)

    o_ref[...] = jnp.dot(hn_ref[...], w_ref[...], preferred_element_type=F32).astype(BF16)


def _inproj(x2, g, w_bf):
    tm, tn = INPROJ_TM, INPROJ_TN
    return pl.pallas_call(
        _inproj_kernel,
        grid=(T // tm, D_IN_PROJ // tn),
        in_specs=[
            pl.BlockSpec((tm, D_MODEL), lambda i, j: (i, 0)),
            pl.BlockSpec((1, D_MODEL), lambda i, j: (0, 0)),
            pl.BlockSpec((D_MODEL, tn), lambda i, j: (0, j)),
        ],
        out_specs=pl.BlockSpec((tm, tn), lambda i, j: (i, j)),
        out_shape=jax.ShapeDtypeStruct((T, D_IN_PROJ), BF16),
        scratch_shapes=[pltpu.VMEM((tm, D_MODEL), BF16)],
        compiler_params=_params(("arbitrary", "arbitrary")),
        name="inproj",
    )(x2, g, w_bf)


N_QUADS = N_HEADS // 4
QUAD = 4 * HEAD_DIM
BIAS_W = 640
BIAS_WIN = 512
P_BUFS = 4


def _bias_kernel(b_ref, o_ref):
    first_head = lax.broadcasted_iota(I32, (KEYS, LANES), 1) < HEAD_DIM
    for hq in range(N_QUADS):
        for col in range(2):
            h0 = 4 * hq + 2 * col
            pieces = []
            for hh in range(2):
                x = jnp.broadcast_to(b_ref[h0 + hh:h0 + hh + 1, :], (KEYS, BIAS_W))
                rolled = pltpu.roll(x, 0, 1, stride=1, stride_axis=0)
                pieces.append(rolled[:, BIAS_WIN:BIAS_WIN + LANES])
            o_ref[hq, :, col * LANES:(col + 1) * LANES] = jnp.where(
                first_head, pieces[0], pieces[1])


def _bias_table(rel_bias):
    r = rel_bias.astype(F32)
    edge = jnp.broadcast_to(r[:, 2 * REL_CLIP:], (N_HEADS, KEYS - REL_CLIP + 1))
    b = jnp.concatenate([r[:, REL_CLIP:2 * REL_CLIP], edge,
                         r[:, REL_CLIP - (BIAS_W - KEYS - 1):REL_CLIP]], axis=1)
    odd = jnp.concatenate([b[:, BIAS_W - HEAD_DIM:], b[:, :BIAS_W - HEAD_DIM]], axis=1)
    is_odd = (jnp.arange(N_HEADS) % 2 == 1)[:, None]
    bvec = jnp.where(is_odd, odd, b)
    return pl.pallas_call(
        _bias_kernel,
        out_shape=jax.ShapeDtypeStruct((N_QUADS, KEYS, QUAD), F32),
        compiler_params=pltpu.CompilerParams(vmem_limit_bytes=VMEM_LIMIT),
        name="biastab",
    )(bvec)


def _attn_kernel(q_ref, kp_ref, kc_ref, vp_ref, vc_ref, bias_ref, g_ref, o_ref,
                 kw_ref, vw_ref, st_ref, p_ref, a_ref):
    i = pl.program_id(1)
    kw_ref[0:ATTN_TILE, :] = kp_ref[...]
    kw_ref[ATTN_TILE:2 * ATTN_TILE, :] = kc_ref[...]
    vw_ref[0:ATTN_TILE, :] = vp_ref[...]
    vw_ref[ATTN_TILE:2 * ATTN_TILE, :] = vc_ref[...]

    head_of_lane = lax.broadcasted_iota(I32, (CHUNK, QUAD), 1) // HEAD_DIM
    first_head = lax.broadcasted_iota(I32, (CHUNK, LANES), 1) < HEAD_DIM
    key_row = lax.broadcasted_iota(I32, (KEYS, QUAD), 0)
    n_rb = KEYS // CHUNK

    all_quads = tuple(range(N_QUADS))
    all_pairs = tuple(range(N_PAIRS))

    def scores(r0, buf, masked, quads=all_quads):
        for hq in quads:
            c0 = hq * QUAD
            q4 = q_ref[pl.ds(r0, CHUNK), c0:c0 + QUAD] * (HEAD_DIM ** -0.5)
            zero = jnp.zeros_like(q4)
            qbd = jnp.concatenate(
                [jnp.where(head_of_lane == h, q4, zero) for h in range(4)], axis=0)
            k4 = kw_ref[pl.ds(r0, KEYS), c0:c0 + QUAD]
            st = lax.dot_general(k4, qbd, (((1,), (1,)), ((), ())),
                                 preferred_element_type=F32)
            st = st + bias_ref[hq]
            if masked:
                st = jnp.where(key_row + r0 >= ATTN_TILE, st, NEG_INF)
            st_ref[buf, hq] = st

    def exps(buf, pbuf, off, quads=all_quads):
        for hq in quads:
            m = st_ref[buf, hq, 0:CHUNK, :]
            for rb in range(1, n_rb):
                m = jnp.maximum(m, st_ref[buf, hq, rb * CHUNK:(rb + 1) * CHUNK, :])
            m = jnp.max(m, axis=0, keepdims=True)
            for rb in range(n_rb):
                rows = slice(rb * CHUNK, (rb + 1) * CHUNK)
                prow = slice(off + rb * CHUNK, off + (rb + 1) * CHUNK)
                p_ref[pbuf, hq, prow, :] = jnp.exp(st_ref[buf, hq, rows, :] - m).astype(BF16)

    def values(r0, pbufs, pairs=all_pairs):
        nq = len(pbufs)
        span = KEYS + (nq - 1) * CHUNK
        ones = jnp.ones((span, LANES), BF16)
        for hp in pairs:
            c0 = hp * LANES
            half = (hp % 2) * LANES
            pt = jnp.concatenate(
                [p_ref[b, hp // 2, 0:span, half:half + LANES] for b in pbufs], axis=1)
            vext = jnp.concatenate([vw_ref[pl.ds(r0, span), c0:c0 + LANES], ones], axis=1)
            oe = lax.dot_general(pt, vext, (((0,), (0,)), ((), ())),
                                 preferred_element_type=F32)
            o = oe[:, :LANES] * (1.0 / oe[:, LANES:])
            for j in range(nq):
                oj = o[2 * j * CHUNK:2 * (j + 1) * CHUNK]
                a_ref[pl.ds(r0 + j * CHUNK, CHUNK), c0:c0 + LANES] = jnp.where(
                    first_head, oj[0:CHUNK], oj[CHUNK:2 * CHUNK])

    n_chunks = ATTN_TILE // CHUNK

    @pl.when(i == 0)
    def _():
        def chunk_body(ci, carry):
            r0 = pl.multiple_of(ci * CHUNK, CHUNK)
            scores(r0, 0, True)
            exps(0, 0, 0)
            values(r0, (0,))
            return carry

        lax.fori_loop(0, n_chunks, chunk_body, 0)

    @pl.when(i > 0)
    def _():
        for b in range(P_BUFS):
            z = slice(KEYS, KEYS + CHUNK) if b % 2 == 0 else slice(0, CHUNK)
            p_ref[b, :, z, :] = jnp.zeros((N_QUADS, CHUNK, QUAD), BF16)
        scores(0, 0, False)
        for ci in range(n_chunks):
            c0 = ci - 2 - ci % 2
            for hq in all_quads:
                if ci + 1 < n_chunks:
                    scores((ci + 1) * CHUNK, (ci + 1) % 2, False, (hq,))
                if c0 >= 0:
                    values(c0 * CHUNK, (c0 % P_BUFS, (c0 + 1) % P_BUFS),
                           ((ci % 2) * N_QUADS + hq,))
                exps(ci % 2, ci % P_BUFS, (ci % 2) * CHUNK, (hq,))
        values((n_chunks - 2) * CHUNK, ((n_chunks - 2) % P_BUFS, (n_chunks - 1) % P_BUFS))

    o_ref[...] = _rms(a_ref[...], g_ref[...]).astype(BF16)


def _attention(proj, bias_t, g):
    n = ATTN_TILES_PER_SEQ
    cur = lambda col: (lambda b, i: (b * n + i, col))
    prev = lambda col: (lambda b, i: (b * n + jnp.maximum(i - 1, 0), col))
    blk = (ATTN_TILE, D_ATTN)
    return pl.pallas_call(
        _attn_kernel,
        grid=(BATCH, n),
        in_specs=[
            pl.BlockSpec(blk, cur(0)),
            pl.BlockSpec(blk, prev(1)),
            pl.BlockSpec(blk, cur(1)),
            pl.BlockSpec(blk, prev(2)),
            pl.BlockSpec(blk, cur(2)),
            pl.BlockSpec((N_QUADS, KEYS, QUAD), lambda b, i: (0, 0, 0)),
            pl.BlockSpec((1, D_ATTN), lambda b, i: (0, 0)),
        ],
        out_specs=pl.BlockSpec(blk, lambda b, i: (b * n + i, 0)),
        out_shape=jax.ShapeDtypeStruct((T, D_ATTN), BF16),
        scratch_shapes=[
            pltpu.VMEM((2 * ATTN_TILE, D_ATTN), BF16),
            pltpu.VMEM((2 * ATTN_TILE, D_ATTN), BF16),
            pltpu.VMEM((2, N_QUADS, KEYS, QUAD), F32),
            pltpu.VMEM((P_BUFS, N_QUADS, KEYS + CHUNK, QUAD), BF16),
            pltpu.VMEM((ATTN_TILE, D_ATTN), F32),
        ],
        compiler_params=_params(("arbitrary", "arbitrary")),
        name="attn",
    )(proj, proj, proj, proj, proj, bias_t, g)


CONV_PREV = 16


def _short_conv_rows(seq_start, b, c, u, cp, up, w, g):
    z = c.astype(F32) * u.astype(F32)
    zp = cp.astype(F32) * up.astype(F32)
    zp = jnp.where(seq_start, 0.0, zp)
    zm1 = zp[CONV_PREV - 1:CONV_PREV]
    zm2 = zp[CONV_PREV - 2:CONV_PREV - 1]
    row = lax.broadcasted_iota(I32, z.shape, 0)
    z1 = jnp.where(row == 0, zm1, pltpu.roll(z, 1, axis=0))
    z2 = jnp.where(row == 0, zm2, jnp.where(row == 1, zm1, pltpu.roll(z, 2, axis=0)))
    y = w[0:1] * z2 + w[1:2] * z1 + w[2:3] * z
    return _rms(b.astype(F32) * y, g).astype(BF16)


OUT_TM = 512
OUT_SUB = 2


def _pack_bf16_pairs(h):
    half = D_MODEL // 2
    return pltpu.pack_elementwise([h[:, :half], h[:, half:]], packed_dtype=BF16)


def _unpack_pairs_f32(u):
    lo = pltpu.unpack_elementwise(u, index=0, packed_dtype=BF16, unpacked_dtype=F32)
    hi = pltpu.unpack_elementwise(u, index=1, packed_dtype=BF16, unpacked_dtype=F32)
    return lo, hi


ROW_TILE = 8


def _store_row_tiles(ref, packed):
    m = packed.shape[0]
    for s in range(ROW_TILE):
        ref[pl.ds(s, m, stride=ROW_TILE), :] = packed[:, s * LANES:(s + 1) * LANES]


def _load_row_tiles(ref, m, first=0, stride=ROW_TILE):
    return jnp.concatenate(
        [ref[pl.ds(first + s, m, stride=stride), :] for s in range(ROW_TILE)], axis=1)


GROUP_LOGIT_ROW = N_EXPERTS
SUBLANES = 8


def _route_cols(lg, base_ref):
    tm = lg.shape[0]
    lt = lg.T
    sub = lax.broadcasted_iota(I32, (SUBLANES, tm), 0)
    sub_f = sub.astype(F32)
    big = jnp.float32(SUBLANES)
    ninf = jnp.float32(-jnp.inf)

    gl = lt[GROUP_LOGIT_ROW:GROUP_LOGIT_ROW + SUBLANES, :]
    gvalid = sub < N_GROUPS
    gmax = jnp.max(jnp.where(gvalid, gl, ninf), axis=0, keepdims=True)
    gsum = jnp.sum(jnp.where(gvalid, jnp.exp(gl - gmax), 0.0), axis=0, keepdims=True)
    p_g = 1.0 / gsum
    g_sel = jnp.min(jnp.where(jnp.logical_and(gvalid, gl == gmax), sub_f, big),
                    axis=0, keepdims=True)
    el = lt[0:EPG, :]
    for g in range(1, N_GROUPS):
        el = jnp.where(g_sel == g, lt[g * EPG:(g + 1) * EPG, :], el)
    v1 = jnp.max(el, axis=0, keepdims=True)
    i1 = jnp.min(jnp.where(el == v1, sub_f, big), axis=0, keepdims=True)
    rest = sub_f != i1
    v2 = jnp.max(jnp.where(rest, el, ninf), axis=0, keepdims=True)
    i2 = jnp.min(jnp.where(jnp.logical_and(rest, el == v2), sub_f, big),
                 axis=0, keepdims=True)
    t = jnp.exp(v2 - v1)
    w1 = p_g * (1.0 / (1.0 + t))
    w2 = p_g * (t / (1.0 + t))
    e1 = g_sel * EPG + i1
    e2 = g_sel * EPG + i2

    erow = lax.broadcasted_iota(I32, (N_EXPERTS, tm), 0).astype(F32)
    hit1 = erow == e1
    hit2 = erow == e2
    onehot = jnp.where(jnp.logical_or(hit1, hit2), 1.0, 0.0)
    r = lax.broadcasted_iota(I32, (tm, tm), 0)
    c = lax.broadcasted_iota(I32, (tm, tm), 1)
    tri = jnp.where(r < c, 1.0, 0.0).astype(BF16)
    base = base_ref[...]
    before = (jnp.dot(onehot.astype(BF16), tri, preferred_element_type=F32)
              + jnp.concatenate([base] * (tm // LANES), axis=1))
    r1 = jnp.sum(jnp.where(hit1, before, 0.0), axis=0, keepdims=True)
    r2 = jnp.sum(jnp.where(hit2, before, 0.0), axis=0, keepdims=True)
    base_ref[...] = base + jnp.broadcast_to(
        jnp.sum(onehot, axis=1, keepdims=True), (N_EXPERTS, LANES))

    ri_t = jnp.where(sub == 0, e1, jnp.where(sub == 1, e2, jnp.where(sub == 2, r1, r2)))
    w_t = jnp.where(sub == 0, w1, jnp.where(sub == 1, w2, 0.0))
    w_t = jnp.concatenate([w_t, jnp.zeros((LANES - SUBLANES, tm), F32)], axis=0)
    return ri_t.astype(I32), w_t.T


def _outproj_kernel(a_ref, b_ref, c_ref, u_ref, cp_ref, up_ref, cw_ref, gc_ref,
                    x_ref, wo_ref, g2_ref, wr_ref, br_ref,
                    x1_ref, hp_ref, ri_ref, rw_ref, cnt_ref, base_ref, cn_ref):
    i = pl.program_id(0)

    @pl.when(i == 0)
    def _():
        base_ref[...] = jnp.zeros_like(base_ref)

    sub = OUT_TM // OUT_SUB
    rows_of = lambda s: slice(s * sub, (s + 1) * sub)

    def conv(s):
        rows = rows_of(s)
        if s == 0:
            seq_start = (i % (SEQ // OUT_TM)) == 0
            cp, up = cp_ref[...], up_ref[...]
        else:
            seq_start = False
            before = slice(s * sub - CONV_PREV, s * sub)
            cp, up = c_ref[before, :], u_ref[before, :]
        cn_ref[rows, :] = _short_conv_rows(
            seq_start, b_ref[rows, :], c_ref[rows, :], u_ref[rows, :],
            cp, up, cw_ref[...], gc_ref[...])

    def project(s):
        rows = rows_of(s)
        acc = jnp.dot(a_ref[rows, :], wo_ref[0:D_ATTN, :], preferred_element_type=F32)
        acc = acc + jnp.dot(cn_ref[rows, :], wo_ref[D_ATTN:, :], preferred_element_type=F32)
        x1_ref[rows, :] = x_ref[rows, :] + acc

    def route(s):
        rows = rows_of(s)
        h2 = _rms(x1_ref[rows, :], g2_ref[...])
        logits = jnp.dot(h2.astype(BF16), wr_ref[...],
                         preferred_element_type=F32) + br_ref[...]
        _store_row_tiles(hp_ref.at[pl.ds(s * sub * ROW_TILE, sub * ROW_TILE), :],
                         _pack_bf16_pairs(h2))
        ri_t, rw = _route_cols(logits, base_ref)
        ri_ref[:, rows] = ri_t
        rw_ref[rows, :] = rw

    conv(0)
    for s in range(OUT_SUB):
        project(s)
        if s + 1 < OUT_SUB:
            conv(s + 1)
        if s > 0:
            route(s - 1)
    route(OUT_SUB - 1)
    cnt_ref[...] = base_ref[...]


def _outproj(a_n, proj, conv_w, g_conv, x2, wo_bf, g2, wr_bf, br):
    tm = OUT_TM
    per = tm // CONV_PREV
    row = lambda i: (i, 0)
    fixed = lambda i: (0, 0)
    col = lambda c: (lambda i: (i, c))
    prev = lambda c: (lambda i: (jnp.maximum(i * per - 1, 0), c))
    return pl.pallas_call(
        _outproj_kernel,
        grid=(T // tm,),
        in_specs=[
            pl.BlockSpec((tm, D_ATTN), row),
            pl.BlockSpec((tm, D_CONV), col(3)),
            pl.BlockSpec((tm, D_CONV), col(4)),
            pl.BlockSpec((tm, D_CONV), col(5)),
            pl.BlockSpec((CONV_PREV, D_CONV), prev(4)),
            pl.BlockSpec((CONV_PREV, D_CONV), prev(5)),
            pl.BlockSpec((8, D_CONV), fixed),
            pl.BlockSpec((1, D_CONV), fixed),
            pl.BlockSpec((tm, D_MODEL), row),
            pl.BlockSpec((D_MODEL, D_MODEL), fixed),
            pl.BlockSpec((1, D_MODEL), fixed),
            pl.BlockSpec((D_MODEL, LANES), fixed),
            pl.BlockSpec((1, LANES), fixed),
        ],
        out_specs=[
            pl.BlockSpec((tm, D_MODEL), row),
            pl.BlockSpec((tm * ROW_TILE, LANES), row),
            pl.BlockSpec((SUBLANES, tm), lambda i: (0, i)),
            pl.BlockSpec((tm, LANES), row),
            pl.BlockSpec((N_EXPERTS, LANES), fixed),
        ],
        out_shape=[
            jax.ShapeDtypeStruct((T, D_MODEL), F32),
            jax.ShapeDtypeStruct((T * ROW_TILE, LANES), I32),
            jax.ShapeDtypeStruct((SUBLANES, T), I32),
            jax.ShapeDtypeStruct((T, LANES), F32),
            jax.ShapeDtypeStruct((N_EXPERTS, LANES), F32),
        ],
        scratch_shapes=[pltpu.VMEM((N_EXPERTS, LANES), F32), pltpu.VMEM((tm, D_CONV), BF16)],
        compiler_params=_params(("arbitrary",)),
        name="outproj",
    )(a_n, proj, proj, proj, proj, proj, conv_w, g_conv, x2, wo_bf, g2, wr_bf, br)


BLOCK_LANES = 384
DEST_TM = 2048


def _dest_kernel(ri_ref, cnt_ref, dest_ref, be_ref, seg_ref):
    tm = DEST_TM
    erow = lax.broadcasted_iota(I32, (N_EXPERTS, LANES), 0)
    cnt = cnt_ref[...].astype(I32)
    pcnt = ((cnt + (MOE_BLOCK - 1)) >> MOE_BLOCK_SHIFT) << MOE_BLOCK_SHIFT
    pend = pcnt
    for s in (1, 2, 4, 8, 16):
        pend = pend + jnp.where(erow >= s, pltpu.roll(pend, s, axis=0), 0)
    pstart = (pend - pcnt).astype(F32)

    ri = ri_ref[...].astype(F32)
    etok = lax.broadcasted_iota(I32, (N_EXPERTS, tm), 0).astype(F32)
    pstart_t = jnp.concatenate([pstart] * (tm // LANES), axis=1)
    d1 = jnp.sum(jnp.where(etok == ri[0:1], pstart_t, 0.0), axis=0, keepdims=True) + ri[2:3]
    d2 = jnp.sum(jnp.where(etok == ri[1:2], pstart_t, 0.0), axis=0, keepdims=True) + ri[3:4]
    sub = lax.broadcasted_iota(I32, (SUBLANES, tm), 0)
    dest_ref[...] = jnp.where(sub == 0, d1, d2).astype(I32)

    pend_f = pend.astype(F32)
    pend_b = jnp.concatenate([pend_f] * (BLOCK_LANES // LANES), axis=1)
    blk_lane = lax.broadcasted_iota(I32, (1, BLOCK_LANES), 1)
    le = jnp.where(pend_b <= (blk_lane * MOE_BLOCK).astype(F32), 1.0, 0.0)
    be = jnp.minimum(jnp.sum(le, axis=0, keepdims=True), float(N_EXPERTS - 1))
    total = pend_b[N_EXPERTS - 1:N_EXPERTS, :]
    be = jnp.where(blk_lane == N_BLOCKS, total * (1.0 / MOE_BLOCK), be)
    be_ref[...] = jnp.broadcast_to(be, (SUBLANES, BLOCK_LANES)).astype(I32)
    seg_ref[0:N_EXPERTS, :] = pend - pcnt + cnt
    seg_ref[N_EXPERTS:2 * N_EXPERTS, :] = pend


def _dest(ri_t, cnt):
    tm = DEST_TM
    fixed = lambda i: (0, 0)
    tok = lambda i: (0, i)
    return pl.pallas_call(
        _dest_kernel,
        grid=(T // tm,),
        in_specs=[pl.BlockSpec((SUBLANES, tm), tok), pl.BlockSpec((N_EXPERTS, LANES), fixed)],
        out_specs=[pl.BlockSpec((SUBLANES, tm), tok),
                   pl.BlockSpec((SUBLANES, BLOCK_LANES), fixed),
                   pl.BlockSpec((2 * N_EXPERTS, LANES), fixed)],
        out_shape=[jax.ShapeDtypeStruct((SUBLANES, T), I32),
                   jax.ShapeDtypeStruct((SUBLANES, BLOCK_LANES), I32),
                   jax.ShapeDtypeStruct((2 * N_EXPERTS, LANES), I32)],
        compiler_params=_params(("arbitrary",)),
        name="dest",
    )(ri_t, cnt)


GROUP = 3
INV_LEN = (-(-N_BLOCKS // GROUP) * GROUP + GROUP - 1) * MOE_BLOCK
DUMP_ROWS = 4 * MOE_BLOCK


PAD_STEP = 8


def _invert_permutation(d1_ref, d2_ref, vend_ref, pend_ref, inv_ref, pos_end):
    def pad_range(lo, hi):
        def body(k, carry):
            p0 = lo + k * PAD_STEP
            for j in range(PAD_STEP):
                inv_ref[p0 + j] = N_ROUTED + ((p0 + j) & (DUMP_ROWS - 1))
            return carry

        lax.fori_loop(0, (hi - lo + (PAD_STEP - 1)) // PAD_STEP, body, 0)

    def pad_expert(e, carry):
        pad_range(vend_ref[e], pend_ref[e])
        return carry

    lax.fori_loop(0, N_EXPERTS, pad_expert, 0)
    pad_range(pend_ref[N_EXPERTS - 1], pos_end)

    def place(t, carry):
        inv_ref[d1_ref[t]] = 2 * t
        inv_ref[d2_ref[t]] = 2 * t + 1
        return carry

    lax.fori_loop(0, T, place, 0, unroll=16)


BLOCK_TILE_ROWS = MOE_BLOCK * ROW_TILE
Y_ROWS = (N_ROUTED + DUMP_ROWS) * ROW_TILE


def _moe_kernel(meta_ref, d1_ref, d2_ref, vend_ref, pend_ref, h_ref, wg_hbm, wu_hbm, wd_hbm, y_ref,
                xb0, xb1, xb2, yb0, yb1, yb2, wg_s0, wu_s0, wd_s0, wg_s1, wu_s1, wd_s1,
                wg_b, wu_b, wd_b, inv_ref, nth_ref, sem_x, sem_y, sem_w):
    xbufs = (xb0, xb1, xb2)
    ybufs = (yb0, yb1, yb2)
    stages = ((wg_s0, wu_s0, wd_s0), (wg_s1, wu_s1, wd_s1))
    n_used = meta_ref[N_BLOCKS]
    n_groups = (n_used + (GROUP - 1)) // GROUP
    half = D_MODEL // 2

    def issue_gather(b, v):
        for r in range(MOE_BLOCK):
            tok = (inv_ref[b * MOE_BLOCK + r] >> 1) & (T - 1)
            src = h_ref.at[pl.ds(pl.multiple_of(tok * ROW_TILE, ROW_TILE), ROW_TILE), :]
            pltpu.make_async_copy(src, xbufs[v].at[pl.ds(r * ROW_TILE, ROW_TILE), :],
                                  sem_x.at[v]).start()

    def wait_gather(v):
        pltpu.make_async_copy(h_ref.at[pl.ds(0, BLOCK_TILE_ROWS), :], xbufs[v],
                              sem_x.at[v]).wait()

    def issue_scatter(b, v):
        for r in range(MOE_BLOCK):
            slot = inv_ref[b * MOE_BLOCK + r]
            dst = y_ref.at[pl.ds(pl.multiple_of(slot * ROW_TILE, ROW_TILE), ROW_TILE), :]
            pltpu.make_async_copy(ybufs[v].at[pl.ds(r * ROW_TILE, ROW_TILE), :], dst,
                                  sem_y.at[v]).start()

    def wait_scatter(v):
        pltpu.make_async_copy(ybufs[v], y_ref.at[pl.ds(0, BLOCK_TILE_ROWS), :],
                              sem_y.at[v]).wait()

    def weight_copies(e, s):
        wg_s, wu_s, wd_s = stages[s]
        return (pltpu.make_async_copy(wg_hbm.at[e], wg_s, sem_w.at[s]),
                pltpu.make_async_copy(wu_hbm.at[e], wu_s, sem_w.at[s]),
                pltpu.make_async_copy(wd_hbm.at[e], wd_s, sem_w.at[s]))

    def next_expert_block(b):
        e = meta_ref[jnp.minimum(b, N_BLOCKS - 1)]
        return lax.while_loop(
            lambda j: jnp.logical_and(j < n_used, meta_ref[jnp.minimum(j, N_BLOCKS - 1)] == e),
            lambda j: j + 1, b + 1)

    def stage_expert_of(j, s):
        @pl.when(j < n_used)
        def _():
            for c in weight_copies(meta_ref[j], s):
                c.start(priority=1)

    def load_expert(b):
        e = meta_ref[b]
        prev = meta_ref[jnp.maximum(b - 1, 0)]
        first = jnp.logical_and(b < n_used, jnp.logical_or(b == 0, e != prev))

        @pl.when(first)
        def _():
            k = nth_ref[0]
            nth_ref[0] = k + 1
            for s in range(2):
                @pl.when((k & 1) == s)
                def _():
                    for c in weight_copies(e, s):
                        c.wait()
                    wg_s, wu_s, wd_s = stages[s]
                    wg_b[...] = wg_s[...].astype(BF16)
                    wu_b[...] = wu_s[...].astype(BF16)
                    wd_b[...] = wd_s[...].astype(BF16)
                    stage_expert_of(next_expert_block(next_expert_block(b)), s)

    def compute(v):
        lo, hi = _unpack_pairs_f32(_load_row_tiles(xbufs[v], MOE_BLOCK))
        lo, hi = lo.astype(BF16), hi.astype(BF16)
        g = (jnp.dot(lo, wg_b[0:half, :], preferred_element_type=F32)
             + jnp.dot(hi, wg_b[half:, :], preferred_element_type=F32))
        u = (jnp.dot(lo, wu_b[0:half, :], preferred_element_type=F32)
             + jnp.dot(hi, wu_b[half:, :], preferred_element_type=F32))
        a = g * (1.0 / (1.0 + jnp.exp(-g))) * u
        y = jnp.dot(a.astype(BF16), wd_b[...], preferred_element_type=F32)
        _store_row_tiles(ybufs[v], _pack_bf16_pairs(y))

    def block(b, v, scatter_prev, wait_y):
        wait_gather(v)
        if wait_y:
            wait_scatter(v)
        load_expert(b)
        issue_gather(b + 2, (v + 2) % GROUP)
        if scatter_prev:
            issue_scatter(b - 1, (v + 2) % GROUP)
        compute(v)

    nth_ref[0] = 0
    for c in weight_copies(meta_ref[0], 0):
        c.start()
    stage_expert_of(next_expert_block(0), 1)
    yb2[...] = jnp.zeros_like(yb2)
    dump_fills = [
        pltpu.make_async_copy(
            yb2, y_ref.at[pl.ds((N_ROUTED + k * MOE_BLOCK) * ROW_TILE, BLOCK_TILE_ROWS), :],
            sem_y.at[GROUP - 1])
        for k in range(DUMP_ROWS // MOE_BLOCK)]
    for c in dump_fills:
        c.start()
    _invert_permutation(d1_ref, d2_ref, vend_ref, pend_ref, inv_ref,
                        (n_groups * GROUP + GROUP - 1) * MOE_BLOCK)
    for c in dump_fills:
        c.wait()
    issue_gather(0, 0)
    issue_gather(1, 1)
    for v in range(GROUP):
        block(v, v, scatter_prev=v > 0, wait_y=False)

    def group(gi, carry):
        for v in range(GROUP):
            block(gi * GROUP + v, v, scatter_prev=True, wait_y=True)
        return carry

    lax.fori_loop(1, n_groups, group, 0)

    last = n_groups * GROUP
    issue_scatter(last - 1, GROUP - 1)
    wait_gather(0)
    wait_gather(1)
    for v in range(GROUP):
        wait_scatter(v)


def _moe(meta, d1, d2, vend, pend, h_tiles, w_gate, w_up, w_down):
    any_spec = pl.BlockSpec(memory_space=pl.ANY)
    xy = pltpu.VMEM((BLOCK_TILE_ROWS, LANES), I32)
    return pl.pallas_call(
        _moe_kernel,
        grid_spec=pltpu.PrefetchScalarGridSpec(
            num_scalar_prefetch=5,
            grid=(1,),
            in_specs=[any_spec, any_spec, any_spec, any_spec],
            out_specs=any_spec,
            scratch_shapes=[
                xy, xy, xy, xy, xy, xy,
                pltpu.VMEM((D_MODEL, D_EXPERT), F32),
                pltpu.VMEM((D_MODEL, D_EXPERT), F32),
                pltpu.VMEM((D_EXPERT, D_MODEL), F32),
                pltpu.VMEM((D_MODEL, D_EXPERT), F32),
                pltpu.VMEM((D_MODEL, D_EXPERT), F32),
                pltpu.VMEM((D_EXPERT, D_MODEL), F32),
                pltpu.VMEM((D_MODEL, D_EXPERT), BF16),
                pltpu.VMEM((D_MODEL, D_EXPERT), BF16),
                pltpu.VMEM((D_EXPERT, D_MODEL), BF16),
                pltpu.SMEM((INV_LEN,), I32),
                pltpu.SMEM((1,), I32),
                pltpu.SemaphoreType.DMA((GROUP,)),
                pltpu.SemaphoreType.DMA((GROUP,)),
                pltpu.SemaphoreType.DMA((2,)),
            ],
        ),
        out_shape=jax.ShapeDtypeStruct((Y_ROWS, LANES), I32),
        compiler_params=_params(("arbitrary",)),
        name="moe",
    )(meta, d1, d2, vend, pend, h_tiles, w_gate, w_up, w_down)


COMB_TM = 512


def _combine_kernel(x1_ref, rw_ref, gf_ref, y_ref, o_ref):
    tm = COMB_TM
    w = rw_ref[...]
    w0, w1 = w[:, 0:1], w[:, 1:2]
    lo0, hi0 = _unpack_pairs_f32(_load_row_tiles(y_ref, tm, 0, 2 * ROW_TILE))
    lo1, hi1 = _unpack_pairs_f32(_load_row_tiles(y_ref, tm, ROW_TILE, 2 * ROW_TILE))
    moe = jnp.concatenate([lo0 * w0 + lo1 * w1, hi0 * w0 + hi1 * w1], axis=1)
    o_ref[...] = _rms(x1_ref[...] + moe, gf_ref[...])


def _combine(x1, rw, gf, y_tiles):
    tm = COMB_TM
    row = lambda i: (i, 0)
    return pl.pallas_call(
        _combine_kernel,
        grid=(T // tm,),
        in_specs=[
            pl.BlockSpec((tm, D_MODEL), row),
            pl.BlockSpec((tm, LANES), row),
            pl.BlockSpec((1, D_MODEL), lambda i: (0, 0)),
            pl.BlockSpec((tm * 2 * ROW_TILE, LANES), row),
        ],
        out_specs=pl.BlockSpec((tm, D_MODEL), row),
        out_shape=jax.ShapeDtypeStruct((T, D_MODEL), F32),
        compiler_params=_params(("arbitrary",)),
        name="combine",
    )(x1, rw, gf, y_tiles)


def _layer(x2, norm1, w_in, rel_bias, conv_w, g_out_attn, g_out_conv, w_out, norm2,
           w_rg, b_rg, w_re, b_re, w_gate, w_up, w_down):
    proj = _inproj(x2, norm1[None, :], w_in.astype(BF16))
    a_n = _attention(proj, _bias_table(rel_bias), g_out_attn[None, :])
    conv_w8 = jnp.zeros((8, D_CONV), F32).at[0:3].set(conv_w)

    w_r = jnp.zeros((D_MODEL, LANES), F32)
    g0 = GROUP_LOGIT_ROW
    w_r = w_r.at[:, 0:N_EXPERTS].set(w_re).at[:, g0:g0 + N_GROUPS].set(w_rg)
    b_r = jnp.zeros((1, LANES), F32)
    b_r = b_r.at[0, 0:N_EXPERTS].set(b_re).at[0, g0:g0 + N_GROUPS].set(b_rg)
    x1, hp, ri_t, rw, cnt = _outproj(a_n, proj, conv_w8, g_out_conv[None, :], x2,
                                     w_out.astype(BF16), norm2[None, :], w_r.astype(BF16), b_r)

    dest_t, be, seg = _dest(ri_t, cnt)
    y_tiles = _moe(be[0], dest_t[0], dest_t[1], seg[0:N_EXPERTS, 0],
                   seg[N_EXPERTS:2 * N_EXPERTS, 0], hp, w_gate, w_up, w_down)
    return x1, rw, y_tiles


def kernel(x, norm1, w_in, rel_bias, conv_w, g_out_attn, g_out_conv, w_out, norm2,
           w_router_group, b_router_group, w_router_expert, b_router_expert,
           w_gate, w_up, w_down, norm_final):
    assert x.shape == (BATCH, SEQ, D_MODEL) and norm1.shape[0] == 1
    x2 = x.reshape(T, D_MODEL)
    x1, rw, y_tiles = _layer(
        x2, norm1[0], w_in[0], rel_bias[0], conv_w[0], g_out_attn[0], g_out_conv[0],
        w_out[0], norm2[0], w_router_group[0], b_router_group[0], w_router_expert[0],
        b_router_expert[0], w_gate[0], w_up[0], w_down[0])
    out = _combine(x1, rw, norm_final[None, :], y_tiles)
    return out.reshape(BATCH, SEQ, D_MODEL)
```

```python
import functools

import numpy as np
import jax
import jax.numpy as jnp
from jax import lax
from jax.experimental import pallas as pl
from jax.experimental.pallas import tpu as pltpu

F32 = jnp.float32
BF16 = jnp.bfloat16
I32 = jnp.int32

D_MODEL = 2048
BATCH = 4
SEQ = 4096
T = BATCH * SEQ
CHUNK = 64
LEFT_CHUNKS = 8
BAND = LEFT_CHUNKS + 1
KEYS = BAND * CHUNK
D_ATTN = 1024
D_CONV = 1024
HEAD_DIM = 64
N_HEADS = 16
N_PAIRS = N_HEADS // 2
REL_CLIP = 256
D_IN_PROJ = 6 * 1024
N_GROUPS = 4
EPG = 8
N_EXPERTS = 32
D_EXPERT = 512
MOE_BLOCK = 256
MOE_BLOCK_SHIFT = 8
N_ROUTED = 2 * T
N_BLOCKS = N_ROUTED // MOE_BLOCK + N_EXPERTS
EPS = 1e-6
NEG_INF = -1e30
LANES = 128
VMEM_LIMIT = 52 * 1024 * 1024

ATTN_TILE = 512
ATTN_TILES_PER_SEQ = SEQ // ATTN_TILE


def _params(sem):
    return pltpu.CompilerParams(dimension_semantics=sem, vmem_limit_bytes=VMEM_LIMIT)


def _rms(x, g):
    ms = jnp.mean(x * x, axis=-1, keepdims=True)
    return x * lax.rsqrt(ms + EPS) * g


INPROJ_TM = 1024
INPROJ_TN = 1536


def _inproj_kernel(x_ref, g_ref, w_ref, o_ref, hn_ref):
    @pl.when(pl.program_id(1) == 0)
    def _():
        hn_ref[...] = _rms(x_ref[...], g_ref[...]).astype(BF16)

    o_ref[...] = jnp.dot(hn_ref[...], w_ref[...], preferred_element_type=F32).astype(BF16)


def _inproj(x2, g, w_bf):
    tm, tn = INPROJ_TM, INPROJ_TN
    return pl.pallas_call(
        _inproj_kernel,
        grid=(T // tm, D_IN_PROJ // tn),
        in_specs=[
            pl.BlockSpec((tm, D_MODEL), lambda i, j: (i, 0)),
            pl.BlockSpec((1, D_MODEL), lambda i, j: (0, 0)),
            pl.BlockSpec((D_MODEL, tn), lambda i, j: (0, j)),
        ],
        out_specs=pl.BlockSpec((tm, tn), lambda i, j: (i, j)),
        out_shape=jax.ShapeDtypeStruct((T, D_IN_PROJ), BF16),
        scratch_shapes=[pltpu.VMEM((tm, D_MODEL), BF16)],
        compiler_params=_params(("arbitrary", "arbitrary")),
        name="inproj",
    )(x2, g, w_bf)


N_QUADS = N_HEADS // 4
QUAD = 4 * HEAD_DIM
BIAS_W = 640
BIAS_WIN = 512
P_BUFS = 4


def _bias_kernel(b_ref, o_ref):
    first_head = lax.broadcasted_iota(I32, (KEYS, LANES), 1) < HEAD_DIM
    for hq in range(N_QUADS):
        for col in range(2):
            h0 = 4 * hq + 2 * col
            pieces = []
            for hh in range(2):
                x = jnp.broadcast_to(b_ref[h0 + hh:h0 + hh + 1, :], (KEYS, BIAS_W))
                rolled = pltpu.roll(x, 0, 1, stride=1, stride_axis=0)
                pieces.append(rolled[:, BIAS_WIN:BIAS_WIN + LANES])
            o_ref[hq, :, col * LANES:(col + 1) * LANES] = jnp.where(
                first_head, pieces[0], pieces[1])


def _bias_table(rel_bias):
    r = rel_bias.astype(F32)
    edge = jnp.broadcast_to(r[:, 2 * REL_CLIP:], (N_HEADS, KEYS - REL_CLIP + 1))
    b = jnp.concatenate([r[:, REL_CLIP:2 * REL_CLIP], edge,
                         r[:, REL_CLIP - (BIAS_W - KEYS - 1):REL_CLIP]], axis=1)
    odd = jnp.concatenate([b[:, BIAS_W - HEAD_DIM:], b[:, :BIAS_W - HEAD_DIM]], axis=1)
    is_odd = (jnp.arange(N_HEADS) % 2 == 1)[:, None]
    bvec = jnp.where(is_odd, odd, b)
    return pl.pallas_call(
        _bias_kernel,
        out_shape=jax.ShapeDtypeStruct((N_QUADS, KEYS, QUAD), F32),
        compiler_params=pltpu.CompilerParams(vmem_limit_bytes=VMEM_LIMIT),
        name="biastab",
    )(bvec)


def _attn_kernel(q_ref, kp_ref, kc_ref, vp_ref, vc_ref, bias_ref, g_ref, o_ref,
                 kw_ref, vw_ref, st_ref, p_ref, a_ref):
    i = pl.program_id(1)
    kw_ref[0:ATTN_TILE, :] = kp_ref[...]
    kw_ref[ATTN_TILE:2 * ATTN_TILE, :] = kc_ref[...]
    vw_ref[0:ATTN_TILE, :] = vp_ref[...]
    vw_ref[ATTN_TILE:2 * ATTN_TILE, :] = vc_ref[...]

    head_of_lane = lax.broadcasted_iota(I32, (CHUNK, QUAD), 1) // HEAD_DIM
    first_head = lax.broadcasted_iota(I32, (CHUNK, LANES), 1) < HEAD_DIM
    key_row = lax.broadcasted_iota(I32, (KEYS, QUAD), 0)
    n_rb = KEYS // CHUNK

    all_quads = tuple(range(N_QUADS))
    all_pairs = tuple(range(N_PAIRS))

    def scores(r0, buf, masked, quads=all_quads):
        for hq in quads:
            c0 = hq * QUAD
            q4 = q_ref[pl.ds(r0, CHUNK), c0:c0 + QUAD] * (HEAD_DIM ** -0.5)
            zero = jnp.zeros_like(q4)
            qbd = jnp.concatenate(
                [jnp.where(head_of_lane == h, q4, zero) for h in range(4)], axis=0)
            k4 = kw_ref[pl.ds(r0, KEYS), c0:c0 + QUAD]
            st = lax.dot_general(k4, qbd, (((1,), (1,)), ((), ())),
                                 preferred_element_type=F32)
            st = st + bias_ref[hq]
            if masked:
                st = jnp.where(key_row + r0 >= ATTN_TILE, st, NEG_INF)
            st_ref[buf, hq] = st

    def exps(buf, pbuf, off, quads=all_quads):
        for hq in quads:
            m = st_ref[buf, hq, 0:CHUNK, :]
            for rb in range(1, n_rb):
                m = jnp.maximum(m, st_ref[buf, hq, rb * CHUNK:(rb + 1) * CHUNK, :])
            m = jnp.max(m, axis=0, keepdims=True)
            for rb in range(n_rb):
                rows = slice(rb * CHUNK, (rb + 1) * CHUNK)
                prow = slice(off + rb * CHUNK, off + (rb + 1) * CHUNK)
                p_ref[pbuf, hq, prow, :] = jnp.exp(st_ref[buf, hq, rows, :] - m).astype(BF16)

    def values(r0, pbufs, pairs=all_pairs):
        nq = len(pbufs)
        span = KEYS + (nq - 1) * CHUNK
        ones = jnp.ones((span, LANES), BF16)
        for hp in pairs:
            c0 = hp * LANES
            half = (hp % 2) * LANES
            pt = jnp.concatenate(
                [p_ref[b, hp // 2, 0:span, half:half + LANES] for b in pbufs], axis=1)
            vext = jnp.concatenate([vw_ref[pl.ds(r0, span), c0:c0 + LANES], ones], axis=1)
            oe = lax.dot_general(pt, vext, (((0,), (0,)), ((), ())),
                                 preferred_element_type=F32)
            o = oe[:, :LANES] * (1.0 / oe[:, LANES:])
            for j in range(nq):
                oj = o[2 * j * CHUNK:2 * (j + 1) * CHUNK]
                a_ref[pl.ds(r0 + j * CHUNK, CHUNK), c0:c0 + LANES] = jnp.where(
                    first_head, oj[0:CHUNK], oj[CHUNK:2 * CHUNK])

    n_chunks = ATTN_TILE // CHUNK

    def tile(masked):
        for b in range(P_BUFS):
            z = slice(KEYS, KEYS + CHUNK) if b % 2 == 0 else slice(0, CHUNK)
            p_ref[b, :, z, :] = jnp.zeros((N_QUADS, CHUNK, QUAD), BF16)
        scores(0, 0, masked)
        for ci in range(n_chunks):
            c0 = ci - 2 - ci % 2
            for hq in all_quads:
                if ci + 1 < n_chunks:
                    scores((ci + 1) * CHUNK, (ci + 1) % 2, masked, (hq,))
                if c0 >= 0:
                    values(c0 * CHUNK, (c0 % P_BUFS, (c0 + 1) % P_BUFS),
                           ((ci % 2) * N_QUADS + hq,))
                exps(ci % 2, ci % P_BUFS, (ci % 2) * CHUNK, (hq,))
        values((n_chunks - 2) * CHUNK, ((n_chunks - 2) % P_BUFS, (n_chunks - 1) % P_BUFS))

    pl.when(i == 0)(functools.partial(tile, True))
    pl.when(i > 0)(functools.partial(tile, False))

    o_ref[...] = _rms(a_ref[...], g_ref[...]).astype(BF16)


def _attention(proj, bias_t, g):
    n = ATTN_TILES_PER_SEQ
    cur = lambda col: (lambda b, i: (b * n + i, col))
    prev = lambda col: (lambda b, i: (b * n + jnp.maximum(i - 1, 0), col))
    blk = (ATTN_TILE, D_ATTN)
    return pl.pallas_call(
        _attn_kernel,
        grid=(BATCH, n),
        in_specs=[
            pl.BlockSpec(blk, cur(0)),
            pl.BlockSpec(blk, prev(1)),
            pl.BlockSpec(blk, cur(1)),
            pl.BlockSpec(blk, prev(2)),
            pl.BlockSpec(blk, cur(2)),
            pl.BlockSpec((N_QUADS, KEYS, QUAD), lambda b, i: (0, 0, 0)),
            pl.BlockSpec((1, D_ATTN), lambda b, i: (0, 0)),
        ],
        out_specs=pl.BlockSpec(blk, lambda b, i: (b * n + i, 0)),
        out_shape=jax.ShapeDtypeStruct((T, D_ATTN), BF16),
        scratch_shapes=[
            pltpu.VMEM((2 * ATTN_TILE, D_ATTN), BF16),
            pltpu.VMEM((2 * ATTN_TILE, D_ATTN), BF16),
            pltpu.VMEM((2, N_QUADS, KEYS, QUAD), F32),
            pltpu.VMEM((P_BUFS, N_QUADS, KEYS + CHUNK, QUAD), BF16),
            pltpu.VMEM((ATTN_TILE, D_ATTN), F32),
        ],
        compiler_params=_params(("arbitrary", "arbitrary")),
        name="attn",
    )(proj, proj, proj, proj, proj, bias_t, g)


CONV_PREV = 16


def _short_conv_rows(seq_start, b, c, u, cp, up, w, g):
    z = c.astype(F32) * u.astype(F32)
    zp = cp.astype(F32) * up.astype(F32)
    zp = jnp.where(seq_start, 0.0, zp)
    zm1 = zp[CONV_PREV - 1:CONV_PREV]
    zm2 = zp[CONV_PREV - 2:CONV_PREV - 1]
    row = lax.broadcasted_iota(I32, z.shape, 0)
    z1 = jnp.where(row == 0, zm1, pltpu.roll(z, 1, axis=0))
    z2 = jnp.where(row == 0, zm2, jnp.where(row == 1, zm1, pltpu.roll(z, 2, axis=0)))
    y = w[0:1] * z2 + w[1:2] * z1 + w[2:3] * z
    return _rms(b.astype(F32) * y, g).astype(BF16)


OUT_TM = 512
OUT_SUB = 2


def _pack_bf16_pairs(h):
    half = D_MODEL // 2
    return pltpu.pack_elementwise([h[:, :half], h[:, half:]], packed_dtype=BF16)


def _unpack_pairs_f32(u):
    lo = pltpu.unpack_elementwise(u, index=0, packed_dtype=BF16, unpacked_dtype=F32)
    hi = pltpu.unpack_elementwise(u, index=1, packed_dtype=BF16, unpacked_dtype=F32)
    return lo, hi


ROW_TILE = 8


def _store_row_tiles(ref, packed):
    m = packed.shape[0]
    for s in range(ROW_TILE):
        ref[pl.ds(s, m, stride=ROW_TILE), :] = packed[:, s * LANES:(s + 1) * LANES]


def _load_row_tiles(ref, m, first=0, stride=ROW_TILE):
    return jnp.concatenate(
        [ref[pl.ds(first + s, m, stride=stride), :] for s in range(ROW_TILE)], axis=1)


GROUP_LOGIT_ROW = N_EXPERTS
SUBLANES = 8


def _route_cols(lg, base_ref):
    tm = lg.shape[0]
    lt = lg.T
    sub = lax.broadcasted_iota(I32, (SUBLANES, tm), 0)
    sub_f = sub.astype(F32)
    big = jnp.float32(SUBLANES)
    ninf = jnp.float32(-jnp.inf)

    gl = lt[GROUP_LOGIT_ROW:GROUP_LOGIT_ROW + SUBLANES, :]
    gvalid = sub < N_GROUPS
    gmax = jnp.max(jnp.where(gvalid, gl, ninf), axis=0, keepdims=True)
    gsum = jnp.sum(jnp.where(gvalid, jnp.exp(gl - gmax), 0.0), axis=0, keepdims=True)
    p_g = 1.0 / gsum
    g_sel = jnp.min(jnp.where(jnp.logical_and(gvalid, gl == gmax), sub_f, big),
                    axis=0, keepdims=True)
    el = lt[0:EPG, :]
    for g in range(1, N_GROUPS):
        el = jnp.where(g_sel == g, lt[g * EPG:(g + 1) * EPG, :], el)
    v1 = jnp.max(el, axis=0, keepdims=True)
    i1 = jnp.min(jnp.where(el == v1, sub_f, big), axis=0, keepdims=True)
    rest = sub_f != i1
    v2 = jnp.max(jnp.where(rest, el, ninf), axis=0, keepdims=True)
    i2 = jnp.min(jnp.where(jnp.logical_and(rest, el == v2), sub_f, big),
                 axis=0, keepdims=True)
    t = jnp.exp(v2 - v1)
    w1 = p_g * (1.0 / (1.0 + t))
    w2 = p_g * (t / (1.0 + t))
    e1 = g_sel * EPG + i1
    e2 = g_sel * EPG + i2

    erow = lax.broadcasted_iota(I32, (N_EXPERTS, tm), 0).astype(F32)
    hit1 = erow == e1
    hit2 = erow == e2
    onehot = jnp.where(jnp.logical_or(hit1, hit2), 1.0, 0.0)
    r = lax.broadcasted_iota(I32, (tm, tm), 0)
    c = lax.broadcasted_iota(I32, (tm, tm), 1)
    tri = jnp.where(r < c, 1.0, 0.0).astype(BF16)
    base = base_ref[...]
    before = (jnp.dot(onehot.astype(BF16), tri, preferred_element_type=F32)
              + jnp.concatenate([base] * (tm // LANES), axis=1))
    r1 = jnp.sum(jnp.where(hit1, before, 0.0), axis=0, keepdims=True)
    r2 = jnp.sum(jnp.where(hit2, before, 0.0), axis=0, keepdims=True)
    base_ref[...] = base + jnp.broadcast_to(
        jnp.sum(onehot, axis=1, keepdims=True), (N_EXPERTS, LANES))

    ri_t = jnp.where(sub == 0, e1, jnp.where(sub == 1, e2, jnp.where(sub == 2, r1, r2)))
    w_t = jnp.where(sub == 0, w1, jnp.where(sub == 1, w2, 0.0))
    w_t = jnp.concatenate([w_t, jnp.zeros((LANES - SUBLANES, tm), F32)], axis=0)
    return ri_t.astype(I32), w_t.T


def _outproj_kernel(a_ref, b_ref, c_ref, u_ref, cp_ref, up_ref, cw_ref, gc_ref,
                    x_ref, wo_ref, g2_ref, wr_ref, br_ref,
                    x1_ref, hp_ref, ri_ref, rw_ref, cnt_ref, base_ref, cn_ref):
    i = pl.program_id(0)

    @pl.when(i == 0)
    def _():
        base_ref[...] = jnp.zeros_like(base_ref)

    sub = OUT_TM // OUT_SUB
    rows_of = lambda s: slice(s * sub, (s + 1) * sub)

    def conv(s):
        rows = rows_of(s)
        if s == 0:
            seq_start = (i % (SEQ // OUT_TM)) == 0
            cp, up = cp_ref[...], up_ref[...]
        else:
            seq_start = False
            before = slice(s * sub - CONV_PREV, s * sub)
            cp, up = c_ref[before, :], u_ref[before, :]
        cn_ref[rows, :] = _short_conv_rows(
            seq_start, b_ref[rows, :], c_ref[rows, :], u_ref[rows, :],
            cp, up, cw_ref[...], gc_ref[...])

    def project(s):
        rows = rows_of(s)
        acc = jnp.dot(a_ref[rows, :], wo_ref[0:D_ATTN, :], preferred_element_type=F32)
        acc = acc + jnp.dot(cn_ref[rows, :], wo_ref[D_ATTN:, :], preferred_element_type=F32)
        x1_ref[rows, :] = x_ref[rows, :] + acc

    def route(s):
        rows = rows_of(s)
        h2 = _rms(x1_ref[rows, :], g2_ref[...])
        logits = jnp.dot(h2.astype(BF16), wr_ref[...],
                         preferred_element_type=F32) + br_ref[...]
        _store_row_tiles(hp_ref.at[pl.ds(s * sub * ROW_TILE, sub * ROW_TILE), :],
                         _pack_bf16_pairs(h2))
        ri_t, rw = _route_cols(logits, base_ref)
        ri_ref[:, rows] = ri_t
        rw_ref[rows, :] = rw

    conv(0)
    for s in range(OUT_SUB):
        project(s)
        if s + 1 < OUT_SUB:
            conv(s + 1)
        if s > 0:
            route(s - 1)
    route(OUT_SUB - 1)
    cnt_ref[...] = base_ref[...]


def _outproj(a_n, proj, conv_w, g_conv, x2, wo_bf, g2, wr_bf, br):
    tm = OUT_TM
    per = tm // CONV_PREV
    row = lambda i: (i, 0)
    fixed = lambda i: (0, 0)
    col = lambda c: (lambda i: (i, c))
    prev = lambda c: (lambda i: (jnp.maximum(i * per - 1, 0), c))
    return pl.pallas_call(
        _outproj_kernel,
        grid=(T // tm,),
        in_specs=[
            pl.BlockSpec((tm, D_ATTN), row),
            pl.BlockSpec((tm, D_CONV), col(3)),
            pl.BlockSpec((tm, D_CONV), col(4)),
            pl.BlockSpec((tm, D_CONV), col(5)),
            pl.BlockSpec((CONV_PREV, D_CONV), prev(4)),
            pl.BlockSpec((CONV_PREV, D_CONV), prev(5)),
            pl.BlockSpec((8, D_CONV), fixed),
            pl.BlockSpec((1, D_CONV), fixed),
            pl.BlockSpec((tm, D_MODEL), row),
            pl.BlockSpec((D_MODEL, D_MODEL), fixed),
            pl.BlockSpec((1, D_MODEL), fixed),
            pl.BlockSpec((D_MODEL, LANES), fixed),
            pl.BlockSpec((1, LANES), fixed),
        ],
        out_specs=[
            pl.BlockSpec((tm, D_MODEL), row),
            pl.BlockSpec((tm * ROW_TILE, LANES), row),
            pl.BlockSpec((SUBLANES, tm), lambda i: (0, i)),
            pl.BlockSpec((tm, LANES), row),
            pl.BlockSpec((N_EXPERTS, LANES), fixed),
        ],
        out_shape=[
            jax.ShapeDtypeStruct((T, D_MODEL), F32),
            jax.ShapeDtypeStruct((T * ROW_TILE, LANES), I32),
            jax.ShapeDtypeStruct((SUBLANES, T), I32),
            jax.ShapeDtypeStruct((T, LANES), F32),
            jax.ShapeDtypeStruct((N_EXPERTS, LANES), F32),
        ],
        scratch_shapes=[pltpu.VMEM((N_EXPERTS, LANES), F32), pltpu.VMEM((tm, D_CONV), BF16)],
        compiler_params=_params(("arbitrary",)),
        name="outproj",
    )(a_n, proj, proj, proj, proj, proj, conv_w, g_conv, x2, wo_bf, g2, wr_bf, br)


BLOCK_LANES = 384
DEST_TM = 2048


def _dest_kernel(ri_ref, cnt_ref, dest_ref, be_ref, seg_ref):
    tm = DEST_TM
    erow = lax.broadcasted_iota(I32, (N_EXPERTS, LANES), 0)
    cnt = cnt_ref[...].astype(I32)
    pcnt = ((cnt + (MOE_BLOCK - 1)) >> MOE_BLOCK_SHIFT) << MOE_BLOCK_SHIFT
    pend = pcnt
    for s in (1, 2, 4, 8, 16):
        pend = pend + jnp.where(erow >= s, pltpu.roll(pend, s, axis=0), 0)
    pstart = (pend - pcnt).astype(F32)

    ri = ri_ref[...].astype(F32)
    etok = lax.broadcasted_iota(I32, (N_EXPERTS, tm), 0).astype(F32)
    pstart_t = jnp.concatenate([pstart] * (tm // LANES), axis=1)
    d1 = jnp.sum(jnp.where(etok == ri[0:1], pstart_t, 0.0), axis=0, keepdims=True) + ri[2:3]
    d2 = jnp.sum(jnp.where(etok == ri[1:2], pstart_t, 0.0), axis=0, keepdims=True) + ri[3:4]
    sub = lax.broadcasted_iota(I32, (SUBLANES, tm), 0)
    dest_ref[...] = jnp.where(sub == 0, d1, d2).astype(I32)

    pend_f = pend.astype(F32)
    pend_b = jnp.concatenate([pend_f] * (BLOCK_LANES // LANES), axis=1)
    blk_lane = lax.broadcasted_iota(I32, (1, BLOCK_LANES), 1)
    le = jnp.where(pend_b <= (blk_lane * MOE_BLOCK).astype(F32), 1.0, 0.0)
    be = jnp.minimum(jnp.sum(le, axis=0, keepdims=True), float(N_EXPERTS - 1))
    total = pend_b[N_EXPERTS - 1:N_EXPERTS, :]
    be = jnp.where(blk_lane == N_BLOCKS, total * (1.0 / MOE_BLOCK), be)
    be_ref[...] = jnp.broadcast_to(be, (SUBLANES, BLOCK_LANES)).astype(I32)
    seg_ref[0:N_EXPERTS, :] = pend - pcnt + cnt
    seg_ref[N_EXPERTS:2 * N_EXPERTS, :] = pend


def _dest(ri_t, cnt):
    tm = DEST_TM
    fixed = lambda i: (0, 0)
    tok = lambda i: (0, i)
    return pl.pallas_call(
        _dest_kernel,
        grid=(T // tm,),
        in_specs=[pl.BlockSpec((SUBLANES, tm), tok), pl.BlockSpec((N_EXPERTS, LANES), fixed)],
        out_specs=[pl.BlockSpec((SUBLANES, tm), tok),
                   pl.BlockSpec((SUBLANES, BLOCK_LANES), fixed),
                   pl.BlockSpec((2 * N_EXPERTS, LANES), fixed)],
        out_shape=[jax.ShapeDtypeStruct((SUBLANES, T), I32),
                   jax.ShapeDtypeStruct((SUBLANES, BLOCK_LANES), I32),
                   jax.ShapeDtypeStruct((2 * N_EXPERTS, LANES), I32)],
        compiler_params=_params(("arbitrary",)),
        name="dest",
    )(ri_t, cnt)


GROUP = 3
INV_LEN = (-(-N_BLOCKS // GROUP) * GROUP + GROUP - 1) * MOE_BLOCK
DUMP_ROWS = 4 * MOE_BLOCK


PAD_STEP = 8


def _invert_permutation(d1_ref, d2_ref, vend_ref, pend_ref, inv_ref, pos_end):
    def pad_range(lo, hi):
        def body(k, carry):
            p0 = lo + k * PAD_STEP
            for j in range(PAD_STEP):
                inv_ref[p0 + j] = N_ROUTED + ((p0 + j) & (DUMP_ROWS - 1))
            return carry

        lax.fori_loop(0, (hi - lo + (PAD_STEP - 1)) // PAD_STEP, body, 0)

    def pad_expert(e, carry):
        pad_range(vend_ref[e], pend_ref[e])
        return carry

    lax.fori_loop(0, N_EXPERTS, pad_expert, 0)
    pad_range(pend_ref[N_EXPERTS - 1], pos_end)

    def place(t, carry):
        inv_ref[d1_ref[t]] = 2 * t
        inv_ref[d2_ref[t]] = 2 * t + 1
        return carry

    lax.fori_loop(0, T, place, 0, unroll=16)


BLOCK_TILE_ROWS = MOE_BLOCK * ROW_TILE
Y_ROWS = (N_ROUTED + DUMP_ROWS) * ROW_TILE


def _moe_kernel(meta_ref, d1_ref, d2_ref, vend_ref, pend_ref, h_ref, wg_hbm, wu_hbm, wd_hbm, y_ref,
                xb0, xb1, xb2, yb0, yb1, yb2, wg_s, wu_s, wd_s, wg_b, wu_b, wd_b, inv_ref,
                sem_x, sem_y, sem_w):
    xbufs = (xb0, xb1, xb2)
    ybufs = (yb0, yb1, yb2)
    n_used = meta_ref[N_BLOCKS]
    n_groups = (n_used + (GROUP - 1)) // GROUP
    half = D_MODEL // 2

    def issue_gather(b, v):
        for r in range(MOE_BLOCK):
            tok = (inv_ref[b * MOE_BLOCK + r] >> 1) & (T - 1)
            src = h_ref.at[pl.ds(pl.multiple_of(tok * ROW_TILE, ROW_TILE), ROW_TILE), :]
            pltpu.make_async_copy(src, xbufs[v].at[pl.ds(r * ROW_TILE, ROW_TILE), :],
                                  sem_x.at[v]).start()

    def wait_gather(v):
        pltpu.make_async_copy(h_ref.at[pl.ds(0, BLOCK_TILE_ROWS), :], xbufs[v],
                              sem_x.at[v]).wait()

    def issue_scatter(b, v):
        for r in range(MOE_BLOCK):
            slot = inv_ref[b * MOE_BLOCK + r]
            dst = y_ref.at[pl.ds(pl.multiple_of(slot * ROW_TILE, ROW_TILE), ROW_TILE), :]
            pltpu.make_async_copy(ybufs[v].at[pl.ds(r * ROW_TILE, ROW_TILE), :], dst,
                                  sem_y.at[v]).start(priority=1)

    def wait_scatter(v):
        pltpu.make_async_copy(ybufs[v], y_ref.at[pl.ds(0, BLOCK_TILE_ROWS), :],
                              sem_y.at[v]).wait()

    def weight_copies(e):
        return (pltpu.make_async_copy(wg_hbm.at[e], wg_s, sem_w),
                pltpu.make_async_copy(wu_hbm.at[e], wu_s, sem_w),
                pltpu.make_async_copy(wd_hbm.at[e], wd_s, sem_w))

    def load_expert(b):
        e = meta_ref[b]
        prev = meta_ref[jnp.maximum(b - 1, 0)]
        first = jnp.logical_and(b < n_used, jnp.logical_or(b == 0, e != prev))

        @pl.when(first)
        def _():
            for c in weight_copies(e):
                c.wait()
            wg_b[...] = wg_s[...].astype(BF16)
            wu_b[...] = wu_s[...].astype(BF16)
            wd_b[...] = wd_s[...].astype(BF16)
            nxt = lax.while_loop(
                lambda j: jnp.logical_and(j < n_used, meta_ref[jnp.minimum(j, N_BLOCKS - 1)] == e),
                lambda j: j + 1, b + 1)

            @pl.when(nxt < n_used)
            def _():
                for c in weight_copies(meta_ref[nxt]):
                    c.start(priority=1)

    def compute(v):
        lo, hi = _unpack_pairs_f32(_load_row_tiles(xbufs[v], MOE_BLOCK))
        lo, hi = lo.astype(BF16), hi.astype(BF16)
        g = (jnp.dot(lo, wg_b[0:half, :], preferred_element_type=F32)
             + jnp.dot(hi, wg_b[half:, :], preferred_element_type=F32))
        u = (jnp.dot(lo, wu_b[0:half, :], preferred_element_type=F32)
             + jnp.dot(hi, wu_b[half:, :], preferred_element_type=F32))
        a = g * (1.0 / (1.0 + jnp.exp(-g))) * u
        y = jnp.dot(a.astype(BF16), wd_b[...], preferred_element_type=F32)
        _store_row_tiles(ybufs[v], _pack_bf16_pairs(y))

    def block(b, v, scatter_prev, wait_y):
        wait_gather(v)
        if wait_y:
            wait_scatter(v)
        load_expert(b)
        issue_gather(b + 2, (v + 2) % GROUP)
        if scatter_prev:
            issue_scatter(b - 1, (v + 2) % GROUP)
        compute(v)

    for c in weight_copies(meta_ref[0]):
        c.start()
    yb2[...] = jnp.zeros_like(yb2)
    dump_fills = [
        pltpu.make_async_copy(
            yb2, y_ref.at[pl.ds((N_ROUTED + k * MOE_BLOCK) * ROW_TILE, BLOCK_TILE_ROWS), :],
            sem_y.at[GROUP - 1])
        for k in range(DUMP_ROWS // MOE_BLOCK)]
    for c in dump_fills:
        c.start()
    _invert_permutation(d1_ref, d2_ref, vend_ref, pend_ref, inv_ref,
                        (n_groups * GROUP + GROUP - 1) * MOE_BLOCK)
    for c in dump_fills:
        c.wait()
    issue_gather(0, 0)
    issue_gather(1, 1)
    for v in range(GROUP):
        block(v, v, scatter_prev=v > 0, wait_y=False)

    def group(gi, carry):
        for v in range(GROUP):
            block(gi * GROUP + v, v, scatter_prev=True, wait_y=True)
        return carry

    lax.fori_loop(1, n_groups, group, 0)

    last = n_groups * GROUP
    issue_scatter(last - 1, GROUP - 1)
    wait_gather(0)
    wait_gather(1)
    for v in range(GROUP):
        wait_scatter(v)


def _moe(meta, d1, d2, vend, pend, h_tiles, w_gate, w_up, w_down):
    any_spec = pl.BlockSpec(memory_space=pl.ANY)
    xy = pltpu.VMEM((BLOCK_TILE_ROWS, LANES), I32)
    return pl.pallas_call(
        _moe_kernel,
        grid_spec=pltpu.PrefetchScalarGridSpec(
            num_scalar_prefetch=5,
            grid=(1,),
            in_specs=[any_spec, any_spec, any_spec, any_spec],
            out_specs=any_spec,
            scratch_shapes=[
                xy, xy, xy, xy, xy, xy,
                pltpu.VMEM((D_MODEL, D_EXPERT), F32),
                pltpu.VMEM((D_MODEL, D_EXPERT), F32),
                pltpu.VMEM((D_EXPERT, D_MODEL), F32),
                pltpu.VMEM((D_MODEL, D_EXPERT), BF16),
                pltpu.VMEM((D_MODEL, D_EXPERT), BF16),
                pltpu.VMEM((D_EXPERT, D_MODEL), BF16),
                pltpu.SMEM((INV_LEN,), I32),
                pltpu.SemaphoreType.DMA((GROUP,)),
                pltpu.SemaphoreType.DMA((GROUP,)),
                pltpu.SemaphoreType.DMA(()),
            ],
        ),
        out_shape=jax.ShapeDtypeStruct((Y_ROWS, LANES), I32),
        compiler_params=_params(("arbitrary",)),
        name="moe",
    )(meta, d1, d2, vend, pend, h_tiles, w_gate, w_up, w_down)


COMB_TM = 512


def _combine_kernel(x1_ref, rw_ref, gf_ref, y_ref, o_ref):
    tm = COMB_TM
    w = rw_ref[...]
    w0, w1 = w[:, 0:1], w[:, 1:2]
    lo0, hi0 = _unpack_pairs_f32(_load_row_tiles(y_ref, tm, 0, 2 * ROW_TILE))
    lo1, hi1 = _unpack_pairs_f32(_load_row_tiles(y_ref, tm, ROW_TILE, 2 * ROW_TILE))
    moe = jnp.concatenate([lo0 * w0 + lo1 * w1, hi0 * w0 + hi1 * w1], axis=1)
    o_ref[...] = _rms(x1_ref[...] + moe, gf_ref[...])


def _combine(x1, rw, gf, y_tiles):
    tm = COMB_TM
    row = lambda i: (i, 0)
    return pl.pallas_call(
        _combine_kernel,
        grid=(T // tm,),
        in_specs=[
            pl.BlockSpec((tm, D_MODEL), row),
            pl.BlockSpec((tm, LANES), row),
            pl.BlockSpec((1, D_MODEL), lambda i: (0, 0)),
            pl.BlockSpec((tm * 2 * ROW_TILE, LANES), row),
        ],
        out_specs=pl.BlockSpec((tm, D_MODEL), row),
        out_shape=jax.ShapeDtypeStruct((T, D_MODEL), F32),
        compiler_params=_params(("arbitrary",)),
        name="combine",
    )(x1, rw, gf, y_tiles)


def _layer(x2, norm1, w_in, rel_bias, conv_w, g_out_attn, g_out_conv, w_out, norm2,
           w_rg, b_rg, w_re, b_re, w_gate, w_up, w_down):
    proj = _inproj(x2, norm1[None, :], w_in.astype(BF16))
    a_n = _attention(proj, _bias_table(rel_bias), g_out_attn[None, :])
    conv_w8 = jnp.zeros((8, D_CONV), F32).at[0:3].set(conv_w)

    w_r = jnp.zeros((D_MODEL, LANES), F32)
    g0 = GROUP_LOGIT_ROW
    w_r = w_r.at[:, 0:N_EXPERTS].set(w_re).at[:, g0:g0 + N_GROUPS].set(w_rg)
    b_r = jnp.zeros((1, LANES), F32)
    b_r = b_r.at[0, 0:N_EXPERTS].set(b_re).at[0, g0:g0 + N_GROUPS].set(b_rg)
    x1, hp, ri_t, rw, cnt = _outproj(a_n, proj, conv_w8, g_out_conv[None, :], x2,
                                     w_out.astype(BF16), norm2[None, :], w_r.astype(BF16), b_r)

    dest_t, be, seg = _dest(ri_t, cnt)
    y_tiles = _moe(be[0], dest_t[0], dest_t[1], seg[0:N_EXPERTS, 0],
                   seg[N_EXPERTS:2 * N_EXPERTS, 0], hp, w_gate, w_up, w_down)
    return x1, rw, y_tiles


def kernel(x, norm1, w_in, rel_bias, conv_w, g_out_attn, g_out_conv, w_out, norm2,
           w_router_group, b_router_group, w_router_expert, b_router_expert,
           w_gate, w_up, w_down, norm_final):
    assert x.shape == (BATCH, SEQ, D_MODEL) and norm1.shape[0] == 1
    x2 = x.reshape(T, D_MODEL)
    x1, rw, y_tiles = _layer(
        x2, norm1[0], w_in[0], rel_bias[0], conv_w[0], g_out_attn[0], g_out_conv[0],
        w_out[0], norm2[0], w_router_group[0], b_router_group[0], w_router_expert[0],
        b_router_expert[0], w_gate[0], w_up[0], w_down[0])
    out = _combine(x1, rw, norm_final[None, :], y_tiles)
    return out.reshape(BATCH, SEQ, D_MODEL)
```

```python
import functools

import numpy as np
import jax
import jax.numpy as jnp
from jax import lax
from jax.experimental import pallas as pl
from jax.experimental.pallas import tpu as pltpu

F32 = jnp.float32
BF16 = jnp.bfloat16
I32 = jnp.int32

D_MODEL = 2048
BATCH = 4
SEQ = 4096
T = BATCH * SEQ
CHUNK = 64
LEFT_CHUNKS = 8
BAND = LEFT_CHUNKS + 1
KEYS = BAND * CHUNK
D_ATTN = 1024
D_CONV = 1024
HEAD_DIM = 64
N_HEADS = 16
N_PAIRS = N_HEADS // 2
REL_CLIP = 256
D_IN_PROJ = 6 * 1024
N_GROUPS = 4
EPG = 8
N_EXPERTS = 32
D_EXPERT = 512
MOE_BLOCK = 256
MOE_BLOCK_SHIFT = 8
N_ROUTED = 2 * T
N_BLOCKS = N_ROUTED // MOE_BLOCK + N_EXPERTS
EPS = 1e-6
NEG_INF = -1e30
LANES = 128
VMEM_LIMIT = 52 * 1024 * 1024

ATTN_TILE = 512
ATTN_TILES_PER_SEQ = SEQ // ATTN_TILE


def _params(sem):
    return pltpu.CompilerParams(dimension_semantics=sem, vmem_limit_bytes=VMEM_LIMIT)


def _rms(x, g):
    ms = jnp.mean(x * x, axis=-1, keepdims=True)
    return x * lax.rsqrt(ms + EPS) * g


INPROJ_TM = 512
INPROJ_TN = 1536


def _inproj_kernel(x_ref, g_ref, w_ref, o_ref):
    hn = _rms(x_ref[...], g_ref[...]).astype(BF16)
    for j in range(D_IN_PROJ // INPROJ_TN):
        cols = slice(j * INPROJ_TN, (j + 1) * INPROJ_TN)
        o_ref[:, cols] = jnp.dot(hn, w_ref[:, cols], preferred_element_type=F32).astype(BF16)


def _inproj(x2, g, w_bf):
    tm = INPROJ_TM
    return pl.pallas_call(
        _inproj_kernel,
        grid=(T // tm,),
        in_specs=[
            pl.BlockSpec((tm, D_MODEL), lambda i: (i, 0)),
            pl.BlockSpec((1, D_MODEL), lambda i: (0, 0)),
            pl.BlockSpec((D_MODEL, D_IN_PROJ), lambda i: (0, 0), pipeline_mode=pl.Buffered(1)),
        ],
        out_specs=pl.BlockSpec((tm, D_IN_PROJ), lambda i: (i, 0)),
        out_shape=jax.ShapeDtypeStruct((T, D_IN_PROJ), BF16),
        compiler_params=_params(("arbitrary",)),
        name="inproj",
    )(x2, g, w_bf)


N_QUADS = N_HEADS // 4
QUAD = 4 * HEAD_DIM
BIAS_W = 640
BIAS_WIN = 512
P_BUFS = 4


def _bias_kernel(b_ref, o_ref):
    first_head = lax.broadcasted_iota(I32, (KEYS, LANES), 1) < HEAD_DIM
    for hq in range(N_QUADS):
        for col in range(2):
            h0 = 4 * hq + 2 * col
            pieces = []
            for hh in range(2):
                x = jnp.broadcast_to(b_ref[h0 + hh:h0 + hh + 1, :], (KEYS, BIAS_W))
                rolled = pltpu.roll(x, 0, 1, stride=1, stride_axis=0)
                pieces.append(rolled[:, BIAS_WIN:BIAS_WIN + LANES])
            o_ref[hq, :, col * LANES:(col + 1) * LANES] = jnp.where(
                first_head, pieces[0], pieces[1])


def _bias_table(rel_bias):
    r = rel_bias.astype(F32)
    edge = jnp.broadcast_to(r[:, 2 * REL_CLIP:], (N_HEADS, KEYS - REL_CLIP + 1))
    b = jnp.concatenate([r[:, REL_CLIP:2 * REL_CLIP], edge,
                         r[:, REL_CLIP - (BIAS_W - KEYS - 1):REL_CLIP]], axis=1)
    odd = jnp.concatenate([b[:, BIAS_W - HEAD_DIM:], b[:, :BIAS_W - HEAD_DIM]], axis=1)
    is_odd = (jnp.arange(N_HEADS) % 2 == 1)[:, None]
    bvec = jnp.where(is_odd, odd, b)
    return pl.pallas_call(
        _bias_kernel,
        out_shape=jax.ShapeDtypeStruct((N_QUADS, KEYS, QUAD), F32),
        compiler_params=pltpu.CompilerParams(vmem_limit_bytes=VMEM_LIMIT),
        name="biastab",
    )(bvec)


def _attn_kernel(q_ref, kp_ref, kc_ref, vp_ref, vc_ref, bias_ref, g_ref, o_ref,
                 kw_ref, vw_ref, st_ref, p_ref, a_ref):
    i = pl.program_id(1)
    kw_ref[0:ATTN_TILE, :] = kp_ref[...]
    kw_ref[ATTN_TILE:2 * ATTN_TILE, :] = kc_ref[...]
    vw_ref[0:ATTN_TILE, :] = vp_ref[...]
    vw_ref[ATTN_TILE:2 * ATTN_TILE, :] = vc_ref[...]

    head_of_lane = lax.broadcasted_iota(I32, (CHUNK, QUAD), 1) // HEAD_DIM
    first_head = lax.broadcasted_iota(I32, (CHUNK, LANES), 1) < HEAD_DIM
    key_row = lax.broadcasted_iota(I32, (KEYS, QUAD), 0)
    n_rb = KEYS // CHUNK

    all_quads = tuple(range(N_QUADS))
    all_pairs = tuple(range(N_PAIRS))

    def scores(r0, buf, masked, quads=all_quads):
        for hq in quads:
            c0 = hq * QUAD
            q4 = q_ref[pl.ds(r0, CHUNK), c0:c0 + QUAD] * (HEAD_DIM ** -0.5)
            zero = jnp.zeros_like(q4)
            qbd = jnp.concatenate(
                [jnp.where(head_of_lane == h, q4, zero) for h in range(4)], axis=0)
            k4 = kw_ref[pl.ds(r0, KEYS), c0:c0 + QUAD]
            st = lax.dot_general(k4, qbd, (((1,), (1,)), ((), ())),
                                 preferred_element_type=F32)
            st = st + bias_ref[hq]
            if masked:
                st = jnp.where(key_row + r0 >= ATTN_TILE, st, NEG_INF)
            st_ref[buf, hq] = st

    def exps(buf, pbuf, off, quads=all_quads):
        for hq in quads:
            m = st_ref[buf, hq, 0:CHUNK, :]
            for rb in range(1, n_rb):
                m = jnp.maximum(m, st_ref[buf, hq, rb * CHUNK:(rb + 1) * CHUNK, :])
            m = jnp.max(m, axis=0, keepdims=True)
            for rb in range(n_rb):
                rows = slice(rb * CHUNK, (rb + 1) * CHUNK)
                prow = slice(off + rb * CHUNK, off + (rb + 1) * CHUNK)
                p_ref[pbuf, hq, prow, :] = jnp.exp(st_ref[buf, hq, rows, :] - m).astype(BF16)

    def values(r0, pbufs, pairs=all_pairs):
        nq = len(pbufs)
        span = KEYS + (nq - 1) * CHUNK
        ones = jnp.ones((span, LANES), BF16)
        for hp in pairs:
            c0 = hp * LANES
            half = (hp % 2) * LANES
            pt = jnp.concatenate(
                [p_ref[b, hp // 2, 0:span, half:half + LANES] for b in pbufs], axis=1)
            vext = jnp.concatenate([vw_ref[pl.ds(r0, span), c0:c0 + LANES], ones], axis=1)
            oe = lax.dot_general(pt, vext, (((0,), (0,)), ((), ())),
                                 preferred_element_type=F32)
            o = oe[:, :LANES] * (1.0 / oe[:, LANES:])
            for j in range(nq):
                oj = o[2 * j * CHUNK:2 * (j + 1) * CHUNK]
                a_ref[pl.ds(r0 + j * CHUNK, CHUNK), c0:c0 + LANES] = jnp.where(
                    first_head, oj[0:CHUNK], oj[CHUNK:2 * CHUNK])

    n_chunks = ATTN_TILE // CHUNK

    def tile(masked):
        for b in range(P_BUFS):
            z = slice(KEYS, KEYS + CHUNK) if b % 2 == 0 else slice(0, CHUNK)
            p_ref[b, :, z, :] = jnp.zeros((N_QUADS, CHUNK, QUAD), BF16)
        scores(0, 0, masked)
        for ci in range(n_chunks):
            c0 = ci - 2 - ci % 2
            for hq in all_quads:
                if ci + 1 < n_chunks:
                    scores((ci + 1) * CHUNK, (ci + 1) % 2, masked, (hq,))
                if c0 >= 0:
                    values(c0 * CHUNK, (c0 % P_BUFS, (c0 + 1) % P_BUFS),
                           ((ci % 2) * N_QUADS + hq,))
                exps(ci % 2, ci % P_BUFS, (ci % 2) * CHUNK, (hq,))
        values((n_chunks - 2) * CHUNK, ((n_chunks - 2) % P_BUFS, (n_chunks - 1) % P_BUFS))

    pl.when(i == 0)(functools.partial(tile, True))
    pl.when(i > 0)(functools.partial(tile, False))

    o_ref[...] = _rms(a_ref[...], g_ref[...]).astype(BF16)


def _attention(proj, bias_t, g):
    n = ATTN_TILES_PER_SEQ
    cur = lambda col: (lambda b, i: (b * n + i, col))
    prev = lambda col: (lambda b, i: (b * n + jnp.maximum(i - 1, 0), col))
    blk = (ATTN_TILE, D_ATTN)
    return pl.pallas_call(
        _attn_kernel,
        grid=(BATCH, n),
        in_specs=[
            pl.BlockSpec(blk, cur(0)),
            pl.BlockSpec(blk, prev(1)),
            pl.BlockSpec(blk, cur(1)),
            pl.BlockSpec(blk, prev(2)),
            pl.BlockSpec(blk, cur(2)),
            pl.BlockSpec((N_QUADS, KEYS, QUAD), lambda b, i: (0, 0, 0)),
            pl.BlockSpec((1, D_ATTN), lambda b, i: (0, 0)),
        ],
        out_specs=pl.BlockSpec(blk, lambda b, i: (b * n + i, 0)),
        out_shape=jax.ShapeDtypeStruct((T, D_ATTN), BF16),
        scratch_shapes=[
            pltpu.VMEM((2 * ATTN_TILE, D_ATTN), BF16),
            pltpu.VMEM((2 * ATTN_TILE, D_ATTN), BF16),
            pltpu.VMEM((2, N_QUADS, KEYS, QUAD), F32),
            pltpu.VMEM((P_BUFS, N_QUADS, KEYS + CHUNK, QUAD), BF16),
            pltpu.VMEM((ATTN_TILE, D_ATTN), F32),
        ],
        compiler_params=_params(("arbitrary", "arbitrary")),
        name="attn",
    )(proj, proj, proj, proj, proj, bias_t, g)


CONV_PREV = 16


def _short_conv_rows(seq_start, b, c, u, cp, up, w, g):
    z = c.astype(F32) * u.astype(F32)
    zp = cp.astype(F32) * up.astype(F32)
    zp = jnp.where(seq_start, 0.0, zp)
    n = z.shape[0]
    ze = jnp.concatenate([zp[CONV_PREV - SUBLANES:], z], axis=0)
    z1 = pltpu.roll(ze, 1, axis=0)[SUBLANES:SUBLANES + n]
    z2 = pltpu.roll(ze, 2, axis=0)[SUBLANES:SUBLANES + n]
    y = w[0:1] * z2 + w[1:2] * z1 + w[2:3] * z
    return _rms(b.astype(F32) * y, g).astype(BF16)


OUT_TM = 512
OUT_SUB = 2


def _pack_bf16_pairs(h):
    half = D_MODEL // 2
    return pltpu.pack_elementwise([h[:, :half], h[:, half:]], packed_dtype=BF16)


def _unpack_pairs_f32(u):
    lo = pltpu.unpack_elementwise(u, index=0, packed_dtype=BF16, unpacked_dtype=F32)
    hi = pltpu.unpack_elementwise(u, index=1, packed_dtype=BF16, unpacked_dtype=F32)
    return lo, hi


ROW_TILE = 8


def _store_row_tiles(ref, packed):
    m = packed.shape[0]
    for s in range(ROW_TILE):
        ref[pl.ds(s, m, stride=ROW_TILE), :] = packed[:, s * LANES:(s + 1) * LANES]


def _load_row_tiles(ref, m, first=0, stride=ROW_TILE):
    return jnp.concatenate(
        [ref[pl.ds(first + s, m, stride=stride), :] for s in range(ROW_TILE)], axis=1)


GROUP_LOGIT_ROW = N_EXPERTS
SUBLANES = 8


def _route_cols(lg, base_ref):
    tm = lg.shape[0]
    lt = lg.T
    sub = lax.broadcasted_iota(I32, (SUBLANES, tm), 0)
    sub_f = sub.astype(F32)
    big = jnp.float32(SUBLANES)
    ninf = jnp.float32(-jnp.inf)

    gl = lt[GROUP_LOGIT_ROW:GROUP_LOGIT_ROW + SUBLANES, :]
    gvalid = sub < N_GROUPS
    gmax = jnp.max(jnp.where(gvalid, gl, ninf), axis=0, keepdims=True)
    gsum = jnp.sum(jnp.where(gvalid, jnp.exp(gl - gmax), 0.0), axis=0, keepdims=True)
    p_g = 1.0 / gsum
    g_sel = jnp.min(jnp.where(jnp.logical_and(gvalid, gl == gmax), sub_f, big),
                    axis=0, keepdims=True)
    el = lt[0:EPG, :]
    for g in range(1, N_GROUPS):
        el = jnp.where(g_sel == g, lt[g * EPG:(g + 1) * EPG, :], el)
    v1 = jnp.max(el, axis=0, keepdims=True)
    i1 = jnp.min(jnp.where(el == v1, sub_f, big), axis=0, keepdims=True)
    rest = sub_f != i1
    v2 = jnp.max(jnp.where(rest, el, ninf), axis=0, keepdims=True)
    i2 = jnp.min(jnp.where(jnp.logical_and(rest, el == v2), sub_f, big),
                 axis=0, keepdims=True)
    t = jnp.exp(v2 - v1)
    w1 = p_g * (1.0 / (1.0 + t))
    w2 = p_g * (t / (1.0 + t))
    e1 = g_sel * EPG + i1
    e2 = g_sel * EPG + i2

    erow = lax.broadcasted_iota(I32, (N_EXPERTS, tm), 0).astype(F32)
    hit1 = erow == e1
    hit2 = erow == e2
    onehot = jnp.where(jnp.logical_or(hit1, hit2), 1.0, 0.0)
    r = lax.broadcasted_iota(I32, (tm, tm), 0)
    c = lax.broadcasted_iota(I32, (tm, tm), 1)
    tri = jnp.where(r < c, 1.0, 0.0).astype(BF16)
    base = base_ref[...]
    before = (jnp.dot(onehot.astype(BF16), tri, preferred_element_type=F32)
              + jnp.concatenate([base] * (tm // LANES), axis=1))
    r1 = jnp.sum(jnp.where(hit1, before, 0.0), axis=0, keepdims=True)
    r2 = jnp.sum(jnp.where(hit2, before, 0.0), axis=0, keepdims=True)
    base_ref[...] = base + jnp.broadcast_to(
        jnp.sum(onehot, axis=1, keepdims=True), (N_EXPERTS, LANES))

    ri_t = jnp.where(sub == 0, e1, jnp.where(sub == 1, e2, jnp.where(sub == 2, r1, r2)))
    w_t = jnp.where(sub == 0, w1, jnp.where(sub == 1, w2, 0.0))
    w_t = jnp.concatenate([w_t, jnp.zeros((LANES - SUBLANES, tm), F32)], axis=0)
    return ri_t.astype(I32), w_t.T


def _outproj_kernel(a_ref, b_ref, c_ref, u_ref, cp_ref, up_ref, cw_ref, gc_ref,
                    x_ref, wo_ref, g2_ref, wr_ref, br_ref,
                    x1_ref, hp_ref, ri_ref, rw_ref, cnt_ref, base_ref, cn_ref):
    i = pl.program_id(0)

    @pl.when(i == 0)
    def _():
        base_ref[...] = jnp.zeros_like(base_ref)

    sub = OUT_TM // OUT_SUB
    rows_of = lambda s: slice(s * sub, (s + 1) * sub)

    def conv(s):
        rows = rows_of(s)
        if s == 0:
            seq_start = (i % (SEQ // OUT_TM)) == 0
            cp, up = cp_ref[...], up_ref[...]
        else:
            seq_start = False
            before = slice(s * sub - CONV_PREV, s * sub)
            cp, up = c_ref[before, :], u_ref[before, :]
        cn_ref[rows, :] = _short_conv_rows(
            seq_start, b_ref[rows, :], c_ref[rows, :], u_ref[rows, :],
            cp, up, cw_ref[...], gc_ref[...])

    def project(s):
        rows = rows_of(s)
        acc = jnp.dot(a_ref[rows, :], wo_ref[0:D_ATTN, :], preferred_element_type=F32)
        acc = acc + jnp.dot(cn_ref[rows, :], wo_ref[D_ATTN:, :], preferred_element_type=F32)
        x1_ref[rows, :] = x_ref[rows, :] + acc

    def route(s):
        rows = rows_of(s)
        h2 = _rms(x1_ref[rows, :], g2_ref[...])
        logits = jnp.dot(h2.astype(BF16), wr_ref[...],
                         preferred_element_type=F32) + br_ref[...]
        _store_row_tiles(hp_ref.at[pl.ds(s * sub * ROW_TILE, sub * ROW_TILE), :],
                         _pack_bf16_pairs(h2))
        ri_t, rw = _route_cols(logits, base_ref)
        ri_ref[:, rows] = ri_t
        rw_ref[rows, :] = rw

    conv(0)
    for s in range(OUT_SUB):
        project(s)
        if s + 1 < OUT_SUB:
            conv(s + 1)
        if s > 0:
            route(s - 1)
    route(OUT_SUB - 1)
    cnt_ref[...] = base_ref[...]


def _outproj(a_n, proj, conv_w, g_conv, x2, wo_bf, g2, wr_bf, br):
    tm = OUT_TM
    per = tm // CONV_PREV
    row = lambda i: (i, 0)
    fixed = lambda i: (0, 0)
    col = lambda c: (lambda i: (i, c))
    prev = lambda c: (lambda i: (jnp.maximum(i * per - 1, 0), c))
    return pl.pallas_call(
        _outproj_kernel,
        grid=(T // tm,),
        in_specs=[
            pl.BlockSpec((tm, D_ATTN), row),
            pl.BlockSpec((tm, D_CONV), col(3)),
            pl.BlockSpec((tm, D_CONV), col(4)),
            pl.BlockSpec((tm, D_CONV), col(5)),
            pl.BlockSpec((CONV_PREV, D_CONV), prev(4)),
            pl.BlockSpec((CONV_PREV, D_CONV), prev(5)),
            pl.BlockSpec((8, D_CONV), fixed),
            pl.BlockSpec((1, D_CONV), fixed),
            pl.BlockSpec((tm, D_MODEL), row),
            pl.BlockSpec((D_MODEL, D_MODEL), fixed),
            pl.BlockSpec((1, D_MODEL), fixed),
            pl.BlockSpec((D_MODEL, LANES), fixed),
            pl.BlockSpec((1, LANES), fixed),
        ],
        out_specs=[
            pl.BlockSpec((tm, D_MODEL), row),
            pl.BlockSpec((tm * ROW_TILE, LANES), row),
            pl.BlockSpec((SUBLANES, tm), lambda i: (0, i)),
            pl.BlockSpec((tm, LANES), row),
            pl.BlockSpec((N_EXPERTS, LANES), fixed),
        ],
        out_shape=[
            jax.ShapeDtypeStruct((T, D_MODEL), F32),
            jax.ShapeDtypeStruct((T * ROW_TILE, LANES), I32),
            jax.ShapeDtypeStruct((SUBLANES, T), I32),
            jax.ShapeDtypeStruct((T, LANES), F32),
            jax.ShapeDtypeStruct((N_EXPERTS, LANES), F32),
        ],
        scratch_shapes=[pltpu.VMEM((N_EXPERTS, LANES), F32), pltpu.VMEM((tm, D_CONV), BF16)],
        compiler_params=_params(("arbitrary",)),
        name="outproj",
    )(a_n, proj, proj, proj, proj, proj, conv_w, g_conv, x2, wo_bf, g2, wr_bf, br)


BLOCK_LANES = 384
DEST_TM = 2048


def _dest_kernel(ri_ref, cnt_ref, dest_ref, be_ref, seg_ref):
    tm = DEST_TM
    erow = lax.broadcasted_iota(I32, (N_EXPERTS, LANES), 0)
    cnt = cnt_ref[...].astype(I32)
    pcnt = ((cnt + (MOE_BLOCK - 1)) >> MOE_BLOCK_SHIFT) << MOE_BLOCK_SHIFT
    pend = pcnt
    for s in (1, 2, 4, 8, 16):
        pend = pend + jnp.where(erow >= s, pltpu.roll(pend, s, axis=0), 0)
    pstart = (pend - pcnt).astype(F32)

    ri = ri_ref[...].astype(F32)
    etok = lax.broadcasted_iota(I32, (N_EXPERTS, tm), 0).astype(F32)
    pstart_t = jnp.concatenate([pstart] * (tm // LANES), axis=1)
    d1 = jnp.sum(jnp.where(etok == ri[0:1], pstart_t, 0.0), axis=0, keepdims=True) + ri[2:3]
    d2 = jnp.sum(jnp.where(etok == ri[1:2], pstart_t, 0.0), axis=0, keepdims=True) + ri[3:4]
    sub = lax.broadcasted_iota(I32, (SUBLANES, tm), 0)
    dest_ref[...] = jnp.where(sub == 0, d1, d2).astype(I32)

    pend_f = pend.astype(F32)
    pend_b = jnp.concatenate([pend_f] * (BLOCK_LANES // LANES), axis=1)
    blk_lane = lax.broadcasted_iota(I32, (1, BLOCK_LANES), 1)
    le = jnp.where(pend_b <= (blk_lane * MOE_BLOCK).astype(F32), 1.0, 0.0)
    be = jnp.minimum(jnp.sum(le, axis=0, keepdims=True), float(N_EXPERTS - 1))
    total = pend_b[N_EXPERTS - 1:N_EXPERTS, :]
    be = jnp.where(blk_lane == N_BLOCKS, total * (1.0 / MOE_BLOCK), be)
    be_ref[...] = jnp.broadcast_to(be, (SUBLANES, BLOCK_LANES)).astype(I32)
    seg_ref[0:N_EXPERTS, :] = pend - pcnt + cnt
    seg_ref[N_EXPERTS:2 * N_EXPERTS, :] = pend


def _dest(ri_t, cnt):
    tm = DEST_TM
    fixed = lambda i: (0, 0)
    tok = lambda i: (0, i)
    return pl.pallas_call(
        _dest_kernel,
        grid=(T // tm,),
        in_specs=[pl.BlockSpec((SUBLANES, tm), tok), pl.BlockSpec((N_EXPERTS, LANES), fixed)],
        out_specs=[pl.BlockSpec((SUBLANES, tm), tok),
                   pl.BlockSpec((SUBLANES, BLOCK_LANES), fixed),
                   pl.BlockSpec((2 * N_EXPERTS, LANES), fixed)],
        out_shape=[jax.ShapeDtypeStruct((SUBLANES, T), I32),
                   jax.ShapeDtypeStruct((SUBLANES, BLOCK_LANES), I32),
                   jax.ShapeDtypeStruct((2 * N_EXPERTS, LANES), I32)],
        compiler_params=_params(("arbitrary",)),
        name="dest",
    )(ri_t, cnt)


GROUP = 3
INV_LEN = (-(-N_BLOCKS // GROUP) * GROUP + GROUP - 1) * MOE_BLOCK
DUMP_ROWS = 4 * MOE_BLOCK


PAD_STEP = 8


def _invert_permutation(d1_ref, d2_ref, vend_ref, pend_ref, inv_ref, pos_end):
    def pad_range(lo, hi):
        def body(k, carry):
            p0 = lo + k * PAD_STEP
            for j in range(PAD_STEP):
                inv_ref[p0 + j] = N_ROUTED + ((p0 + j) & (DUMP_ROWS - 1))
            return carry

        lax.fori_loop(0, (hi - lo + (PAD_STEP - 1)) // PAD_STEP, body, 0)

    def pad_expert(e, carry):
        pad_range(vend_ref[e], pend_ref[e])
        return carry

    lax.fori_loop(0, N_EXPERTS, pad_expert, 0)
    pad_range(pend_ref[N_EXPERTS - 1], pos_end)

    def place(t, carry):
        inv_ref[d1_ref[t]] = 2 * t
        inv_ref[d2_ref[t]] = 2 * t + 1
        return carry

    lax.fori_loop(0, T, place, 0, unroll=16)


BLOCK_TILE_ROWS = MOE_BLOCK * ROW_TILE
Y_ROWS = (N_ROUTED + DUMP_ROWS) * ROW_TILE


def _moe_kernel(meta_ref, d1_ref, d2_ref, vend_ref, pend_ref, h_ref, wg_hbm, wu_hbm, wd_hbm, y_ref,
                xb0, xb1, xb2, yb0, yb1, yb2, wg_s, wu_s, wd_s, wg_b, wu_b, wd_b, inv_ref,
                sem_x, sem_y, sem_w):
    xbufs = (xb0, xb1, xb2)
    ybufs = (yb0, yb1, yb2)
    n_used = meta_ref[N_BLOCKS]
    n_groups = (n_used + (GROUP - 1)) // GROUP
    half = D_MODEL // 2

    def issue_gather(b, v):
        for r in range(MOE_BLOCK):
            tok = (inv_ref[b * MOE_BLOCK + r] >> 1) & (T - 1)
            src = h_ref.at[pl.ds(pl.multiple_of(tok * ROW_TILE, ROW_TILE), ROW_TILE), :]
            pltpu.make_async_copy(src, xbufs[v].at[pl.ds(r * ROW_TILE, ROW_TILE), :],
                                  sem_x.at[v]).start()

    def wait_gather(v):
        pltpu.make_async_copy(h_ref.at[pl.ds(0, BLOCK_TILE_ROWS), :], xbufs[v],
                              sem_x.at[v]).wait()

    def issue_scatter(b, v):
        for r in range(MOE_BLOCK):
            slot = inv_ref[b * MOE_BLOCK + r]
            dst = y_ref.at[pl.ds(pl.multiple_of(slot * ROW_TILE, ROW_TILE), ROW_TILE), :]
            pltpu.make_async_copy(ybufs[v].at[pl.ds(r * ROW_TILE, ROW_TILE), :], dst,
                                  sem_y.at[v]).start(priority=1)

    def wait_scatter(v):
        pltpu.make_async_copy(ybufs[v], y_ref.at[pl.ds(0, BLOCK_TILE_ROWS), :],
                              sem_y.at[v]).wait()

    def weight_copies(e):
        return (pltpu.make_async_copy(wg_hbm.at[e], wg_s, sem_w),
                pltpu.make_async_copy(wu_hbm.at[e], wu_s, sem_w),
                pltpu.make_async_copy(wd_hbm.at[e], wd_s, sem_w))

    def load_expert(b):
        e = meta_ref[b]
        prev = meta_ref[jnp.maximum(b - 1, 0)]
        first = jnp.logical_and(b < n_used, jnp.logical_or(b == 0, e != prev))

        @pl.when(first)
        def _():
            for c in weight_copies(e):
                c.wait()
            wg_b[...] = wg_s[...].astype(BF16)
            wu_b[...] = wu_s[...].astype(BF16)
            wd_b[...] = wd_s[...].astype(BF16)
            nxt = lax.while_loop(
                lambda j: jnp.logical_and(j < n_used, meta_ref[jnp.minimum(j, N_BLOCKS - 1)] == e),
                lambda j: j + 1, b + 1)

            @pl.when(nxt < n_used)
            def _():
                for c in weight_copies(meta_ref[nxt]):
                    c.start(priority=1)

    def compute(v):
        lo, hi = _unpack_pairs_f32(_load_row_tiles(xbufs[v], MOE_BLOCK))
        lo, hi = lo.astype(BF16), hi.astype(BF16)
        g = (jnp.dot(lo, wg_b[0:half, :], preferred_element_type=F32)
             + jnp.dot(hi, wg_b[half:, :], preferred_element_type=F32))
        u = (jnp.dot(lo, wu_b[0:half, :], preferred_element_type=F32)
             + jnp.dot(hi, wu_b[half:, :], preferred_element_type=F32))
        a = g * (1.0 / (1.0 + jnp.exp(-g))) * u
        y = jnp.dot(a.astype(BF16), wd_b[...], preferred_element_type=F32)
        _store_row_tiles(ybufs[v], _pack_bf16_pairs(y))

    def block(b, v, scatter_prev, wait_y):
        wait_gather(v)
        if wait_y:
            wait_scatter(v)
        load_expert(b)
        issue_gather(b + 2, (v + 2) % GROUP)
        if scatter_prev:
            issue_scatter(b - 1, (v + 2) % GROUP)
        compute(v)

    for c in weight_copies(meta_ref[0]):
        c.start()
    yb2[...] = jnp.zeros_like(yb2)
    dump_fills = [
        pltpu.make_async_copy(
            yb2, y_ref.at[pl.ds((N_ROUTED + k * MOE_BLOCK) * ROW_TILE, BLOCK_TILE_ROWS), :],
            sem_y.at[GROUP - 1])
        for k in range(DUMP_ROWS // MOE_BLOCK)]
    for c in dump_fills:
        c.start()
    _invert_permutation(d1_ref, d2_ref, vend_ref, pend_ref, inv_ref,
                        (n_groups * GROUP + GROUP - 1) * MOE_BLOCK)
    for c in dump_fills:
        c.wait()
    issue_gather(0, 0)
    issue_gather(1, 1)
    for v in range(GROUP):
        block(v, v, scatter_prev=v > 0, wait_y=False)

    def group(gi, carry):
        for v in range(GROUP):
            block(gi * GROUP + v, v, scatter_prev=True, wait_y=True)
        return carry

    lax.fori_loop(1, n_groups, group, 0)

    last = n_groups * GROUP
    issue_scatter(last - 1, GROUP - 1)
    wait_gather(0)
    wait_gather(1)
    for v in range(GROUP):
        wait_scatter(v)


def _moe(meta, d1, d2, vend, pend, h_tiles, w_gate, w_up, w_down):
    any_spec = pl.BlockSpec(memory_space=pl.ANY)
    xy = pltpu.VMEM((BLOCK_TILE_ROWS, LANES), I32)
    return pl.pallas_call(
        _moe_kernel,
        grid_spec=pltpu.PrefetchScalarGridSpec(
            num_scalar_prefetch=5,
            grid=(1,),
            in_specs=[any_spec, any_spec, any_spec, any_spec],
            out_specs=any_spec,
            scratch_shapes=[
                xy, xy, xy, xy, xy, xy,
                pltpu.VMEM((D_MODEL, D_EXPERT), F32),
                pltpu.VMEM((D_MODEL, D_EXPERT), F32),
                pltpu.VMEM((D_EXPERT, D_MODEL), F32),
                pltpu.VMEM((D_MODEL, D_EXPERT), BF16),
                pltpu.VMEM((D_MODEL, D_EXPERT), BF16),
                pltpu.VMEM((D_EXPERT, D_MODEL), BF16),
                pltpu.SMEM((INV_LEN,), I32),
                pltpu.SemaphoreType.DMA((GROUP,)),
                pltpu.SemaphoreType.DMA((GROUP,)),
                pltpu.SemaphoreType.DMA(()),
            ],
        ),
        out_shape=jax.ShapeDtypeStruct((Y_ROWS, LANES), I32),
        compiler_params=_params(("arbitrary",)),
        name="moe",
    )(meta, d1, d2, vend, pend, h_tiles, w_gate, w_up, w_down)


COMB_TM = 512


def _combine_kernel(x1_ref, rw_ref, gf_ref, y_ref, o_ref):
    tm = COMB_TM
    w = rw_ref[...]
    w0, w1 = w[:, 0:1], w[:, 1:2]
    lo0, hi0 = _unpack_pairs_f32(_load_row_tiles(y_ref, tm, 0, 2 * ROW_TILE))
    lo1, hi1 = _unpack_pairs_f32(_load_row_tiles(y_ref, tm, ROW_TILE, 2 * ROW_TILE))
    moe = jnp.concatenate([lo0 * w0 + lo1 * w1, hi0 * w0 + hi1 * w1], axis=1)
    o_ref[...] = _rms(x1_ref[...] + moe, gf_ref[...])


def _combine(x1, rw, gf, y_tiles):
    tm = COMB_TM
    row = lambda i: (i, 0)
    return pl.pallas_call(
        _combine_kernel,
        grid=(T // tm,),
        in_specs=[
            pl.BlockSpec((tm, D_MODEL), row),
            pl.BlockSpec((tm, LANES), row),
            pl.BlockSpec((1, D_MODEL), lambda i: (0, 0)),
            pl.BlockSpec((tm * 2 * ROW_TILE, LANES), row),
        ],
        out_specs=pl.BlockSpec((tm, D_MODEL), row),
        out_shape=jax.ShapeDtypeStruct((T, D_MODEL), F32),
        compiler_params=_params(("arbitrary",)),
        name="combine",
    )(x1, rw, gf, y_tiles)


def _layer(x2, norm1, w_in, rel_bias, conv_w, g_out_attn, g_out_conv, w_out, norm2,
           w_rg, b_rg, w_re, b_re, w_gate, w_up, w_down):
    proj = _inproj(x2, norm1[None, :], w_in.astype(BF16))
    a_n = _attention(proj, _bias_table(rel_bias), g_out_attn[None, :])
    conv_w8 = jnp.zeros((8, D_CONV), F32).at[0:3].set(conv_w)

    w_r = jnp.zeros((D_MODEL, LANES), F32)
    g0 = GROUP_LOGIT_ROW
    w_r = w_r.at[:, 0:N_EXPERTS].set(w_re).at[:, g0:g0 + N_GROUPS].set(w_rg)
    b_r = jnp.zeros((1, LANES), F32)
    b_r = b_r.at[0, 0:N_EXPERTS].set(b_re).at[0, g0:g0 + N_GROUPS].set(b_rg)
    x1, hp, ri_t, rw, cnt = _outproj(a_n, proj, conv_w8, g_out_conv[None, :], x2,
                                     w_out.astype(BF16), norm2[None, :], w_r.astype(BF16), b_r)

    dest_t, be, seg = _dest(ri_t, cnt)
    y_tiles = _moe(be[0], dest_t[0], dest_t[1], seg[0:N_EXPERTS, 0],
                   seg[N_EXPERTS:2 * N_EXPERTS, 0], hp, w_gate, w_up, w_down)
    return x1, rw, y_tiles


def kernel(x, norm1, w_in, rel_bias, conv_w, g_out_attn, g_out_conv, w_out, norm2,
           w_router_group, b_router_group, w_router_expert, b_router_expert,
           w_gate, w_up, w_down, norm_final):
    assert x.shape == (BATCH, SEQ, D_MODEL) and norm1.shape[0] == 1
    x2 = x.reshape(T, D_MODEL)
    x1, rw, y_tiles = _layer(
        x2, norm1[0], w_in[0], rel_bias[0], conv_w[0], g_out_attn[0], g_out_conv[0],
        w_out[0], norm2[0], w_router_group[0], b_router_group[0], w_router_expert[0],
        b_router_expert[0], w_gate[0], w_up[0], w_down[0])
    out = _combine(x1, rw, norm_final[None, :], y_tiles)
    return out.reshape(BATCH, SEQ, D_MODEL)
```

```python
import functools

import jax
import jax.numpy as jnp
from jax import lax
from jax.experimental import pallas as pl
from jax.experimental.pallas import tpu as pltpu

F32 = jnp.float32
BF16 = jnp.bfloat16
I32 = jnp.int32

D_MODEL = 2048
BATCH = 4
SEQ = 4096
T = BATCH * SEQ
CHUNK = 64
LEFT_CHUNKS = 8
BAND = LEFT_CHUNKS + 1
KEYS = BAND * CHUNK
D_ATTN = 1024
D_CONV = 1024
HEAD_DIM = 64
N_HEADS = 16
N_PAIRS = N_HEADS // 2
REL_CLIP = 256
D_IN_PROJ = 6 * 1024
N_GROUPS = 4
EPG = 8
N_EXPERTS = 32
D_EXPERT = 512
MOE_BLOCK = 256
MOE_BLOCK_SHIFT = MOE_BLOCK.bit_length() - 1
assert MOE_BLOCK == 1 << MOE_BLOCK_SHIFT
N_ROUTED = 2 * T
N_BLOCKS = N_ROUTED // MOE_BLOCK + N_EXPERTS
EPS = 1e-6
NEG_INF = -1e30
LANES = 128
VMEM_LIMIT = 52 * 1024 * 1024

ATTN_TILE = 512
ATTN_TILES_PER_SEQ = SEQ // ATTN_TILE


def _params(sem):
    return pltpu.CompilerParams(dimension_semantics=sem, vmem_limit_bytes=VMEM_LIMIT)


def _rms(x, g):
    ms = jnp.mean(x * x, axis=-1, keepdims=True)
    return x * lax.rsqrt(ms + EPS) * g


INPROJ_TM = 512
INPROJ_TN = 1536


D_QKV = 3 * D_ATTN
SUBLANES = 8


def _short_conv(b, z, z_before, w, g):
    n = z.shape[0]
    ze = jnp.concatenate([z_before, z], axis=0)
    z1 = pltpu.roll(ze, 1, axis=0)[SUBLANES:SUBLANES + n]
    z2 = pltpu.roll(ze, 2, axis=0)[SUBLANES:SUBLANES + n]
    y = w[0:1] * z2 + w[1:2] * z1 + w[2:3] * z
    return _rms(b * y, g).astype(BF16)


def _inproj_kernel(x_ref, g_ref, w_ref, cw_ref, gc_ref, qkv_ref, cn_ref, ztail_ref):
    i = pl.program_id(0)

    @pl.when(i == 0)
    def _():
        ztail_ref[...] = jnp.zeros_like(ztail_ref)

    hn = _rms(x_ref[...], g_ref[...]).astype(BF16)
    group = lambda k: jnp.dot(hn, w_ref[:, D_QKV + k * D_CONV:D_QKV + (k + 1) * D_CONV],
                              preferred_element_type=F32)
    b, z = group(0), group(1) * group(2)
    seq_start = (i % (SEQ // INPROJ_TM)) == 0
    z_before = jnp.where(seq_start, 0.0, ztail_ref[...])
    ztail_ref[...] = z[INPROJ_TM - SUBLANES:]
    cn_ref[...] = _short_conv(b, z, z_before, cw_ref[...], gc_ref[...])
    for j in range(D_QKV // INPROJ_TN):
        cols = slice(j * INPROJ_TN, (j + 1) * INPROJ_TN)
        qkv_ref[:, cols] = jnp.dot(hn, w_ref[:, cols],
                                   preferred_element_type=F32).astype(BF16)


def _inproj(x2, g, w_bf, conv_w, g_conv):
    tm = INPROJ_TM
    fixed = lambda i: (0, 0)
    row = lambda i: (i, 0)
    return pl.pallas_call(
        _inproj_kernel,
        grid=(T // tm,),
        in_specs=[
            pl.BlockSpec((tm, D_MODEL), row),
            pl.BlockSpec((1, D_MODEL), fixed),
            pl.BlockSpec((D_MODEL, D_IN_PROJ), fixed, pipeline_mode=pl.Buffered(1)),
            pl.BlockSpec((SUBLANES, D_CONV), fixed),
            pl.BlockSpec((1, D_CONV), fixed),
        ],
        out_specs=[pl.BlockSpec((tm, D_QKV), row), pl.BlockSpec((tm, D_CONV), row)],
        out_shape=[jax.ShapeDtypeStruct((T, D_QKV), BF16),
                   jax.ShapeDtypeStruct((T, D_CONV), BF16)],
        scratch_shapes=[pltpu.VMEM((SUBLANES, D_CONV), F32)],
        compiler_params=_params(("arbitrary",)),
        name="inproj",
    )(x2, g, w_bf, conv_w, g_conv)


N_QUADS = N_HEADS // 4
QUAD = 4 * HEAD_DIM
BIAS_W = 640
BIAS_WIN = 512
P_BUFS = 4


def _bias_kernel(b_ref, o_ref):
    first_head = lax.broadcasted_iota(I32, (KEYS, LANES), 1) < HEAD_DIM
    for hq in range(N_QUADS):
        for col in range(2):
            h0 = 4 * hq + 2 * col
            pieces = []
            for hh in range(2):
                x = jnp.broadcast_to(b_ref[h0 + hh:h0 + hh + 1, :], (KEYS, BIAS_W))
                rolled = pltpu.roll(x, 0, 1, stride=1, stride_axis=0)
                pieces.append(rolled[:, BIAS_WIN:BIAS_WIN + LANES])
            o_ref[hq, :, col * LANES:(col + 1) * LANES] = jnp.where(
                first_head, pieces[0], pieces[1])


def _bias_table(rel_bias):
    r = rel_bias.astype(F32)
    edge = jnp.broadcast_to(r[:, 2 * REL_CLIP:], (N_HEADS, KEYS - REL_CLIP + 1))
    b = jnp.concatenate([r[:, REL_CLIP:2 * REL_CLIP], edge,
                         r[:, REL_CLIP - (BIAS_W - KEYS - 1):REL_CLIP]], axis=1)
    odd = jnp.concatenate([b[:, BIAS_W - HEAD_DIM:], b[:, :BIAS_W - HEAD_DIM]], axis=1)
    is_odd = (jnp.arange(N_HEADS) % 2 == 1)[:, None]
    bvec = jnp.where(is_odd, odd, b)
    return pl.pallas_call(
        _bias_kernel,
        out_shape=jax.ShapeDtypeStruct((N_QUADS, KEYS, QUAD), F32),
        compiler_params=pltpu.CompilerParams(vmem_limit_bytes=VMEM_LIMIT),
        name="biastab",
    )(bvec)


def _attn_kernel(q_ref, kp_ref, kc_ref, vp_ref, vc_ref, bias_ref, g_ref, o_ref,
                 st_ref, p_ref, a_ref):
    i = pl.program_id(1)

    def window(prev_ref, cur_ref, r0, n, cols):
        parts = []
        if r0 < ATTN_TILE:
            parts.append(prev_ref[r0:min(ATTN_TILE, r0 + n), cols])
        if r0 + n > ATTN_TILE:
            parts.append(cur_ref[max(r0, ATTN_TILE) - ATTN_TILE:r0 + n - ATTN_TILE, cols])
        return parts[0] if len(parts) == 1 else jnp.concatenate(parts, axis=0)

    head_of_lane = lax.broadcasted_iota(I32, (CHUNK, QUAD), 1) // HEAD_DIM
    first_head = lax.broadcasted_iota(I32, (CHUNK, LANES), 1) < HEAD_DIM
    key_row = lax.broadcasted_iota(I32, (KEYS, QUAD), 0)
    n_rb = KEYS // CHUNK

    all_quads = tuple(range(N_QUADS))
    all_pairs = tuple(range(N_PAIRS))

    def scores(r0, buf, masked, quads=all_quads):
        for hq in quads:
            c0 = hq * QUAD
            q4 = q_ref[pl.ds(r0, CHUNK), c0:c0 + QUAD] * (HEAD_DIM ** -0.5)
            zero = jnp.zeros_like(q4)
            qbd = jnp.concatenate(
                [jnp.where(head_of_lane == h, q4, zero) for h in range(4)], axis=0)
            k4 = window(kp_ref, kc_ref, r0, KEYS, slice(c0, c0 + QUAD))
            st = lax.dot_general(k4, qbd, (((1,), (1,)), ((), ())),
                                 preferred_element_type=F32)
            st = st + bias_ref[hq]
            if masked:
                st = jnp.where(key_row + r0 >= ATTN_TILE, st, NEG_INF)
            st_ref[buf, hq] = st

    def exps(buf, pbuf, off, quads=all_quads):
        for hq in quads:
            m = st_ref[buf, hq, 0:CHUNK, :]
            for rb in range(1, n_rb):
                m = jnp.maximum(m, st_ref[buf, hq, rb * CHUNK:(rb + 1) * CHUNK, :])
            m = jnp.max(m, axis=0, keepdims=True)
            for rb in range(n_rb):
                rows = slice(rb * CHUNK, (rb + 1) * CHUNK)
                prow = slice(off + rb * CHUNK, off + (rb + 1) * CHUNK)
                p_ref[pbuf, hq, prow, :] = jnp.exp(st_ref[buf, hq, rows, :] - m).astype(BF16)

    def values(r0, pbufs, pairs=all_pairs):
        nq = len(pbufs)
        span = KEYS + (nq - 1) * CHUNK
        ones = jnp.ones((span, LANES), BF16)
        for hp in pairs:
            c0 = hp * LANES
            half = (hp % 2) * LANES
            pt = jnp.concatenate(
                [p_ref[b, hp // 2, 0:span, half:half + LANES] for b in pbufs], axis=1)
            vext = jnp.concatenate(
                [window(vp_ref, vc_ref, r0, span, slice(c0, c0 + LANES)), ones], axis=1)
            oe = lax.dot_general(pt, vext, (((0,), (0,)), ((), ())),
                                 preferred_element_type=F32)
            o = oe[:, :LANES] * (1.0 / oe[:, LANES:])
            for j in range(nq):
                oj = o[2 * j * CHUNK:2 * (j + 1) * CHUNK]
                a_ref[pl.ds(r0 + j * CHUNK, CHUNK), c0:c0 + LANES] = jnp.where(
                    first_head, oj[0:CHUNK], oj[CHUNK:2 * CHUNK])

    n_chunks = ATTN_TILE // CHUNK

    def tile(masked):
        for b in range(P_BUFS):
            z = slice(KEYS, KEYS + CHUNK) if b % 2 == 0 else slice(0, CHUNK)
            p_ref[b, :, z, :] = jnp.zeros((N_QUADS, CHUNK, QUAD), BF16)
        scores(0, 0, masked)
        for ci in range(n_chunks):
            c0 = ci - 2 - ci % 2
            for hq in all_quads:
                if ci + 1 < n_chunks:
                    scores((ci + 1) * CHUNK, (ci + 1) % 2, masked, (hq,))
                if c0 >= 0:
                    values(c0 * CHUNK, (c0 % P_BUFS, (c0 + 1) % P_BUFS),
                           ((ci % 2) * N_QUADS + hq,))
                exps(ci % 2, ci % P_BUFS, (ci % 2) * CHUNK, (hq,))
        values((n_chunks - 2) * CHUNK, ((n_chunks - 2) % P_BUFS, (n_chunks - 1) % P_BUFS))

    pl.when(i == 0)(functools.partial(tile, True))
    pl.when(i > 0)(functools.partial(tile, False))

    o_ref[...] = _rms(a_ref[...], g_ref[...]).astype(BF16)


def _attention(proj, bias_t, g):
    n = ATTN_TILES_PER_SEQ
    cur = lambda col: (lambda b, i: (b * n + i, col))
    prev = lambda col: (lambda b, i: (b * n + jnp.maximum(i - 1, 0), col))
    blk = (ATTN_TILE, D_ATTN)
    return pl.pallas_call(
        _attn_kernel,
        grid=(BATCH, n),
        in_specs=[
            pl.BlockSpec(blk, cur(0)),
            pl.BlockSpec(blk, prev(1)),
            pl.BlockSpec(blk, cur(1)),
            pl.BlockSpec(blk, prev(2)),
            pl.BlockSpec(blk, cur(2)),
            pl.BlockSpec((N_QUADS, KEYS, QUAD), lambda b, i: (0, 0, 0)),
            pl.BlockSpec((1, D_ATTN), lambda b, i: (0, 0)),
        ],
        out_specs=pl.BlockSpec(blk, lambda b, i: (b * n + i, 0)),
        out_shape=jax.ShapeDtypeStruct((T, D_ATTN), BF16),
        scratch_shapes=[
            pltpu.VMEM((2, N_QUADS, KEYS, QUAD), F32),
            pltpu.VMEM((P_BUFS, N_QUADS, KEYS + CHUNK, QUAD), BF16),
            pltpu.VMEM((ATTN_TILE, D_ATTN), F32),
        ],
        compiler_params=_params(("arbitrary", "arbitrary")),
        name="attn",
    )(proj, proj, proj, proj, proj, bias_t, g)


OUT_TM = 512
OUT_SUB = 2


def _pack_bf16_pairs(h):
    half = D_MODEL // 2
    return pltpu.pack_elementwise([h[:, :half], h[:, half:]], packed_dtype=BF16)


def _unpack_pairs_f32(u):
    lo = pltpu.unpack_elementwise(u, index=0, packed_dtype=BF16, unpacked_dtype=F32)
    hi = pltpu.unpack_elementwise(u, index=1, packed_dtype=BF16, unpacked_dtype=F32)
    return lo, hi


ROW_TILE = 8


def _store_row_tiles(ref, packed):
    m = packed.shape[0]
    for s in range(ROW_TILE):
        ref[pl.ds(s, m, stride=ROW_TILE), :] = packed[:, s * LANES:(s + 1) * LANES]


def _load_row_tiles(ref, m, first=0, stride=ROW_TILE):
    return jnp.concatenate(
        [ref[pl.ds(first + s, m, stride=stride), :] for s in range(ROW_TILE)], axis=1)


GROUP_LOGIT_ROW = N_EXPERTS


def _route_cols(lg, base_ref):
    tm = lg.shape[0]
    lt = lg.T
    sub = lax.broadcasted_iota(I32, (SUBLANES, tm), 0)
    sub_f = sub.astype(F32)
    big = jnp.float32(SUBLANES)
    ninf = jnp.float32(-jnp.inf)

    gl = lt[GROUP_LOGIT_ROW:GROUP_LOGIT_ROW + SUBLANES, :]
    gvalid = sub < N_GROUPS
    gmax = jnp.max(jnp.where(gvalid, gl, ninf), axis=0, keepdims=True)
    gsum = jnp.sum(jnp.where(gvalid, jnp.exp(gl - gmax), 0.0), axis=0, keepdims=True)
    p_g = 1.0 / gsum
    g_sel = jnp.min(jnp.where(jnp.logical_and(gvalid, gl == gmax), sub_f, big),
                    axis=0, keepdims=True)
    el = lt[0:EPG, :]
    for g in range(1, N_GROUPS):
        el = jnp.where(g_sel == g, lt[g * EPG:(g + 1) * EPG, :], el)
    v1 = jnp.max(el, axis=0, keepdims=True)
    i1 = jnp.min(jnp.where(el == v1, sub_f, big), axis=0, keepdims=True)
    rest = sub_f != i1
    v2 = jnp.max(jnp.where(rest, el, ninf), axis=0, keepdims=True)
    i2 = jnp.min(jnp.where(jnp.logical_and(rest, el == v2), sub_f, big),
                 axis=0, keepdims=True)
    t = jnp.exp(v2 - v1)
    w1 = p_g * (1.0 / (1.0 + t))
    w2 = p_g * (t / (1.0 + t))
    e1 = g_sel * EPG + i1
    e2 = g_sel * EPG + i2

    erow = lax.broadcasted_iota(I32, (N_EXPERTS, tm), 0).astype(F32)
    hit1 = erow == e1
    hit2 = erow == e2
    onehot = jnp.where(jnp.logical_or(hit1, hit2), 1.0, 0.0)
    r = lax.broadcasted_iota(I32, (tm, tm), 0)
    c = lax.broadcasted_iota(I32, (tm, tm), 1)
    tri = jnp.where(r < c, 1.0, 0.0).astype(BF16)
    base = base_ref[...]
    before = (jnp.dot(onehot.astype(BF16), tri, preferred_element_type=F32)
              + jnp.concatenate([base] * (tm // LANES), axis=1))
    r1 = jnp.sum(jnp.where(hit1, before, 0.0), axis=0, keepdims=True)
    r2 = jnp.sum(jnp.where(hit2, before, 0.0), axis=0, keepdims=True)
    base_ref[...] = base + jnp.broadcast_to(
        jnp.sum(onehot, axis=1, keepdims=True), (N_EXPERTS, LANES))

    ri_t = jnp.where(sub == 0, e1, jnp.where(sub == 1, e2, jnp.where(sub == 2, r1, r2)))
    w_t = jnp.where(sub == 0, w1, jnp.where(sub == 1, w2, 0.0))
    w_t = jnp.concatenate([w_t, jnp.zeros((LANES - SUBLANES, tm), F32)], axis=0)
    return ri_t.astype(I32), w_t.T


def _outproj_kernel(a_ref, cn_ref, x_ref, wo_ref, g2_ref, wr_ref, br_ref,
                    x1_ref, hp_ref, ri_ref, rw_ref, cnt_ref, base_ref):
    @pl.when(pl.program_id(0) == 0)
    def _():
        base_ref[...] = jnp.zeros_like(base_ref)

    sub = OUT_TM // OUT_SUB
    rows_of = lambda s: slice(s * sub, (s + 1) * sub)

    def project(s):
        rows = rows_of(s)
        acc = jnp.dot(a_ref[rows, :], wo_ref[0:D_ATTN, :], preferred_element_type=F32)
        acc = acc + jnp.dot(cn_ref[rows, :], wo_ref[D_ATTN:, :], preferred_element_type=F32)
        x1_ref[rows, :] = x_ref[rows, :] + acc

    def route(s):
        rows = rows_of(s)
        h2 = _rms(x1_ref[rows, :], g2_ref[...])
        logits = jnp.dot(h2.astype(BF16), wr_ref[...],
                         preferred_element_type=F32) + br_ref[...]
        _store_row_tiles(hp_ref.at[pl.ds(s * sub * ROW_TILE, sub * ROW_TILE), :],
                         _pack_bf16_pairs(h2))
        ri_t, rw = _route_cols(logits, base_ref)
        ri_ref[:, rows] = ri_t
        rw_ref[rows, :] = rw

    for s in range(OUT_SUB):
        project(s)
        if s > 0:
            route(s - 1)
    route(OUT_SUB - 1)
    cnt_ref[...] = base_ref[...]


def _outproj(a_n, c_n, x2, wo_bf, g2, wr_bf, br):
    tm = OUT_TM
    row = lambda i: (i, 0)
    fixed = lambda i: (0, 0)
    return pl.pallas_call(
        _outproj_kernel,
        grid=(T // tm,),
        in_specs=[
            pl.BlockSpec((tm, D_ATTN), row),
            pl.BlockSpec((tm, D_CONV), row),
            pl.BlockSpec((tm, D_MODEL), row),
            pl.BlockSpec((D_MODEL, D_MODEL), fixed),
            pl.BlockSpec((1, D_MODEL), fixed),
            pl.BlockSpec((D_MODEL, LANES), fixed),
            pl.BlockSpec((1, LANES), fixed),
        ],
        out_specs=[
            pl.BlockSpec((tm, D_MODEL), row),
            pl.BlockSpec((tm * ROW_TILE, LANES), row),
            pl.BlockSpec((SUBLANES, tm), lambda i: (0, i)),
            pl.BlockSpec((tm, LANES), row),
            pl.BlockSpec((N_EXPERTS, LANES), fixed),
        ],
        out_shape=[
            jax.ShapeDtypeStruct((T, D_MODEL), F32),
            jax.ShapeDtypeStruct((T * ROW_TILE, LANES), I32),
            jax.ShapeDtypeStruct((SUBLANES, T), I32),
            jax.ShapeDtypeStruct((T, LANES), F32),
            jax.ShapeDtypeStruct((N_EXPERTS, LANES), F32),
        ],
        scratch_shapes=[pltpu.VMEM((N_EXPERTS, LANES), F32)],
        compiler_params=_params(("arbitrary",)),
        name="outproj",
    )(a_n, c_n, x2, wo_bf, g2, wr_bf, br)


BLOCK_LANES = 384
DEST_TM = 2048


def _dest_kernel(ri_ref, cnt_ref, dest_ref, be_ref, seg_ref):
    tm = DEST_TM
    erow = lax.broadcasted_iota(I32, (N_EXPERTS, LANES), 0)
    cnt = cnt_ref[...].astype(I32)
    pcnt = ((cnt + (MOE_BLOCK - 1)) >> MOE_BLOCK_SHIFT) << MOE_BLOCK_SHIFT
    pend = pcnt
    for s in (1, 2, 4, 8, 16):
        pend = pend + jnp.where(erow >= s, pltpu.roll(pend, s, axis=0), 0)
    pstart = (pend - pcnt).astype(F32)

    ri = ri_ref[...].astype(F32)
    etok = lax.broadcasted_iota(I32, (N_EXPERTS, tm), 0).astype(F32)
    pstart_t = jnp.concatenate([pstart] * (tm // LANES), axis=1)
    d1 = jnp.sum(jnp.where(etok == ri[0:1], pstart_t, 0.0), axis=0, keepdims=True) + ri[2:3]
    d2 = jnp.sum(jnp.where(etok == ri[1:2], pstart_t, 0.0), axis=0, keepdims=True) + ri[3:4]
    sub = lax.broadcasted_iota(I32, (SUBLANES, tm), 0)
    dest_ref[...] = jnp.where(sub == 0, d1, d2).astype(I32)

    pend_f = pend.astype(F32)
    pend_b = jnp.concatenate([pend_f] * (BLOCK_LANES // LANES), axis=1)
    blk_lane = lax.broadcasted_iota(I32, (1, BLOCK_LANES), 1)
    le = jnp.where(pend_b <= (blk_lane * MOE_BLOCK).astype(F32), 1.0, 0.0)
    be = jnp.minimum(jnp.sum(le, axis=0, keepdims=True), float(N_EXPERTS - 1))
    total = pend_b[N_EXPERTS - 1:N_EXPERTS, :]
    be = jnp.where(blk_lane == N_BLOCKS, total * (1.0 / MOE_BLOCK), be)
    be_ref[...] = jnp.broadcast_to(be, (SUBLANES, BLOCK_LANES)).astype(I32)
    seg_ref[0:N_EXPERTS, :] = pend - pcnt + cnt
    seg_ref[N_EXPERTS:2 * N_EXPERTS, :] = pend


def _dest(ri_t, cnt):
    tm = DEST_TM
    fixed = lambda i: (0, 0)
    tok = lambda i: (0, i)
    return pl.pallas_call(
        _dest_kernel,
        grid=(T // tm,),
        in_specs=[pl.BlockSpec((SUBLANES, tm), tok), pl.BlockSpec((N_EXPERTS, LANES), fixed)],
        out_specs=[pl.BlockSpec((SUBLANES, tm), tok),
                   pl.BlockSpec((SUBLANES, BLOCK_LANES), fixed),
                   pl.BlockSpec((2 * N_EXPERTS, LANES), fixed)],
        out_shape=[jax.ShapeDtypeStruct((SUBLANES, T), I32),
                   jax.ShapeDtypeStruct((SUBLANES, BLOCK_LANES), I32),
                   jax.ShapeDtypeStruct((2 * N_EXPERTS, LANES), I32)],
        compiler_params=_params(("arbitrary",)),
        name="dest",
    )(ri_t, cnt)


GROUP = 3
INV_LEN = (-(-N_BLOCKS // GROUP) * GROUP + GROUP - 1) * MOE_BLOCK
DUMP_ROWS = 4 * MOE_BLOCK


PAD_STEP = 8


def _invert_permutation(d1_ref, d2_ref, vend_ref, pend_ref, inv_ref, pos_end):
    def pad_range(lo, hi):
        def body(k, carry):
            p0 = lo + k * PAD_STEP
            for j in range(PAD_STEP):
                inv_ref[p0 + j] = N_ROUTED + ((p0 + j) & (DUMP_ROWS - 1))
            return carry

        lax.fori_loop(0, (hi - lo + (PAD_STEP - 1)) // PAD_STEP, body, 0)

    def pad_expert(e, carry):
        pad_range(vend_ref[e], pend_ref[e])
        return carry

    lax.fori_loop(0, N_EXPERTS, pad_expert, 0)
    pad_range(pend_ref[N_EXPERTS - 1], pos_end)

    def place(t, carry):
        inv_ref[d1_ref[t]] = 2 * t
        inv_ref[d2_ref[t]] = 2 * t + 1
        return carry

    lax.fori_loop(0, T, place, 0, unroll=16)


BLOCK_TILE_ROWS = MOE_BLOCK * ROW_TILE
Y_ROWS = (N_ROUTED + DUMP_ROWS) * ROW_TILE


def _moe_kernel(meta_ref, d1_ref, d2_ref, vend_ref, pend_ref, h_ref, wg_hbm, wu_hbm, wd_hbm, y_ref,
                xb0, xb1, xb2, yb0, yb1, yb2, wg_s, wu_s, wd_s, wg_b, wu_b, wd_b, inv_ref,
                sem_x, sem_y, sem_w):
    xbufs = (xb0, xb1, xb2)
    ybufs = (yb0, yb1, yb2)
    n_used = meta_ref[N_BLOCKS]
    n_groups = (n_used + (GROUP - 1)) // GROUP
    half = D_MODEL // 2

    def issue_gather(b, v):
        for r in range(MOE_BLOCK):
            tok = (inv_ref[b * MOE_BLOCK + r] >> 1) & (T - 1)
            src = h_ref.at[pl.ds(pl.multiple_of(tok * ROW_TILE, ROW_TILE), ROW_TILE), :]
            pltpu.make_async_copy(src, xbufs[v].at[pl.ds(r * ROW_TILE, ROW_TILE), :],
                                  sem_x.at[v]).start()

    def wait_gather(v):
        pltpu.make_async_copy(h_ref.at[pl.ds(0, BLOCK_TILE_ROWS), :], xbufs[v],
                              sem_x.at[v]).wait()

    def issue_scatter(b, v):
        for r in range(MOE_BLOCK):
            slot = inv_ref[b * MOE_BLOCK + r]
            dst = y_ref.at[pl.ds(pl.multiple_of(slot * ROW_TILE, ROW_TILE), ROW_TILE), :]
            pltpu.make_async_copy(ybufs[v].at[pl.ds(r * ROW_TILE, ROW_TILE), :], dst,
                                  sem_y.at[v]).start()

    def wait_scatter(v):
        pltpu.make_async_copy(ybufs[v], y_ref.at[pl.ds(0, BLOCK_TILE_ROWS), :],
                              sem_y.at[v]).wait()

    def weight_copies(e):
        return (pltpu.make_async_copy(wg_hbm.at[e], wg_s, sem_w),
                pltpu.make_async_copy(wu_hbm.at[e], wu_s, sem_w),
                pltpu.make_async_copy(wd_hbm.at[e], wd_s, sem_w))

    def load_expert(b):
        e = meta_ref[b]
        prev = meta_ref[jnp.maximum(b - 1, 0)]
        first = jnp.logical_and(b < n_used, jnp.logical_or(b == 0, e != prev))

        @pl.when(first)
        def _():
            for c in weight_copies(e):
                c.wait()
            wg_b[...] = wg_s[...].astype(BF16)
            wu_b[...] = wu_s[...].astype(BF16)
            wd_b[...] = wd_s[...].astype(BF16)
            nxt = lax.while_loop(
                lambda j: jnp.logical_and(j < n_used, meta_ref[jnp.minimum(j, N_BLOCKS - 1)] == e),
                lambda j: j + 1, b + 1)

            @pl.when(nxt < n_used)
            def _():
                for c in weight_copies(meta_ref[nxt]):
                    c.start(priority=1)

    def compute(v):
        lo, hi = _unpack_pairs_f32(_load_row_tiles(xbufs[v], MOE_BLOCK))
        lo, hi = lo.astype(BF16), hi.astype(BF16)
        g = (jnp.dot(lo, wg_b[0:half, :], preferred_element_type=F32)
             + jnp.dot(hi, wg_b[half:, :], preferred_element_type=F32))
        u = (jnp.dot(lo, wu_b[0:half, :], preferred_element_type=F32)
             + jnp.dot(hi, wu_b[half:, :], preferred_element_type=F32))
        a = g * (1.0 / (1.0 + jnp.exp(-g))) * u
        y = jnp.dot(a.astype(BF16), wd_b[...], preferred_element_type=F32)
        _store_row_tiles(ybufs[v], _pack_bf16_pairs(y))

    def block(b, v, scatter_prev, wait_y):
        wait_gather(v)
        if wait_y:
            wait_scatter(v)
        load_expert(b)
        issue_gather(b + 2, (v + 2) % GROUP)
        if scatter_prev:
            issue_scatter(b - 1, (v + 2) % GROUP)
        compute(v)

    for c in weight_copies(meta_ref[0]):
        c.start()
    yb2[...] = jnp.zeros_like(yb2)
    dump_fills = [
        pltpu.make_async_copy(
            yb2, y_ref.at[pl.ds((N_ROUTED + k * MOE_BLOCK) * ROW_TILE, BLOCK_TILE_ROWS), :],
            sem_y.at[GROUP - 1])
        for k in range(DUMP_ROWS // MOE_BLOCK)]
    for c in dump_fills:
        c.start()
    _invert_permutation(d1_ref, d2_ref, vend_ref, pend_ref, inv_ref,
                        (n_groups * GROUP + GROUP - 1) * MOE_BLOCK)
    for c in dump_fills:
        c.wait()
    issue_gather(0, 0)
    issue_gather(1, 1)
    for v in range(GROUP):
        block(v, v, scatter_prev=v > 0, wait_y=False)

    def group(gi, carry):
        for v in range(GROUP):
            block(gi * GROUP + v, v, scatter_prev=True, wait_y=True)
        return carry

    lax.fori_loop(1, n_groups, group, 0)

    last = n_groups * GROUP
    issue_scatter(last - 1, GROUP - 1)
    wait_gather(0)
    wait_gather(1)
    for v in range(GROUP):
        wait_scatter(v)


def _moe(meta, d1, d2, vend, pend, h_tiles, w_gate, w_up, w_down):
    any_spec = pl.BlockSpec(memory_space=pl.ANY)
    xy = pltpu.VMEM((BLOCK_TILE_ROWS, LANES), I32)
    return pl.pallas_call(
        _moe_kernel,
        grid_spec=pltpu.PrefetchScalarGridSpec(
            num_scalar_prefetch=5,
            grid=(1,),
            in_specs=[any_spec, any_spec, any_spec, any_spec],
            out_specs=any_spec,
            scratch_shapes=[
                xy, xy, xy, xy, xy, xy,
                pltpu.VMEM((D_MODEL, D_EXPERT), F32),
                pltpu.VMEM((D_MODEL, D_EXPERT), F32),
                pltpu.VMEM((D_EXPERT, D_MODEL), F32),
                pltpu.VMEM((D_MODEL, D_EXPERT), BF16),
                pltpu.VMEM((D_MODEL, D_EXPERT), BF16),
                pltpu.VMEM((D_EXPERT, D_MODEL), BF16),
                pltpu.SMEM((INV_LEN,), I32),
                pltpu.SemaphoreType.DMA((GROUP,)),
                pltpu.SemaphoreType.DMA((GROUP,)),
                pltpu.SemaphoreType.DMA(()),
            ],
        ),
        out_shape=jax.ShapeDtypeStruct((Y_ROWS, LANES), I32),
        compiler_params=_params(("arbitrary",)),
        name="moe",
    )(meta, d1, d2, vend, pend, h_tiles, w_gate, w_up, w_down)


COMB_TM = 512


def _combine_kernel(x1_ref, rw_ref, gf_ref, y_ref, o_ref):
    tm = COMB_TM
    w = rw_ref[...]
    w0, w1 = w[:, 0:1], w[:, 1:2]
    lo0, hi0 = _unpack_pairs_f32(_load_row_tiles(y_ref, tm, 0, 2 * ROW_TILE))
    lo1, hi1 = _unpack_pairs_f32(_load_row_tiles(y_ref, tm, ROW_TILE, 2 * ROW_TILE))
    moe = jnp.concatenate([lo0 * w0 + lo1 * w1, hi0 * w0 + hi1 * w1], axis=1)
    o_ref[...] = _rms(x1_ref[...] + moe, gf_ref[...])


def _combine(x1, rw, gf, y_tiles):
    tm = COMB_TM
    row = lambda i: (i, 0)
    return pl.pallas_call(
        _combine_kernel,
        grid=(T // tm,),
        in_specs=[
            pl.BlockSpec((tm, D_MODEL), row),
            pl.BlockSpec((tm, LANES), row),
            pl.BlockSpec((1, D_MODEL), lambda i: (0, 0)),
            pl.BlockSpec((tm * 2 * ROW_TILE, LANES), row),
        ],
        out_specs=pl.BlockSpec((tm, D_MODEL), row),
        out_shape=jax.ShapeDtypeStruct((T, D_MODEL), F32),
        compiler_params=_params(("arbitrary",)),
        name="combine",
    )(x1, rw, gf, y_tiles)


def _layer(x2, norm1, w_in, rel_bias, conv_w, g_out_attn, g_out_conv, w_out, norm2,
           w_rg, b_rg, w_re, b_re, w_gate, w_up, w_down):
    conv_w8 = jnp.zeros((SUBLANES, D_CONV), F32).at[0:3].set(conv_w)
    qkv, c_n = _inproj(x2, norm1[None, :], w_in.astype(BF16), conv_w8, g_out_conv[None, :])
    a_n = _attention(qkv, _bias_table(rel_bias), g_out_attn[None, :])

    w_r = jnp.zeros((D_MODEL, LANES), F32)
    g0 = GROUP_LOGIT_ROW
    w_r = w_r.at[:, 0:N_EXPERTS].set(w_re).at[:, g0:g0 + N_GROUPS].set(w_rg)
    b_r = jnp.zeros((1, LANES), F32)
    b_r = b_r.at[0, 0:N_EXPERTS].set(b_re).at[0, g0:g0 + N_GROUPS].set(b_rg)
    x1, hp, ri_t, rw, cnt = _outproj(a_n, c_n, x2, w_out.astype(BF16), norm2[None, :],
                                     w_r.astype(BF16), b_r)

    dest_t, be, seg = _dest(ri_t, cnt)
    y_tiles = _moe(be[0], dest_t[0], dest_t[1], seg[0:N_EXPERTS, 0],
                   seg[N_EXPERTS:2 * N_EXPERTS, 0], hp, w_gate, w_up, w_down)
    return x1, rw, y_tiles


def kernel(x, norm1, w_in, rel_bias, conv_w, g_out_attn, g_out_conv, w_out, norm2,
           w_router_group, b_router_group, w_router_expert, b_router_expert,
           w_gate, w_up, w_down, norm_final):
    assert x.shape == (BATCH, SEQ, D_MODEL) and norm1.shape[0] == 1
    x2 = x.reshape(T, D_MODEL)
    x1, rw, y_tiles = _layer(
        x2, norm1[0], w_in[0], rel_bias[0], conv_w[0], g_out_attn[0], g_out_conv[0],
        w_out[0], norm2[0], w_router_group[0], b_router_group[0], w_router_expert[0],
        b_router_expert[0], w_gate[0], w_up[0], w_down[0])
    out = _combine(x1, rw, norm_final[None, :], y_tiles)
    return out.reshape(BATCH, SEQ, D_MODEL)
```

```python
import functools

import jax
import jax.numpy as jnp
from jax import lax
from jax.experimental import pallas as pl
from jax.experimental.pallas import tpu as pltpu

F32 = jnp.float32
BF16 = jnp.bfloat16
I32 = jnp.int32

D_MODEL = 2048
BATCH = 4
SEQ = 4096
T = BATCH * SEQ
CHUNK = 64
LEFT_CHUNKS = 8
BAND = LEFT_CHUNKS + 1
KEYS = BAND * CHUNK
D_ATTN = 1024
D_CONV = 1024
HEAD_DIM = 64
N_HEADS = 16
N_PAIRS = N_HEADS // 2
REL_CLIP = 256
D_IN_PROJ = 6 * 1024
N_GROUPS = 4
EPG = 8
N_EXPERTS = 32
D_EXPERT = 512
MOE_BLOCK = 256
MOE_BLOCK_SHIFT = MOE_BLOCK.bit_length() - 1
assert MOE_BLOCK == 1 << MOE_BLOCK_SHIFT
N_ROUTED = 2 * T
N_BLOCKS = N_ROUTED // MOE_BLOCK + N_EXPERTS
EPS = 1e-6
NEG_INF = -1e30
LANES = 128
VMEM_LIMIT = 56 * 1024 * 1024

ATTN_TILE = 512
ATTN_TILES_PER_SEQ = SEQ // ATTN_TILE


def _params(sem):
    return pltpu.CompilerParams(dimension_semantics=sem, vmem_limit_bytes=VMEM_LIMIT)


def _rms(x, g):
    ms = jnp.mean(x * x, axis=-1, keepdims=True)
    return x * lax.rsqrt(ms + EPS) * g


W_STAGE_COLS = 256


def _load_weights_bf16(w_hbm, w_bf, stage, sem):
    n = w_bf.shape[1] // W_STAGE_COLS

    def slab(c):
        return pltpu.make_async_copy(
            w_hbm.at[:, pl.ds(c * W_STAGE_COLS, W_STAGE_COLS)], stage.at[c % 2], sem.at[c % 2])

    slab(0).start()
    for c in range(n):
        if c + 1 < n:
            slab(c + 1).start()
        slab(c).wait()
        w_bf[:, c * W_STAGE_COLS:(c + 1) * W_STAGE_COLS] = stage[c % 2].astype(BF16)


def _weight_scratch(rows, cols):
    return [pltpu.VMEM((rows, cols), BF16), pltpu.VMEM((2, rows, W_STAGE_COLS), F32),
            pltpu.SemaphoreType.DMA((2,))]


INPROJ_TM = 512
INPROJ_TN = 1536


D_QKV = 3 * D_ATTN
SUBLANES = 8


def _short_conv(b, z, z_before, w, g):
    n = z.shape[0]
    ze = jnp.concatenate([z_before, z], axis=0)
    z1 = pltpu.roll(ze, 1, axis=0)[SUBLANES:SUBLANES + n]
    z2 = pltpu.roll(ze, 2, axis=0)[SUBLANES:SUBLANES + n]
    y = w[0:1] * z2 + w[1:2] * z1 + w[2:3] * z
    return _rms(b * y, g).astype(BF16)


def _inproj_kernel(x_ref, g_ref, w_hbm, cw_ref, gc_ref, qkv_ref, cn_ref,
                   ztail_ref, w_ref, w_stage, w_sem):
    i = pl.program_id(0)

    @pl.when(i == 0)
    def _():
        ztail_ref[...] = jnp.zeros_like(ztail_ref)
        _load_weights_bf16(w_hbm, w_ref, w_stage, w_sem)

    hn = _rms(x_ref[...], g_ref[...]).astype(BF16)
    group = lambda k: jnp.dot(hn, w_ref[:, D_QKV + k * D_CONV:D_QKV + (k + 1) * D_CONV],
                              preferred_element_type=F32)
    b, z = group(0), group(1) * group(2)
    seq_start = (i % (SEQ // INPROJ_TM)) == 0
    z_before = jnp.where(seq_start, 0.0, ztail_ref[...])
    ztail_ref[...] = z[INPROJ_TM - SUBLANES:]
    cn_ref[...] = _short_conv(b, z, z_before, cw_ref[...], gc_ref[...])
    for j in range(D_QKV // INPROJ_TN):
        cols = slice(j * INPROJ_TN, (j + 1) * INPROJ_TN)
        qkv_ref[:, cols] = jnp.dot(hn, w_ref[:, cols],
                                   preferred_element_type=F32).astype(BF16)


def _inproj(x2, g, w_in, conv_w, g_conv):
    tm = INPROJ_TM
    fixed = lambda i: (0, 0)
    row = lambda i: (i, 0)
    return pl.pallas_call(
        _inproj_kernel,
        grid=(T // tm,),
        in_specs=[
            pl.BlockSpec((tm, D_MODEL), row),
            pl.BlockSpec((1, D_MODEL), fixed),
            pl.BlockSpec(memory_space=pl.ANY),
            pl.BlockSpec((SUBLANES, D_CONV), fixed),
            pl.BlockSpec((1, D_CONV), fixed),
        ],
        out_specs=[pl.BlockSpec((tm, D_QKV), row), pl.BlockSpec((tm, D_CONV), row)],
        out_shape=[jax.ShapeDtypeStruct((T, D_QKV), BF16),
                   jax.ShapeDtypeStruct((T, D_CONV), BF16)],
        scratch_shapes=[pltpu.VMEM((SUBLANES, D_CONV), F32)]
        + _weight_scratch(D_MODEL, D_IN_PROJ),
        compiler_params=_params(("arbitrary",)),
        name="inproj",
    )(x2, g, w_in, conv_w, g_conv)


N_QUADS = N_HEADS // 4
QUAD = 4 * HEAD_DIM
BIAS_W = 640
BIAS_WIN = 512
P_BUFS = 4


def _bias_kernel(b_ref, o_ref):
    first_head = lax.broadcasted_iota(I32, (KEYS, LANES), 1) < HEAD_DIM
    for hq in range(N_QUADS):
        for col in range(2):
            h0 = 4 * hq + 2 * col
            pieces = []
            for hh in range(2):
                x = jnp.broadcast_to(b_ref[h0 + hh:h0 + hh + 1, :], (KEYS, BIAS_W))
                rolled = pltpu.roll(x, 0, 1, stride=1, stride_axis=0)
                pieces.append(rolled[:, BIAS_WIN:BIAS_WIN + LANES])
            o_ref[hq, :, col * LANES:(col + 1) * LANES] = jnp.where(
                first_head, pieces[0], pieces[1])


def _bias_table(rel_bias):
    r = rel_bias.astype(F32)
    edge = jnp.broadcast_to(r[:, 2 * REL_CLIP:], (N_HEADS, KEYS - REL_CLIP + 1))
    b = jnp.concatenate([r[:, REL_CLIP:2 * REL_CLIP], edge,
                         r[:, REL_CLIP - (BIAS_W - KEYS - 1):REL_CLIP]], axis=1)
    odd = jnp.concatenate([b[:, BIAS_W - HEAD_DIM:], b[:, :BIAS_W - HEAD_DIM]], axis=1)
    is_odd = (jnp.arange(N_HEADS) % 2 == 1)[:, None]
    bvec = jnp.where(is_odd, odd, b)
    return pl.pallas_call(
        _bias_kernel,
        out_shape=jax.ShapeDtypeStruct((N_QUADS, KEYS, QUAD), F32),
        compiler_params=pltpu.CompilerParams(vmem_limit_bytes=VMEM_LIMIT),
        name="biastab",
    )(bvec)


def _attn_kernel(q_ref, kp_ref, kc_ref, vp_ref, vc_ref, bias_ref, g_ref, o_ref,
                 st_ref, p_ref, a_ref):
    i = pl.program_id(1)

    def window(prev_ref, cur_ref, r0, n, cols):
        parts = []
        if r0 < ATTN_TILE:
            parts.append(prev_ref[r0:min(ATTN_TILE, r0 + n), cols])
        if r0 + n > ATTN_TILE:
            parts.append(cur_ref[max(r0, ATTN_TILE) - ATTN_TILE:r0 + n - ATTN_TILE, cols])
        return parts[0] if len(parts) == 1 else jnp.concatenate(parts, axis=0)

    head_of_lane = lax.broadcasted_iota(I32, (CHUNK, QUAD), 1) // HEAD_DIM
    first_head = lax.broadcasted_iota(I32, (CHUNK, LANES), 1) < HEAD_DIM
    key_row = lax.broadcasted_iota(I32, (KEYS, QUAD), 0)
    n_rb = KEYS // CHUNK

    all_quads = tuple(range(N_QUADS))
    all_pairs = tuple(range(N_PAIRS))

    def scores(r0, buf, masked, quads=all_quads):
        for hq in quads:
            c0 = hq * QUAD
            q4 = q_ref[pl.ds(r0, CHUNK), c0:c0 + QUAD] * (HEAD_DIM ** -0.5)
            zero = jnp.zeros_like(q4)
            qbd = jnp.concatenate(
                [jnp.where(head_of_lane == h, q4, zero) for h in range(4)], axis=0)
            k4 = window(kp_ref, kc_ref, r0, KEYS, slice(c0, c0 + QUAD))
            st = lax.dot_general(k4, qbd, (((1,), (1,)), ((), ())),
                                 preferred_element_type=F32)
            st = st + bias_ref[hq]
            if masked:
                st = jnp.where(key_row + r0 >= ATTN_TILE, st, NEG_INF)
            st_ref[buf, hq] = st

    def exps(buf, pbuf, off, quads=all_quads):
        for hq in quads:
            m = st_ref[buf, hq, 0:CHUNK, :]
            for rb in range(1, n_rb):
                m = jnp.maximum(m, st_ref[buf, hq, rb * CHUNK:(rb + 1) * CHUNK, :])
            m = jnp.max(m, axis=0, keepdims=True)
            for rb in range(n_rb):
                rows = slice(rb * CHUNK, (rb + 1) * CHUNK)
                prow = slice(off + rb * CHUNK, off + (rb + 1) * CHUNK)
                p_ref[pbuf, hq, prow, :] = jnp.exp(st_ref[buf, hq, rows, :] - m).astype(BF16)

    def values(r0, pbufs, pairs=all_pairs):
        nq = len(pbufs)
        span = KEYS + (nq - 1) * CHUNK
        ones = jnp.ones((span, LANES), BF16)
        for hp in pairs:
            c0 = hp * LANES
            half = (hp % 2) * LANES
            pt = jnp.concatenate(
                [p_ref[b, hp // 2, 0:span, half:half + LANES] for b in pbufs], axis=1)
            vext = jnp.concatenate(
                [window(vp_ref, vc_ref, r0, span, slice(c0, c0 + LANES)), ones], axis=1)
            oe = lax.dot_general(pt, vext, (((0,), (0,)), ((), ())),
                                 preferred_element_type=F32)
            o = oe[:, :LANES] * (1.0 / oe[:, LANES:])
            for j in range(nq):
                oj = o[2 * j * CHUNK:2 * (j + 1) * CHUNK]
                a_ref[pl.ds(r0 + j * CHUNK, CHUNK), c0:c0 + LANES] = jnp.where(
                    first_head, oj[0:CHUNK], oj[CHUNK:2 * CHUNK])

    n_chunks = ATTN_TILE // CHUNK

    def tile(masked):
        for b in range(P_BUFS):
            z = slice(KEYS, KEYS + CHUNK) if b % 2 == 0 else slice(0, CHUNK)
            p_ref[b, :, z, :] = jnp.zeros((N_QUADS, CHUNK, QUAD), BF16)
        scores(0, 0, masked)
        for ci in range(n_chunks):
            c0 = ci - 2 - ci % 2
            for hq in all_quads:
                if ci + 1 < n_chunks:
                    scores((ci + 1) * CHUNK, (ci + 1) % 2, masked, (hq,))
                if c0 >= 0:
                    values(c0 * CHUNK, (c0 % P_BUFS, (c0 + 1) % P_BUFS),
                           ((ci % 2) * N_QUADS + hq,))
                exps(ci % 2, ci % P_BUFS, (ci % 2) * CHUNK, (hq,))
        values((n_chunks - 2) * CHUNK, ((n_chunks - 2) % P_BUFS, (n_chunks - 1) % P_BUFS))

    pl.when(i == 0)(functools.partial(tile, True))
    pl.when(i > 0)(functools.partial(tile, False))

    o_ref[...] = _rms(a_ref[...], g_ref[...]).astype(BF16)


def _attention(proj, bias_t, g):
    n = ATTN_TILES_PER_SEQ
    cur = lambda col: (lambda b, i: (b * n + i, col))
    prev = lambda col: (lambda b, i: (b * n + jnp.maximum(i - 1, 0), col))
    blk = (ATTN_TILE, D_ATTN)
    return pl.pallas_call(
        _attn_kernel,
        grid=(BATCH, n),
        in_specs=[
            pl.BlockSpec(blk, cur(0)),
            pl.BlockSpec(blk, prev(1)),
            pl.BlockSpec(blk, cur(1)),
            pl.BlockSpec(blk, prev(2)),
            pl.BlockSpec(blk, cur(2)),
            pl.BlockSpec((N_QUADS, KEYS, QUAD), lambda b, i: (0, 0, 0)),
            pl.BlockSpec((1, D_ATTN), lambda b, i: (0, 0)),
        ],
        out_specs=pl.BlockSpec(blk, lambda b, i: (b * n + i, 0)),
        out_shape=jax.ShapeDtypeStruct((T, D_ATTN), BF16),
        scratch_shapes=[
            pltpu.VMEM((2, N_QUADS, KEYS, QUAD), F32),
            pltpu.VMEM((P_BUFS, N_QUADS, KEYS + CHUNK, QUAD), BF16),
            pltpu.VMEM((ATTN_TILE, D_ATTN), F32),
        ],
        compiler_params=_params(("arbitrary", "arbitrary")),
        name="attn",
    )(proj, proj, proj, proj, proj, bias_t, g)


OUT_TM = 512
OUT_SUB = 2


def _pack_bf16_pairs(h):
    half = D_MODEL // 2
    return pltpu.pack_elementwise([h[:, :half], h[:, half:]], packed_dtype=BF16)


def _unpack_pairs_f32(u):
    lo = pltpu.unpack_elementwise(u, index=0, packed_dtype=BF16, unpacked_dtype=F32)
    hi = pltpu.unpack_elementwise(u, index=1, packed_dtype=BF16, unpacked_dtype=F32)
    return lo, hi


ROW_TILE = 8


def _store_row_tiles(ref, packed):
    m = packed.shape[0]
    for s in range(ROW_TILE):
        ref[pl.ds(s, m, stride=ROW_TILE), :] = packed[:, s * LANES:(s + 1) * LANES]


def _load_row_tiles(ref, m, first=0, stride=ROW_TILE):
    return jnp.concatenate(
        [ref[pl.ds(first + s, m, stride=stride), :] for s in range(ROW_TILE)], axis=1)


GROUP_LOGIT_ROW = N_EXPERTS


def _route_cols(lg, base_ref):
    tm = lg.shape[0]
    lt = lg.T
    sub = lax.broadcasted_iota(I32, (SUBLANES, tm), 0)
    sub_f = sub.astype(F32)
    big = jnp.float32(SUBLANES)
    ninf = jnp.float32(-jnp.inf)

    gl = lt[GROUP_LOGIT_ROW:GROUP_LOGIT_ROW + SUBLANES, :]
    gvalid = sub < N_GROUPS
    gmax = jnp.max(jnp.where(gvalid, gl, ninf), axis=0, keepdims=True)
    gsum = jnp.sum(jnp.where(gvalid, jnp.exp(gl - gmax), 0.0), axis=0, keepdims=True)
    p_g = 1.0 / gsum
    g_sel = jnp.min(jnp.where(jnp.logical_and(gvalid, gl == gmax), sub_f, big),
                    axis=0, keepdims=True)
    el = lt[0:EPG, :]
    for g in range(1, N_GROUPS):
        el = jnp.where(g_sel == g, lt[g * EPG:(g + 1) * EPG, :], el)
    v1 = jnp.max(el, axis=0, keepdims=True)
    i1 = jnp.min(jnp.where(el == v1, sub_f, big), axis=0, keepdims=True)
    rest = sub_f != i1
    v2 = jnp.max(jnp.where(rest, el, ninf), axis=0, keepdims=True)
    i2 = jnp.min(jnp.where(jnp.logical_and(rest, el == v2), sub_f, big),
                 axis=0, keepdims=True)
    t = jnp.exp(v2 - v1)
    w1 = p_g * (1.0 / (1.0 + t))
    w2 = p_g * (t / (1.0 + t))
    e1 = g_sel * EPG + i1
    e2 = g_sel * EPG + i2

    erow = lax.broadcasted_iota(I32, (N_EXPERTS, tm), 0).astype(F32)
    hit1 = erow == e1
    hit2 = erow == e2
    onehot = jnp.where(jnp.logical_or(hit1, hit2), 1.0, 0.0)
    r = lax.broadcasted_iota(I32, (tm, tm), 0)
    c = lax.broadcasted_iota(I32, (tm, tm), 1)
    tri = jnp.where(r < c, 1.0, 0.0).astype(BF16)
    base = base_ref[...]
    before = (jnp.dot(onehot.astype(BF16), tri, preferred_element_type=F32)
              + jnp.concatenate([base] * (tm // LANES), axis=1))
    r1 = jnp.sum(jnp.where(hit1, before, 0.0), axis=0, keepdims=True)
    r2 = jnp.sum(jnp.where(hit2, before, 0.0), axis=0, keepdims=True)
    base_ref[...] = base + jnp.broadcast_to(
        jnp.sum(onehot, axis=1, keepdims=True), (N_EXPERTS, LANES))

    ri_t = jnp.where(sub == 0, e1, jnp.where(sub == 1, e2, jnp.where(sub == 2, r1, r2)))
    w_t = jnp.where(sub == 0, w1, jnp.where(sub == 1, w2, 0.0))
    w_t = jnp.concatenate([w_t, jnp.zeros((LANES - SUBLANES, tm), F32)], axis=0)
    return ri_t.astype(I32), w_t.T


def _outproj_kernel(a_ref, cn_ref, x_ref, wo_hbm, g2_ref, wr_ref, br_ref,
                    x1_ref, hp_ref, ri_ref, rw_ref, cnt_ref, base_ref, wo_ref, wo_stage, wo_sem):
    @pl.when(pl.program_id(0) == 0)
    def _():
        base_ref[...] = jnp.zeros_like(base_ref)
        _load_weights_bf16(wo_hbm, wo_ref, wo_stage, wo_sem)

    sub = OUT_TM // OUT_SUB
    rows_of = lambda s: slice(s * sub, (s + 1) * sub)

    def project(s):
        rows = rows_of(s)
        acc = jnp.dot(a_ref[rows, :], wo_ref[0:D_ATTN, :], preferred_element_type=F32)
        acc = acc + jnp.dot(cn_ref[rows, :], wo_ref[D_ATTN:, :], preferred_element_type=F32)
        x1_ref[rows, :] = x_ref[rows, :] + acc

    def route(s):
        rows = rows_of(s)
        h2 = _rms(x1_ref[rows, :], g2_ref[...])
        logits = jnp.dot(h2.astype(BF16), wr_ref[...],
                         preferred_element_type=F32) + br_ref[...]
        _store_row_tiles(hp_ref.at[pl.ds(s * sub * ROW_TILE, sub * ROW_TILE), :],
                         _pack_bf16_pairs(h2))
        ri_t, rw = _route_cols(logits, base_ref)
        ri_ref[:, rows] = ri_t
        rw_ref[rows, :] = rw

    for s in range(OUT_SUB):
        project(s)
        if s > 0:
            route(s - 1)
    route(OUT_SUB - 1)
    cnt_ref[...] = base_ref[...]


def _outproj(a_n, c_n, x2, w_out, g2, wr_bf, br):
    tm = OUT_TM
    row = lambda i: (i, 0)
    fixed = lambda i: (0, 0)
    return pl.pallas_call(
        _outproj_kernel,
        grid=(T // tm,),
        in_specs=[
            pl.BlockSpec((tm, D_ATTN), row),
            pl.BlockSpec((tm, D_CONV), row),
            pl.BlockSpec((tm, D_MODEL), row),
            pl.BlockSpec(memory_space=pl.ANY),
            pl.BlockSpec((1, D_MODEL), fixed),
            pl.BlockSpec((D_MODEL, LANES), fixed),
            pl.BlockSpec((1, LANES), fixed),
        ],
        out_specs=[
            pl.BlockSpec((tm, D_MODEL), row),
            pl.BlockSpec((tm * ROW_TILE, LANES), row),
            pl.BlockSpec((SUBLANES, tm), lambda i: (0, i)),
            pl.BlockSpec((tm, LANES), row),
            pl.BlockSpec((N_EXPERTS, LANES), fixed),
        ],
        out_shape=[
            jax.ShapeDtypeStruct((T, D_MODEL), F32),
            jax.ShapeDtypeStruct((T * ROW_TILE, LANES), I32),
            jax.ShapeDtypeStruct((SUBLANES, T), I32),
            jax.ShapeDtypeStruct((T, LANES), F32),
            jax.ShapeDtypeStruct((N_EXPERTS, LANES), F32),
        ],
        scratch_shapes=[pltpu.VMEM((N_EXPERTS, LANES), F32)]
        + _weight_scratch(D_MODEL, D_MODEL),
        compiler_params=_params(("arbitrary",)),
        name="outproj",
    )(a_n, c_n, x2, w_out, g2, wr_bf, br)


BLOCK_LANES = 384
DEST_TM = 2048


def _dest_kernel(ri_ref, cnt_ref, dest_ref, be_ref, seg_ref):
    tm = DEST_TM
    erow = lax.broadcasted_iota(I32, (N_EXPERTS, LANES), 0)
    cnt = cnt_ref[...].astype(I32)
    pcnt = ((cnt + (MOE_BLOCK - 1)) >> MOE_BLOCK_SHIFT) << MOE_BLOCK_SHIFT
    pend = pcnt
    for s in (1, 2, 4, 8, 16):
        pend = pend + jnp.where(erow >= s, pltpu.roll(pend, s, axis=0), 0)
    pstart = (pend - pcnt).astype(F32)

    ri = ri_ref[...].astype(F32)
    etok = lax.broadcasted_iota(I32, (N_EXPERTS, tm), 0).astype(F32)
    pstart_t = jnp.concatenate([pstart] * (tm // LANES), axis=1)
    d1 = jnp.sum(jnp.where(etok == ri[0:1], pstart_t, 0.0), axis=0, keepdims=True) + ri[2:3]
    d2 = jnp.sum(jnp.where(etok == ri[1:2], pstart_t, 0.0), axis=0, keepdims=True) + ri[3:4]
    sub = lax.broadcasted_iota(I32, (SUBLANES, tm), 0)
    dest_ref[...] = jnp.where(sub == 0, d1, d2).astype(I32)

    pend_f = pend.astype(F32)
    pend_b = jnp.concatenate([pend_f] * (BLOCK_LANES // LANES), axis=1)
    blk_lane = lax.broadcasted_iota(I32, (1, BLOCK_LANES), 1)
    le = jnp.where(pend_b <= (blk_lane * MOE_BLOCK).astype(F32), 1.0, 0.0)
    be = jnp.minimum(jnp.sum(le, axis=0, keepdims=True), float(N_EXPERTS - 1))
    total = pend_b[N_EXPERTS - 1:N_EXPERTS, :]
    be = jnp.where(blk_lane == N_BLOCKS, total * (1.0 / MOE_BLOCK), be)
    be_ref[...] = jnp.broadcast_to(be, (SUBLANES, BLOCK_LANES)).astype(I32)
    seg_ref[0:N_EXPERTS, :] = pend - pcnt + cnt
    seg_ref[N_EXPERTS:2 * N_EXPERTS, :] = pend


def _dest(ri_t, cnt):
    tm = DEST_TM
    fixed = lambda i: (0, 0)
    tok = lambda i: (0, i)
    return pl.pallas_call(
        _dest_kernel,
        grid=(T // tm,),
        in_specs=[pl.BlockSpec((SUBLANES, tm), tok), pl.BlockSpec((N_EXPERTS, LANES), fixed)],
        out_specs=[pl.BlockSpec((SUBLANES, tm), tok),
                   pl.BlockSpec((SUBLANES, BLOCK_LANES), fixed),
                   pl.BlockSpec((2 * N_EXPERTS, LANES), fixed)],
        out_shape=[jax.ShapeDtypeStruct((SUBLANES, T), I32),
                   jax.ShapeDtypeStruct((SUBLANES, BLOCK_LANES), I32),
                   jax.ShapeDtypeStruct((2 * N_EXPERTS, LANES), I32)],
        compiler_params=_params(("arbitrary",)),
        name="dest",
    )(ri_t, cnt)


GROUP = 3
INV_LEN = (-(-N_BLOCKS // GROUP) * GROUP + GROUP - 1) * MOE_BLOCK
DUMP_ROWS = 4 * MOE_BLOCK


PAD_STEP = 8


def _invert_permutation(d1_ref, d2_ref, vend_ref, pend_ref, inv_ref, pos_end):
    def pad_range(lo, hi):
        def body(k, carry):
            p0 = lo + k * PAD_STEP
            for j in range(PAD_STEP):
                inv_ref[p0 + j] = N_ROUTED + ((p0 + j) & (DUMP_ROWS - 1))
            return carry

        lax.fori_loop(0, (hi - lo + (PAD_STEP - 1)) // PAD_STEP, body, 0)

    def pad_expert(e, carry):
        pad_range(vend_ref[e], pend_ref[e])
        return carry

    lax.fori_loop(0, N_EXPERTS, pad_expert, 0)
    pad_range(pend_ref[N_EXPERTS - 1], pos_end)

    def place(t, carry):
        inv_ref[d1_ref[t]] = 2 * t
        inv_ref[d2_ref[t]] = 2 * t + 1
        return carry

    lax.fori_loop(0, T, place, 0, unroll=16)


BLOCK_TILE_ROWS = MOE_BLOCK * ROW_TILE
Y_ROWS = (N_ROUTED + DUMP_ROWS) * ROW_TILE


def _moe_kernel(meta_ref, d1_ref, d2_ref, vend_ref, pend_ref, h_ref, wg_hbm, wu_hbm, wd_hbm, y_ref,
                xb0, xb1, xb2, yb0, yb1, yb2, wg_s, wu_s, wd_s, wg_b, wu_b, wd_b, inv_ref,
                sem_x, sem_y, sem_w):
    xbufs = (xb0, xb1, xb2)
    ybufs = (yb0, yb1, yb2)
    n_used = meta_ref[N_BLOCKS]
    n_groups = (n_used + (GROUP - 1)) // GROUP
    half = D_MODEL // 2

    def issue_gather(b, v):
        for r in range(MOE_BLOCK):
            tok = (inv_ref[b * MOE_BLOCK + r] >> 1) & (T - 1)
            src = h_ref.at[pl.ds(pl.multiple_of(tok * ROW_TILE, ROW_TILE), ROW_TILE), :]
            pltpu.make_async_copy(src, xbufs[v].at[pl.ds(r * ROW_TILE, ROW_TILE), :],
                                  sem_x.at[v]).start()

    def wait_gather(v):
        pltpu.make_async_copy(h_ref.at[pl.ds(0, BLOCK_TILE_ROWS), :], xbufs[v],
                              sem_x.at[v]).wait()

    def issue_scatter(b, v):
        for r in range(MOE_BLOCK):
            slot = inv_ref[b * MOE_BLOCK + r]
            dst = y_ref.at[pl.ds(pl.multiple_of(slot * ROW_TILE, ROW_TILE), ROW_TILE), :]
            pltpu.make_async_copy(ybufs[v].at[pl.ds(r * ROW_TILE, ROW_TILE), :], dst,
                                  sem_y.at[v]).start()

    def wait_scatter(v):
        pltpu.make_async_copy(ybufs[v], y_ref.at[pl.ds(0, BLOCK_TILE_ROWS), :],
                              sem_y.at[v]).wait()

    def weight_copies(e):
        return (pltpu.make_async_copy(wg_hbm.at[e], wg_s, sem_w),
                pltpu.make_async_copy(wu_hbm.at[e], wu_s, sem_w),
                pltpu.make_async_copy(wd_hbm.at[e], wd_s, sem_w))

    def load_expert(b):
        e = meta_ref[b]
        prev = meta_ref[jnp.maximum(b - 1, 0)]
        first = jnp.logical_and(b < n_used, jnp.logical_or(b == 0, e != prev))

        @pl.when(first)
        def _():
            for c in weight_copies(e):
                c.wait()
            wg_b[...] = wg_s[...].astype(BF16)
            wu_b[...] = wu_s[...].astype(BF16)
            wd_b[...] = wd_s[...].astype(BF16)
            nxt = lax.while_loop(
                lambda j: jnp.logical_and(j < n_used, meta_ref[jnp.minimum(j, N_BLOCKS - 1)] == e),
                lambda j: j + 1, b + 1)

            @pl.when(nxt < n_used)
            def _():
                for c in weight_copies(meta_ref[nxt]):
                    c.start(priority=1)

    def compute(v):
        lo, hi = _unpack_pairs_f32(_load_row_tiles(xbufs[v], MOE_BLOCK))
        lo, hi = lo.astype(BF16), hi.astype(BF16)
        g = (jnp.dot(lo, wg_b[0:half, :], preferred_element_type=F32)
             + jnp.dot(hi, wg_b[half:, :], preferred_element_type=F32))
        u = (jnp.dot(lo, wu_b[0:half, :], preferred_element_type=F32)
             + jnp.dot(hi, wu_b[half:, :], preferred_element_type=F32))
        a = g * (1.0 / (1.0 + jnp.exp(-g))) * u
        y = jnp.dot(a.astype(BF16), wd_b[...], preferred_element_type=F32)
        _store_row_tiles(ybufs[v], _pack_bf16_pairs(y))

    def block(b, v, scatter_prev, wait_y):
        wait_gather(v)
        if wait_y:
            wait_scatter(v)
        load_expert(b)
        issue_gather(b + 2, (v + 2) % GROUP)
        if scatter_prev:
            issue_scatter(b - 1, (v + 2) % GROUP)
        compute(v)

    for c in weight_copies(meta_ref[0]):
        c.start()
    yb2[...] = jnp.zeros_like(yb2)
    dump_fills = [
        pltpu.make_async_copy(
            yb2, y_ref.at[pl.ds((N_ROUTED + k * MOE_BLOCK) * ROW_TILE, BLOCK_TILE_ROWS), :],
            sem_y.at[GROUP - 1])
        for k in range(DUMP_ROWS // MOE_BLOCK)]
    for c in dump_fills:
        c.start()
    _invert_permutation(d1_ref, d2_ref, vend_ref, pend_ref, inv_ref,
                        (n_groups * GROUP + GROUP - 1) * MOE_BLOCK)
    for c in dump_fills:
        c.wait()
    issue_gather(0, 0)
    issue_gather(1, 1)
    for v in range(GROUP):
        block(v, v, scatter_prev=v > 0, wait_y=False)

    def group(gi, carry):
        for v in range(GROUP):
            block(gi * GROUP + v, v, scatter_prev=True, wait_y=True)
        return carry

    lax.fori_loop(1, n_groups, group, 0)

    last = n_groups * GROUP
    issue_scatter(last - 1, GROUP - 1)
    wait_gather(0)
    wait_gather(1)
    for v in range(GROUP):
        wait_scatter(v)


def _moe(meta, d1, d2, vend, pend, h_tiles, w_gate, w_up, w_down):
    any_spec = pl.BlockSpec(memory_space=pl.ANY)
    xy = pltpu.VMEM((BLOCK_TILE_ROWS, LANES), I32)
    return pl.pallas_call(
        _moe_kernel,
        grid_spec=pltpu.PrefetchScalarGridSpec(
            num_scalar_prefetch=5,
            grid=(1,),
            in_specs=[any_spec, any_spec, any_spec, any_spec],
            out_specs=any_spec,
            scratch_shapes=[
                xy, xy, xy, xy, xy, xy,
                pltpu.VMEM((D_MODEL, D_EXPERT), F32),
                pltpu.VMEM((D_MODEL, D_EXPERT), F32),
                pltpu.VMEM((D_EXPERT, D_MODEL), F32),
                pltpu.VMEM((D_MODEL, D_EXPERT), BF16),
                pltpu.VMEM((D_MODEL, D_EXPERT), BF16),
                pltpu.VMEM((D_EXPERT, D_MODEL), BF16),
                pltpu.SMEM((INV_LEN,), I32),
                pltpu.SemaphoreType.DMA((GROUP,)),
                pltpu.SemaphoreType.DMA((GROUP,)),
                pltpu.SemaphoreType.DMA(()),
            ],
        ),
        out_shape=jax.ShapeDtypeStruct((Y_ROWS, LANES), I32),
        compiler_params=_params(("arbitrary",)),
        name="moe",
    )(meta, d1, d2, vend, pend, h_tiles, w_gate, w_up, w_down)


COMB_TM = 512


def _combine_kernel(x1_ref, rw_ref, gf_ref, y_ref, o_ref):
    tm = COMB_TM
    w = rw_ref[...]
    w0, w1 = w[:, 0:1], w[:, 1:2]
    lo0, hi0 = _unpack_pairs_f32(_load_row_tiles(y_ref, tm, 0, 2 * ROW_TILE))
    lo1, hi1 = _unpack_pairs_f32(_load_row_tiles(y_ref, tm, ROW_TILE, 2 * ROW_TILE))
    moe = jnp.concatenate([lo0 * w0 + lo1 * w1, hi0 * w0 + hi1 * w1], axis=1)
    o_ref[...] = _rms(x1_ref[...] + moe, gf_ref[...])


def _combine(x1, rw, gf, y_tiles):
    tm = COMB_TM
    row = lambda i: (i, 0)
    return pl.pallas_call(
        _combine_kernel,
        grid=(T // tm,),
        in_specs=[
            pl.BlockSpec((tm, D_MODEL), row),
            pl.BlockSpec((tm, LANES), row),
            pl.BlockSpec((1, D_MODEL), lambda i: (0, 0)),
            pl.BlockSpec((tm * 2 * ROW_TILE, LANES), row),
        ],
        out_specs=pl.BlockSpec((tm, D_MODEL), row),
        out_shape=jax.ShapeDtypeStruct((T, D_MODEL), F32),
        compiler_params=_params(("arbitrary",)),
        name="combine",
    )(x1, rw, gf, y_tiles)


def _layer(x2, norm1, w_in, rel_bias, conv_w, g_out_attn, g_out_conv, w_out, norm2,
           w_rg, b_rg, w_re, b_re, w_gate, w_up, w_down):
    conv_w8 = jnp.zeros((SUBLANES, D_CONV), F32).at[0:3].set(conv_w)
    qkv, c_n = _inproj(x2, norm1[None, :], w_in, conv_w8, g_out_conv[None, :])
    a_n = _attention(qkv, _bias_table(rel_bias), g_out_attn[None, :])

    w_r = jnp.zeros((D_MODEL, LANES), F32)
    g0 = GROUP_LOGIT_ROW
    w_r = w_r.at[:, 0:N_EXPERTS].set(w_re).at[:, g0:g0 + N_GROUPS].set(w_rg)
    b_r = jnp.zeros((1, LANES), F32)
    b_r = b_r.at[0, 0:N_EXPERTS].set(b_re).at[0, g0:g0 + N_GROUPS].set(b_rg)
    x1, hp, ri_t, rw, cnt = _outproj(a_n, c_n, x2, w_out, norm2[None, :],
                                     w_r.astype(BF16), b_r)

    dest_t, be, seg = _dest(ri_t, cnt)
    y_tiles = _moe(be[0], dest_t[0], dest_t[1], seg[0:N_EXPERTS, 0],
                   seg[N_EXPERTS:2 * N_EXPERTS, 0], hp, w_gate, w_up, w_down)
    return x1, rw, y_tiles


def kernel(x, norm1, w_in, rel_bias, conv_w, g_out_attn, g_out_conv, w_out, norm2,
           w_router_group, b_router_group, w_router_expert, b_router_expert,
           w_gate, w_up, w_down, norm_final):
    assert x.shape == (BATCH, SEQ, D_MODEL) and norm1.shape[0] == 1
    x2 = x.reshape(T, D_MODEL)
    x1, rw, y_tiles = _layer(
        x2, norm1[0], w_in[0], rel_bias[0], conv_w[0], g_out_attn[0], g_out_conv[0],
        w_out[0], norm2[0], w_router_group[0], b_router_group[0], w_router_expert[0],
        b_router_expert[0], w_gate[0], w_up[0], w_down[0])
    out = _combine(x1, rw, norm_final[None, :], y_tiles)
    return out.reshape(BATCH, SEQ, D_MODEL)
```

```python
import functools

import jax
import jax.numpy as jnp
from jax import lax
from jax.experimental import pallas as pl
from jax.experimental.pallas import tpu as pltpu

F32 = jnp.float32
BF16 = jnp.bfloat16
I32 = jnp.int32

D_MODEL = 2048
BATCH = 4
SEQ = 4096
T = BATCH * SEQ
CHUNK = 64
LEFT_CHUNKS = 8
BAND = LEFT_CHUNKS + 1
KEYS = BAND * CHUNK
D_ATTN = 1024
D_CONV = 1024
HEAD_DIM = 64
N_HEADS = 16
N_PAIRS = N_HEADS // 2
REL_CLIP = 256
D_IN_PROJ = 6 * 1024
N_GROUPS = 4
EPG = 8
N_EXPERTS = 32
D_EXPERT = 512
MOE_BLOCK = 256
MOE_BLOCK_SHIFT = MOE_BLOCK.bit_length() - 1
assert MOE_BLOCK == 1 << MOE_BLOCK_SHIFT
N_ROUTED = 2 * T
N_BLOCKS = N_ROUTED // MOE_BLOCK + N_EXPERTS
EPS = 1e-6
NEG_INF = -1e30
LANES = 128
VMEM_LIMIT = 52 * 1024 * 1024

ATTN_TILE = 512
ATTN_TILES_PER_SEQ = SEQ // ATTN_TILE


def _params(sem):
    return pltpu.CompilerParams(dimension_semantics=sem, vmem_limit_bytes=VMEM_LIMIT)


def _rms(x, g):
    ms = jnp.mean(x * x, axis=-1, keepdims=True)
    return x * lax.rsqrt(ms + EPS) * g


INPROJ_TM = 512
INPROJ_TN = 1536


D_QKV = 3 * D_ATTN
SUBLANES = 8


def _short_conv(b, z, z_before, w, g):
    n = z.shape[0]
    ze = jnp.concatenate([z_before, z], axis=0)
    z1 = pltpu.roll(ze, 1, axis=0)[SUBLANES:SUBLANES + n]
    z2 = pltpu.roll(ze, 2, axis=0)[SUBLANES:SUBLANES + n]
    y = w[0:1] * z2 + w[1:2] * z1 + w[2:3] * z
    return _rms(b * y, g).astype(BF16)


def _inproj_kernel(x_ref, g_ref, w_ref, cw_ref, gc_ref, qkv_ref, cn_ref, ztail_ref):
    i = pl.program_id(0)

    @pl.when(i == 0)
    def _():
        ztail_ref[...] = jnp.zeros_like(ztail_ref)

    hn = _rms(x_ref[...], g_ref[...]).astype(BF16)
    group = lambda k: jnp.dot(hn, w_ref[:, D_QKV + k * D_CONV:D_QKV + (k + 1) * D_CONV],
                              preferred_element_type=F32)
    b, z = group(0), group(1) * group(2)
    seq_start = (i % (SEQ // INPROJ_TM)) == 0
    z_before = jnp.where(seq_start, 0.0, ztail_ref[...])
    ztail_ref[...] = z[INPROJ_TM - SUBLANES:]
    cn_ref[...] = _short_conv(b, z, z_before, cw_ref[...], gc_ref[...])
    for j in range(D_QKV // INPROJ_TN):
        cols = slice(j * INPROJ_TN, (j + 1) * INPROJ_TN)
        qkv_ref[:, cols] = jnp.dot(hn, w_ref[:, cols],
                                   preferred_element_type=F32).astype(BF16)


def _inproj(x2, g, w_bf, conv_w, g_conv):
    tm = INPROJ_TM
    fixed = lambda i: (0, 0)
    row = lambda i: (i, 0)
    return pl.pallas_call(
        _inproj_kernel,
        grid=(T // tm,),
        in_specs=[
            pl.BlockSpec((tm, D_MODEL), row),
            pl.BlockSpec((1, D_MODEL), fixed),
            pl.BlockSpec((D_MODEL, D_IN_PROJ), fixed, pipeline_mode=pl.Buffered(1)),
            pl.BlockSpec((SUBLANES, D_CONV), fixed),
            pl.BlockSpec((1, D_CONV), fixed),
        ],
        out_specs=[pl.BlockSpec((tm, D_QKV), row), pl.BlockSpec((tm, D_CONV), row)],
        out_shape=[jax.ShapeDtypeStruct((T, D_QKV), BF16),
                   jax.ShapeDtypeStruct((T, D_CONV), BF16)],
        scratch_shapes=[pltpu.VMEM((SUBLANES, D_CONV), F32)],
        compiler_params=_params(("arbitrary",)),
        name="inproj",
    )(x2, g, w_bf, conv_w, g_conv)


N_QUADS = N_HEADS // 4
QUAD = 4 * HEAD_DIM
BIAS_W = 640
BIAS_WIN = 512
P_BUFS = 4


def _bias_kernel(b_ref, o_ref):
    first_head = lax.broadcasted_iota(I32, (KEYS, LANES), 1) < HEAD_DIM
    for hq in range(N_QUADS):
        for col in range(2):
            h0 = 4 * hq + 2 * col
            pieces = []
            for hh in range(2):
                x = jnp.broadcast_to(b_ref[h0 + hh:h0 + hh + 1, :], (KEYS, BIAS_W))
                rolled = pltpu.roll(x, 0, 1, stride=1, stride_axis=0)
                pieces.append(rolled[:, BIAS_WIN:BIAS_WIN + LANES])
            o_ref[hq, :, col * LANES:(col + 1) * LANES] = jnp.where(
                first_head, pieces[0], pieces[1])


def _bias_table(rel_bias):
    r = rel_bias.astype(F32)
    edge = jnp.broadcast_to(r[:, 2 * REL_CLIP:], (N_HEADS, KEYS - REL_CLIP + 1))
    b = jnp.concatenate([r[:, REL_CLIP:2 * REL_CLIP], edge,
                         r[:, REL_CLIP - (BIAS_W - KEYS - 1):REL_CLIP]], axis=1)
    odd = jnp.concatenate([b[:, BIAS_W - HEAD_DIM:], b[:, :BIAS_W - HEAD_DIM]], axis=1)
    is_odd = (jnp.arange(N_HEADS) % 2 == 1)[:, None]
    bvec = jnp.where(is_odd, odd, b)
    return pl.pallas_call(
        _bias_kernel,
        out_shape=jax.ShapeDtypeStruct((N_QUADS, KEYS, QUAD), F32),
        compiler_params=pltpu.CompilerParams(vmem_limit_bytes=VMEM_LIMIT),
        name="biastab",
    )(bvec)


def _attn_kernel(q_ref, kp_ref, kc_ref, vp_ref, vc_ref, bias_ref, g_ref, o_ref,
                 st_ref, p_ref, a_ref):
    i = pl.program_id(1)

    def window(prev_ref, cur_ref, r0, n, cols):
        parts = []
        if r0 < ATTN_TILE:
            parts.append(prev_ref[r0:min(ATTN_TILE, r0 + n), cols])
        if r0 + n > ATTN_TILE:
            parts.append(cur_ref[max(r0, ATTN_TILE) - ATTN_TILE:r0 + n - ATTN_TILE, cols])
        return parts[0] if len(parts) == 1 else jnp.concatenate(parts, axis=0)

    head_of_lane = lax.broadcasted_iota(I32, (CHUNK, QUAD), 1) // HEAD_DIM
    first_head = lax.broadcasted_iota(I32, (CHUNK, LANES), 1) < HEAD_DIM
    key_row = lax.broadcasted_iota(I32, (KEYS, QUAD), 0)
    n_rb = KEYS // CHUNK

    all_quads = tuple(range(N_QUADS))
    all_pairs = tuple(range(N_PAIRS))

    def scores(r0, buf, masked, quads=all_quads):
        for hq in quads:
            c0 = hq * QUAD
            q4 = q_ref[pl.ds(r0, CHUNK), c0:c0 + QUAD] * (HEAD_DIM ** -0.5)
            zero = jnp.zeros_like(q4)
            qbd = jnp.concatenate(
                [jnp.where(head_of_lane == h, q4, zero) for h in range(4)], axis=0)
            k4 = window(kp_ref, kc_ref, r0, KEYS, slice(c0, c0 + QUAD))
            st = lax.dot_general(k4, qbd, (((1,), (1,)), ((), ())),
                                 preferred_element_type=F32)
            st = st + bias_ref[hq]
            if masked:
                st = jnp.where(key_row + r0 >= ATTN_TILE, st, NEG_INF)
            st_ref[buf, hq] = st

    def exps(buf, pbuf, off, quads=all_quads):
        for hq in quads:
            m = st_ref[buf, hq, 0:CHUNK, :]
            for rb in range(1, n_rb):
                m = jnp.maximum(m, st_ref[buf, hq, rb * CHUNK:(rb + 1) * CHUNK, :])
            m = jnp.max(m, axis=0, keepdims=True)
            for rb in range(n_rb):
                rows = slice(rb * CHUNK, (rb + 1) * CHUNK)
                prow = slice(off + rb * CHUNK, off + (rb + 1) * CHUNK)
                p_ref[pbuf, hq, prow, :] = jnp.exp(st_ref[buf, hq, rows, :] - m).astype(BF16)

    def values(r0, pbufs, pairs=all_pairs):
        nq = len(pbufs)
        span = KEYS + (nq - 1) * CHUNK
        ones = jnp.ones((span, LANES), BF16)
        for hp in pairs:
            c0 = hp * LANES
            half = (hp % 2) * LANES
            pt = jnp.concatenate(
                [p_ref[b, hp // 2, 0:span, half:half + LANES] for b in pbufs], axis=1)
            vext = jnp.concatenate(
                [window(vp_ref, vc_ref, r0, span, slice(c0, c0 + LANES)), ones], axis=1)
            oe = lax.dot_general(pt, vext, (((0,), (0,)), ((), ())),
                                 preferred_element_type=F32)
            o = oe[:, :LANES] * (1.0 / oe[:, LANES:])
            for j in range(nq):
                oj = o[2 * j * CHUNK:2 * (j + 1) * CHUNK]
                a_ref[pl.ds(r0 + j * CHUNK, CHUNK), c0:c0 + LANES] = jnp.where(
                    first_head, oj[0:CHUNK], oj[CHUNK:2 * CHUNK])

    n_chunks = ATTN_TILE // CHUNK

    def tile(masked):
        for b in range(P_BUFS):
            z = slice(KEYS, KEYS + CHUNK) if b % 2 == 0 else slice(0, CHUNK)
            p_ref[b, :, z, :] = jnp.zeros((N_QUADS, CHUNK, QUAD), BF16)
        scores(0, 0, masked)
        for ci in range(n_chunks):
            c0 = ci - 2 - ci % 2
            for hq in all_quads:
                if ci + 1 < n_chunks:
                    scores((ci + 1) * CHUNK, (ci + 1) % 2, masked, (hq,))
                if c0 >= 0:
                    values(c0 * CHUNK, (c0 % P_BUFS, (c0 + 1) % P_BUFS),
                           ((ci % 2) * N_QUADS + hq,))
                exps(ci % 2, ci % P_BUFS, (ci % 2) * CHUNK, (hq,))
        values((n_chunks - 2) * CHUNK, ((n_chunks - 2) % P_BUFS, (n_chunks - 1) % P_BUFS))

    pl.when(i == 0)(functools.partial(tile, True))
    pl.when(i > 0)(functools.partial(tile, False))

    o_ref[...] = _rms(a_ref[...], g_ref[...]).astype(BF16)


def _attention(proj, bias_t, g):
    n = ATTN_TILES_PER_SEQ
    cur = lambda col: (lambda b, i: (b * n + i, col))
    prev = lambda col: (lambda b, i: (b * n + jnp.maximum(i - 1, 0), col))
    blk = (ATTN_TILE, D_ATTN)
    return pl.pallas_call(
        _attn_kernel,
        grid=(BATCH, n),
        in_specs=[
            pl.BlockSpec(blk, cur(0)),
            pl.BlockSpec(blk, prev(1)),
            pl.BlockSpec(blk, cur(1)),
            pl.BlockSpec(blk, prev(2)),
            pl.BlockSpec(blk, cur(2)),
            pl.BlockSpec((N_QUADS, KEYS, QUAD), lambda b, i: (0, 0, 0)),
            pl.BlockSpec((1, D_ATTN), lambda b, i: (0, 0)),
        ],
        out_specs=pl.BlockSpec(blk, lambda b, i: (b * n + i, 0)),
        out_shape=jax.ShapeDtypeStruct((T, D_ATTN), BF16),
        scratch_shapes=[
            pltpu.VMEM((2, N_QUADS, KEYS, QUAD), F32),
            pltpu.VMEM((P_BUFS, N_QUADS, KEYS + CHUNK, QUAD), BF16),
            pltpu.VMEM((ATTN_TILE, D_ATTN), F32),
        ],
        compiler_params=_params(("arbitrary", "arbitrary")),
        name="attn",
    )(proj, proj, proj, proj, proj, bias_t, g)


OUT_TM = 512
OUT_SUB = 2


def _pack_bf16_pairs(h):
    half = D_MODEL // 2
    return pltpu.pack_elementwise([h[:, :half], h[:, half:]], packed_dtype=BF16)


def _unpack_pairs_f32(u):
    lo = pltpu.unpack_elementwise(u, index=0, packed_dtype=BF16, unpacked_dtype=F32)
    hi = pltpu.unpack_elementwise(u, index=1, packed_dtype=BF16, unpacked_dtype=F32)
    return lo, hi


ROW_TILE = 8


def _store_row_tiles(ref, packed):
    m = packed.shape[0]
    for s in range(ROW_TILE):
        ref[pl.ds(s, m, stride=ROW_TILE), :] = packed[:, s * LANES:(s + 1) * LANES]


def _load_row_tiles(ref, m, first=0, stride=ROW_TILE):
    return jnp.concatenate(
        [ref[pl.ds(first + s, m, stride=stride), :] for s in range(ROW_TILE)], axis=1)


GROUP_LOGIT_ROW = N_EXPERTS


def _route_cols(lg, base_ref):
    tm = lg.shape[0]
    lt = lg.T
    sub = lax.broadcasted_iota(I32, (SUBLANES, tm), 0)
    sub_f = sub.astype(F32)
    big = jnp.float32(SUBLANES)
    ninf = jnp.float32(-jnp.inf)

    gl = lt[GROUP_LOGIT_ROW:GROUP_LOGIT_ROW + SUBLANES, :]
    gvalid = sub < N_GROUPS
    gmax = jnp.max(jnp.where(gvalid, gl, ninf), axis=0, keepdims=True)
    gsum = jnp.sum(jnp.where(gvalid, jnp.exp(gl - gmax), 0.0), axis=0, keepdims=True)
    p_g = 1.0 / gsum
    g_sel = jnp.min(jnp.where(jnp.logical_and(gvalid, gl == gmax), sub_f, big),
                    axis=0, keepdims=True)
    el = lt[0:EPG, :]
    for g in range(1, N_GROUPS):
        el = jnp.where(g_sel == g, lt[g * EPG:(g + 1) * EPG, :], el)
    v1 = jnp.max(el, axis=0, keepdims=True)
    i1 = jnp.min(jnp.where(el == v1, sub_f, big), axis=0, keepdims=True)
    rest = sub_f != i1
    v2 = jnp.max(jnp.where(rest, el, ninf), axis=0, keepdims=True)
    i2 = jnp.min(jnp.where(jnp.logical_and(rest, el == v2), sub_f, big),
                 axis=0, keepdims=True)
    t = jnp.exp(v2 - v1)
    w1 = p_g * (1.0 / (1.0 + t))
    w2 = p_g * (t / (1.0 + t))
    e1 = g_sel * EPG + i1
    e2 = g_sel * EPG + i2

    erow = lax.broadcasted_iota(I32, (N_EXPERTS, tm), 0).astype(F32)
    hit1 = erow == e1
    hit2 = erow == e2
    onehot = jnp.where(jnp.logical_or(hit1, hit2), 1.0, 0.0)
    r = lax.broadcasted_iota(I32, (tm, tm), 0)
    c = lax.broadcasted_iota(I32, (tm, tm), 1)
    tri = jnp.where(r < c, 1.0, 0.0).astype(BF16)
    base = base_ref[...]
    before = (jnp.dot(onehot.astype(BF16), tri, preferred_element_type=F32)
              + jnp.concatenate([base] * (tm // LANES), axis=1))
    r1 = jnp.sum(jnp.where(hit1, before, 0.0), axis=0, keepdims=True)
    r2 = jnp.sum(jnp.where(hit2, before, 0.0), axis=0, keepdims=True)
    base_ref[...] = base + jnp.broadcast_to(
        jnp.sum(onehot, axis=1, keepdims=True), (N_EXPERTS, LANES))

    ri_t = jnp.where(sub == 0, e1, jnp.where(sub == 1, e2, jnp.where(sub == 2, r1, r2)))
    w_t = jnp.where(sub == 0, w1, jnp.where(sub == 1, w2, 0.0))
    w_t = jnp.concatenate([w_t, jnp.zeros((LANES - SUBLANES, tm), F32)], axis=0)
    return ri_t.astype(I32), w_t.T


def _outproj_kernel(a_ref, cn_ref, x_ref, wo_ref, g2_ref, wr_ref, br_ref,
                    x1_ref, hp_ref, ri_ref, rw_ref, cnt_ref, base_ref):
    @pl.when(pl.program_id(0) == 0)
    def _():
        base_ref[...] = jnp.zeros_like(base_ref)

    sub = OUT_TM // OUT_SUB
    rows_of = lambda s: slice(s * sub, (s + 1) * sub)

    def project(s):
        rows = rows_of(s)
        acc = jnp.dot(a_ref[rows, :], wo_ref[0:D_ATTN, :], preferred_element_type=F32)
        acc = acc + jnp.dot(cn_ref[rows, :], wo_ref[D_ATTN:, :], preferred_element_type=F32)
        x1_ref[rows, :] = x_ref[rows, :] + acc

    def route(s):
        rows = rows_of(s)
        h2 = _rms(x1_ref[rows, :], g2_ref[...])
        logits = jnp.dot(h2.astype(BF16), wr_ref[...],
                         preferred_element_type=F32) + br_ref[...]
        _store_row_tiles(hp_ref.at[pl.ds(s * sub * ROW_TILE, sub * ROW_TILE), :],
                         _pack_bf16_pairs(h2))
        ri_t, rw = _route_cols(logits, base_ref)
        ri_ref[:, rows] = ri_t
        rw_ref[rows, :] = rw

    for s in range(OUT_SUB):
        project(s)
        if s > 0:
            route(s - 1)
    route(OUT_SUB - 1)
    cnt_ref[...] = base_ref[...]


def _outproj(a_n, c_n, x2, wo_bf, g2, wr_bf, br):
    tm = OUT_TM
    row = lambda i: (i, 0)
    fixed = lambda i: (0, 0)
    return pl.pallas_call(
        _outproj_kernel,
        grid=(T // tm,),
        in_specs=[
            pl.BlockSpec((tm, D_ATTN), row),
            pl.BlockSpec((tm, D_CONV), row),
            pl.BlockSpec((tm, D_MODEL), row),
            pl.BlockSpec((D_MODEL, D_MODEL), fixed),
            pl.BlockSpec((1, D_MODEL), fixed),
            pl.BlockSpec((D_MODEL, LANES), fixed),
            pl.BlockSpec((1, LANES), fixed),
        ],
        out_specs=[
            pl.BlockSpec((tm, D_MODEL), row),
            pl.BlockSpec((tm * ROW_TILE, LANES), row),
            pl.BlockSpec((SUBLANES, tm), lambda i: (0, i)),
            pl.BlockSpec((tm, LANES), row),
            pl.BlockSpec((N_EXPERTS, LANES), fixed),
        ],
        out_shape=[
            jax.ShapeDtypeStruct((T, D_MODEL), F32),
            jax.ShapeDtypeStruct((T * ROW_TILE, LANES), I32),
            jax.ShapeDtypeStruct((SUBLANES, T), I32),
            jax.ShapeDtypeStruct((T, LANES), F32),
            jax.ShapeDtypeStruct((N_EXPERTS, LANES), F32),
        ],
        scratch_shapes=[pltpu.VMEM((N_EXPERTS, LANES), F32)],
        compiler_params=_params(("arbitrary",)),
        name="outproj",
    )(a_n, c_n, x2, wo_bf, g2, wr_bf, br)


BLOCK_LANES = 384
DEST_TM = 2048


def _dest_kernel(ri_ref, cnt_ref, dest_ref, be_ref, seg_ref):
    tm = DEST_TM
    erow = lax.broadcasted_iota(I32, (N_EXPERTS, LANES), 0)
    cnt = cnt_ref[...].astype(I32)
    pcnt = ((cnt + (MOE_BLOCK - 1)) >> MOE_BLOCK_SHIFT) << MOE_BLOCK_SHIFT
    pend = pcnt
    for s in (1, 2, 4, 8, 16):
        pend = pend + jnp.where(erow >= s, pltpu.roll(pend, s, axis=0), 0)
    pstart = (pend - pcnt).astype(F32)

    ri = ri_ref[...].astype(F32)
    etok = lax.broadcasted_iota(I32, (N_EXPERTS, tm), 0).astype(F32)
    pstart_t = jnp.concatenate([pstart] * (tm // LANES), axis=1)
    d1 = jnp.sum(jnp.where(etok == ri[0:1], pstart_t, 0.0), axis=0, keepdims=True) + ri[2:3]
    d2 = jnp.sum(jnp.where(etok == ri[1:2], pstart_t, 0.0), axis=0, keepdims=True) + ri[3:4]
    sub = lax.broadcasted_iota(I32, (SUBLANES, tm), 0)
    dest_ref[...] = jnp.where(sub == 0, d1, d2).astype(I32)

    pend_f = pend.astype(F32)
    pend_b = jnp.concatenate([pend_f] * (BLOCK_LANES // LANES), axis=1)
    blk_lane = lax.broadcasted_iota(I32, (1, BLOCK_LANES), 1)
    le = jnp.where(pend_b <= (blk_lane * MOE_BLOCK).astype(F32), 1.0, 0.0)
    be = jnp.minimum(jnp.sum(le, axis=0, keepdims=True), float(N_EXPERTS - 1))
    total = pend_b[N_EXPERTS - 1:N_EXPERTS, :]
    be = jnp.where(blk_lane == N_BLOCKS, total * (1.0 / MOE_BLOCK), be)
    be_ref[...] = jnp.broadcast_to(be, (SUBLANES, BLOCK_LANES)).astype(I32)
    seg_ref[0:N_EXPERTS, :] = pend - pcnt + cnt
    seg_ref[N_EXPERTS:2 * N_EXPERTS, :] = pend


def _dest(ri_t, cnt):
    tm = DEST_TM
    fixed = lambda i: (0, 0)
    tok = lambda i: (0, i)
    return pl.pallas_call(
        _dest_kernel,
        grid=(T // tm,),
        in_specs=[pl.BlockSpec((SUBLANES, tm), tok), pl.BlockSpec((N_EXPERTS, LANES), fixed)],
        out_specs=[pl.BlockSpec((SUBLANES, tm), tok),
                   pl.BlockSpec((SUBLANES, BLOCK_LANES), fixed),
                   pl.BlockSpec((2 * N_EXPERTS, LANES), fixed)],
        out_shape=[jax.ShapeDtypeStruct((SUBLANES, T), I32),
                   jax.ShapeDtypeStruct((SUBLANES, BLOCK_LANES), I32),
                   jax.ShapeDtypeStruct((2 * N_EXPERTS, LANES), I32)],
        compiler_params=_params(("arbitrary",)),
        name="dest",
    )(ri_t, cnt)


GROUP = 4
INV_LEN = (-(-N_BLOCKS // GROUP) * GROUP + GROUP - 1) * MOE_BLOCK
DUMP_ROWS = 8 * MOE_BLOCK


PAD_STEP = 8


def _invert_permutation(d1_ref, d2_ref, vend_ref, pend_ref, inv_ref, pos_end):
    def pad_range(lo, hi):
        def body(k, carry):
            p0 = lo + k * PAD_STEP
            for j in range(PAD_STEP):
                inv_ref[p0 + j] = N_ROUTED + ((p0 + j) & (DUMP_ROWS - 1))
            return carry

        lax.fori_loop(0, (hi - lo + (PAD_STEP - 1)) // PAD_STEP, body, 0)

    def pad_expert(e, carry):
        pad_range(vend_ref[e], pend_ref[e])
        return carry

    lax.fori_loop(0, N_EXPERTS, pad_expert, 0)
    pad_range(pend_ref[N_EXPERTS - 1], pos_end)

    def place(t, carry):
        inv_ref[d1_ref[t]] = 2 * t
        inv_ref[d2_ref[t]] = 2 * t + 1
        return carry

    lax.fori_loop(0, T, place, 0, unroll=16)


BLOCK_TILE_ROWS = MOE_BLOCK * ROW_TILE
Y_ROWS = (N_ROUTED + DUMP_ROWS) * ROW_TILE


def _moe_kernel(meta_ref, d1_ref, d2_ref, vend_ref, pend_ref, h_ref, wg_hbm, wu_hbm, wd_hbm, y_ref,
                *scratch):
    xbufs, ybufs = scratch[:GROUP], scratch[GROUP:2 * GROUP]
    wg_s, wu_s, wd_s, wg_b, wu_b, wd_b, inv_ref, sem_x, sem_y, sem_w = scratch[2 * GROUP:]
    n_used = meta_ref[N_BLOCKS]
    n_groups = (n_used + (GROUP - 1)) // GROUP
    half = D_MODEL // 2

    def issue_gather(b, v):
        for r in range(MOE_BLOCK):
            tok = (inv_ref[b * MOE_BLOCK + r] >> 1) & (T - 1)
            src = h_ref.at[pl.ds(pl.multiple_of(tok * ROW_TILE, ROW_TILE), ROW_TILE), :]
            pltpu.make_async_copy(src, xbufs[v].at[pl.ds(r * ROW_TILE, ROW_TILE), :],
                                  sem_x.at[v]).start()

    def wait_gather(v):
        pltpu.make_async_copy(h_ref.at[pl.ds(0, BLOCK_TILE_ROWS), :], xbufs[v],
                              sem_x.at[v]).wait()

    def issue_scatter(b, v):
        for r in range(MOE_BLOCK):
            slot = inv_ref[b * MOE_BLOCK + r]
            dst = y_ref.at[pl.ds(pl.multiple_of(slot * ROW_TILE, ROW_TILE), ROW_TILE), :]
            pltpu.make_async_copy(ybufs[v].at[pl.ds(r * ROW_TILE, ROW_TILE), :], dst,
                                  sem_y.at[v]).start()

    def wait_scatter(v):
        pltpu.make_async_copy(ybufs[v], y_ref.at[pl.ds(0, BLOCK_TILE_ROWS), :],
                              sem_y.at[v]).wait()

    def weight_copies(e):
        return (pltpu.make_async_copy(wg_hbm.at[e], wg_s, sem_w),
                pltpu.make_async_copy(wu_hbm.at[e], wu_s, sem_w),
                pltpu.make_async_copy(wd_hbm.at[e], wd_s, sem_w))

    def load_expert(b):
        e = meta_ref[b]
        prev = meta_ref[jnp.maximum(b - 1, 0)]
        first = jnp.logical_and(b < n_used, jnp.logical_or(b == 0, e != prev))

        @pl.when(first)
        def _():
            for c in weight_copies(e):
                c.wait()
            wg_b[...] = wg_s[...].astype(BF16)
            wu_b[...] = wu_s[...].astype(BF16)
            wd_b[...] = wd_s[...].astype(BF16)
            nxt = lax.while_loop(
                lambda j: jnp.logical_and(j < n_used, meta_ref[jnp.minimum(j, N_BLOCKS - 1)] == e),
                lambda j: j + 1, b + 1)

            @pl.when(nxt < n_used)
            def _():
                for c in weight_copies(meta_ref[nxt]):
                    c.start(priority=1)

    def compute(v):
        lo, hi = _unpack_pairs_f32(_load_row_tiles(xbufs[v], MOE_BLOCK))
        lo, hi = lo.astype(BF16), hi.astype(BF16)
        g = (jnp.dot(lo, wg_b[0:half, :], preferred_element_type=F32)
             + jnp.dot(hi, wg_b[half:, :], preferred_element_type=F32))
        u = (jnp.dot(lo, wu_b[0:half, :], preferred_element_type=F32)
             + jnp.dot(hi, wu_b[half:, :], preferred_element_type=F32))
        a = g * (1.0 / (1.0 + jnp.exp(-g))) * u
        y = jnp.dot(a.astype(BF16), wd_b[...], preferred_element_type=F32)
        _store_row_tiles(ybufs[v], _pack_bf16_pairs(y))

    def block(b, v, scatter_prev, wait_y):
        wait_gather(v)
        if wait_y:
            wait_scatter(v)
        load_expert(b)
        issue_gather(b + GROUP - 1, (v + GROUP - 1) % GROUP)
        if scatter_prev:
            issue_scatter(b - 1, (v + GROUP - 1) % GROUP)
        compute(v)

    for c in weight_copies(meta_ref[0]):
        c.start()
    zeros_buf = ybufs[GROUP - 1]
    zeros_buf[...] = jnp.zeros_like(zeros_buf)
    dump_fills = [
        pltpu.make_async_copy(
            zeros_buf, y_ref.at[pl.ds((N_ROUTED + k * MOE_BLOCK) * ROW_TILE, BLOCK_TILE_ROWS), :],
            sem_y.at[GROUP - 1])
        for k in range(DUMP_ROWS // MOE_BLOCK)]
    for c in dump_fills:
        c.start()
    _invert_permutation(d1_ref, d2_ref, vend_ref, pend_ref, inv_ref,
                        (n_groups * GROUP + GROUP - 1) * MOE_BLOCK)
    for c in dump_fills:
        c.wait()
    for v in range(GROUP - 1):
        issue_gather(v, v)
    for v in range(GROUP):
        block(v, v, scatter_prev=v > 0, wait_y=False)

    def group(gi, carry):
        for v in range(GROUP):
            block(gi * GROUP + v, v, scatter_prev=True, wait_y=True)
        return carry

    lax.fori_loop(1, n_groups, group, 0)

    last = n_groups * GROUP
    issue_scatter(last - 1, GROUP - 1)
    for v in range(GROUP - 1):
        wait_gather(v)
    for v in range(GROUP):
        wait_scatter(v)


def _moe(meta, d1, d2, vend, pend, h_tiles, w_gate, w_up, w_down):
    any_spec = pl.BlockSpec(memory_space=pl.ANY)
    xy = pltpu.VMEM((BLOCK_TILE_ROWS, LANES), I32)
    return pl.pallas_call(
        _moe_kernel,
        grid_spec=pltpu.PrefetchScalarGridSpec(
            num_scalar_prefetch=5,
            grid=(1,),
            in_specs=[any_spec, any_spec, any_spec, any_spec],
            out_specs=any_spec,
            scratch_shapes=[
                *([xy] * (2 * GROUP)),
                pltpu.VMEM((D_MODEL, D_EXPERT), F32),
                pltpu.VMEM((D_MODEL, D_EXPERT), F32),
                pltpu.VMEM((D_EXPERT, D_MODEL), F32),
                pltpu.VMEM((D_MODEL, D_EXPERT), BF16),
                pltpu.VMEM((D_MODEL, D_EXPERT), BF16),
                pltpu.VMEM((D_EXPERT, D_MODEL), BF16),
                pltpu.SMEM((INV_LEN,), I32),
                pltpu.SemaphoreType.DMA((GROUP,)),
                pltpu.SemaphoreType.DMA((GROUP,)),
                pltpu.SemaphoreType.DMA(()),
            ],
        ),
        out_shape=jax.ShapeDtypeStruct((Y_ROWS, LANES), I32),
        compiler_params=_params(("arbitrary",)),
        name="moe",
    )(meta, d1, d2, vend, pend, h_tiles, w_gate, w_up, w_down)


COMB_TM = 512


def _combine_kernel(x1_ref, rw_ref, gf_ref, y_ref, o_ref):
    tm = COMB_TM
    w = rw_ref[...]
    w0, w1 = w[:, 0:1], w[:, 1:2]
    lo0, hi0 = _unpack_pairs_f32(_load_row_tiles(y_ref, tm, 0, 2 * ROW_TILE))
    lo1, hi1 = _unpack_pairs_f32(_load_row_tiles(y_ref, tm, ROW_TILE, 2 * ROW_TILE))
    moe = jnp.concatenate([lo0 * w0 + lo1 * w1, hi0 * w0 + hi1 * w1], axis=1)
    o_ref[...] = _rms(x1_ref[...] + moe, gf_ref[...])


def _combine(x1, rw, gf, y_tiles):
    tm = COMB_TM
    row = lambda i: (i, 0)
    return pl.pallas_call(
        _combine_kernel,
        grid=(T // tm,),
        in_specs=[
            pl.BlockSpec((tm, D_MODEL), row),
            pl.BlockSpec((tm, LANES), row),
            pl.BlockSpec((1, D_MODEL), lambda i: (0, 0)),
            pl.BlockSpec((tm * 2 * ROW_TILE, LANES), row),
        ],
        out_specs=pl.BlockSpec((tm, D_MODEL), row),
        out_shape=jax.ShapeDtypeStruct((T, D_MODEL), F32),
        compiler_params=_params(("arbitrary",)),
        name="combine",
    )(x1, rw, gf, y_tiles)


def _layer(x2, norm1, w_in, rel_bias, conv_w, g_out_attn, g_out_conv, w_out, norm2,
           w_rg, b_rg, w_re, b_re, w_gate, w_up, w_down):
    conv_w8 = jnp.zeros((SUBLANES, D_CONV), F32).at[0:3].set(conv_w)
    qkv, c_n = _inproj(x2, norm1[None, :], w_in.astype(BF16), conv_w8, g_out_conv[None, :])
    a_n = _attention(qkv, _bias_table(rel_bias), g_out_attn[None, :])

    w_r = jnp.zeros((D_MODEL, LANES), F32)
    g0 = GROUP_LOGIT_ROW
    w_r = w_r.at[:, 0:N_EXPERTS].set(w_re).at[:, g0:g0 + N_GROUPS].set(w_rg)
    b_r = jnp.zeros((1, LANES), F32)
    b_r = b_r.at[0, 0:N_EXPERTS].set(b_re).at[0, g0:g0 + N_GROUPS].set(b_rg)
    x1, hp, ri_t, rw, cnt = _outproj(a_n, c_n, x2, w_out.astype(BF16), norm2[None, :],
                                     w_r.astype(BF16), b_r)

    dest_t, be, seg = _dest(ri_t, cnt)
    y_tiles = _moe(be[0], dest_t[0], dest_t[1], seg[0:N_EXPERTS, 0],
                   seg[N_EXPERTS:2 * N_EXPERTS, 0], hp, w_gate, w_up, w_down)
    return x1, rw, y_tiles


def kernel(x, norm1, w_in, rel_bias, conv_w, g_out_attn, g_out_conv, w_out, norm2,
           w_router_group, b_router_group, w_router_expert, b_router_expert,
           w_gate, w_up, w_down, norm_final):
    assert x.shape == (BATCH, SEQ, D_MODEL) and norm1.shape[0] == 1
    x2 = x.reshape(T, D_MODEL)
    x1, rw, y_tiles = _layer(
        x2, norm1[0], w_in[0], rel_bias[0], conv_w[0], g_out_attn[0], g_out_conv[0],
        w_out[0], norm2[0], w_router_group[0], b_router_group[0], w_router_expert[0],
        b_router_expert[0], w_gate[0], w_up[0], w_down[0])
    out = _combine(x1, rw, norm_final[None, :], y_tiles)
    return out.reshape(BATCH, SEQ, D_MODEL)
```

```python
import functools

import jax
import jax.numpy as jnp
from jax import lax
from jax.experimental import pallas as pl
from jax.experimental.pallas import tpu as pltpu

F32 = jnp.float32
BF16 = jnp.bfloat16
I32 = jnp.int32

D_MODEL = 2048
BATCH = 4
SEQ = 4096
T = BATCH * SEQ
CHUNK = 64
LEFT_CHUNKS = 8
BAND = LEFT_CHUNKS + 1
KEYS = BAND * CHUNK
D_ATTN = 1024
D_CONV = 1024
HEAD_DIM = 64
N_HEADS = 16
N_PAIRS = N_HEADS // 2
REL_CLIP = 256
D_IN_PROJ = 6 * 1024
N_GROUPS = 4
EPG = 8
N_EXPERTS = 32
D_EXPERT = 512
MOE_BLOCK = 256
MOE_BLOCK_SHIFT = MOE_BLOCK.bit_length() - 1
assert MOE_BLOCK == 1 << MOE_BLOCK_SHIFT
N_ROUTED = 2 * T
N_BLOCKS = N_ROUTED // MOE_BLOCK + N_EXPERTS
EPS = 1e-6
NEG_INF = -1e30
LANES = 128
VMEM_LIMIT = 52 * 1024 * 1024

ATTN_TILE = 512
ATTN_TILES_PER_SEQ = SEQ // ATTN_TILE


def _params(sem):
    return pltpu.CompilerParams(dimension_semantics=sem, vmem_limit_bytes=VMEM_LIMIT)


def _rms(x, g):
    ms = jnp.mean(x * x, axis=-1, keepdims=True)
    return x * lax.rsqrt(ms + EPS) * g


INPROJ_TM = 512
INPROJ_TN = 1536


D_QKV = 3 * D_ATTN
SUBLANES = 8


def _short_conv(b, z, z_before, w, g):
    n = z.shape[0]
    ze = jnp.concatenate([z_before, z], axis=0)
    z1 = pltpu.roll(ze, 1, axis=0)[SUBLANES:SUBLANES + n]
    z2 = pltpu.roll(ze, 2, axis=0)[SUBLANES:SUBLANES + n]
    y = w[0:1] * z2 + w[1:2] * z1 + w[2:3] * z
    return _rms(b * y, g).astype(BF16)


W_STAGE_ROWS = 64


def _load_weights_bf16(w_hbm, w_bf, stage, sem):
    n = w_bf.shape[0] // W_STAGE_ROWS

    def slab(c):
        return pltpu.make_async_copy(
            w_hbm.at[pl.ds(c * W_STAGE_ROWS, W_STAGE_ROWS), :], stage.at[c % 2], sem.at[c % 2])

    slab(0).start()
    for c in range(n):
        if c + 1 < n:
            slab(c + 1).start()
        slab(c).wait()
        w_bf[c * W_STAGE_ROWS:(c + 1) * W_STAGE_ROWS, :] = stage[c % 2].astype(BF16)


def _inproj_kernel(x_ref, g_ref, w_hbm, cw_ref, gc_ref, qkv_ref, cn_ref,
                   ztail_ref, w_ref, w_stage, w_sem):
    i = pl.program_id(0)

    @pl.when(i == 0)
    def _():
        ztail_ref[...] = jnp.zeros_like(ztail_ref)
        _load_weights_bf16(w_hbm, w_ref, w_stage, w_sem)

    hn = _rms(x_ref[...], g_ref[...]).astype(BF16)
    group = lambda k: jnp.dot(hn, w_ref[:, D_QKV + k * D_CONV:D_QKV + (k + 1) * D_CONV],
                              preferred_element_type=F32)
    b, z = group(0), group(1) * group(2)
    seq_start = (i % (SEQ // INPROJ_TM)) == 0
    z_before = jnp.where(seq_start, 0.0, ztail_ref[...])
    ztail_ref[...] = z[INPROJ_TM - SUBLANES:]
    cn_ref[...] = _short_conv(b, z, z_before, cw_ref[...], gc_ref[...])
    for j in range(D_QKV // INPROJ_TN):
        cols = slice(j * INPROJ_TN, (j + 1) * INPROJ_TN)
        qkv_ref[:, cols] = jnp.dot(hn, w_ref[:, cols],
                                   preferred_element_type=F32).astype(BF16)


def _inproj(x2, g, w_in, conv_w, g_conv):
    tm = INPROJ_TM
    fixed = lambda i: (0, 0)
    row = lambda i: (i, 0)
    return pl.pallas_call(
        _inproj_kernel,
        grid=(T // tm,),
        in_specs=[
            pl.BlockSpec((tm, D_MODEL), row),
            pl.BlockSpec((1, D_MODEL), fixed),
            pl.BlockSpec(memory_space=pl.ANY),
            pl.BlockSpec((SUBLANES, D_CONV), fixed),
            pl.BlockSpec((1, D_CONV), fixed),
        ],
        out_specs=[pl.BlockSpec((tm, D_QKV), row), pl.BlockSpec((tm, D_CONV), row)],
        out_shape=[jax.ShapeDtypeStruct((T, D_QKV), BF16),
                   jax.ShapeDtypeStruct((T, D_CONV), BF16)],
        scratch_shapes=[pltpu.VMEM((SUBLANES, D_CONV), F32),
                        pltpu.VMEM((D_MODEL, D_IN_PROJ), BF16),
                        pltpu.VMEM((2, W_STAGE_ROWS, D_IN_PROJ), F32),
                        pltpu.SemaphoreType.DMA((2,))],
        compiler_params=_params(("arbitrary",)),
        name="inproj",
    )(x2, g, w_in, conv_w, g_conv)


N_QUADS = N_HEADS // 4
QUAD = 4 * HEAD_DIM
BIAS_W = 640
BIAS_WIN = 512
P_BUFS = 4


def _bias_kernel(b_ref, o_ref):
    first_head = lax.broadcasted_iota(I32, (KEYS, LANES), 1) < HEAD_DIM
    for hq in range(N_QUADS):
        for col in range(2):
            h0 = 4 * hq + 2 * col
            pieces = []
            for hh in range(2):
                x = jnp.broadcast_to(b_ref[h0 + hh:h0 + hh + 1, :], (KEYS, BIAS_W))
                rolled = pltpu.roll(x, 0, 1, stride=1, stride_axis=0)
                pieces.append(rolled[:, BIAS_WIN:BIAS_WIN + LANES])
            o_ref[hq, :, col * LANES:(col + 1) * LANES] = jnp.where(
                first_head, pieces[0], pieces[1])


def _bias_table(rel_bias):
    r = rel_bias.astype(F32)
    edge = jnp.broadcast_to(r[:, 2 * REL_CLIP:], (N_HEADS, KEYS - REL_CLIP + 1))
    b = jnp.concatenate([r[:, REL_CLIP:2 * REL_CLIP], edge,
                         r[:, REL_CLIP - (BIAS_W - KEYS - 1):REL_CLIP]], axis=1)
    odd = jnp.concatenate([b[:, BIAS_W - HEAD_DIM:], b[:, :BIAS_W - HEAD_DIM]], axis=1)
    is_odd = (jnp.arange(N_HEADS) % 2 == 1)[:, None]
    bvec = jnp.where(is_odd, odd, b)
    return pl.pallas_call(
        _bias_kernel,
        out_shape=jax.ShapeDtypeStruct((N_QUADS, KEYS, QUAD), F32),
        compiler_params=pltpu.CompilerParams(vmem_limit_bytes=VMEM_LIMIT),
        name="biastab",
    )(bvec)


def _attn_kernel(q_ref, kp_ref, kc_ref, vp_ref, vc_ref, bias_ref, g_ref, o_ref,
                 st_ref, p_ref, a_ref):
    i = pl.program_id(1)

    def window(prev_ref, cur_ref, r0, n, cols):
        parts = []
        if r0 < ATTN_TILE:
            parts.append(prev_ref[r0:min(ATTN_TILE, r0 + n), cols])
        if r0 + n > ATTN_TILE:
            parts.append(cur_ref[max(r0, ATTN_TILE) - ATTN_TILE:r0 + n - ATTN_TILE, cols])
        return parts[0] if len(parts) == 1 else jnp.concatenate(parts, axis=0)

    head_of_lane = lax.broadcasted_iota(I32, (CHUNK, QUAD), 1) // HEAD_DIM
    first_head = lax.broadcasted_iota(I32, (CHUNK, LANES), 1) < HEAD_DIM
    key_row = lax.broadcasted_iota(I32, (KEYS, QUAD), 0)
    n_rb = KEYS // CHUNK

    all_quads = tuple(range(N_QUADS))
    all_pairs = tuple(range(N_PAIRS))

    def scores(r0, buf, masked, quads=all_quads):
        for hq in quads:
            c0 = hq * QUAD
            q4 = q_ref[pl.ds(r0, CHUNK), c0:c0 + QUAD] * (HEAD_DIM ** -0.5)
            zero = jnp.zeros_like(q4)
            qbd = jnp.concatenate(
                [jnp.where(head_of_lane == h, q4, zero) for h in range(4)], axis=0)
            k4 = window(kp_ref, kc_ref, r0, KEYS, slice(c0, c0 + QUAD))
            st = lax.dot_general(k4, qbd, (((1,), (1,)), ((), ())),
                                 preferred_element_type=F32)
            st = st + bias_ref[hq]
            if masked:
                st = jnp.where(key_row + r0 >= ATTN_TILE, st, NEG_INF)
            st_ref[buf, hq] = st

    def exps(buf, pbuf, off, quads=all_quads):
        for hq in quads:
            m = st_ref[buf, hq, 0:CHUNK, :]
            for rb in range(1, n_rb):
                m = jnp.maximum(m, st_ref[buf, hq, rb * CHUNK:(rb + 1) * CHUNK, :])
            m = jnp.max(m, axis=0, keepdims=True)
            for rb in range(n_rb):
                rows = slice(rb * CHUNK, (rb + 1) * CHUNK)
                prow = slice(off + rb * CHUNK, off + (rb + 1) * CHUNK)
                p_ref[pbuf, hq, prow, :] = jnp.exp(st_ref[buf, hq, rows, :] - m).astype(BF16)

    def values(r0, pbufs, pairs=all_pairs):
        nq = len(pbufs)
        span = KEYS + (nq - 1) * CHUNK
        ones = jnp.ones((span, LANES), BF16)
        for hp in pairs:
            c0 = hp * LANES
            half = (hp % 2) * LANES
            pt = jnp.concatenate(
                [p_ref[b, hp // 2, 0:span, half:half + LANES] for b in pbufs], axis=1)
            vext = jnp.concatenate(
                [window(vp_ref, vc_ref, r0, span, slice(c0, c0 + LANES)), ones], axis=1)
            oe = lax.dot_general(pt, vext, (((0,), (0,)), ((), ())),
                                 preferred_element_type=F32)
            o = oe[:, :LANES] * (1.0 / oe[:, LANES:])
            for j in range(nq):
                oj = o[2 * j * CHUNK:2 * (j + 1) * CHUNK]
                a_ref[pl.ds(r0 + j * CHUNK, CHUNK), c0:c0 + LANES] = jnp.where(
                    first_head, oj[0:CHUNK], oj[CHUNK:2 * CHUNK])

    n_chunks = ATTN_TILE // CHUNK

    def tile(masked):
        for b in range(P_BUFS):
            z = slice(KEYS, KEYS + CHUNK) if b % 2 == 0 else slice(0, CHUNK)
            p_ref[b, :, z, :] = jnp.zeros((N_QUADS, CHUNK, QUAD), BF16)
        scores(0, 0, masked)
        for ci in range(n_chunks):
            c0 = ci - 2 - ci % 2
            for hq in all_quads:
                if ci + 1 < n_chunks:
                    scores((ci + 1) * CHUNK, (ci + 1) % 2, masked, (hq,))
                if c0 >= 0:
                    values(c0 * CHUNK, (c0 % P_BUFS, (c0 + 1) % P_BUFS),
                           ((ci % 2) * N_QUADS + hq,))
                exps(ci % 2, ci % P_BUFS, (ci % 2) * CHUNK, (hq,))
        values((n_chunks - 2) * CHUNK, ((n_chunks - 2) % P_BUFS, (n_chunks - 1) % P_BUFS))

    pl.when(i == 0)(functools.partial(tile, True))
    pl.when(i > 0)(functools.partial(tile, False))

    o_ref[...] = _rms(a_ref[...], g_ref[...]).astype(BF16)


def _attention(proj, bias_t, g):
    n = ATTN_TILES_PER_SEQ
    cur = lambda col: (lambda b, i: (b * n + i, col))
    prev = lambda col: (lambda b, i: (b * n + jnp.maximum(i - 1, 0), col))
    blk = (ATTN_TILE, D_ATTN)
    return pl.pallas_call(
        _attn_kernel,
        grid=(BATCH, n),
        in_specs=[
            pl.BlockSpec(blk, cur(0)),
            pl.BlockSpec(blk, prev(1)),
            pl.BlockSpec(blk, cur(1)),
            pl.BlockSpec(blk, prev(2)),
            pl.BlockSpec(blk, cur(2)),
            pl.BlockSpec((N_QUADS, KEYS, QUAD), lambda b, i: (0, 0, 0)),
            pl.BlockSpec((1, D_ATTN), lambda b, i: (0, 0)),
        ],
        out_specs=pl.BlockSpec(blk, lambda b, i: (b * n + i, 0)),
        out_shape=jax.ShapeDtypeStruct((T, D_ATTN), BF16),
        scratch_shapes=[
            pltpu.VMEM((2, N_QUADS, KEYS, QUAD), F32),
            pltpu.VMEM((P_BUFS, N_QUADS, KEYS + CHUNK, QUAD), BF16),
            pltpu.VMEM((ATTN_TILE, D_ATTN), F32),
        ],
        compiler_params=_params(("arbitrary", "arbitrary")),
        name="attn",
    )(proj, proj, proj, proj, proj, bias_t, g)


OUT_TM = 512
OUT_SUB = 2


def _pack_bf16_pairs(h):
    half = D_MODEL // 2
    return pltpu.pack_elementwise([h[:, :half], h[:, half:]], packed_dtype=BF16)


def _unpack_pairs_f32(u):
    lo = pltpu.unpack_elementwise(u, index=0, packed_dtype=BF16, unpacked_dtype=F32)
    hi = pltpu.unpack_elementwise(u, index=1, packed_dtype=BF16, unpacked_dtype=F32)
    return lo, hi


ROW_TILE = 8


def _store_row_tiles(ref, packed):
    m = packed.shape[0]
    for s in range(ROW_TILE):
        ref[pl.ds(s, m, stride=ROW_TILE), :] = packed[:, s * LANES:(s + 1) * LANES]


def _load_row_tiles(ref, m, first=0, stride=ROW_TILE):
    return jnp.concatenate(
        [ref[pl.ds(first + s, m, stride=stride), :] for s in range(ROW_TILE)], axis=1)


GROUP_LOGIT_ROW = N_EXPERTS


def _route_cols(lg, base_ref):
    tm = lg.shape[0]
    lt = lg.T
    sub = lax.broadcasted_iota(I32, (SUBLANES, tm), 0)
    sub_f = sub.astype(F32)
    big = jnp.float32(SUBLANES)
    ninf = jnp.float32(-jnp.inf)

    gl = lt[GROUP_LOGIT_ROW:GROUP_LOGIT_ROW + SUBLANES, :]
    gvalid = sub < N_GROUPS
    gmax = jnp.max(jnp.where(gvalid, gl, ninf), axis=0, keepdims=True)
    gsum = jnp.sum(jnp.where(gvalid, jnp.exp(gl - gmax), 0.0), axis=0, keepdims=True)
    p_g = 1.0 / gsum
    g_sel = jnp.min(jnp.where(jnp.logical_and(gvalid, gl == gmax), sub_f, big),
                    axis=0, keepdims=True)
    el = lt[0:EPG, :]
    for g in range(1, N_GROUPS):
        el = jnp.where(g_sel == g, lt[g * EPG:(g + 1) * EPG, :], el)
    v1 = jnp.max(el, axis=0, keepdims=True)
    i1 = jnp.min(jnp.where(el == v1, sub_f, big), axis=0, keepdims=True)
    rest = sub_f != i1
    v2 = jnp.max(jnp.where(rest, el, ninf), axis=0, keepdims=True)
    i2 = jnp.min(jnp.where(jnp.logical_and(rest, el == v2), sub_f, big),
                 axis=0, keepdims=True)
    t = jnp.exp(v2 - v1)
    w1 = p_g * (1.0 / (1.0 + t))
    w2 = p_g * (t / (1.0 + t))
    e1 = g_sel * EPG + i1
    e2 = g_sel * EPG + i2

    erow = lax.broadcasted_iota(I32, (N_EXPERTS, tm), 0).astype(F32)
    hit1 = erow == e1
    hit2 = erow == e2
    onehot = jnp.where(jnp.logical_or(hit1, hit2), 1.0, 0.0)
    r = lax.broadcasted_iota(I32, (tm, tm), 0)
    c = lax.broadcasted_iota(I32, (tm, tm), 1)
    tri = jnp.where(r < c, 1.0, 0.0).astype(BF16)
    base = base_ref[...]
    before = (jnp.dot(onehot.astype(BF16), tri, preferred_element_type=F32)
              + jnp.concatenate([base] * (tm // LANES), axis=1))
    r1 = jnp.sum(jnp.where(hit1, before, 0.0), axis=0, keepdims=True)
    r2 = jnp.sum(jnp.where(hit2, before, 0.0), axis=0, keepdims=True)
    base_ref[...] = base + jnp.broadcast_to(
        jnp.sum(onehot, axis=1, keepdims=True), (N_EXPERTS, LANES))

    ri_t = jnp.where(sub == 0, e1, jnp.where(sub == 1, e2, jnp.where(sub == 2, r1, r2)))
    w_t = jnp.where(sub == 0, w1, jnp.where(sub == 1, w2, 0.0))
    w_t = jnp.concatenate([w_t, jnp.zeros((LANES - SUBLANES, tm), F32)], axis=0)
    return ri_t.astype(I32), w_t.T


def _outproj_kernel(a_ref, cn_ref, x_ref, wo_ref, g2_ref, wr_ref, br_ref,
                    x1_ref, hp_ref, ri_ref, rw_ref, cnt_ref, base_ref):
    @pl.when(pl.program_id(0) == 0)
    def _():
        base_ref[...] = jnp.zeros_like(base_ref)

    sub = OUT_TM // OUT_SUB
    rows_of = lambda s: slice(s * sub, (s + 1) * sub)

    def project(s):
        rows = rows_of(s)
        acc = jnp.dot(a_ref[rows, :], wo_ref[0:D_ATTN, :], preferred_element_type=F32)
        acc = acc + jnp.dot(cn_ref[rows, :], wo_ref[D_ATTN:, :], preferred_element_type=F32)
        x1_ref[rows, :] = x_ref[rows, :] + acc

    def route(s):
        rows = rows_of(s)
        h2 = _rms(x1_ref[rows, :], g2_ref[...])
        logits = jnp.dot(h2.astype(BF16), wr_ref[...],
                         preferred_element_type=F32) + br_ref[...]
        _store_row_tiles(hp_ref.at[pl.ds(s * sub * ROW_TILE, sub * ROW_TILE), :],
                         _pack_bf16_pairs(h2))
        ri_t, rw = _route_cols(logits, base_ref)
        ri_ref[:, rows] = ri_t
        rw_ref[rows, :] = rw

    for s in range(OUT_SUB):
        project(s)
        if s > 0:
            route(s - 1)
    route(OUT_SUB - 1)
    cnt_ref[...] = base_ref[...]


def _outproj(a_n, c_n, x2, wo_bf, g2, wr_bf, br):
    tm = OUT_TM
    row = lambda i: (i, 0)
    fixed = lambda i: (0, 0)
    return pl.pallas_call(
        _outproj_kernel,
        grid=(T // tm,),
        in_specs=[
            pl.BlockSpec((tm, D_ATTN), row),
            pl.BlockSpec((tm, D_CONV), row),
            pl.BlockSpec((tm, D_MODEL), row),
            pl.BlockSpec((D_MODEL, D_MODEL), fixed),
            pl.BlockSpec((1, D_MODEL), fixed),
            pl.BlockSpec((D_MODEL, LANES), fixed),
            pl.BlockSpec((1, LANES), fixed),
        ],
        out_specs=[
            pl.BlockSpec((tm, D_MODEL), row),
            pl.BlockSpec((tm * ROW_TILE, LANES), row),
            pl.BlockSpec((SUBLANES, tm), lambda i: (0, i)),
            pl.BlockSpec((tm, LANES), row),
            pl.BlockSpec((N_EXPERTS, LANES), fixed),
        ],
        out_shape=[
            jax.ShapeDtypeStruct((T, D_MODEL), F32),
            jax.ShapeDtypeStruct((T * ROW_TILE, LANES), I32),
            jax.ShapeDtypeStruct((SUBLANES, T), I32),
            jax.ShapeDtypeStruct((T, LANES), F32),
            jax.ShapeDtypeStruct((N_EXPERTS, LANES), F32),
        ],
        scratch_shapes=[pltpu.VMEM((N_EXPERTS, LANES), F32)],
        compiler_params=_params(("arbitrary",)),
        name="outproj",
    )(a_n, c_n, x2, wo_bf, g2, wr_bf, br)


BLOCK_LANES = 384
DEST_TM = 2048


def _dest_kernel(ri_ref, cnt_ref, dest_ref, be_ref, seg_ref):
    tm = DEST_TM
    erow = lax.broadcasted_iota(I32, (N_EXPERTS, LANES), 0)
    cnt = cnt_ref[...].astype(I32)
    pcnt = ((cnt + (MOE_BLOCK - 1)) >> MOE_BLOCK_SHIFT) << MOE_BLOCK_SHIFT
    pend = pcnt
    for s in (1, 2, 4, 8, 16):
        pend = pend + jnp.where(erow >= s, pltpu.roll(pend, s, axis=0), 0)
    pstart = (pend - pcnt).astype(F32)

    ri = ri_ref[...].astype(F32)
    etok = lax.broadcasted_iota(I32, (N_EXPERTS, tm), 0).astype(F32)
    pstart_t = jnp.concatenate([pstart] * (tm // LANES), axis=1)
    d1 = jnp.sum(jnp.where(etok == ri[0:1], pstart_t, 0.0), axis=0, keepdims=True) + ri[2:3]
    d2 = jnp.sum(jnp.where(etok == ri[1:2], pstart_t, 0.0), axis=0, keepdims=True) + ri[3:4]
    sub = lax.broadcasted_iota(I32, (SUBLANES, tm), 0)
    dest_ref[...] = jnp.where(sub == 0, d1, d2).astype(I32)

    pend_f = pend.astype(F32)
    pend_b = jnp.concatenate([pend_f] * (BLOCK_LANES // LANES), axis=1)
    blk_lane = lax.broadcasted_iota(I32, (1, BLOCK_LANES), 1)
    le = jnp.where(pend_b <= (blk_lane * MOE_BLOCK).astype(F32), 1.0, 0.0)
    be = jnp.minimum(jnp.sum(le, axis=0, keepdims=True), float(N_EXPERTS - 1))
    total = pend_b[N_EXPERTS - 1:N_EXPERTS, :]
    be = jnp.where(blk_lane == N_BLOCKS, total * (1.0 / MOE_BLOCK), be)
    be_ref[...] = jnp.broadcast_to(be, (SUBLANES, BLOCK_LANES)).astype(I32)
    seg_ref[0:N_EXPERTS, :] = pend - pcnt + cnt
    seg_ref[N_EXPERTS:2 * N_EXPERTS, :] = pend


def _dest(ri_t, cnt):
    tm = DEST_TM
    fixed = lambda i: (0, 0)
    tok = lambda i: (0, i)
    return pl.pallas_call(
        _dest_kernel,
        grid=(T // tm,),
        in_specs=[pl.BlockSpec((SUBLANES, tm), tok), pl.BlockSpec((N_EXPERTS, LANES), fixed)],
        out_specs=[pl.BlockSpec((SUBLANES, tm), tok),
                   pl.BlockSpec((SUBLANES, BLOCK_LANES), fixed),
                   pl.BlockSpec((2 * N_EXPERTS, LANES), fixed)],
        out_shape=[jax.ShapeDtypeStruct((SUBLANES, T), I32),
                   jax.ShapeDtypeStruct((SUBLANES, BLOCK_LANES), I32),
                   jax.ShapeDtypeStruct((2 * N_EXPERTS, LANES), I32)],
        compiler_params=_params(("arbitrary",)),
        name="dest",
    )(ri_t, cnt)


GROUP = 3
INV_LEN = (-(-N_BLOCKS // GROUP) * GROUP + GROUP - 1) * MOE_BLOCK
DUMP_ROWS = 4 * MOE_BLOCK


PAD_STEP = 8


def _invert_permutation(d1_ref, d2_ref, vend_ref, pend_ref, inv_ref, pos_end):
    def pad_range(lo, hi):
        def body(k, carry):
            p0 = lo + k * PAD_STEP
            for j in range(PAD_STEP):
                inv_ref[p0 + j] = N_ROUTED + ((p0 + j) & (DUMP_ROWS - 1))
            return carry

        lax.fori_loop(0, (hi - lo + (PAD_STEP - 1)) // PAD_STEP, body, 0)

    def pad_expert(e, carry):
        pad_range(vend_ref[e], pend_ref[e])
        return carry

    lax.fori_loop(0, N_EXPERTS, pad_expert, 0)
    pad_range(pend_ref[N_EXPERTS - 1], pos_end)

    def place(t, carry):
        inv_ref[d1_ref[t]] = 2 * t
        inv_ref[d2_ref[t]] = 2 * t + 1
        return carry

    lax.fori_loop(0, T, place, 0, unroll=16)


BLOCK_TILE_ROWS = MOE_BLOCK * ROW_TILE
Y_ROWS = (N_ROUTED + DUMP_ROWS) * ROW_TILE


def _moe_kernel(meta_ref, d1_ref, d2_ref, vend_ref, pend_ref, h_ref, wg_hbm, wu_hbm, wd_hbm, y_ref,
                *scratch):
    xbufs, ybufs = scratch[:GROUP], scratch[GROUP:2 * GROUP]
    wg_s, wu_s, wd_s, wg_b, wu_b, wd_b, inv_ref, sem_x, sem_y, sem_w = scratch[2 * GROUP:]
    n_used = meta_ref[N_BLOCKS]
    n_groups = (n_used + (GROUP - 1)) // GROUP
    half = D_MODEL // 2

    def issue_gather(b, v):
        for r in range(MOE_BLOCK):
            tok = (inv_ref[b * MOE_BLOCK + r] >> 1) & (T - 1)
            src = h_ref.at[pl.ds(pl.multiple_of(tok * ROW_TILE, ROW_TILE), ROW_TILE), :]
            pltpu.make_async_copy(src, xbufs[v].at[pl.ds(r * ROW_TILE, ROW_TILE), :],
                                  sem_x.at[v]).start()

    def wait_gather(v):
        pltpu.make_async_copy(h_ref.at[pl.ds(0, BLOCK_TILE_ROWS), :], xbufs[v],
                              sem_x.at[v]).wait()

    def issue_scatter(b, v):
        for r in range(MOE_BLOCK):
            slot = inv_ref[b * MOE_BLOCK + r]
            dst = y_ref.at[pl.ds(pl.multiple_of(slot * ROW_TILE, ROW_TILE), ROW_TILE), :]
            pltpu.make_async_copy(ybufs[v].at[pl.ds(r * ROW_TILE, ROW_TILE), :], dst,
                                  sem_y.at[v]).start()

    def wait_scatter(v):
        pltpu.make_async_copy(ybufs[v], y_ref.at[pl.ds(0, BLOCK_TILE_ROWS), :],
                              sem_y.at[v]).wait()

    def weight_copies(e):
        return (pltpu.make_async_copy(wg_hbm.at[e], wg_s, sem_w),
                pltpu.make_async_copy(wu_hbm.at[e], wu_s, sem_w),
                pltpu.make_async_copy(wd_hbm.at[e], wd_s, sem_w))

    def load_expert(b):
        e = meta_ref[b]
        prev = meta_ref[jnp.maximum(b - 1, 0)]
        first = jnp.logical_and(b < n_used, jnp.logical_or(b == 0, e != prev))

        @pl.when(first)
        def _():
            for c in weight_copies(e):
                c.wait()
            wg_b[...] = wg_s[...].astype(BF16)
            wu_b[...] = wu_s[...].astype(BF16)
            wd_b[...] = wd_s[...].astype(BF16)
            nxt = lax.while_loop(
                lambda j: jnp.logical_and(j < n_used, meta_ref[jnp.minimum(j, N_BLOCKS - 1)] == e),
                lambda j: j + 1, b + 1)

            @pl.when(nxt < n_used)
            def _():
                for c in weight_copies(meta_ref[nxt]):
                    c.start(priority=1)

    def compute(v):
        lo, hi = _unpack_pairs_f32(_load_row_tiles(xbufs[v], MOE_BLOCK))
        lo, hi = lo.astype(BF16), hi.astype(BF16)
        g = (jnp.dot(lo, wg_b[0:half, :], preferred_element_type=F32)
             + jnp.dot(hi, wg_b[half:, :], preferred_element_type=F32))
        u = (jnp.dot(lo, wu_b[0:half, :], preferred_element_type=F32)
             + jnp.dot(hi, wu_b[half:, :], preferred_element_type=F32))
        a = g * (1.0 / (1.0 + jnp.exp(-g))) * u
        y = jnp.dot(a.astype(BF16), wd_b[...], preferred_element_type=F32)
        _store_row_tiles(ybufs[v], _pack_bf16_pairs(y))

    def block(b, v, scatter_prev, wait_y):
        wait_gather(v)
        if wait_y:
            wait_scatter(v)
        load_expert(b)
        issue_gather(b + GROUP - 1, (v + GROUP - 1) % GROUP)
        if scatter_prev:
            issue_scatter(b - 1, (v + GROUP - 1) % GROUP)
        compute(v)

    for c in weight_copies(meta_ref[0]):
        c.start()
    zeros_buf = ybufs[GROUP - 1]
    zeros_buf[...] = jnp.zeros_like(zeros_buf)
    dump_fills = [
        pltpu.make_async_copy(
            zeros_buf, y_ref.at[pl.ds((N_ROUTED + k * MOE_BLOCK) * ROW_TILE, BLOCK_TILE_ROWS), :],
            sem_y.at[GROUP - 1])
        for k in range(DUMP_ROWS // MOE_BLOCK)]
    for c in dump_fills:
        c.start()
    _invert_permutation(d1_ref, d2_ref, vend_ref, pend_ref, inv_ref,
                        (n_groups * GROUP + GROUP - 1) * MOE_BLOCK)
    for c in dump_fills:
        c.wait()
    for v in range(GROUP - 1):
        issue_gather(v, v)
    for v in range(GROUP):
        block(v, v, scatter_prev=v > 0, wait_y=False)

    def group(gi, carry):
        for v in range(GROUP):
            block(gi * GROUP + v, v, scatter_prev=True, wait_y=True)
        return carry

    lax.fori_loop(1, n_groups, group, 0)

    last = n_groups * GROUP
    issue_scatter(last - 1, GROUP - 1)
    for v in range(GROUP - 1):
        wait_gather(v)
    for v in range(GROUP):
        wait_scatter(v)


def _moe(meta, d1, d2, vend, pend, h_tiles, w_gate, w_up, w_down):
    any_spec = pl.BlockSpec(memory_space=pl.ANY)
    xy = pltpu.VMEM((BLOCK_TILE_ROWS, LANES), I32)
    return pl.pallas_call(
        _moe_kernel,
        grid_spec=pltpu.PrefetchScalarGridSpec(
            num_scalar_prefetch=5,
            grid=(1,),
            in_specs=[any_spec, any_spec, any_spec, any_spec],
            out_specs=any_spec,
            scratch_shapes=[
                *([xy] * (2 * GROUP)),
                pltpu.VMEM((D_MODEL, D_EXPERT), F32),
                pltpu.VMEM((D_MODEL, D_EXPERT), F32),
                pltpu.VMEM((D_EXPERT, D_MODEL), F32),
                pltpu.VMEM((D_MODEL, D_EXPERT), BF16),
                pltpu.VMEM((D_MODEL, D_EXPERT), BF16),
                pltpu.VMEM((D_EXPERT, D_MODEL), BF16),
                pltpu.SMEM((INV_LEN,), I32),
                pltpu.SemaphoreType.DMA((GROUP,)),
                pltpu.SemaphoreType.DMA((GROUP,)),
                pltpu.SemaphoreType.DMA(()),
            ],
        ),
        out_shape=jax.ShapeDtypeStruct((Y_ROWS, LANES), I32),
        compiler_params=_params(("arbitrary",)),
        name="moe",
    )(meta, d1, d2, vend, pend, h_tiles, w_gate, w_up, w_down)


COMB_TM = 512


def _combine_kernel(x1_ref, rw_ref, gf_ref, y_ref, o_ref):
    tm = COMB_TM
    w = rw_ref[...]
    w0, w1 = w[:, 0:1], w[:, 1:2]
    lo0, hi0 = _unpack_pairs_f32(_load_row_tiles(y_ref, tm, 0, 2 * ROW_TILE))
    lo1, hi1 = _unpack_pairs_f32(_load_row_tiles(y_ref, tm, ROW_TILE, 2 * ROW_TILE))
    moe = jnp.concatenate([lo0 * w0 + lo1 * w1, hi0 * w0 + hi1 * w1], axis=1)
    o_ref[...] = _rms(x1_ref[...] + moe, gf_ref[...])


def _combine(x1, rw, gf, y_tiles):
    tm = COMB_TM
    row = lambda i: (i, 0)
    return pl.pallas_call(
        _combine_kernel,
        grid=(T // tm,),
        in_specs=[
            pl.BlockSpec((tm, D_MODEL), row),
            pl.BlockSpec((tm, LANES), row),
            pl.BlockSpec((1, D_MODEL), lambda i: (0, 0)),
            pl.BlockSpec((tm * 2 * ROW_TILE, LANES), row),
        ],
        out_specs=pl.BlockSpec((tm, D_MODEL), row),
        out_shape=jax.ShapeDtypeStruct((T, D_MODEL), F32),
        compiler_params=_params(("arbitrary",)),
        name="combine",
    )(x1, rw, gf, y_tiles)


def _layer(x2, norm1, w_in, rel_bias, conv_w, g_out_attn, g_out_conv, w_out, norm2,
           w_rg, b_rg, w_re, b_re, w_gate, w_up, w_down):
    conv_w8 = jnp.zeros((SUBLANES, D_CONV), F32).at[0:3].set(conv_w)
    qkv, c_n = _inproj(x2, norm1[None, :], w_in, conv_w8, g_out_conv[None, :])
    a_n = _attention(qkv, _bias_table(rel_bias), g_out_attn[None, :])

    w_r = jnp.zeros((D_MODEL, LANES), F32)
    g0 = GROUP_LOGIT_ROW
    w_r = w_r.at[:, 0:N_EXPERTS].set(w_re).at[:, g0:g0 + N_GROUPS].set(w_rg)
    b_r = jnp.zeros((1, LANES), F32)
    b_r = b_r.at[0, 0:N_EXPERTS].set(b_re).at[0, g0:g0 + N_GROUPS].set(b_rg)
    x1, hp, ri_t, rw, cnt = _outproj(a_n, c_n, x2, w_out.astype(BF16), norm2[None, :],
                                     w_r.astype(BF16), b_r)

    dest_t, be, seg = _dest(ri_t, cnt)
    y_tiles = _moe(be[0], dest_t[0], dest_t[1], seg[0:N_EXPERTS, 0],
                   seg[N_EXPERTS:2 * N_EXPERTS, 0], hp, w_gate, w_up, w_down)
    return x1, rw, y_tiles


def kernel(x, norm1, w_in, rel_bias, conv_w, g_out_attn, g_out_conv, w_out, norm2,
           w_router_group, b_router_group, w_router_expert, b_router_expert,
           w_gate, w_up, w_down, norm_final):
    assert x.shape == (BATCH, SEQ, D_MODEL) and norm1.shape[0] == 1
    x2 = x.reshape(T, D_MODEL)
    x1, rw, y_tiles = _layer(
        x2, norm1[0], w_in[0], rel_bias[0], conv_w[0], g_out_attn[0], g_out_conv[0],
        w_out[0], norm2[0], w_router_group[0], b_router_group[0], w_router_expert[0],
        b_router_expert[0], w_gate[0], w_up[0], w_down[0])
    out = _combine(x1, rw, norm_final[None, :], y_tiles)
    return out.reshape(BATCH, SEQ, D_MODEL)
```

```python
import functools

import jax
import jax.numpy as jnp
from jax import lax
from jax.experimental import pallas as pl
from jax.experimental.pallas import tpu as pltpu

F32 = jnp.float32
BF16 = jnp.bfloat16
I32 = jnp.int32

D_MODEL = 2048
BATCH = 4
SEQ = 4096
T = BATCH * SEQ
CHUNK = 64
LEFT_CHUNKS = 8
BAND = LEFT_CHUNKS + 1
KEYS = BAND * CHUNK
D_ATTN = 1024
D_CONV = 1024
HEAD_DIM = 64
N_HEADS = 16
N_PAIRS = N_HEADS // 2
REL_CLIP = 256
D_IN_PROJ = 6 * 1024
N_GROUPS = 4
EPG = 8
N_EXPERTS = 32
D_EXPERT = 512
MOE_BLOCK = 256
MOE_BLOCK_SHIFT = MOE_BLOCK.bit_length() - 1
assert MOE_BLOCK == 1 << MOE_BLOCK_SHIFT
N_ROUTED = 2 * T
N_BLOCKS = N_ROUTED // MOE_BLOCK + N_EXPERTS
EPS = 1e-6
NEG_INF = -1e30
LANES = 128
VMEM_LIMIT = 52 * 1024 * 1024

ATTN_TILE = 1024
ATTN_TILES_PER_SEQ = SEQ // ATTN_TILE
LOOKBACK = LEFT_CHUNKS * CHUNK
assert LOOKBACK <= ATTN_TILE


def _params(sem):
    return pltpu.CompilerParams(dimension_semantics=sem, vmem_limit_bytes=VMEM_LIMIT)


def _rms(x, g):
    ms = jnp.mean(x * x, axis=-1, keepdims=True)
    return x * lax.rsqrt(ms + EPS) * g


INPROJ_TM = 512
INPROJ_TN = 1536


D_QKV = 3 * D_ATTN
SUBLANES = 8


def _short_conv(b, z, z_before, w, g):
    n = z.shape[0]
    ze = jnp.concatenate([z_before, z], axis=0)
    z1 = pltpu.roll(ze, 1, axis=0)[SUBLANES:SUBLANES + n]
    z2 = pltpu.roll(ze, 2, axis=0)[SUBLANES:SUBLANES + n]
    y = w[0:1] * z2 + w[1:2] * z1 + w[2:3] * z
    return _rms(b * y, g).astype(BF16)


def _inproj_kernel(x_ref, g_ref, w_ref, cw_ref, gc_ref, qkv_ref, cn_ref, ztail_ref):
    i = pl.program_id(0)

    @pl.when(i == 0)
    def _():
        ztail_ref[...] = jnp.zeros_like(ztail_ref)

    hn = _rms(x_ref[...], g_ref[...]).astype(BF16)
    group = lambda k: jnp.dot(hn, w_ref[:, D_QKV + k * D_CONV:D_QKV + (k + 1) * D_CONV],
                              preferred_element_type=F32)
    b, z = group(0), group(1) * group(2)
    seq_start = (i % (SEQ // INPROJ_TM)) == 0
    z_before = jnp.where(seq_start, 0.0, ztail_ref[...])
    ztail_ref[...] = z[INPROJ_TM - SUBLANES:]
    cn_ref[...] = _short_conv(b, z, z_before, cw_ref[...], gc_ref[...])
    for j in range(D_QKV // INPROJ_TN):
        cols = slice(j * INPROJ_TN, (j + 1) * INPROJ_TN)
        qkv_ref[:, cols] = jnp.dot(hn, w_ref[:, cols],
                                   preferred_element_type=F32).astype(BF16)


def _inproj(x2, g, w_bf, conv_w, g_conv):
    tm = INPROJ_TM
    fixed = lambda i: (0, 0)
    row = lambda i: (i, 0)
    return pl.pallas_call(
        _inproj_kernel,
        grid=(T // tm,),
        in_specs=[
            pl.BlockSpec((tm, D_MODEL), row),
            pl.BlockSpec((1, D_MODEL), fixed),
            pl.BlockSpec((D_MODEL, D_IN_PROJ), fixed, pipeline_mode=pl.Buffered(1)),
            pl.BlockSpec((SUBLANES, D_CONV), fixed),
            pl.BlockSpec((1, D_CONV), fixed),
        ],
        out_specs=[pl.BlockSpec((tm, D_QKV), row), pl.BlockSpec((tm, D_CONV), row)],
        out_shape=[jax.ShapeDtypeStruct((T, D_QKV), BF16),
                   jax.ShapeDtypeStruct((T, D_CONV), BF16)],
        scratch_shapes=[pltpu.VMEM((SUBLANES, D_CONV), F32)],
        compiler_params=_params(("arbitrary",)),
        name="inproj",
    )(x2, g, w_bf, conv_w, g_conv)


N_QUADS = N_HEADS // 4
QUAD = 4 * HEAD_DIM
BIAS_W = 640
BIAS_WIN = 512
P_BUFS = 4


def _bias_kernel(b_ref, o_ref):
    first_head = lax.broadcasted_iota(I32, (KEYS, LANES), 1) < HEAD_DIM
    for hq in range(N_QUADS):
        for col in range(2):
            h0 = 4 * hq + 2 * col
            pieces = []
            for hh in range(2):
                x = jnp.broadcast_to(b_ref[h0 + hh:h0 + hh + 1, :], (KEYS, BIAS_W))
                rolled = pltpu.roll(x, 0, 1, stride=1, stride_axis=0)
                pieces.append(rolled[:, BIAS_WIN:BIAS_WIN + LANES])
            o_ref[hq, :, col * LANES:(col + 1) * LANES] = jnp.where(
                first_head, pieces[0], pieces[1])


def _bias_table(rel_bias):
    r = rel_bias.astype(F32)
    edge = jnp.broadcast_to(r[:, 2 * REL_CLIP:], (N_HEADS, KEYS - REL_CLIP + 1))
    b = jnp.concatenate([r[:, REL_CLIP:2 * REL_CLIP], edge,
                         r[:, REL_CLIP - (BIAS_W - KEYS - 1):REL_CLIP]], axis=1)
    odd = jnp.concatenate([b[:, BIAS_W - HEAD_DIM:], b[:, :BIAS_W - HEAD_DIM]], axis=1)
    is_odd = (jnp.arange(N_HEADS) % 2 == 1)[:, None]
    bvec = jnp.where(is_odd, odd, b)
    return pl.pallas_call(
        _bias_kernel,
        out_shape=jax.ShapeDtypeStruct((N_QUADS, KEYS, QUAD), F32),
        compiler_params=pltpu.CompilerParams(vmem_limit_bytes=VMEM_LIMIT),
        name="biastab",
    )(bvec)


def _attn_kernel(q_ref, kp_ref, kc_ref, vp_ref, vc_ref, bias_ref, g_ref, o_ref,
                 st_ref, p_ref, a_ref):
    i = pl.program_id(1)

    def window(prev_ref, cur_ref, r0, n, cols):
        parts = []
        skip = ATTN_TILE - LOOKBACK
        if r0 < LOOKBACK:
            parts.append(prev_ref[skip + r0:skip + min(LOOKBACK, r0 + n), cols])
        if r0 + n > LOOKBACK:
            parts.append(cur_ref[max(r0, LOOKBACK) - LOOKBACK:r0 + n - LOOKBACK, cols])
        return parts[0] if len(parts) == 1 else jnp.concatenate(parts, axis=0)

    head_of_lane = lax.broadcasted_iota(I32, (CHUNK, QUAD), 1) // HEAD_DIM
    first_head = lax.broadcasted_iota(I32, (CHUNK, LANES), 1) < HEAD_DIM
    key_row = lax.broadcasted_iota(I32, (KEYS, QUAD), 0)
    n_rb = KEYS // CHUNK

    all_quads = tuple(range(N_QUADS))
    all_pairs = tuple(range(N_PAIRS))

    def scores(r0, buf, masked, quads=all_quads):
        for hq in quads:
            c0 = hq * QUAD
            q4 = q_ref[pl.ds(r0, CHUNK), c0:c0 + QUAD] * (HEAD_DIM ** -0.5)
            zero = jnp.zeros_like(q4)
            qbd = jnp.concatenate(
                [jnp.where(head_of_lane == h, q4, zero) for h in range(4)], axis=0)
            k4 = window(kp_ref, kc_ref, r0, KEYS, slice(c0, c0 + QUAD))
            st = lax.dot_general(k4, qbd, (((1,), (1,)), ((), ())),
                                 preferred_element_type=F32)
            st = st + bias_ref[hq]
            if masked:
                st = jnp.where(key_row + r0 >= LOOKBACK, st, NEG_INF)
            st_ref[buf, hq] = st

    def exps(buf, pbuf, off, quads=all_quads):
        for hq in quads:
            m = st_ref[buf, hq, 0:CHUNK, :]
            for rb in range(1, n_rb):
                m = jnp.maximum(m, st_ref[buf, hq, rb * CHUNK:(rb + 1) * CHUNK, :])
            m = jnp.max(m, axis=0, keepdims=True)
            for rb in range(n_rb):
                rows = slice(rb * CHUNK, (rb + 1) * CHUNK)
                prow = slice(off + rb * CHUNK, off + (rb + 1) * CHUNK)
                p_ref[pbuf, hq, prow, :] = jnp.exp(st_ref[buf, hq, rows, :] - m).astype(BF16)

    def values(r0, pbufs, pairs=all_pairs):
        nq = len(pbufs)
        span = KEYS + (nq - 1) * CHUNK
        ones = jnp.ones((span, LANES), BF16)
        for hp in pairs:
            c0 = hp * LANES
            half = (hp % 2) * LANES
            pt = jnp.concatenate(
                [p_ref[b, hp // 2, 0:span, half:half + LANES] for b in pbufs], axis=1)
            vext = jnp.concatenate(
                [window(vp_ref, vc_ref, r0, span, slice(c0, c0 + LANES)), ones], axis=1)
            oe = lax.dot_general(pt, vext, (((0,), (0,)), ((), ())),
                                 preferred_element_type=F32)
            o = oe[:, :LANES] * (1.0 / oe[:, LANES:])
            for j in range(nq):
                oj = o[2 * j * CHUNK:2 * (j + 1) * CHUNK]
                a_ref[pl.ds(r0 + j * CHUNK, CHUNK), c0:c0 + LANES] = jnp.where(
                    first_head, oj[0:CHUNK], oj[CHUNK:2 * CHUNK])

    n_chunks = ATTN_TILE // CHUNK

    def tile(masked):
        for b in range(P_BUFS):
            z = slice(KEYS, KEYS + CHUNK) if b % 2 == 0 else slice(0, CHUNK)
            p_ref[b, :, z, :] = jnp.zeros((N_QUADS, CHUNK, QUAD), BF16)
        scores(0, 0, masked)
        for ci in range(n_chunks):
            c0 = ci - 2 - ci % 2
            for hq in all_quads:
                if ci + 1 < n_chunks:
                    scores((ci + 1) * CHUNK, (ci + 1) % 2, masked, (hq,))
                if c0 >= 0:
                    values(c0 * CHUNK, (c0 % P_BUFS, (c0 + 1) % P_BUFS),
                           ((ci % 2) * N_QUADS + hq,))
                exps(ci % 2, ci % P_BUFS, (ci % 2) * CHUNK, (hq,))
        values((n_chunks - 2) * CHUNK, ((n_chunks - 2) % P_BUFS, (n_chunks - 1) % P_BUFS))

    pl.when(i == 0)(functools.partial(tile, True))
    pl.when(i > 0)(functools.partial(tile, False))

    o_ref[...] = _rms(a_ref[...], g_ref[...]).astype(BF16)


def _attention(proj, bias_t, g):
    n = ATTN_TILES_PER_SEQ
    cur = lambda col: (lambda b, i: (b * n + i, col))
    prev = lambda col: (lambda b, i: (b * n + jnp.maximum(i - 1, 0), col))
    blk = (ATTN_TILE, D_ATTN)
    return pl.pallas_call(
        _attn_kernel,
        grid=(BATCH, n),
        in_specs=[
            pl.BlockSpec(blk, cur(0)),
            pl.BlockSpec(blk, prev(1)),
            pl.BlockSpec(blk, cur(1)),
            pl.BlockSpec(blk, prev(2)),
            pl.BlockSpec(blk, cur(2)),
            pl.BlockSpec((N_QUADS, KEYS, QUAD), lambda b, i: (0, 0, 0)),
            pl.BlockSpec((1, D_ATTN), lambda b, i: (0, 0)),
        ],
        out_specs=pl.BlockSpec(blk, lambda b, i: (b * n + i, 0)),
        out_shape=jax.ShapeDtypeStruct((T, D_ATTN), BF16),
        scratch_shapes=[
            pltpu.VMEM((2, N_QUADS, KEYS, QUAD), F32),
            pltpu.VMEM((P_BUFS, N_QUADS, KEYS + CHUNK, QUAD), BF16),
            pltpu.VMEM((ATTN_TILE, D_ATTN), F32),
        ],
        compiler_params=_params(("arbitrary", "arbitrary")),
        name="attn",
    )(proj, proj, proj, proj, proj, bias_t, g)


OUT_TM = 512
OUT_SUB = 2


def _pack_bf16_pairs(h):
    half = D_MODEL // 2
    return pltpu.pack_elementwise([h[:, :half], h[:, half:]], packed_dtype=BF16)


def _unpack_pairs_f32(u):
    lo = pltpu.unpack_elementwise(u, index=0, packed_dtype=BF16, unpacked_dtype=F32)
    hi = pltpu.unpack_elementwise(u, index=1, packed_dtype=BF16, unpacked_dtype=F32)
    return lo, hi


ROW_TILE = 8


def _store_row_tiles(ref, packed):
    m = packed.shape[0]
    for s in range(ROW_TILE):
        ref[pl.ds(s, m, stride=ROW_TILE), :] = packed[:, s * LANES:(s + 1) * LANES]


def _load_row_tiles(ref, m, first=0, stride=ROW_TILE):
    return jnp.concatenate(
        [ref[pl.ds(first + s, m, stride=stride), :] for s in range(ROW_TILE)], axis=1)


GROUP_LOGIT_ROW = N_EXPERTS


def _route_cols(lg, base_ref):
    tm = lg.shape[0]
    lt = lg.T
    sub = lax.broadcasted_iota(I32, (SUBLANES, tm), 0)
    sub_f = sub.astype(F32)
    big = jnp.float32(SUBLANES)
    ninf = jnp.float32(-jnp.inf)

    gl = lt[GROUP_LOGIT_ROW:GROUP_LOGIT_ROW + SUBLANES, :]
    gvalid = sub < N_GROUPS
    gmax = jnp.max(jnp.where(gvalid, gl, ninf), axis=0, keepdims=True)
    gsum = jnp.sum(jnp.where(gvalid, jnp.exp(gl - gmax), 0.0), axis=0, keepdims=True)
    p_g = 1.0 / gsum
    g_sel = jnp.min(jnp.where(jnp.logical_and(gvalid, gl == gmax), sub_f, big),
                    axis=0, keepdims=True)
    el = lt[0:EPG, :]
    for g in range(1, N_GROUPS):
        el = jnp.where(g_sel == g, lt[g * EPG:(g + 1) * EPG, :], el)
    v1 = jnp.max(el, axis=0, keepdims=True)
    i1 = jnp.min(jnp.where(el == v1, sub_f, big), axis=0, keepdims=True)
    rest = sub_f != i1
    v2 = jnp.max(jnp.where(rest, el, ninf), axis=0, keepdims=True)
    i2 = jnp.min(jnp.where(jnp.logical_and(rest, el == v2), sub_f, big),
                 axis=0, keepdims=True)
    t = jnp.exp(v2 - v1)
    w1 = p_g * (1.0 / (1.0 + t))
    w2 = p_g * (t / (1.0 + t))
    e1 = g_sel * EPG + i1
    e2 = g_sel * EPG + i2

    erow = lax.broadcasted_iota(I32, (N_EXPERTS, tm), 0).astype(F32)
    hit1 = erow == e1
    hit2 = erow == e2
    onehot = jnp.where(jnp.logical_or(hit1, hit2), 1.0, 0.0)
    r = lax.broadcasted_iota(I32, (tm, tm), 0)
    c = lax.broadcasted_iota(I32, (tm, tm), 1)
    tri = jnp.where(r < c, 1.0, 0.0).astype(BF16)
    base = base_ref[...]
    before = (jnp.dot(onehot.astype(BF16), tri, preferred_element_type=F32)
              + jnp.concatenate([base] * (tm // LANES), axis=1))
    r1 = jnp.sum(jnp.where(hit1, before, 0.0), axis=0, keepdims=True)
    r2 = jnp.sum(jnp.where(hit2, before, 0.0), axis=0, keepdims=True)
    base_ref[...] = base + jnp.broadcast_to(
        jnp.sum(onehot, axis=1, keepdims=True), (N_EXPERTS, LANES))

    ri_t = jnp.where(sub == 0, e1, jnp.where(sub == 1, e2, jnp.where(sub == 2, r1, r2)))
    w_t = jnp.where(sub == 0, w1, jnp.where(sub == 1, w2, 0.0))
    w_t = jnp.concatenate([w_t, jnp.zeros((LANES - SUBLANES, tm), F32)], axis=0)
    return ri_t.astype(I32), w_t.T


def _outproj_kernel(a_ref, cn_ref, x_ref, wo_ref, g2_ref, wr_ref, br_ref,
                    x1_ref, hp_ref, ri_ref, rw_ref, cnt_ref, base_ref):
    @pl.when(pl.program_id(0) == 0)
    def _():
        base_ref[...] = jnp.zeros_like(base_ref)

    sub = OUT_TM // OUT_SUB
    rows_of = lambda s: slice(s * sub, (s + 1) * sub)

    def project(s):
        rows = rows_of(s)
        acc = jnp.dot(a_ref[rows, :], wo_ref[0:D_ATTN, :], preferred_element_type=F32)
        acc = acc + jnp.dot(cn_ref[rows, :], wo_ref[D_ATTN:, :], preferred_element_type=F32)
        x1_ref[rows, :] = x_ref[rows, :] + acc

    def route(s):
        rows = rows_of(s)
        h2 = _rms(x1_ref[rows, :], g2_ref[...])
        logits = jnp.dot(h2.astype(BF16), wr_ref[...],
                         preferred_element_type=F32) + br_ref[...]
        _store_row_tiles(hp_ref.at[pl.ds(s * sub * ROW_TILE, sub * ROW_TILE), :],
                         _pack_bf16_pairs(h2))
        ri_t, rw = _route_cols(logits, base_ref)
        ri_ref[:, rows] = ri_t
        rw_ref[rows, :] = rw

    for s in range(OUT_SUB):
        project(s)
        if s > 0:
            route(s - 1)
    route(OUT_SUB - 1)
    cnt_ref[...] = base_ref[...]


def _outproj(a_n, c_n, x2, wo_bf, g2, wr_bf, br):
    tm = OUT_TM
    row = lambda i: (i, 0)
    fixed = lambda i: (0, 0)
    return pl.pallas_call(
        _outproj_kernel,
        grid=(T // tm,),
        in_specs=[
            pl.BlockSpec((tm, D_ATTN), row),
            pl.BlockSpec((tm, D_CONV), row),
            pl.BlockSpec((tm, D_MODEL), row),
            pl.BlockSpec((D_MODEL, D_MODEL), fixed),
            pl.BlockSpec((1, D_MODEL), fixed),
            pl.BlockSpec((D_MODEL, LANES), fixed),
            pl.BlockSpec((1, LANES), fixed),
        ],
        out_specs=[
            pl.BlockSpec((tm, D_MODEL), row),
            pl.BlockSpec((tm * ROW_TILE, LANES), row),
            pl.BlockSpec((SUBLANES, tm), lambda i: (0, i)),
            pl.BlockSpec((tm, LANES), row),
            pl.BlockSpec((N_EXPERTS, LANES), fixed),
        ],
        out_shape=[
            jax.ShapeDtypeStruct((T, D_MODEL), F32),
            jax.ShapeDtypeStruct((T * ROW_TILE, LANES), I32),
            jax.ShapeDtypeStruct((SUBLANES, T), I32),
            jax.ShapeDtypeStruct((T, LANES), F32),
            jax.ShapeDtypeStruct((N_EXPERTS, LANES), F32),
        ],
        scratch_shapes=[pltpu.VMEM((N_EXPERTS, LANES), F32)],
        compiler_params=_params(("arbitrary",)),
        name="outproj",
    )(a_n, c_n, x2, wo_bf, g2, wr_bf, br)


BLOCK_LANES = 384
DEST_TM = 2048


def _dest_kernel(ri_ref, cnt_ref, dest_ref, be_ref, seg_ref):
    tm = DEST_TM
    erow = lax.broadcasted_iota(I32, (N_EXPERTS, LANES), 0)
    cnt = cnt_ref[...].astype(I32)
    pcnt = ((cnt + (MOE_BLOCK - 1)) >> MOE_BLOCK_SHIFT) << MOE_BLOCK_SHIFT
    pend = pcnt
    for s in (1, 2, 4, 8, 16):
        pend = pend + jnp.where(erow >= s, pltpu.roll(pend, s, axis=0), 0)
    pstart = (pend - pcnt).astype(F32)

    ri = ri_ref[...].astype(F32)
    etok = lax.broadcasted_iota(I32, (N_EXPERTS, tm), 0).astype(F32)
    pstart_t = jnp.concatenate([pstart] * (tm // LANES), axis=1)
    d1 = jnp.sum(jnp.where(etok == ri[0:1], pstart_t, 0.0), axis=0, keepdims=True) + ri[2:3]
    d2 = jnp.sum(jnp.where(etok == ri[1:2], pstart_t, 0.0), axis=0, keepdims=True) + ri[3:4]
    sub = lax.broadcasted_iota(I32, (SUBLANES, tm), 0)
    dest_ref[...] = jnp.where(sub == 0, d1, d2).astype(I32)

    pend_f = pend.astype(F32)
    pend_b = jnp.concatenate([pend_f] * (BLOCK_LANES // LANES), axis=1)
    blk_lane = lax.broadcasted_iota(I32, (1, BLOCK_LANES), 1)
    le = jnp.where(pend_b <= (blk_lane * MOE_BLOCK).astype(F32), 1.0, 0.0)
    be = jnp.minimum(jnp.sum(le, axis=0, keepdims=True), float(N_EXPERTS - 1))
    total = pend_b[N_EXPERTS - 1:N_EXPERTS, :]
    be = jnp.where(blk_lane == N_BLOCKS, total * (1.0 / MOE_BLOCK), be)
    be_ref[...] = jnp.broadcast_to(be, (SUBLANES, BLOCK_LANES)).astype(I32)
    seg_ref[0:N_EXPERTS, :] = pend - pcnt + cnt
    seg_ref[N_EXPERTS:2 * N_EXPERTS, :] = pend


def _dest(ri_t, cnt):
    tm = DEST_TM
    fixed = lambda i: (0, 0)
    tok = lambda i: (0, i)
    return pl.pallas_call(
        _dest_kernel,
        grid=(T // tm,),
        in_specs=[pl.BlockSpec((SUBLANES, tm), tok), pl.BlockSpec((N_EXPERTS, LANES), fixed)],
        out_specs=[pl.BlockSpec((SUBLANES, tm), tok),
                   pl.BlockSpec((SUBLANES, BLOCK_LANES), fixed),
                   pl.BlockSpec((2 * N_EXPERTS, LANES), fixed)],
        out_shape=[jax.ShapeDtypeStruct((SUBLANES, T), I32),
                   jax.ShapeDtypeStruct((SUBLANES, BLOCK_LANES), I32),
                   jax.ShapeDtypeStruct((2 * N_EXPERTS, LANES), I32)],
        compiler_params=_params(("arbitrary",)),
        name="dest",
    )(ri_t, cnt)


GROUP = 3
INV_LEN = (-(-N_BLOCKS // GROUP) * GROUP + GROUP - 1) * MOE_BLOCK
DUMP_ROWS = 4 * MOE_BLOCK


PAD_STEP = 8


def _invert_permutation(d1_ref, d2_ref, vend_ref, pend_ref, inv_ref, pos_end):
    def pad_range(lo, hi):
        def body(k, carry):
            p0 = lo + k * PAD_STEP
            for j in range(PAD_STEP):
                inv_ref[p0 + j] = N_ROUTED + ((p0 + j) & (DUMP_ROWS - 1))
            return carry

        lax.fori_loop(0, (hi - lo + (PAD_STEP - 1)) // PAD_STEP, body, 0)

    def pad_expert(e, carry):
        pad_range(vend_ref[e], pend_ref[e])
        return carry

    lax.fori_loop(0, N_EXPERTS, pad_expert, 0)
    pad_range(pend_ref[N_EXPERTS - 1], pos_end)

    def place(t, carry):
        inv_ref[d1_ref[t]] = 2 * t
        inv_ref[d2_ref[t]] = 2 * t + 1
        return carry

    lax.fori_loop(0, T, place, 0, unroll=16)


BLOCK_TILE_ROWS = MOE_BLOCK * ROW_TILE
Y_ROWS = (N_ROUTED + DUMP_ROWS) * ROW_TILE


def _moe_kernel(meta_ref, d1_ref, d2_ref, vend_ref, pend_ref, h_ref, wg_hbm, wu_hbm, wd_hbm, y_ref,
                *scratch):
    xbufs, ybufs = scratch[:GROUP], scratch[GROUP:2 * GROUP]
    wg_s, wu_s, wd_s, wg_b, wu_b, wd_b, inv_ref, sem_x, sem_y, sem_w = scratch[2 * GROUP:]
    n_used = meta_ref[N_BLOCKS]
    n_groups = (n_used + (GROUP - 1)) // GROUP
    half = D_MODEL // 2

    def issue_gather(b, v):
        for r in range(MOE_BLOCK):
            tok = (inv_ref[b * MOE_BLOCK + r] >> 1) & (T - 1)
            src = h_ref.at[pl.ds(pl.multiple_of(tok * ROW_TILE, ROW_TILE), ROW_TILE), :]
            pltpu.make_async_copy(src, xbufs[v].at[pl.ds(r * ROW_TILE, ROW_TILE), :],
                                  sem_x.at[v]).start()

    def wait_gather(v):
        pltpu.make_async_copy(h_ref.at[pl.ds(0, BLOCK_TILE_ROWS), :], xbufs[v],
                              sem_x.at[v]).wait()

    def issue_scatter(b, v):
        for r in range(MOE_BLOCK):
            slot = inv_ref[b * MOE_BLOCK + r]
            dst = y_ref.at[pl.ds(pl.multiple_of(slot * ROW_TILE, ROW_TILE), ROW_TILE), :]
            pltpu.make_async_copy(ybufs[v].at[pl.ds(r * ROW_TILE, ROW_TILE), :], dst,
                                  sem_y.at[v]).start()

    def wait_scatter(v):
        pltpu.make_async_copy(ybufs[v], y_ref.at[pl.ds(0, BLOCK_TILE_ROWS), :],
                              sem_y.at[v]).wait()

    def weight_copies(e):
        return (pltpu.make_async_copy(wg_hbm.at[e], wg_s, sem_w),
                pltpu.make_async_copy(wu_hbm.at[e], wu_s, sem_w),
                pltpu.make_async_copy(wd_hbm.at[e], wd_s, sem_w))

    def load_expert(b):
        e = meta_ref[b]
        prev = meta_ref[jnp.maximum(b - 1, 0)]
        first = jnp.logical_and(b < n_used, jnp.logical_or(b == 0, e != prev))

        @pl.when(first)
        def _():
            for c in weight_copies(e):
                c.wait()
            wg_b[...] = wg_s[...].astype(BF16)
            wu_b[...] = wu_s[...].astype(BF16)
            wd_b[...] = wd_s[...].astype(BF16)
            nxt = lax.while_loop(
                lambda j: jnp.logical_and(j < n_used, meta_ref[jnp.minimum(j, N_BLOCKS - 1)] == e),
                lambda j: j + 1, b + 1)

            @pl.when(nxt < n_used)
            def _():
                for c in weight_copies(meta_ref[nxt]):
                    c.start(priority=1)

    def compute(v):
        lo, hi = _unpack_pairs_f32(_load_row_tiles(xbufs[v], MOE_BLOCK))
        lo, hi = lo.astype(BF16), hi.astype(BF16)
        g = (jnp.dot(lo, wg_b[0:half, :], preferred_element_type=F32)
             + jnp.dot(hi, wg_b[half:, :], preferred_element_type=F32))
        u = (jnp.dot(lo, wu_b[0:half, :], preferred_element_type=F32)
             + jnp.dot(hi, wu_b[half:, :], preferred_element_type=F32))
        a = g * (1.0 / (1.0 + jnp.exp(-g))) * u
        y = jnp.dot(a.astype(BF16), wd_b[...], preferred_element_type=F32)
        _store_row_tiles(ybufs[v], _pack_bf16_pairs(y))

    def block(b, v, scatter_prev, wait_y):
        wait_gather(v)
        if wait_y:
            wait_scatter(v)
        load_expert(b)
        issue_gather(b + GROUP - 1, (v + GROUP - 1) % GROUP)
        if scatter_prev:
            issue_scatter(b - 1, (v + GROUP - 1) % GROUP)
        compute(v)

    for c in weight_copies(meta_ref[0]):
        c.start()
    zeros_buf = ybufs[GROUP - 1]
    zeros_buf[...] = jnp.zeros_like(zeros_buf)
    dump_fills = [
        pltpu.make_async_copy(
            zeros_buf, y_ref.at[pl.ds((N_ROUTED + k * MOE_BLOCK) * ROW_TILE, BLOCK_TILE_ROWS), :],
            sem_y.at[GROUP - 1])
        for k in range(DUMP_ROWS // MOE_BLOCK)]
    for c in dump_fills:
        c.start()
    _invert_permutation(d1_ref, d2_ref, vend_ref, pend_ref, inv_ref,
                        (n_groups * GROUP + GROUP - 1) * MOE_BLOCK)
    for c in dump_fills:
        c.wait()
    for v in range(GROUP - 1):
        issue_gather(v, v)
    for v in range(GROUP):
        block(v, v, scatter_prev=v > 0, wait_y=False)

    def group(gi, carry):
        for v in range(GROUP):
            block(gi * GROUP + v, v, scatter_prev=True, wait_y=True)
        return carry

    lax.fori_loop(1, n_groups, group, 0)

    last = n_groups * GROUP
    issue_scatter(last - 1, GROUP - 1)
    for v in range(GROUP - 1):
        wait_gather(v)
    for v in range(GROUP):
        wait_scatter(v)


def _moe(meta, d1, d2, vend, pend, h_tiles, w_gate, w_up, w_down):
    any_spec = pl.BlockSpec(memory_space=pl.ANY)
    xy = pltpu.VMEM((BLOCK_TILE_ROWS, LANES), I32)
    return pl.pallas_call(
        _moe_kernel,
        grid_spec=pltpu.PrefetchScalarGridSpec(
            num_scalar_prefetch=5,
            grid=(1,),
            in_specs=[any_spec, any_spec, any_spec, any_spec],
            out_specs=any_spec,
            scratch_shapes=[
                *([xy] * (2 * GROUP)),
                pltpu.VMEM((D_MODEL, D_EXPERT), F32),
                pltpu.VMEM((D_MODEL, D_EXPERT), F32),
                pltpu.VMEM((D_EXPERT, D_MODEL), F32),
                pltpu.VMEM((D_MODEL, D_EXPERT), BF16),
                pltpu.VMEM((D_MODEL, D_EXPERT), BF16),
                pltpu.VMEM((D_EXPERT, D_MODEL), BF16),
                pltpu.SMEM((INV_LEN,), I32),
                pltpu.SemaphoreType.DMA((GROUP,)),
                pltpu.SemaphoreType.DMA((GROUP,)),
                pltpu.SemaphoreType.DMA(()),
            ],
        ),
        out_shape=jax.ShapeDtypeStruct((Y_ROWS, LANES), I32),
        compiler_params=_params(("arbitrary",)),
        name="moe",
    )(meta, d1, d2, vend, pend, h_tiles, w_gate, w_up, w_down)


COMB_TM = 512


def _combine_kernel(x1_ref, rw_ref, gf_ref, y_ref, o_ref):
    tm = COMB_TM
    w = rw_ref[...]
    w0, w1 = w[:, 0:1], w[:, 1:2]
    lo0, hi0 = _unpack_pairs_f32(_load_row_tiles(y_ref, tm, 0, 2 * ROW_TILE))
    lo1, hi1 = _unpack_pairs_f32(_load_row_tiles(y_ref, tm, ROW_TILE, 2 * ROW_TILE))
    moe = jnp.concatenate([lo0 * w0 + lo1 * w1, hi0 * w0 + hi1 * w1], axis=1)
    o_ref[...] = _rms(x1_ref[...] + moe, gf_ref[...])


def _combine(x1, rw, gf, y_tiles):
    tm = COMB_TM
    row = lambda i: (i, 0)
    return pl.pallas_call(
        _combine_kernel,
        grid=(T // tm,),
        in_specs=[
            pl.BlockSpec((tm, D_MODEL), row),
            pl.BlockSpec((tm, LANES), row),
            pl.BlockSpec((1, D_MODEL), lambda i: (0, 0)),
            pl.BlockSpec((tm * 2 * ROW_TILE, LANES), row),
        ],
        out_specs=pl.BlockSpec((tm, D_MODEL), row),
        out_shape=jax.ShapeDtypeStruct((T, D_MODEL), F32),
        compiler_params=_params(("arbitrary",)),
        name="combine",
    )(x1, rw, gf, y_tiles)


def _layer(x2, norm1, w_in, rel_bias, conv_w, g_out_attn, g_out_conv, w_out, norm2,
           w_rg, b_rg, w_re, b_re, w_gate, w_up, w_down):
    conv_w8 = jnp.zeros((SUBLANES, D_CONV), F32).at[0:3].set(conv_w)
    qkv, c_n = _inproj(x2, norm1[None, :], w_in.astype(BF16), conv_w8, g_out_conv[None, :])
    a_n = _attention(qkv, _bias_table(rel_bias), g_out_attn[None, :])

    w_r = jnp.zeros((D_MODEL, LANES), F32)
    g0 = GROUP_LOGIT_ROW
    w_r = w_r.at[:, 0:N_EXPERTS].set(w_re).at[:, g0:g0 + N_GROUPS].set(w_rg)
    b_r = jnp.zeros((1, LANES), F32)
    b_r = b_r.at[0, 0:N_EXPERTS].set(b_re).at[0, g0:g0 + N_GROUPS].set(b_rg)
    x1, hp, ri_t, rw, cnt = _outproj(a_n, c_n, x2, w_out.astype(BF16), norm2[None, :],
                                     w_r.astype(BF16), b_r)

    dest_t, be, seg = _dest(ri_t, cnt)
    y_tiles = _moe(be[0], dest_t[0], dest_t[1], seg[0:N_EXPERTS, 0],
                   seg[N_EXPERTS:2 * N_EXPERTS, 0], hp, w_gate, w_up, w_down)
    return x1, rw, y_tiles


def kernel(x, norm1, w_in, rel_bias, conv_w, g_out_attn, g_out_conv, w_out, norm2,
           w_router_group, b_router_group, w_router_expert, b_router_expert,
           w_gate, w_up, w_down, norm_final):
    assert x.shape == (BATCH, SEQ, D_MODEL) and norm1.shape[0] == 1
    x2 = x.reshape(T, D_MODEL)
    x1, rw, y_tiles = _layer(
        x2, norm1[0], w_in[0], rel_bias[0], conv_w[0], g_out_attn[0], g_out_conv[0],
        w_out[0], norm2[0], w_router_group[0], b_router_group[0], w_router_expert[0],
        b_router_expert[0], w_gate[0], w_up[0], w_down[0])
    out = _combine(x1, rw, norm_final[None, :], y_tiles)
    return out.reshape(BATCH, SEQ, D_MODEL)
```

```python
import functools

import jax
import jax.numpy as jnp
from jax import lax
from jax.experimental import pallas as pl
from jax.experimental.pallas import tpu as pltpu

F32 = jnp.float32
BF16 = jnp.bfloat16
I32 = jnp.int32

D_MODEL = 2048
BATCH = 4
SEQ = 4096
T = BATCH * SEQ
CHUNK = 64
LEFT_CHUNKS = 8
BAND = LEFT_CHUNKS + 1
KEYS = BAND * CHUNK
D_ATTN = 1024
D_CONV = 1024
HEAD_DIM = 64
N_HEADS = 16
N_PAIRS = N_HEADS // 2
REL_CLIP = 256
D_IN_PROJ = 6 * 1024
N_GROUPS = 4
EPG = 8
N_EXPERTS = 32
D_EXPERT = 512
MOE_BLOCK = 256
MOE_BLOCK_SHIFT = MOE_BLOCK.bit_length() - 1
assert MOE_BLOCK == 1 << MOE_BLOCK_SHIFT
N_ROUTED = 2 * T
N_BLOCKS = N_ROUTED // MOE_BLOCK + N_EXPERTS
EPS = 1e-6
NEG_INF = -1e30
LANES = 128
VMEM_LIMIT = 52 * 1024 * 1024

ATTN_TILE = 512
ATTN_TILES_PER_SEQ = SEQ // ATTN_TILE


def _params(sem):
    return pltpu.CompilerParams(dimension_semantics=sem, vmem_limit_bytes=VMEM_LIMIT)


def _rms(x, g):
    ms = jnp.mean(x * x, axis=-1, keepdims=True)
    return x * lax.rsqrt(ms + EPS) * g


INPROJ_TM = 512
INPROJ_TN = 1536


D_QKV = 3 * D_ATTN
SUBLANES = 8


def _short_conv(b, z, z_before, w, g):
    n = z.shape[0]
    ze = jnp.concatenate([z_before, z], axis=0)
    z1 = pltpu.roll(ze, 1, axis=0)[SUBLANES:SUBLANES + n]
    z2 = pltpu.roll(ze, 2, axis=0)[SUBLANES:SUBLANES + n]
    y = w[0:1] * z2 + w[1:2] * z1 + w[2:3] * z
    return _rms(b * y, g).astype(BF16)


def _inproj_kernel(x_ref, g_ref, w_ref, cw_ref, gc_ref, qkv_ref, cn_ref, ztail_ref):
    i = pl.program_id(0)

    @pl.when(i == 0)
    def _():
        ztail_ref[...] = jnp.zeros_like(ztail_ref)

    hn = _rms(x_ref[...], g_ref[...]).astype(BF16)
    group = lambda k: jnp.dot(hn, w_ref[:, D_QKV + k * D_CONV:D_QKV + (k + 1) * D_CONV],
                              preferred_element_type=F32)
    b, z = group(0), group(1) * group(2)
    seq_start = (i % (SEQ // INPROJ_TM)) == 0
    z_before = jnp.where(seq_start, 0.0, ztail_ref[...])
    ztail_ref[...] = z[INPROJ_TM - SUBLANES:]
    cn_ref[...] = _short_conv(b, z, z_before, cw_ref[...], gc_ref[...])
    for j in range(D_QKV // INPROJ_TN):
        cols = slice(j * INPROJ_TN, (j + 1) * INPROJ_TN)
        qkv_ref[:, cols] = jnp.dot(hn, w_ref[:, cols],
                                   preferred_element_type=F32).astype(BF16)


def _inproj(x2, g, w_bf, conv_w, g_conv):
    tm = INPROJ_TM
    fixed = lambda i: (0, 0)
    row = lambda i: (i, 0)
    return pl.pallas_call(
        _inproj_kernel,
        grid=(T // tm,),
        in_specs=[
            pl.BlockSpec((tm, D_MODEL), row),
            pl.BlockSpec((1, D_MODEL), fixed),
            pl.BlockSpec((D_MODEL, D_IN_PROJ), fixed, pipeline_mode=pl.Buffered(1)),
            pl.BlockSpec((SUBLANES, D_CONV), fixed),
            pl.BlockSpec((1, D_CONV), fixed),
        ],
        out_specs=[pl.BlockSpec((tm, D_QKV), row), pl.BlockSpec((tm, D_CONV), row)],
        out_shape=[jax.ShapeDtypeStruct((T, D_QKV), BF16),
                   jax.ShapeDtypeStruct((T, D_CONV), BF16)],
        scratch_shapes=[pltpu.VMEM((SUBLANES, D_CONV), F32)],
        compiler_params=_params(("arbitrary",)),
        name="inproj",
    )(x2, g, w_bf, conv_w, g_conv)


N_QUADS = N_HEADS // 4
QUAD = 4 * HEAD_DIM
BIAS_W = 640
BIAS_WIN = 512
VAL_CHUNKS = 4
P_ROWS = KEYS + (VAL_CHUNKS - 1) * CHUNK
P_BUFS = 2 * VAL_CHUNKS


def _bias_kernel(b_ref, o_ref):
    first_head = lax.broadcasted_iota(I32, (KEYS, LANES), 1) < HEAD_DIM
    for hq in range(N_QUADS):
        for col in range(2):
            h0 = 4 * hq + 2 * col
            pieces = []
            for hh in range(2):
                x = jnp.broadcast_to(b_ref[h0 + hh:h0 + hh + 1, :], (KEYS, BIAS_W))
                rolled = pltpu.roll(x, 0, 1, stride=1, stride_axis=0)
                pieces.append(rolled[:, BIAS_WIN:BIAS_WIN + LANES])
            o_ref[hq, :, col * LANES:(col + 1) * LANES] = jnp.where(
                first_head, pieces[0], pieces[1])


def _bias_table(rel_bias):
    r = rel_bias.astype(F32)
    edge = jnp.broadcast_to(r[:, 2 * REL_CLIP:], (N_HEADS, KEYS - REL_CLIP + 1))
    b = jnp.concatenate([r[:, REL_CLIP:2 * REL_CLIP], edge,
                         r[:, REL_CLIP - (BIAS_W - KEYS - 1):REL_CLIP]], axis=1)
    odd = jnp.concatenate([b[:, BIAS_W - HEAD_DIM:], b[:, :BIAS_W - HEAD_DIM]], axis=1)
    is_odd = (jnp.arange(N_HEADS) % 2 == 1)[:, None]
    bvec = jnp.where(is_odd, odd, b)
    return pl.pallas_call(
        _bias_kernel,
        out_shape=jax.ShapeDtypeStruct((N_QUADS, KEYS, QUAD), F32),
        compiler_params=pltpu.CompilerParams(vmem_limit_bytes=VMEM_LIMIT),
        name="biastab",
    )(bvec)


def _attn_kernel(q_ref, kp_ref, kc_ref, vp_ref, vc_ref, bias_ref, g_ref, o_ref,
                 st_ref, p_ref, a_ref):
    i = pl.program_id(1)

    def window(prev_ref, cur_ref, r0, n, cols):
        parts = []
        if r0 < ATTN_TILE:
            parts.append(prev_ref[r0:min(ATTN_TILE, r0 + n), cols])
        if r0 + n > ATTN_TILE:
            parts.append(cur_ref[max(r0, ATTN_TILE) - ATTN_TILE:r0 + n - ATTN_TILE, cols])
        return parts[0] if len(parts) == 1 else jnp.concatenate(parts, axis=0)

    head_of_lane = lax.broadcasted_iota(I32, (CHUNK, QUAD), 1) // HEAD_DIM
    first_head = lax.broadcasted_iota(I32, (CHUNK, LANES), 1) < HEAD_DIM
    key_row = lax.broadcasted_iota(I32, (KEYS, QUAD), 0)
    n_rb = KEYS // CHUNK

    all_quads = tuple(range(N_QUADS))
    all_pairs = tuple(range(N_PAIRS))

    def scores(r0, buf, masked, quads=all_quads):
        for hq in quads:
            c0 = hq * QUAD
            q4 = q_ref[pl.ds(r0, CHUNK), c0:c0 + QUAD] * (HEAD_DIM ** -0.5)
            zero = jnp.zeros_like(q4)
            qbd = jnp.concatenate(
                [jnp.where(head_of_lane == h, q4, zero) for h in range(4)], axis=0)
            k4 = window(kp_ref, kc_ref, r0, KEYS, slice(c0, c0 + QUAD))
            st = lax.dot_general(k4, qbd, (((1,), (1,)), ((), ())),
                                 preferred_element_type=F32)
            st = st + bias_ref[hq]
            if masked:
                st = jnp.where(key_row + r0 >= ATTN_TILE, st, NEG_INF)
            st_ref[buf, hq] = st

    def exps(buf, pbuf, off, quads=all_quads):
        for hq in quads:
            m = st_ref[buf, hq, 0:CHUNK, :]
            for rb in range(1, n_rb):
                m = jnp.maximum(m, st_ref[buf, hq, rb * CHUNK:(rb + 1) * CHUNK, :])
            m = jnp.max(m, axis=0, keepdims=True)
            for rb in range(n_rb):
                rows = slice(rb * CHUNK, (rb + 1) * CHUNK)
                prow = slice(off + rb * CHUNK, off + (rb + 1) * CHUNK)
                p_ref[pbuf, hq, prow, :] = jnp.exp(st_ref[buf, hq, rows, :] - m).astype(BF16)

    def values(r0, pbufs, pairs=all_pairs):
        nq = len(pbufs)
        span = KEYS + (nq - 1) * CHUNK
        ones = jnp.ones((span, LANES), BF16)
        for hp in pairs:
            c0 = hp * LANES
            half = (hp % 2) * LANES
            pt = jnp.concatenate(
                [p_ref[b, hp // 2, 0:span, half:half + LANES] for b in pbufs], axis=1)
            vext = jnp.concatenate(
                [window(vp_ref, vc_ref, r0, span, slice(c0, c0 + LANES)), ones], axis=1)
            oe = lax.dot_general(pt, vext, (((0,), (0,)), ((), ())),
                                 preferred_element_type=F32)
            o = oe[:, :LANES] * (1.0 / oe[:, LANES:])
            for j in range(nq):
                oj = o[2 * j * CHUNK:2 * (j + 1) * CHUNK]
                a_ref[pl.ds(r0 + j * CHUNK, CHUNK), c0:c0 + LANES] = jnp.where(
                    first_head, oj[0:CHUNK], oj[CHUNK:2 * CHUNK])

    n_chunks = ATTN_TILE // CHUNK

    def group_bufs(c0):
        return tuple((c0 + j) % P_BUFS for j in range(VAL_CHUNKS))

    def tile(masked):
        for b in range(P_BUFS):
            off = (b % VAL_CHUNKS) * CHUNK
            for z in (slice(0, off), slice(off + KEYS, P_ROWS)):
                if z.stop > z.start:
                    p_ref[b, :, z, :] = jnp.zeros((N_QUADS, z.stop - z.start, QUAD), BF16)
        scores(0, 0, masked)
        for ci in range(n_chunks):
            c0 = (ci // VAL_CHUNKS - 1) * VAL_CHUNKS
            k = ci % VAL_CHUNKS
            for hq in all_quads:
                if ci + 1 < n_chunks:
                    scores((ci + 1) * CHUNK, (ci + 1) % 2, masked, (hq,))
                if c0 >= 0 and hq % 2 == 0:
                    values(c0 * CHUNK, group_bufs(c0), (2 * k + hq // 2,))
                exps(ci % 2, ci % P_BUFS, k * CHUNK, (hq,))
        last = n_chunks - VAL_CHUNKS
        values(last * CHUNK, group_bufs(last))

    pl.when(i == 0)(functools.partial(tile, True))
    pl.when(i > 0)(functools.partial(tile, False))

    o_ref[...] = _rms(a_ref[...], g_ref[...]).astype(BF16)


def _attention(proj, bias_t, g):
    n = ATTN_TILES_PER_SEQ
    cur = lambda col: (lambda b, i: (b * n + i, col))
    prev = lambda col: (lambda b, i: (b * n + jnp.maximum(i - 1, 0), col))
    blk = (ATTN_TILE, D_ATTN)
    return pl.pallas_call(
        _attn_kernel,
        grid=(BATCH, n),
        in_specs=[
            pl.BlockSpec(blk, cur(0)),
            pl.BlockSpec(blk, prev(1)),
            pl.BlockSpec(blk, cur(1)),
            pl.BlockSpec(blk, prev(2)),
            pl.BlockSpec(blk, cur(2)),
            pl.BlockSpec((N_QUADS, KEYS, QUAD), lambda b, i: (0, 0, 0)),
            pl.BlockSpec((1, D_ATTN), lambda b, i: (0, 0)),
        ],
        out_specs=pl.BlockSpec(blk, lambda b, i: (b * n + i, 0)),
        out_shape=jax.ShapeDtypeStruct((T, D_ATTN), BF16),
        scratch_shapes=[
            pltpu.VMEM((2, N_QUADS, KEYS, QUAD), F32),
            pltpu.VMEM((P_BUFS, N_QUADS, P_ROWS, QUAD), BF16),
            pltpu.VMEM((ATTN_TILE, D_ATTN), F32),
        ],
        compiler_params=_params(("arbitrary", "arbitrary")),
        name="attn",
    )(proj, proj, proj, proj, proj, bias_t, g)


OUT_TM = 512
OUT_SUB = 2


def _pack_bf16_pairs(h):
    half = D_MODEL // 2
    return pltpu.pack_elementwise([h[:, :half], h[:, half:]], packed_dtype=BF16)


def _unpack_pairs_f32(u):
    lo = pltpu.unpack_elementwise(u, index=0, packed_dtype=BF16, unpacked_dtype=F32)
    hi = pltpu.unpack_elementwise(u, index=1, packed_dtype=BF16, unpacked_dtype=F32)
    return lo, hi


ROW_TILE = 8


def _store_row_tiles(ref, packed):
    m = packed.shape[0]
    for s in range(ROW_TILE):
        ref[pl.ds(s, m, stride=ROW_TILE), :] = packed[:, s * LANES:(s + 1) * LANES]


def _load_row_tiles(ref, m, first=0, stride=ROW_TILE):
    return jnp.concatenate(
        [ref[pl.ds(first + s, m, stride=stride), :] for s in range(ROW_TILE)], axis=1)


GROUP_LOGIT_ROW = N_EXPERTS


def _route_cols(lg, base_ref):
    tm = lg.shape[0]
    lt = lg.T
    sub = lax.broadcasted_iota(I32, (SUBLANES, tm), 0)
    sub_f = sub.astype(F32)
    big = jnp.float32(SUBLANES)
    ninf = jnp.float32(-jnp.inf)

    gl = lt[GROUP_LOGIT_ROW:GROUP_LOGIT_ROW + SUBLANES, :]
    gvalid = sub < N_GROUPS
    gmax = jnp.max(jnp.where(gvalid, gl, ninf), axis=0, keepdims=True)
    gsum = jnp.sum(jnp.where(gvalid, jnp.exp(gl - gmax), 0.0), axis=0, keepdims=True)
    p_g = 1.0 / gsum
    g_sel = jnp.min(jnp.where(jnp.logical_and(gvalid, gl == gmax), sub_f, big),
                    axis=0, keepdims=True)
    el = lt[0:EPG, :]
    for g in range(1, N_GROUPS):
        el = jnp.where(g_sel == g, lt[g * EPG:(g + 1) * EPG, :], el)
    v1 = jnp.max(el, axis=0, keepdims=True)
    i1 = jnp.min(jnp.where(el == v1, sub_f, big), axis=0, keepdims=True)
    rest = sub_f != i1
    v2 = jnp.max(jnp.where(rest, el, ninf), axis=0, keepdims=True)
    i2 = jnp.min(jnp.where(jnp.logical_and(rest, el == v2), sub_f, big),
                 axis=0, keepdims=True)
    t = jnp.exp(v2 - v1)
    w1 = p_g * (1.0 / (1.0 + t))
    w2 = p_g * (t / (1.0 + t))
    e1 = g_sel * EPG + i1
    e2 = g_sel * EPG + i2

    erow = lax.broadcasted_iota(I32, (N_EXPERTS, tm), 0).astype(F32)
    hit1 = erow == e1
    hit2 = erow == e2
    onehot = jnp.where(jnp.logical_or(hit1, hit2), 1.0, 0.0)
    r = lax.broadcasted_iota(I32, (tm, tm), 0)
    c = lax.broadcasted_iota(I32, (tm, tm), 1)
    tri = jnp.where(r < c, 1.0, 0.0).astype(BF16)
    base = base_ref[...]
    before = (jnp.dot(onehot.astype(BF16), tri, preferred_element_type=F32)
              + jnp.concatenate([base] * (tm // LANES), axis=1))
    r1 = jnp.sum(jnp.where(hit1, before, 0.0), axis=0, keepdims=True)
    r2 = jnp.sum(jnp.where(hit2, before, 0.0), axis=0, keepdims=True)
    base_ref[...] = base + jnp.broadcast_to(
        jnp.sum(onehot, axis=1, keepdims=True), (N_EXPERTS, LANES))

    ri_t = jnp.where(sub == 0, e1, jnp.where(sub == 1, e2, jnp.where(sub == 2, r1, r2)))
    w_t = jnp.where(sub == 0, w1, jnp.where(sub == 1, w2, 0.0))
    w_t = jnp.concatenate([w_t, jnp.zeros((LANES - SUBLANES, tm), F32)], axis=0)
    return ri_t.astype(I32), w_t.T


def _outproj_kernel(a_ref, cn_ref, x_ref, wo_ref, g2_ref, wr_ref, br_ref,
                    x1_ref, hp_ref, ri_ref, rw_ref, cnt_ref, base_ref):
    @pl.when(pl.program_id(0) == 0)
    def _():
        base_ref[...] = jnp.zeros_like(base_ref)

    sub = OUT_TM // OUT_SUB
    rows_of = lambda s: slice(s * sub, (s + 1) * sub)

    def project(s):
        rows = rows_of(s)
        acc = jnp.dot(a_ref[rows, :], wo_ref[0:D_ATTN, :], preferred_element_type=F32)
        acc = acc + jnp.dot(cn_ref[rows, :], wo_ref[D_ATTN:, :], preferred_element_type=F32)
        x1_ref[rows, :] = x_ref[rows, :] + acc

    def route(s):
        rows = rows_of(s)
        h2 = _rms(x1_ref[rows, :], g2_ref[...])
        logits = jnp.dot(h2.astype(BF16), wr_ref[...],
                         preferred_element_type=F32) + br_ref[...]
        _store_row_tiles(hp_ref.at[pl.ds(s * sub * ROW_TILE, sub * ROW_TILE), :],
                         _pack_bf16_pairs(h2))
        ri_t, rw = _route_cols(logits, base_ref)
        ri_ref[:, rows] = ri_t
        rw_ref[rows, :] = rw

    for s in range(OUT_SUB):
        project(s)
        if s > 0:
            route(s - 1)
    route(OUT_SUB - 1)
    cnt_ref[...] = base_ref[...]


def _outproj(a_n, c_n, x2, wo_bf, g2, wr_bf, br):
    tm = OUT_TM
    row = lambda i: (i, 0)
    fixed = lambda i: (0, 0)
    return pl.pallas_call(
        _outproj_kernel,
        grid=(T // tm,),
        in_specs=[
            pl.BlockSpec((tm, D_ATTN), row),
            pl.BlockSpec((tm, D_CONV), row),
            pl.BlockSpec((tm, D_MODEL), row),
            pl.BlockSpec((D_MODEL, D_MODEL), fixed),
            pl.BlockSpec((1, D_MODEL), fixed),
            pl.BlockSpec((D_MODEL, LANES), fixed),
            pl.BlockSpec((1, LANES), fixed),
        ],
        out_specs=[
            pl.BlockSpec((tm, D_MODEL), row),
            pl.BlockSpec((tm * ROW_TILE, LANES), row),
            pl.BlockSpec((SUBLANES, tm), lambda i: (0, i)),
            pl.BlockSpec((tm, LANES), row),
            pl.BlockSpec((N_EXPERTS, LANES), fixed),
        ],
        out_shape=[
            jax.ShapeDtypeStruct((T, D_MODEL), F32),
            jax.ShapeDtypeStruct((T * ROW_TILE, LANES), I32),
            jax.ShapeDtypeStruct((SUBLANES, T), I32),
            jax.ShapeDtypeStruct((T, LANES), F32),
            jax.ShapeDtypeStruct((N_EXPERTS, LANES), F32),
        ],
        scratch_shapes=[pltpu.VMEM((N_EXPERTS, LANES), F32)],
        compiler_params=_params(("arbitrary",)),
        name="outproj",
    )(a_n, c_n, x2, wo_bf, g2, wr_bf, br)


BLOCK_LANES = 384
DEST_TM = 2048


def _dest_kernel(ri_ref, cnt_ref, dest_ref, be_ref, seg_ref):
    tm = DEST_TM
    erow = lax.broadcasted_iota(I32, (N_EXPERTS, LANES), 0)
    cnt = cnt_ref[...].astype(I32)
    pcnt = ((cnt + (MOE_BLOCK - 1)) >> MOE_BLOCK_SHIFT) << MOE_BLOCK_SHIFT
    pend = pcnt
    for s in (1, 2, 4, 8, 16):
        pend = pend + jnp.where(erow >= s, pltpu.roll(pend, s, axis=0), 0)
    pstart = (pend - pcnt).astype(F32)

    ri = ri_ref[...].astype(F32)
    etok = lax.broadcasted_iota(I32, (N_EXPERTS, tm), 0).astype(F32)
    pstart_t = jnp.concatenate([pstart] * (tm // LANES), axis=1)
    d1 = jnp.sum(jnp.where(etok == ri[0:1], pstart_t, 0.0), axis=0, keepdims=True) + ri[2:3]
    d2 = jnp.sum(jnp.where(etok == ri[1:2], pstart_t, 0.0), axis=0, keepdims=True) + ri[3:4]
    sub = lax.broadcasted_iota(I32, (SUBLANES, tm), 0)
    dest_ref[...] = jnp.where(sub == 0, d1, d2).astype(I32)

    pend_f = pend.astype(F32)
    pend_b = jnp.concatenate([pend_f] * (BLOCK_LANES // LANES), axis=1)
    blk_lane = lax.broadcasted_iota(I32, (1, BLOCK_LANES), 1)
    le = jnp.where(pend_b <= (blk_lane * MOE_BLOCK).astype(F32), 1.0, 0.0)
    be = jnp.minimum(jnp.sum(le, axis=0, keepdims=True), float(N_EXPERTS - 1))
    total = pend_b[N_EXPERTS - 1:N_EXPERTS, :]
    be = jnp.where(blk_lane == N_BLOCKS, total * (1.0 / MOE_BLOCK), be)
    be_ref[...] = jnp.broadcast_to(be, (SUBLANES, BLOCK_LANES)).astype(I32)
    seg_ref[0:N_EXPERTS, :] = pend - pcnt + cnt
    seg_ref[N_EXPERTS:2 * N_EXPERTS, :] = pend


def _dest(ri_t, cnt):
    tm = DEST_TM
    fixed = lambda i: (0, 0)
    tok = lambda i: (0, i)
    return pl.pallas_call(
        _dest_kernel,
        grid=(T // tm,),
        in_specs=[pl.BlockSpec((SUBLANES, tm), tok), pl.BlockSpec((N_EXPERTS, LANES), fixed)],
        out_specs=[pl.BlockSpec((SUBLANES, tm), tok),
                   pl.BlockSpec((SUBLANES, BLOCK_LANES), fixed),
                   pl.BlockSpec((2 * N_EXPERTS, LANES), fixed)],
        out_shape=[jax.ShapeDtypeStruct((SUBLANES, T), I32),
                   jax.ShapeDtypeStruct((SUBLANES, BLOCK_LANES), I32),
                   jax.ShapeDtypeStruct((2 * N_EXPERTS, LANES), I32)],
        compiler_params=_params(("arbitrary",)),
        name="dest",
    )(ri_t, cnt)


GROUP = 3
INV_LEN = (-(-N_BLOCKS // GROUP) * GROUP + GROUP - 1) * MOE_BLOCK
DUMP_ROWS = 4 * MOE_BLOCK


PAD_STEP = 8


def _invert_permutation(d1_ref, d2_ref, vend_ref, pend_ref, inv_ref, pos_end):
    def pad_range(lo, hi):
        def body(k, carry):
            p0 = lo + k * PAD_STEP
            for j in range(PAD_STEP):
                inv_ref[p0 + j] = N_ROUTED + ((p0 + j) & (DUMP_ROWS - 1))
            return carry

        lax.fori_loop(0, (hi - lo + (PAD_STEP - 1)) // PAD_STEP, body, 0)

    def pad_expert(e, carry):
        pad_range(vend_ref[e], pend_ref[e])
        return carry

    lax.fori_loop(0, N_EXPERTS, pad_expert, 0)
    pad_range(pend_ref[N_EXPERTS - 1], pos_end)

    def place(t, carry):
        inv_ref[d1_ref[t]] = 2 * t
        inv_ref[d2_ref[t]] = 2 * t + 1
        return carry

    lax.fori_loop(0, T, place, 0, unroll=16)


BLOCK_TILE_ROWS = MOE_BLOCK * ROW_TILE
Y_ROWS = (N_ROUTED + DUMP_ROWS) * ROW_TILE


def _moe_kernel(meta_ref, d1_ref, d2_ref, vend_ref, pend_ref, h_ref, wg_hbm, wu_hbm, wd_hbm, y_ref,
                *scratch):
    xbufs, ybufs = scratch[:GROUP], scratch[GROUP:2 * GROUP]
    wg_s, wu_s, wd_s, wg_b, wu_b, wd_b, inv_ref, sem_x, sem_y, sem_w = scratch[2 * GROUP:]
    n_used = meta_ref[N_BLOCKS]
    n_groups = (n_used + (GROUP - 1)) // GROUP
    half = D_MODEL // 2

    def issue_gather(b, v):
        for r in range(MOE_BLOCK):
            tok = (inv_ref[b * MOE_BLOCK + r] >> 1) & (T - 1)
            src = h_ref.at[pl.ds(pl.multiple_of(tok * ROW_TILE, ROW_TILE), ROW_TILE), :]
            pltpu.make_async_copy(src, xbufs[v].at[pl.ds(r * ROW_TILE, ROW_TILE), :],
                                  sem_x.at[v]).start()

    def wait_gather(v):
        pltpu.make_async_copy(h_ref.at[pl.ds(0, BLOCK_TILE_ROWS), :], xbufs[v],
                              sem_x.at[v]).wait()

    def issue_scatter(b, v):
        for r in range(MOE_BLOCK):
            slot = inv_ref[b * MOE_BLOCK + r]
            dst = y_ref.at[pl.ds(pl.multiple_of(slot * ROW_TILE, ROW_TILE), ROW_TILE), :]
            pltpu.make_async_copy(ybufs[v].at[pl.ds(r * ROW_TILE, ROW_TILE), :], dst,
                                  sem_y.at[v]).start()

    def wait_scatter(v):
        pltpu.make_async_copy(ybufs[v], y_ref.at[pl.ds(0, BLOCK_TILE_ROWS), :],
                              sem_y.at[v]).wait()

    def weight_copies(e):
        return (pltpu.make_async_copy(wg_hbm.at[e], wg_s, sem_w),
                pltpu.make_async_copy(wu_hbm.at[e], wu_s, sem_w),
                pltpu.make_async_copy(wd_hbm.at[e], wd_s, sem_w))

    def load_expert(b):
        e = meta_ref[b]
        prev = meta_ref[jnp.maximum(b - 1, 0)]
        first = jnp.logical_and(b < n_used, jnp.logical_or(b == 0, e != prev))

        @pl.when(first)
        def _():
            for c in weight_copies(e):
                c.wait()
            wg_b[...] = wg_s[...].astype(BF16)
            wu_b[...] = wu_s[...].astype(BF16)
            wd_b[...] = wd_s[...].astype(BF16)
            nxt = lax.while_loop(
                lambda j: jnp.logical_and(j < n_used, meta_ref[jnp.minimum(j, N_BLOCKS - 1)] == e),
                lambda j: j + 1, b + 1)

            @pl.when(nxt < n_used)
            def _():
                for c in weight_copies(meta_ref[nxt]):
                    c.start(priority=1)

    def compute(v):
        lo, hi = _unpack_pairs_f32(_load_row_tiles(xbufs[v], MOE_BLOCK))
        lo, hi = lo.astype(BF16), hi.astype(BF16)
        g = (jnp.dot(lo, wg_b[0:half, :], preferred_element_type=F32)
             + jnp.dot(hi, wg_b[half:, :], preferred_element_type=F32))
        u = (jnp.dot(lo, wu_b[0:half, :], preferred_element_type=F32)
             + jnp.dot(hi, wu_b[half:, :], preferred_element_type=F32))
        a = g * (1.0 / (1.0 + jnp.exp(-g))) * u
        y = jnp.dot(a.astype(BF16), wd_b[...], preferred_element_type=F32)
        _store_row_tiles(ybufs[v], _pack_bf16_pairs(y))

    def block(b, v, scatter_prev, wait_y):
        wait_gather(v)
        if wait_y:
            wait_scatter(v)
        load_expert(b)
        issue_gather(b + GROUP - 1, (v + GROUP - 1) % GROUP)
        if scatter_prev:
            issue_scatter(b - 1, (v + GROUP - 1) % GROUP)
        compute(v)

    for c in weight_copies(meta_ref[0]):
        c.start()
    zeros_buf = ybufs[GROUP - 1]
    zeros_buf[...] = jnp.zeros_like(zeros_buf)
    dump_fills = [
        pltpu.make_async_copy(
            zeros_buf, y_ref.at[pl.ds((N_ROUTED + k * MOE_BLOCK) * ROW_TILE, BLOCK_TILE_ROWS), :],
            sem_y.at[GROUP - 1])
        for k in range(DUMP_ROWS // MOE_BLOCK)]
    for c in dump_fills:
        c.start()
    _invert_permutation(d1_ref, d2_ref, vend_ref, pend_ref, inv_ref,
                        (n_groups * GROUP + GROUP - 1) * MOE_BLOCK)
    for c in dump_fills:
        c.wait()
    for v in range(GROUP - 1):
        issue_gather(v, v)
    for v in range(GROUP):
        block(v, v, scatter_prev=v > 0, wait_y=False)

    def group(gi, carry):
        for v in range(GROUP):
            block(gi * GROUP + v, v, scatter_prev=True, wait_y=True)
        return carry

    lax.fori_loop(1, n_groups, group, 0)

    last = n_groups * GROUP
    issue_scatter(last - 1, GROUP - 1)
    for v in range(GROUP - 1):
        wait_gather(v)
    for v in range(GROUP):
        wait_scatter(v)


def _moe(meta, d1, d2, vend, pend, h_tiles, w_gate, w_up, w_down):
    any_spec = pl.BlockSpec(memory_space=pl.ANY)
    xy = pltpu.VMEM((BLOCK_TILE_ROWS, LANES), I32)
    return pl.pallas_call(
        _moe_kernel,
        grid_spec=pltpu.PrefetchScalarGridSpec(
            num_scalar_prefetch=5,
            grid=(1,),
            in_specs=[any_spec, any_spec, any_spec, any_spec],
            out_specs=any_spec,
            scratch_shapes=[
                *([xy] * (2 * GROUP)),
                pltpu.VMEM((D_MODEL, D_EXPERT), F32),
                pltpu.VMEM((D_MODEL, D_EXPERT), F32),
                pltpu.VMEM((D_EXPERT, D_MODEL), F32),
                pltpu.VMEM((D_MODEL, D_EXPERT), BF16),
                pltpu.VMEM((D_MODEL, D_EXPERT), BF16),
                pltpu.VMEM((D_EXPERT, D_MODEL), BF16),
                pltpu.SMEM((INV_LEN,), I32),
                pltpu.SemaphoreType.DMA((GROUP,)),
                pltpu.SemaphoreType.DMA((GROUP,)),
                pltpu.SemaphoreType.DMA(()),
            ],
        ),
        out_shape=jax.ShapeDtypeStruct((Y_ROWS, LANES), I32),
        compiler_params=_params(("arbitrary",)),
        name="moe",
    )(meta, d1, d2, vend, pend, h_tiles, w_gate, w_up, w_down)


COMB_TM = 512


def _combine_kernel(x1_ref, rw_ref, gf_ref, y_ref, o_ref):
    tm = COMB_TM
    w = rw_ref[...]
    w0, w1 = w[:, 0:1], w[:, 1:2]
    lo0, hi0 = _unpack_pairs_f32(_load_row_tiles(y_ref, tm, 0, 2 * ROW_TILE))
    lo1, hi1 = _unpack_pairs_f32(_load_row_tiles(y_ref, tm, ROW_TILE, 2 * ROW_TILE))
    moe = jnp.concatenate([lo0 * w0 + lo1 * w1, hi0 * w0 + hi1 * w1], axis=1)
    o_ref[...] = _rms(x1_ref[...] + moe, gf_ref[...])


def _combine(x1, rw, gf, y_tiles):
    tm = COMB_TM
    row = lambda i: (i, 0)
    return pl.pallas_call(
        _combine_kernel,
        grid=(T // tm,),
        in_specs=[
            pl.BlockSpec((tm, D_MODEL), row),
            pl.BlockSpec((tm, LANES), row),
            pl.BlockSpec((1, D_MODEL), lambda i: (0, 0)),
            pl.BlockSpec((tm * 2 * ROW_TILE, LANES), row),
        ],
        out_specs=pl.BlockSpec((tm, D_MODEL), row),
        out_shape=jax.ShapeDtypeStruct((T, D_MODEL), F32),
        compiler_params=_params(("arbitrary",)),
        name="combine",
    )(x1, rw, gf, y_tiles)


def _layer(x2, norm1, w_in, rel_bias, conv_w, g_out_attn, g_out_conv, w_out, norm2,
           w_rg, b_rg, w_re, b_re, w_gate, w_up, w_down):
    conv_w8 = jnp.zeros((SUBLANES, D_CONV), F32).at[0:3].set(conv_w)
    qkv, c_n = _inproj(x2, norm1[None, :], w_in.astype(BF16), conv_w8, g_out_conv[None, :])
    a_n = _attention(qkv, _bias_table(rel_bias), g_out_attn[None, :])

    w_r = jnp.zeros((D_MODEL, LANES), F32)
    g0 = GROUP_LOGIT_ROW
    w_r = w_r.at[:, 0:N_EXPERTS].set(w_re).at[:, g0:g0 + N_GROUPS].set(w_rg)
    b_r = jnp.zeros((1, LANES), F32)
    b_r = b_r.at[0, 0:N_EXPERTS].set(b_re).at[0, g0:g0 + N_GROUPS].set(b_rg)
    x1, hp, ri_t, rw, cnt = _outproj(a_n, c_n, x2, w_out.astype(BF16), norm2[None, :],
                                     w_r.astype(BF16), b_r)

    dest_t, be, seg = _dest(ri_t, cnt)
    y_tiles = _moe(be[0], dest_t[0], dest_t[1], seg[0:N_EXPERTS, 0],
                   seg[N_EXPERTS:2 * N_EXPERTS, 0], hp, w_gate, w_up, w_down)
    return x1, rw, y_tiles


def kernel(x, norm1, w_in, rel_bias, conv_w, g_out_attn, g_out_conv, w_out, norm2,
           w_router_group, b_router_group, w_router_expert, b_router_expert,
           w_gate, w_up, w_down, norm_final):
    assert x.shape == (BATCH, SEQ, D_MODEL) and norm1.shape[0] == 1
    x2 = x.reshape(T, D_MODEL)
    x1, rw, y_tiles = _layer(
        x2, norm1[0], w_in[0], rel_bias[0], conv_w[0], g_out_attn[0], g_out_conv[0],
        w_out[0], norm2[0], w_router_group[0], b_router_group[0], w_router_expert[0],
        b_router_expert[0], w_gate[0], w_up[0], w_down[0])
    out = _combine(x1, rw, norm_final[None, :], y_tiles)
    return out.reshape(BATCH, SEQ, D_MODEL)
```

```python
import functools

import jax
import jax.numpy as jnp
from jax import lax
from jax.experimental import pallas as pl
from jax.experimental.pallas import tpu as pltpu

F32 = jnp.float32
BF16 = jnp.bfloat16
I32 = jnp.int32

D_MODEL = 2048
BATCH = 4
SEQ = 4096
T = BATCH * SEQ
CHUNK = 64
LEFT_CHUNKS = 8
BAND = LEFT_CHUNKS + 1
KEYS = BAND * CHUNK
D_ATTN = 1024
D_CONV = 1024
HEAD_DIM = 64
N_HEADS = 16
N_PAIRS = N_HEADS // 2
REL_CLIP = 256
D_IN_PROJ = 6 * 1024
N_GROUPS = 4
EPG = 8
N_EXPERTS = 32
D_EXPERT = 512
MOE_BLOCK = 256
MOE_BLOCK_SHIFT = MOE_BLOCK.bit_length() - 1
assert MOE_BLOCK == 1 << MOE_BLOCK_SHIFT
N_ROUTED = 2 * T
N_BLOCKS = N_ROUTED // MOE_BLOCK + N_EXPERTS
EPS = 1e-6
NEG_INF = -1e30
LANES = 128
VMEM_LIMIT = 52 * 1024 * 1024

ATTN_TILE = 512
ATTN_TILES_PER_SEQ = SEQ // ATTN_TILE


def _params(sem):
    return pltpu.CompilerParams(dimension_semantics=sem, vmem_limit_bytes=VMEM_LIMIT)


def _rms(x, g):
    ms = jnp.mean(x * x, axis=-1, keepdims=True)
    return x * lax.rsqrt(ms + EPS) * g


INPROJ_TM = 512
INPROJ_TN = 1536


D_QKV = 3 * D_ATTN
SUBLANES = 8


def _short_conv(b, z, z_before, w, g):
    n = z.shape[0]
    ze = jnp.concatenate([z_before, z], axis=0)
    z1 = pltpu.roll(ze, 1, axis=0)[SUBLANES:SUBLANES + n]
    z2 = pltpu.roll(ze, 2, axis=0)[SUBLANES:SUBLANES + n]
    y = w[0:1] * z2 + w[1:2] * z1 + w[2:3] * z
    return _rms(b * y, g).astype(BF16)


def _inproj_kernel(x_ref, g_ref, w_ref, cw_ref, gc_ref, qkv_ref, cn_ref, ztail_ref):
    i = pl.program_id(0)

    @pl.when(i == 0)
    def _():
        ztail_ref[...] = jnp.zeros_like(ztail_ref)

    hn = _rms(x_ref[...], g_ref[...]).astype(BF16)
    group = lambda k: jnp.dot(hn, w_ref[:, D_QKV + k * D_CONV:D_QKV + (k + 1) * D_CONV],
                              preferred_element_type=F32)
    b, z = group(0), group(1) * group(2)
    seq_start = (i % (SEQ // INPROJ_TM)) == 0
    z_before = jnp.where(seq_start, 0.0, ztail_ref[...])
    ztail_ref[...] = z[INPROJ_TM - SUBLANES:]
    cn_ref[...] = _short_conv(b, z, z_before, cw_ref[...], gc_ref[...])
    for j in range(D_QKV // INPROJ_TN):
        cols = slice(j * INPROJ_TN, (j + 1) * INPROJ_TN)
        qkv_ref[:, cols] = jnp.dot(hn, w_ref[:, cols],
                                   preferred_element_type=F32).astype(BF16)


def _inproj(x2, g, w_bf, conv_w, g_conv):
    tm = INPROJ_TM
    fixed = lambda i: (0, 0)
    row = lambda i: (i, 0)
    return pl.pallas_call(
        _inproj_kernel,
        grid=(T // tm,),
        in_specs=[
            pl.BlockSpec((tm, D_MODEL), row),
            pl.BlockSpec((1, D_MODEL), fixed),
            pl.BlockSpec((D_MODEL, D_IN_PROJ), fixed, pipeline_mode=pl.Buffered(1)),
            pl.BlockSpec((SUBLANES, D_CONV), fixed),
            pl.BlockSpec((1, D_CONV), fixed),
        ],
        out_specs=[pl.BlockSpec((tm, D_QKV), row), pl.BlockSpec((tm, D_CONV), row)],
        out_shape=[jax.ShapeDtypeStruct((T, D_QKV), BF16),
                   jax.ShapeDtypeStruct((T, D_CONV), BF16)],
        scratch_shapes=[pltpu.VMEM((SUBLANES, D_CONV), F32)],
        compiler_params=_params(("arbitrary",)),
        name="inproj",
    )(x2, g, w_bf, conv_w, g_conv)


N_QUADS = N_HEADS // 4
QUAD = 4 * HEAD_DIM
BIAS_W = 640
BIAS_WIN = 512
VAL_CHUNKS = 4
P_ROWS = KEYS + (VAL_CHUNKS - 1) * CHUNK
P_BUFS = 2 * VAL_CHUNKS


def _bias_kernel(b_ref, o_ref):
    first_head = lax.broadcasted_iota(I32, (KEYS, LANES), 1) < HEAD_DIM
    for hq in range(N_QUADS):
        for col in range(2):
            h0 = 4 * hq + 2 * col
            pieces = []
            for hh in range(2):
                x = jnp.broadcast_to(b_ref[h0 + hh:h0 + hh + 1, :], (KEYS, BIAS_W))
                rolled = pltpu.roll(x, 0, 1, stride=1, stride_axis=0)
                pieces.append(rolled[:, BIAS_WIN:BIAS_WIN + LANES])
            o_ref[hq, :, col * LANES:(col + 1) * LANES] = jnp.where(
                first_head, pieces[0], pieces[1])


def _bias_table(rel_bias):
    r = rel_bias.astype(F32)
    edge = jnp.broadcast_to(r[:, 2 * REL_CLIP:], (N_HEADS, KEYS - REL_CLIP + 1))
    b = jnp.concatenate([r[:, REL_CLIP:2 * REL_CLIP], edge,
                         r[:, REL_CLIP - (BIAS_W - KEYS - 1):REL_CLIP]], axis=1)
    odd = jnp.concatenate([b[:, BIAS_W - HEAD_DIM:], b[:, :BIAS_W - HEAD_DIM]], axis=1)
    is_odd = (jnp.arange(N_HEADS) % 2 == 1)[:, None]
    bvec = jnp.where(is_odd, odd, b)
    return pl.pallas_call(
        _bias_kernel,
        out_shape=jax.ShapeDtypeStruct((N_QUADS, KEYS, QUAD), F32),
        compiler_params=pltpu.CompilerParams(vmem_limit_bytes=VMEM_LIMIT),
        name="biastab",
    )(bvec)


def _attn_kernel(q_ref, kp_ref, kc_ref, vp_ref, vc_ref, bias_ref, g_ref, o_ref,
                 st_ref, p_ref, a_ref):
    i = pl.program_id(1)

    def window(prev_ref, cur_ref, r0, n, cols):
        parts = []
        if r0 < ATTN_TILE:
            parts.append(prev_ref[r0:min(ATTN_TILE, r0 + n), cols])
        if r0 + n > ATTN_TILE:
            parts.append(cur_ref[max(r0, ATTN_TILE) - ATTN_TILE:r0 + n - ATTN_TILE, cols])
        return parts[0] if len(parts) == 1 else jnp.concatenate(parts, axis=0)

    head_of_lane = lax.broadcasted_iota(I32, (CHUNK, QUAD), 1) // HEAD_DIM
    first_head = lax.broadcasted_iota(I32, (CHUNK, LANES), 1) < HEAD_DIM
    key_row = lax.broadcasted_iota(I32, (KEYS, QUAD), 0)
    n_rb = KEYS // CHUNK

    all_quads = tuple(range(N_QUADS))
    all_pairs = tuple(range(N_PAIRS))

    def scores(r0, buf, masked, quads=all_quads):
        for hq in quads:
            c0 = hq * QUAD
            q4 = q_ref[pl.ds(r0, CHUNK), c0:c0 + QUAD] * (HEAD_DIM ** -0.5)
            zero = jnp.zeros_like(q4)
            qbd = jnp.concatenate(
                [jnp.where(head_of_lane == h, q4, zero) for h in range(4)], axis=0)
            k4 = window(kp_ref, kc_ref, r0, KEYS, slice(c0, c0 + QUAD))
            st = lax.dot_general(k4, qbd, (((1,), (1,)), ((), ())),
                                 preferred_element_type=F32)
            st = st + bias_ref[hq]
            if masked:
                st = jnp.where(key_row + r0 >= ATTN_TILE, st, NEG_INF)
            st_ref[buf, hq] = st

    def exps(buf, pbuf, off, quads=all_quads):
        for hq in quads:
            m = st_ref[buf, hq, 0:CHUNK, :]
            for rb in range(1, n_rb):
                m = jnp.maximum(m, st_ref[buf, hq, rb * CHUNK:(rb + 1) * CHUNK, :])
            m = jnp.max(m, axis=0, keepdims=True)
            for rb in range(n_rb):
                rows = slice(rb * CHUNK, (rb + 1) * CHUNK)
                prow = slice(off + rb * CHUNK, off + (rb + 1) * CHUNK)
                p_ref[pbuf, hq, prow, :] = jnp.exp((st_ref[buf, hq, rows, :] - m).astype(BF16))

    def values(r0, pbufs, pairs=all_pairs):
        nq = len(pbufs)
        span = KEYS + (nq - 1) * CHUNK
        ones = jnp.ones((span, LANES), BF16)
        for hp in pairs:
            c0 = hp * LANES
            half = (hp % 2) * LANES
            pt = jnp.concatenate(
                [p_ref[b, hp // 2, 0:span, half:half + LANES] for b in pbufs], axis=1)
            vext = jnp.concatenate(
                [window(vp_ref, vc_ref, r0, span, slice(c0, c0 + LANES)), ones], axis=1)
            oe = lax.dot_general(pt, vext, (((0,), (0,)), ((), ())),
                                 preferred_element_type=F32)
            o = oe[:, :LANES] * (1.0 / oe[:, LANES:])
            for j in range(nq):
                oj = o[2 * j * CHUNK:2 * (j + 1) * CHUNK]
                a_ref[pl.ds(r0 + j * CHUNK, CHUNK), c0:c0 + LANES] = jnp.where(
                    first_head, oj[0:CHUNK], oj[CHUNK:2 * CHUNK])

    n_chunks = ATTN_TILE // CHUNK

    def group_bufs(c0):
        return tuple((c0 + j) % P_BUFS for j in range(VAL_CHUNKS))

    def tile(masked):
        for b in range(P_BUFS):
            off = (b % VAL_CHUNKS) * CHUNK
            for z in (slice(0, off), slice(off + KEYS, P_ROWS)):
                if z.stop > z.start:
                    p_ref[b, :, z, :] = jnp.zeros((N_QUADS, z.stop - z.start, QUAD), BF16)
        scores(0, 0, masked)
        for ci in range(n_chunks):
            c0 = (ci // VAL_CHUNKS - 1) * VAL_CHUNKS
            k = ci % VAL_CHUNKS
            for hq in all_quads:
                if ci + 1 < n_chunks:
                    scores((ci + 1) * CHUNK, (ci + 1) % 2, masked, (hq,))
                if c0 >= 0 and hq % 2 == 0:
                    values(c0 * CHUNK, group_bufs(c0), (2 * k + hq // 2,))
                exps(ci % 2, ci % P_BUFS, k * CHUNK, (hq,))
        last = n_chunks - VAL_CHUNKS
        values(last * CHUNK, group_bufs(last))

    pl.when(i == 0)(functools.partial(tile, True))
    pl.when(i > 0)(functools.partial(tile, False))

    o_ref[...] = _rms(a_ref[...], g_ref[...]).astype(BF16)


def _attention(proj, bias_t, g):
    n = ATTN_TILES_PER_SEQ
    cur = lambda col: (lambda b, i: (b * n + i, col))
    prev = lambda col: (lambda b, i: (b * n + jnp.maximum(i - 1, 0), col))
    blk = (ATTN_TILE, D_ATTN)
    return pl.pallas_call(
        _attn_kernel,
        grid=(BATCH, n),
        in_specs=[
            pl.BlockSpec(blk, cur(0)),
            pl.BlockSpec(blk, prev(1)),
            pl.BlockSpec(blk, cur(1)),
            pl.BlockSpec(blk, prev(2)),
            pl.BlockSpec(blk, cur(2)),
            pl.BlockSpec((N_QUADS, KEYS, QUAD), lambda b, i: (0, 0, 0)),
            pl.BlockSpec((1, D_ATTN), lambda b, i: (0, 0)),
        ],
        out_specs=pl.BlockSpec(blk, lambda b, i: (b * n + i, 0)),
        out_shape=jax.ShapeDtypeStruct((T, D_ATTN), BF16),
        scratch_shapes=[
            pltpu.VMEM((2, N_QUADS, KEYS, QUAD), F32),
            pltpu.VMEM((P_BUFS, N_QUADS, P_ROWS, QUAD), BF16),
            pltpu.VMEM((ATTN_TILE, D_ATTN), F32),
        ],
        compiler_params=_params(("arbitrary", "arbitrary")),
        name="attn",
    )(proj, proj, proj, proj, proj, bias_t, g)


OUT_TM = 512
OUT_SUB = 2


def _pack_bf16_pairs(h):
    half = D_MODEL // 2
    return pltpu.pack_elementwise([h[:, :half], h[:, half:]], packed_dtype=BF16)


def _unpack_pairs_f32(u):
    lo = pltpu.unpack_elementwise(u, index=0, packed_dtype=BF16, unpacked_dtype=F32)
    hi = pltpu.unpack_elementwise(u, index=1, packed_dtype=BF16, unpacked_dtype=F32)
    return lo, hi


ROW_TILE = 8


def _store_row_tiles(ref, packed):
    m = packed.shape[0]
    for s in range(ROW_TILE):
        ref[pl.ds(s, m, stride=ROW_TILE), :] = packed[:, s * LANES:(s + 1) * LANES]


def _load_row_tiles(ref, m, first=0, stride=ROW_TILE):
    return jnp.concatenate(
        [ref[pl.ds(first + s, m, stride=stride), :] for s in range(ROW_TILE)], axis=1)


GROUP_LOGIT_ROW = N_EXPERTS


def _route_cols(lg, base_ref):
    tm = lg.shape[0]
    lt = lg.T
    sub = lax.broadcasted_iota(I32, (SUBLANES, tm), 0)
    sub_f = sub.astype(F32)
    big = jnp.float32(SUBLANES)
    ninf = jnp.float32(-jnp.inf)

    gl = lt[GROUP_LOGIT_ROW:GROUP_LOGIT_ROW + SUBLANES, :]
    gvalid = sub < N_GROUPS
    gmax = jnp.max(jnp.where(gvalid, gl, ninf), axis=0, keepdims=True)
    gsum = jnp.sum(jnp.where(gvalid, jnp.exp(gl - gmax), 0.0), axis=0, keepdims=True)
    p_g = 1.0 / gsum
    g_sel = jnp.min(jnp.where(jnp.logical_and(gvalid, gl == gmax), sub_f, big),
                    axis=0, keepdims=True)
    el = lt[0:EPG, :]
    for g in range(1, N_GROUPS):
        el = jnp.where(g_sel == g, lt[g * EPG:(g + 1) * EPG, :], el)
    v1 = jnp.max(el, axis=0, keepdims=True)
    i1 = jnp.min(jnp.where(el == v1, sub_f, big), axis=0, keepdims=True)
    rest = sub_f != i1
    v2 = jnp.max(jnp.where(rest, el, ninf), axis=0, keepdims=True)
    i2 = jnp.min(jnp.where(jnp.logical_and(rest, el == v2), sub_f, big),
                 axis=0, keepdims=True)
    t = jnp.exp(v2 - v1)
    w1 = p_g * (1.0 / (1.0 + t))
    w2 = p_g * (t / (1.0 + t))
    e1 = g_sel * EPG + i1
    e2 = g_sel * EPG + i2

    erow = lax.broadcasted_iota(I32, (N_EXPERTS, tm), 0).astype(F32)
    hit1 = erow == e1
    hit2 = erow == e2
    onehot = jnp.where(jnp.logical_or(hit1, hit2), 1.0, 0.0)
    r = lax.broadcasted_iota(I32, (tm, tm), 0)
    c = lax.broadcasted_iota(I32, (tm, tm), 1)
    tri = jnp.where(r < c, 1.0, 0.0).astype(BF16)
    base = base_ref[...]
    before = (jnp.dot(onehot.astype(BF16), tri, preferred_element_type=F32)
              + jnp.concatenate([base] * (tm // LANES), axis=1))
    r1 = jnp.sum(jnp.where(hit1, before, 0.0), axis=0, keepdims=True)
    r2 = jnp.sum(jnp.where(hit2, before, 0.0), axis=0, keepdims=True)
    base_ref[...] = base + jnp.broadcast_to(
        jnp.sum(onehot, axis=1, keepdims=True), (N_EXPERTS, LANES))

    ri_t = jnp.where(sub == 0, e1, jnp.where(sub == 1, e2, jnp.where(sub == 2, r1, r2)))
    w_t = jnp.where(sub == 0, w1, jnp.where(sub == 1, w2, 0.0))
    w_t = jnp.concatenate([w_t, jnp.zeros((LANES - SUBLANES, tm), F32)], axis=0)
    return ri_t.astype(I32), w_t.T


def _outproj_kernel(a_ref, cn_ref, x_ref, wo_ref, g2_ref, wr_ref, br_ref,
                    x1_ref, hp_ref, ri_ref, rw_ref, cnt_ref, base_ref):
    @pl.when(pl.program_id(0) == 0)
    def _():
        base_ref[...] = jnp.zeros_like(base_ref)

    sub = OUT_TM // OUT_SUB
    rows_of = lambda s: slice(s * sub, (s + 1) * sub)

    def project(s):
        rows = rows_of(s)
        acc = jnp.dot(a_ref[rows, :], wo_ref[0:D_ATTN, :], preferred_element_type=F32)
        acc = acc + jnp.dot(cn_ref[rows, :], wo_ref[D_ATTN:, :], preferred_element_type=F32)
        x1_ref[rows, :] = x_ref[rows, :] + acc

    def route(s):
        rows = rows_of(s)
        h2 = _rms(x1_ref[rows, :], g2_ref[...])
        logits = jnp.dot(h2.astype(BF16), wr_ref[...],
                         preferred_element_type=F32) + br_ref[...]
        _store_row_tiles(hp_ref.at[pl.ds(s * sub * ROW_TILE, sub * ROW_TILE), :],
                         _pack_bf16_pairs(h2))
        ri_t, rw = _route_cols(logits, base_ref)
        ri_ref[:, rows] = ri_t
        rw_ref[rows, :] = rw

    for s in range(OUT_SUB):
        project(s)
        if s > 0:
            route(s - 1)
    route(OUT_SUB - 1)
    cnt_ref[...] = base_ref[...]


def _outproj(a_n, c_n, x2, wo_bf, g2, wr_bf, br):
    tm = OUT_TM
    row = lambda i: (i, 0)
    fixed = lambda i: (0, 0)
    return pl.pallas_call(
        _outproj_kernel,
        grid=(T // tm,),
        in_specs=[
            pl.BlockSpec((tm, D_ATTN), row),
            pl.BlockSpec((tm, D_CONV), row),
            pl.BlockSpec((tm, D_MODEL), row),
            pl.BlockSpec((D_MODEL, D_MODEL), fixed),
            pl.BlockSpec((1, D_MODEL), fixed),
            pl.BlockSpec((D_MODEL, LANES), fixed),
            pl.BlockSpec((1, LANES), fixed),
        ],
        out_specs=[
            pl.BlockSpec((tm, D_MODEL), row),
            pl.BlockSpec((tm * ROW_TILE, LANES), row),
            pl.BlockSpec((SUBLANES, tm), lambda i: (0, i)),
            pl.BlockSpec((tm, LANES), row),
            pl.BlockSpec((N_EXPERTS, LANES), fixed),
        ],
        out_shape=[
            jax.ShapeDtypeStruct((T, D_MODEL), F32),
            jax.ShapeDtypeStruct((T * ROW_TILE, LANES), I32),
            jax.ShapeDtypeStruct((SUBLANES, T), I32),
            jax.ShapeDtypeStruct((T, LANES), F32),
            jax.ShapeDtypeStruct((N_EXPERTS, LANES), F32),
        ],
        scratch_shapes=[pltpu.VMEM((N_EXPERTS, LANES), F32)],
        compiler_params=_params(("arbitrary",)),
        name="outproj",
    )(a_n, c_n, x2, wo_bf, g2, wr_bf, br)


BLOCK_LANES = 384
DEST_TM = 2048


def _dest_kernel(ri_ref, cnt_ref, dest_ref, be_ref, seg_ref):
    tm = DEST_TM
    erow = lax.broadcasted_iota(I32, (N_EXPERTS, LANES), 0)
    cnt = cnt_ref[...].astype(I32)
    pcnt = ((cnt + (MOE_BLOCK - 1)) >> MOE_BLOCK_SHIFT) << MOE_BLOCK_SHIFT
    pend = pcnt
    for s in (1, 2, 4, 8, 16):
        pend = pend + jnp.where(erow >= s, pltpu.roll(pend, s, axis=0), 0)
    pstart = (pend - pcnt).astype(F32)

    ri = ri_ref[...].astype(F32)
    etok = lax.broadcasted_iota(I32, (N_EXPERTS, tm), 0).astype(F32)
    pstart_t = jnp.concatenate([pstart] * (tm // LANES), axis=1)
    d1 = jnp.sum(jnp.where(etok == ri[0:1], pstart_t, 0.0), axis=0, keepdims=True) + ri[2:3]
    d2 = jnp.sum(jnp.where(etok == ri[1:2], pstart_t, 0.0), axis=0, keepdims=True) + ri[3:4]
    sub = lax.broadcasted_iota(I32, (SUBLANES, tm), 0)
    dest_ref[...] = jnp.where(sub == 0, d1, d2).astype(I32)

    pend_f = pend.astype(F32)
    pend_b = jnp.concatenate([pend_f] * (BLOCK_LANES // LANES), axis=1)
    blk_lane = lax.broadcasted_iota(I32, (1, BLOCK_LANES), 1)
    le = jnp.where(pend_b <= (blk_lane * MOE_BLOCK).astype(F32), 1.0, 0.0)
    be = jnp.minimum(jnp.sum(le, axis=0, keepdims=True), float(N_EXPERTS - 1))
    total = pend_b[N_EXPERTS - 1:N_EXPERTS, :]
    be = jnp.where(blk_lane == N_BLOCKS, total * (1.0 / MOE_BLOCK), be)
    be_ref[...] = jnp.broadcast_to(be, (SUBLANES, BLOCK_LANES)).astype(I32)
    seg_ref[0:N_EXPERTS, :] = pend - pcnt + cnt
    seg_ref[N_EXPERTS:2 * N_EXPERTS, :] = pend


def _dest(ri_t, cnt):
    tm = DEST_TM
    fixed = lambda i: (0, 0)
    tok = lambda i: (0, i)
    return pl.pallas_call(
        _dest_kernel,
        grid=(T // tm,),
        in_specs=[pl.BlockSpec((SUBLANES, tm), tok), pl.BlockSpec((N_EXPERTS, LANES), fixed)],
        out_specs=[pl.BlockSpec((SUBLANES, tm), tok),
                   pl.BlockSpec((SUBLANES, BLOCK_LANES), fixed),
                   pl.BlockSpec((2 * N_EXPERTS, LANES), fixed)],
        out_shape=[jax.ShapeDtypeStruct((SUBLANES, T), I32),
                   jax.ShapeDtypeStruct((SUBLANES, BLOCK_LANES), I32),
                   jax.ShapeDtypeStruct((2 * N_EXPERTS, LANES), I32)],
        compiler_params=_params(("arbitrary",)),
        name="dest",
    )(ri_t, cnt)


GROUP = 3
INV_LEN = (-(-N_BLOCKS // GROUP) * GROUP + GROUP - 1) * MOE_BLOCK
DUMP_ROWS = 4 * MOE_BLOCK


PAD_STEP = 8


def _invert_permutation(d1_ref, d2_ref, vend_ref, pend_ref, inv_ref, pos_end):
    def pad_range(lo, hi):
        def body(k, carry):
            p0 = lo + k * PAD_STEP
            for j in range(PAD_STEP):
                inv_ref[p0 + j] = N_ROUTED + ((p0 + j) & (DUMP_ROWS - 1))
            return carry

        lax.fori_loop(0, (hi - lo + (PAD_STEP - 1)) // PAD_STEP, body, 0)

    def pad_expert(e, carry):
        pad_range(vend_ref[e], pend_ref[e])
        return carry

    lax.fori_loop(0, N_EXPERTS, pad_expert, 0)
    pad_range(pend_ref[N_EXPERTS - 1], pos_end)

    def place(t, carry):
        inv_ref[d1_ref[t]] = 2 * t
        inv_ref[d2_ref[t]] = 2 * t + 1
        return carry

    lax.fori_loop(0, T, place, 0, unroll=16)


BLOCK_TILE_ROWS = MOE_BLOCK * ROW_TILE
Y_ROWS = (N_ROUTED + DUMP_ROWS) * ROW_TILE


def _moe_kernel(meta_ref, d1_ref, d2_ref, vend_ref, pend_ref, h_ref, wg_hbm, wu_hbm, wd_hbm, y_ref,
                *scratch):
    xbufs, ybufs = scratch[:GROUP], scratch[GROUP:2 * GROUP]
    wg_s, wu_s, wd_s, wg_b, wu_b, wd_b, inv_ref, sem_x, sem_y, sem_w = scratch[2 * GROUP:]
    n_used = meta_ref[N_BLOCKS]
    n_groups = (n_used + (GROUP - 1)) // GROUP
    half = D_MODEL // 2

    def issue_gather(b, v):
        for r in range(MOE_BLOCK):
            tok = (inv_ref[b * MOE_BLOCK + r] >> 1) & (T - 1)
            src = h_ref.at[pl.ds(pl.multiple_of(tok * ROW_TILE, ROW_TILE), ROW_TILE), :]
            pltpu.make_async_copy(src, xbufs[v].at[pl.ds(r * ROW_TILE, ROW_TILE), :],
                                  sem_x.at[v]).start()

    def wait_gather(v):
        pltpu.make_async_copy(h_ref.at[pl.ds(0, BLOCK_TILE_ROWS), :], xbufs[v],
                              sem_x.at[v]).wait()

    def issue_scatter(b, v):
        for r in range(MOE_BLOCK):
            slot = inv_ref[b * MOE_BLOCK + r]
            dst = y_ref.at[pl.ds(pl.multiple_of(slot * ROW_TILE, ROW_TILE), ROW_TILE), :]
            pltpu.make_async_copy(ybufs[v].at[pl.ds(r * ROW_TILE, ROW_TILE), :], dst,
                                  sem_y.at[v]).start()

    def wait_scatter(v):
        pltpu.make_async_copy(ybufs[v], y_ref.at[pl.ds(0, BLOCK_TILE_ROWS), :],
                              sem_y.at[v]).wait()

    def weight_copies(e):
        return (pltpu.make_async_copy(wg_hbm.at[e], wg_s, sem_w),
                pltpu.make_async_copy(wu_hbm.at[e], wu_s, sem_w),
                pltpu.make_async_copy(wd_hbm.at[e], wd_s, sem_w))

    def load_expert(b):
        e = meta_ref[b]
        prev = meta_ref[jnp.maximum(b - 1, 0)]
        first = jnp.logical_and(b < n_used, jnp.logical_or(b == 0, e != prev))

        @pl.when(first)
        def _():
            for c in weight_copies(e):
                c.wait()
            wg_b[...] = wg_s[...].astype(BF16)
            wu_b[...] = wu_s[...].astype(BF16)
            wd_b[...] = wd_s[...].astype(BF16)
            nxt = lax.while_loop(
                lambda j: jnp.logical_and(j < n_used, meta_ref[jnp.minimum(j, N_BLOCKS - 1)] == e),
                lambda j: j + 1, b + 1)

            @pl.when(nxt < n_used)
            def _():
                for c in weight_copies(meta_ref[nxt]):
                    c.start(priority=1)

    def compute(v):
        lo, hi = _unpack_pairs_f32(_load_row_tiles(xbufs[v], MOE_BLOCK))
        lo, hi = lo.astype(BF16), hi.astype(BF16)
        g = (jnp.dot(lo, wg_b[0:half, :], preferred_element_type=F32)
             + jnp.dot(hi, wg_b[half:, :], preferred_element_type=F32))
        u = (jnp.dot(lo, wu_b[0:half, :], preferred_element_type=F32)
             + jnp.dot(hi, wu_b[half:, :], preferred_element_type=F32))
        a = g * (1.0 / (1.0 + jnp.exp(-g))) * u
        y = jnp.dot(a.astype(BF16), wd_b[...], preferred_element_type=F32)
        _store_row_tiles(ybufs[v], _pack_bf16_pairs(y))

    def block(b, v, scatter_prev, wait_y):
        wait_gather(v)
        if wait_y:
            wait_scatter(v)
        load_expert(b)
        issue_gather(b + GROUP - 1, (v + GROUP - 1) % GROUP)
        if scatter_prev:
            issue_scatter(b - 1, (v + GROUP - 1) % GROUP)
        compute(v)

    for c in weight_copies(meta_ref[0]):
        c.start()
    zeros_buf = ybufs[GROUP - 1]
    zeros_buf[...] = jnp.zeros_like(zeros_buf)
    dump_fills = [
        pltpu.make_async_copy(
            zeros_buf, y_ref.at[pl.ds((N_ROUTED + k * MOE_BLOCK) * ROW_TILE, BLOCK_TILE_ROWS), :],
            sem_y.at[GROUP - 1])
        for k in range(DUMP_ROWS // MOE_BLOCK)]
    for c in dump_fills:
        c.start()
    _invert_permutation(d1_ref, d2_ref, vend_ref, pend_ref, inv_ref,
                        (n_groups * GROUP + GROUP - 1) * MOE_BLOCK)
    for c in dump_fills:
        c.wait()
    for v in range(GROUP - 1):
        issue_gather(v, v)
    for v in range(GROUP):
        block(v, v, scatter_prev=v > 0, wait_y=False)

    def group(gi, carry):
        for v in range(GROUP):
            block(gi * GROUP + v, v, scatter_prev=True, wait_y=True)
        return carry

    lax.fori_loop(1, n_groups, group, 0)

    last = n_groups * GROUP
    issue_scatter(last - 1, GROUP - 1)
    for v in range(GROUP - 1):
        wait_gather(v)
    for v in range(GROUP):
        wait_scatter(v)


def _moe(meta, d1, d2, vend, pend, h_tiles, w_gate, w_up, w_down):
    any_spec = pl.BlockSpec(memory_space=pl.ANY)
    xy = pltpu.VMEM((BLOCK_TILE_ROWS, LANES), I32)
    return pl.pallas_call(
        _moe_kernel,
        grid_spec=pltpu.PrefetchScalarGridSpec(
            num_scalar_prefetch=5,
            grid=(1,),
            in_specs=[any_spec, any_spec, any_spec, any_spec],
            out_specs=any_spec,
            scratch_shapes=[
                *([xy] * (2 * GROUP)),
                pltpu.VMEM((D_MODEL, D_EXPERT), F32),
                pltpu.VMEM((D_MODEL, D_EXPERT), F32),
                pltpu.VMEM((D_EXPERT, D_MODEL), F32),
                pltpu.VMEM((D_MODEL, D_EXPERT), BF16),
                pltpu.VMEM((D_MODEL, D_EXPERT), BF16),
                pltpu.VMEM((D_EXPERT, D_MODEL), BF16),
                pltpu.SMEM((INV_LEN,), I32),
                pltpu.SemaphoreType.DMA((GROUP,)),
                pltpu.SemaphoreType.DMA((GROUP,)),
                pltpu.SemaphoreType.DMA(()),
            ],
        ),
        out_shape=jax.ShapeDtypeStruct((Y_ROWS, LANES), I32),
        compiler_params=_params(("arbitrary",)),
        name="moe",
    )(meta, d1, d2, vend, pend, h_tiles, w_gate, w_up, w_down)


COMB_TM = 512


def _combine_kernel(x1_ref, rw_ref, gf_ref, y_ref, o_ref):
    tm = COMB_TM
    w = rw_ref[...]
    w0, w1 = w[:, 0:1], w[:, 1:2]
    lo0, hi0 = _unpack_pairs_f32(_load_row_tiles(y_ref, tm, 0, 2 * ROW_TILE))
    lo1, hi1 = _unpack_pairs_f32(_load_row_tiles(y_ref, tm, ROW_TILE, 2 * ROW_TILE))
    moe = jnp.concatenate([lo0 * w0 + lo1 * w1, hi0 * w0 + hi1 * w1], axis=1)
    o_ref[...] = _rms(x1_ref[...] + moe, gf_ref[...])


def _combine(x1, rw, gf, y_tiles):
    tm = COMB_TM
    row = lambda i: (i, 0)
    return pl.pallas_call(
        _combine_kernel,
        grid=(T // tm,),
        in_specs=[
            pl.BlockSpec((tm, D_MODEL), row),
            pl.BlockSpec((tm, LANES), row),
            pl.BlockSpec((1, D_MODEL), lambda i: (0, 0)),
            pl.BlockSpec((tm * 2 * ROW_TILE, LANES), row),
        ],
        out_specs=pl.BlockSpec((tm, D_MODEL), row),
        out_shape=jax.ShapeDtypeStruct((T, D_MODEL), F32),
        compiler_params=_params(("arbitrary",)),
        name="combine",
    )(x1, rw, gf, y_tiles)


def _layer(x2, norm1, w_in, rel_bias, conv_w, g_out_attn, g_out_conv, w_out, norm2,
           w_rg, b_rg, w_re, b_re, w_gate, w_up, w_down):
    conv_w8 = jnp.zeros((SUBLANES, D_CONV), F32).at[0:3].set(conv_w)
    qkv, c_n = _inproj(x2, norm1[None, :], w_in.astype(BF16), conv_w8, g_out_conv[None, :])
    a_n = _attention(qkv, _bias_table(rel_bias), g_out_attn[None, :])

    w_r = jnp.zeros((D_MODEL, LANES), F32)
    g0 = GROUP_LOGIT_ROW
    w_r = w_r.at[:, 0:N_EXPERTS].set(w_re).at[:, g0:g0 + N_GROUPS].set(w_rg)
    b_r = jnp.zeros((1, LANES), F32)
    b_r = b_r.at[0, 0:N_EXPERTS].set(b_re).at[0, g0:g0 + N_GROUPS].set(b_rg)
    x1, hp, ri_t, rw, cnt = _outproj(a_n, c_n, x2, w_out.astype(BF16), norm2[None, :],
                                     w_r.astype(BF16), b_r)

    dest_t, be, seg = _dest(ri_t, cnt)
    y_tiles = _moe(be[0], dest_t[0], dest_t[1], seg[0:N_EXPERTS, 0],
                   seg[N_EXPERTS:2 * N_EXPERTS, 0], hp, w_gate, w_up, w_down)
    return x1, rw, y_tiles


def kernel(x, norm1, w_in, rel_bias, conv_w, g_out_attn, g_out_conv, w_out, norm2,
           w_router_group, b_router_group, w_router_expert, b_router_expert,
           w_gate, w_up, w_down, norm_final):
    assert x.shape == (BATCH, SEQ, D_MODEL) and norm1.shape[0] == 1
    x2 = x.reshape(T, D_MODEL)
    x1, rw, y_tiles = _layer(
        x2, norm1[0], w_in[0], rel_bias[0], conv_w[0], g_out_attn[0], g_out_conv[0],
        w_out[0], norm2[0], w_router_group[0], b_router_group[0], w_router_expert[0],
        b_router_expert[0], w_gate[0], w_up[0], w_down[0])
    out = _combine(x1, rw, norm_final[None, :], y_tiles)
    return out.reshape(BATCH, SEQ, D_MODEL)
```

```python
import functools

import jax
import jax.numpy as jnp
from jax import lax
from jax.experimental import pallas as pl
from jax.experimental.pallas import tpu as pltpu

F32 = jnp.float32
BF16 = jnp.bfloat16
I32 = jnp.int32

D_MODEL = 2048
BATCH = 4
SEQ = 4096
T = BATCH * SEQ
CHUNK = 64
LEFT_CHUNKS = 8
BAND = LEFT_CHUNKS + 1
KEYS = BAND * CHUNK
D_ATTN = 1024
D_CONV = 1024
HEAD_DIM = 64
N_HEADS = 16
N_PAIRS = N_HEADS // 2
REL_CLIP = 256
D_IN_PROJ = 6 * 1024
N_GROUPS = 4
EPG = 8
N_EXPERTS = 32
D_EXPERT = 512
MOE_BLOCK = 256
MOE_BLOCK_SHIFT = MOE_BLOCK.bit_length() - 1
assert MOE_BLOCK == 1 << MOE_BLOCK_SHIFT
N_ROUTED = 2 * T
N_BLOCKS = N_ROUTED // MOE_BLOCK + N_EXPERTS
EPS = 1e-6
NEG_INF = -1e30
LANES = 128
VMEM_LIMIT = 52 * 1024 * 1024

ATTN_TILE = 512
ATTN_TILES_PER_SEQ = SEQ // ATTN_TILE


def _params(sem):
    return pltpu.CompilerParams(dimension_semantics=sem, vmem_limit_bytes=VMEM_LIMIT)


def _rms(x, g):
    ms = jnp.mean(x * x, axis=-1, keepdims=True)
    return x * lax.rsqrt(ms + EPS) * g


INPROJ_TM = 512
INPROJ_TN = 1536


D_QKV = 3 * D_ATTN
SUBLANES = 8


def _short_conv(b, z, z_before, w, g):
    n = z.shape[0]
    ze = jnp.concatenate([z_before, z], axis=0)
    z1 = pltpu.roll(ze, 1, axis=0)[SUBLANES:SUBLANES + n]
    z2 = pltpu.roll(ze, 2, axis=0)[SUBLANES:SUBLANES + n]
    y = w[0:1] * z2 + w[1:2] * z1 + w[2:3] * z
    return _rms(b * y, g).astype(BF16)


def _inproj_kernel(x_ref, g_ref, w_ref, cw_ref, gc_ref, qkv_ref, cn_ref, ztail_ref):
    i = pl.program_id(0)

    @pl.when(i == 0)
    def _():
        ztail_ref[...] = jnp.zeros_like(ztail_ref)

    hn = _rms(x_ref[...], g_ref[...]).astype(BF16)
    group = lambda k: jnp.dot(hn, w_ref[:, D_QKV + k * D_CONV:D_QKV + (k + 1) * D_CONV],
                              preferred_element_type=F32)
    b, z = group(0), group(1) * group(2)
    seq_start = (i % (SEQ // INPROJ_TM)) == 0
    z_before = jnp.where(seq_start, 0.0, ztail_ref[...])
    ztail_ref[...] = z[INPROJ_TM - SUBLANES:]
    cn_ref[...] = _short_conv(b, z, z_before, cw_ref[...], gc_ref[...])
    for j in range(D_QKV // INPROJ_TN):
        cols = slice(j * INPROJ_TN, (j + 1) * INPROJ_TN)
        qkv_ref[:, cols] = jnp.dot(hn, w_ref[:, cols],
                                   preferred_element_type=F32).astype(BF16)


def _inproj(x2, g, w_bf, conv_w, g_conv):
    tm = INPROJ_TM
    fixed = lambda i: (0, 0)
    row = lambda i: (i, 0)
    return pl.pallas_call(
        _inproj_kernel,
        grid=(T // tm,),
        in_specs=[
            pl.BlockSpec((tm, D_MODEL), row),
            pl.BlockSpec((1, D_MODEL), fixed),
            pl.BlockSpec((D_MODEL, D_IN_PROJ), fixed, pipeline_mode=pl.Buffered(1)),
            pl.BlockSpec((SUBLANES, D_CONV), fixed),
            pl.BlockSpec((1, D_CONV), fixed),
        ],
        out_specs=[pl.BlockSpec((tm, D_QKV), row), pl.BlockSpec((tm, D_CONV), row)],
        out_shape=[jax.ShapeDtypeStruct((T, D_QKV), BF16),
                   jax.ShapeDtypeStruct((T, D_CONV), BF16)],
        scratch_shapes=[pltpu.VMEM((SUBLANES, D_CONV), F32)],
        compiler_params=_params(("arbitrary",)),
        name="inproj",
    )(x2, g, w_bf, conv_w, g_conv)


N_QUADS = N_HEADS // 4
QUAD = 4 * HEAD_DIM
BIAS_W = 640
BIAS_WIN = 512
VAL_CHUNKS = 4
P_ROWS = KEYS + (VAL_CHUNKS - 1) * CHUNK
P_BUFS = 2 * VAL_CHUNKS


def _bias_kernel(b_ref, o_ref):
    first_head = lax.broadcasted_iota(I32, (KEYS, LANES), 1) < HEAD_DIM
    for hq in range(N_QUADS):
        for col in range(2):
            h0 = 4 * hq + 2 * col
            pieces = []
            for hh in range(2):
                x = jnp.broadcast_to(b_ref[h0 + hh:h0 + hh + 1, :], (KEYS, BIAS_W))
                rolled = pltpu.roll(x, 0, 1, stride=1, stride_axis=0)
                pieces.append(rolled[:, BIAS_WIN:BIAS_WIN + LANES])
            o_ref[hq, :, col * LANES:(col + 1) * LANES] = jnp.where(
                first_head, pieces[0], pieces[1])


def _bias_table(rel_bias):
    r = rel_bias.astype(F32)
    edge = jnp.broadcast_to(r[:, 2 * REL_CLIP:], (N_HEADS, KEYS - REL_CLIP + 1))
    b = jnp.concatenate([r[:, REL_CLIP:2 * REL_CLIP], edge,
                         r[:, REL_CLIP - (BIAS_W - KEYS - 1):REL_CLIP]], axis=1)
    odd = jnp.concatenate([b[:, BIAS_W - HEAD_DIM:], b[:, :BIAS_W - HEAD_DIM]], axis=1)
    is_odd = (jnp.arange(N_HEADS) % 2 == 1)[:, None]
    bvec = jnp.where(is_odd, odd, b)
    return pl.pallas_call(
        _bias_kernel,
        out_shape=jax.ShapeDtypeStruct((N_QUADS, KEYS, QUAD), F32),
        compiler_params=pltpu.CompilerParams(vmem_limit_bytes=VMEM_LIMIT),
        name="biastab",
    )(bvec)


def _attn_kernel(q_ref, kp_ref, kc_ref, vp_ref, vc_ref, bias_ref, g_ref, o_ref,
                 st_ref, p_ref, a_ref):
    i = pl.program_id(1)

    def window(prev_ref, cur_ref, r0, n, cols):
        parts = []
        if r0 < ATTN_TILE:
            parts.append(prev_ref[r0:min(ATTN_TILE, r0 + n), cols])
        if r0 + n > ATTN_TILE:
            parts.append(cur_ref[max(r0, ATTN_TILE) - ATTN_TILE:r0 + n - ATTN_TILE, cols])
        return parts[0] if len(parts) == 1 else jnp.concatenate(parts, axis=0)

    head_of_lane = lax.broadcasted_iota(I32, (CHUNK, QUAD), 1) // HEAD_DIM
    first_head = lax.broadcasted_iota(I32, (CHUNK, LANES), 1) < HEAD_DIM
    key_row = lax.broadcasted_iota(I32, (KEYS, QUAD), 0)
    n_rb = KEYS // CHUNK

    all_quads = tuple(range(N_QUADS))
    all_pairs = tuple(range(N_PAIRS))

    def scores(r0, buf, masked, quads=all_quads):
        for hq in quads:
            c0 = hq * QUAD
            q4 = q_ref[pl.ds(r0, CHUNK), c0:c0 + QUAD] * (HEAD_DIM ** -0.5)
            zero = jnp.zeros_like(q4)
            qbd = jnp.concatenate(
                [jnp.where(head_of_lane == h, q4, zero) for h in range(4)], axis=0)
            k4 = window(kp_ref, kc_ref, r0, KEYS, slice(c0, c0 + QUAD))
            st = lax.dot_general(k4, qbd, (((1,), (1,)), ((), ())),
                                 preferred_element_type=F32)
            st = st + bias_ref[hq]
            if masked:
                st = jnp.where(key_row + r0 >= ATTN_TILE, st, NEG_INF)
            st_ref[buf, hq] = st

    def exps(buf, pbuf, off, quads=all_quads):
        for hq in quads:
            m = st_ref[buf, hq, 0:CHUNK, :]
            for rb in range(1, n_rb):
                m = jnp.maximum(m, st_ref[buf, hq, rb * CHUNK:(rb + 1) * CHUNK, :])
            m = jnp.max(m, axis=0, keepdims=True)
            for rb in range(n_rb):
                rows = slice(rb * CHUNK, (rb + 1) * CHUNK)
                prow = slice(off + rb * CHUNK, off + (rb + 1) * CHUNK)
                p_ref[pbuf, hq, prow, :] = jnp.exp(st_ref[buf, hq, rows, :] - m).astype(BF16)

    def values(r0, pbufs, pairs=all_pairs):
        nq = len(pbufs)
        span = KEYS + (nq - 1) * CHUNK
        ones = jnp.ones((span, LANES), BF16)
        for hp in pairs:
            c0 = hp * LANES
            half = (hp % 2) * LANES
            pt = jnp.concatenate(
                [p_ref[b, hp // 2, 0:span, half:half + LANES] for b in pbufs], axis=1)
            vext = jnp.concatenate(
                [window(vp_ref, vc_ref, r0, span, slice(c0, c0 + LANES)), ones], axis=1)
            oe = lax.dot_general(pt, vext, (((0,), (0,)), ((), ())),
                                 preferred_element_type=F32)
            o = oe[:, :LANES] * (1.0 / oe[:, LANES:])
            for j in range(nq):
                oj = o[2 * j * CHUNK:2 * (j + 1) * CHUNK]
                a_ref[pl.ds(r0 + j * CHUNK, CHUNK), c0:c0 + LANES] = jnp.where(
                    first_head, oj[0:CHUNK], oj[CHUNK:2 * CHUNK])

    n_chunks = ATTN_TILE // CHUNK

    def group_bufs(c0):
        return tuple((c0 + j) % P_BUFS for j in range(VAL_CHUNKS))

    def tile(masked):
        for b in range(P_BUFS):
            off = (b % VAL_CHUNKS) * CHUNK
            for z in (slice(0, off), slice(off + KEYS, P_ROWS)):
                if z.stop > z.start:
                    p_ref[b, :, z, :] = jnp.zeros((N_QUADS, z.stop - z.start, QUAD), BF16)
        scores(0, 0, masked)
        for ci in range(n_chunks):
            c0 = (ci // VAL_CHUNKS - 1) * VAL_CHUNKS
            k = ci % VAL_CHUNKS
            for hq in all_quads:
                if ci + 1 < n_chunks:
                    scores((ci + 1) * CHUNK, (ci + 1) % 2, masked, (hq,))
                if c0 >= 0 and hq % 2 == 0:
                    values(c0 * CHUNK, group_bufs(c0), (2 * k + hq // 2,))
                exps(ci % 2, ci % P_BUFS, k * CHUNK, (hq,))
        last = n_chunks - VAL_CHUNKS
        values(last * CHUNK, group_bufs(last))

    pl.when(i == 0)(functools.partial(tile, True))
    pl.when(i > 0)(functools.partial(tile, False))

    o_ref[...] = _rms(a_ref[...], g_ref[...]).astype(BF16)


def _attention(proj, bias_t, g):
    n = ATTN_TILES_PER_SEQ
    cur = lambda col: (lambda b, i: (b * n + i, col))
    prev = lambda col: (lambda b, i: (b * n + jnp.maximum(i - 1, 0), col))
    blk = (ATTN_TILE, D_ATTN)
    return pl.pallas_call(
        _attn_kernel,
        grid=(BATCH, n),
        in_specs=[
            pl.BlockSpec(blk, cur(0)),
            pl.BlockSpec(blk, prev(1)),
            pl.BlockSpec(blk, cur(1)),
            pl.BlockSpec(blk, prev(2)),
            pl.BlockSpec(blk, cur(2)),
            pl.BlockSpec((N_QUADS, KEYS, QUAD), lambda b, i: (0, 0, 0)),
            pl.BlockSpec((1, D_ATTN), lambda b, i: (0, 0)),
        ],
        out_specs=pl.BlockSpec(blk, lambda b, i: (b * n + i, 0)),
        out_shape=jax.ShapeDtypeStruct((T, D_ATTN), BF16),
        scratch_shapes=[
            pltpu.VMEM((2, N_QUADS, KEYS, QUAD), F32),
            pltpu.VMEM((P_BUFS, N_QUADS, P_ROWS, QUAD), BF16),
            pltpu.VMEM((ATTN_TILE, D_ATTN), F32),
        ],
        compiler_params=_params(("arbitrary", "arbitrary")),
        name="attn",
    )(proj, proj, proj, proj, proj, bias_t, g)


OUT_TM = 512
OUT_SUB = 2


def _pack_bf16_pairs(h):
    half = D_MODEL // 2
    return pltpu.pack_elementwise([h[:, :half], h[:, half:]], packed_dtype=BF16)


def _unpack_pairs_f32(u):
    lo = pltpu.unpack_elementwise(u, index=0, packed_dtype=BF16, unpacked_dtype=F32)
    hi = pltpu.unpack_elementwise(u, index=1, packed_dtype=BF16, unpacked_dtype=F32)
    return lo, hi


ROW_TILE = 8


def _store_row_tiles(ref, packed):
    m = packed.shape[0]
    for s in range(ROW_TILE):
        ref[pl.ds(s, m, stride=ROW_TILE), :] = packed[:, s * LANES:(s + 1) * LANES]


def _load_row_tiles(ref, m, first=0, stride=ROW_TILE):
    return jnp.concatenate(
        [ref[pl.ds(first + s, m, stride=stride), :] for s in range(ROW_TILE)], axis=1)


GROUP_LOGIT_ROW = N_EXPERTS


def _route_cols(lg, base_ref):
    tm = lg.shape[0]
    lt = lg.T
    sub = lax.broadcasted_iota(I32, (SUBLANES, tm), 0)
    sub_f = sub.astype(F32)
    big = jnp.float32(SUBLANES)
    ninf = jnp.float32(-jnp.inf)

    gl = lt[GROUP_LOGIT_ROW:GROUP_LOGIT_ROW + SUBLANES, :]
    gvalid = sub < N_GROUPS
    gmax = jnp.max(jnp.where(gvalid, gl, ninf), axis=0, keepdims=True)
    gsum = jnp.sum(jnp.where(gvalid, jnp.exp(gl - gmax), 0.0), axis=0, keepdims=True)
    p_g = 1.0 / gsum
    g_sel = jnp.min(jnp.where(jnp.logical_and(gvalid, gl == gmax), sub_f, big),
                    axis=0, keepdims=True)
    el = lt[0:EPG, :]
    for g in range(1, N_GROUPS):
        el = jnp.where(g_sel == g, lt[g * EPG:(g + 1) * EPG, :], el)
    v1 = jnp.max(el, axis=0, keepdims=True)
    i1 = jnp.min(jnp.where(el == v1, sub_f, big), axis=0, keepdims=True)
    rest = sub_f != i1
    v2 = jnp.max(jnp.where(rest, el, ninf), axis=0, keepdims=True)
    i2 = jnp.min(jnp.where(jnp.logical_and(rest, el == v2), sub_f, big),
                 axis=0, keepdims=True)
    t = jnp.exp(v2 - v1)
    w1 = p_g * (1.0 / (1.0 + t))
    w2 = p_g * (t / (1.0 + t))
    e1 = g_sel * EPG + i1
    e2 = g_sel * EPG + i2

    erow = lax.broadcasted_iota(I32, (N_EXPERTS, tm), 0).astype(F32)
    hit1 = erow == e1
    hit2 = erow == e2
    onehot = jnp.where(jnp.logical_or(hit1, hit2), 1.0, 0.0)
    r = lax.broadcasted_iota(I32, (tm, tm), 0)
    c = lax.broadcasted_iota(I32, (tm, tm), 1)
    tri = jnp.where(r < c, 1.0, 0.0).astype(BF16)
    base = base_ref[...]
    before = (jnp.dot(onehot.astype(BF16), tri, preferred_element_type=F32)
              + jnp.concatenate([base] * (tm // LANES), axis=1))
    r1 = jnp.sum(jnp.where(hit1, before, 0.0), axis=0, keepdims=True)
    r2 = jnp.sum(jnp.where(hit2, before, 0.0), axis=0, keepdims=True)
    base_ref[...] = base + jnp.broadcast_to(
        jnp.sum(onehot, axis=1, keepdims=True), (N_EXPERTS, LANES))

    ri_t = jnp.where(sub == 0, e1, jnp.where(sub == 1, e2, jnp.where(sub == 2, r1, r2)))
    w_t = jnp.where(sub == 0, w1, jnp.where(sub == 1, w2, 0.0))
    w_t = jnp.concatenate([w_t, jnp.zeros((LANES - SUBLANES, tm), F32)], axis=0)
    return ri_t.astype(I32), w_t.T


def _outproj_kernel(a_ref, cn_ref, x_ref, wo_ref, g2_ref, wr_ref, br_ref,
                    x1_ref, hp_ref, ri_ref, rw_ref, cnt_ref, base_ref):
    @pl.when(pl.program_id(0) == 0)
    def _():
        base_ref[...] = jnp.zeros_like(base_ref)

    sub = OUT_TM // OUT_SUB
    rows_of = lambda s: slice(s * sub, (s + 1) * sub)

    def project(s):
        rows = rows_of(s)
        acc = jnp.dot(a_ref[rows, :], wo_ref[0:D_ATTN, :], preferred_element_type=F32)
        acc = acc + jnp.dot(cn_ref[rows, :], wo_ref[D_ATTN:, :], preferred_element_type=F32)
        x1_ref[rows, :] = x_ref[rows, :] + acc

    def route(s):
        rows = rows_of(s)
        h2 = _rms(x1_ref[rows, :], g2_ref[...])
        logits = jnp.dot(h2.astype(BF16), wr_ref[...],
                         preferred_element_type=F32) + br_ref[...]
        _store_row_tiles(hp_ref.at[pl.ds(s * sub * ROW_TILE, sub * ROW_TILE), :],
                         _pack_bf16_pairs(h2))
        ri_t, rw = _route_cols(logits, base_ref)
        ri_ref[:, rows] = ri_t
        rw_ref[rows, :] = rw

    for s in range(OUT_SUB):
        project(s)
        if s > 0:
            route(s - 1)
    route(OUT_SUB - 1)
    cnt_ref[...] = base_ref[...]


def _outproj(a_n, c_n, x2, wo_bf, g2, wr_bf, br):
    tm = OUT_TM
    row = lambda i: (i, 0)
    fixed = lambda i: (0, 0)
    return pl.pallas_call(
        _outproj_kernel,
        grid=(T // tm,),
        in_specs=[
            pl.BlockSpec((tm, D_ATTN), row),
            pl.BlockSpec((tm, D_CONV), row),
            pl.BlockSpec((tm, D_MODEL), row),
            pl.BlockSpec((D_MODEL, D_MODEL), fixed),
            pl.BlockSpec((1, D_MODEL), fixed),
            pl.BlockSpec((D_MODEL, LANES), fixed),
            pl.BlockSpec((1, LANES), fixed),
        ],
        out_specs=[
            pl.BlockSpec((tm, D_MODEL), row),
            pl.BlockSpec((tm * ROW_TILE, LANES), row),
            pl.BlockSpec((SUBLANES, tm), lambda i: (0, i)),
            pl.BlockSpec((tm, LANES), row),
            pl.BlockSpec((N_EXPERTS, LANES), fixed),
        ],
        out_shape=[
            jax.ShapeDtypeStruct((T, D_MODEL), F32),
            jax.ShapeDtypeStruct((T * ROW_TILE, LANES), I32),
            jax.ShapeDtypeStruct((SUBLANES, T), I32),
            jax.ShapeDtypeStruct((T, LANES), F32),
            jax.ShapeDtypeStruct((N_EXPERTS, LANES), F32),
        ],
        scratch_shapes=[pltpu.VMEM((N_EXPERTS, LANES), F32)],
        compiler_params=_params(("arbitrary",)),
        name="outproj",
    )(a_n, c_n, x2, wo_bf, g2, wr_bf, br)


BLOCK_LANES = 384
DEST_TM = 2048


def _dest_kernel(ri_ref, cnt_ref, dest_ref, be_ref, seg_ref):
    tm = DEST_TM
    erow = lax.broadcasted_iota(I32, (N_EXPERTS, LANES), 0)
    cnt = cnt_ref[...].astype(I32)
    pcnt = ((cnt + (MOE_BLOCK - 1)) >> MOE_BLOCK_SHIFT) << MOE_BLOCK_SHIFT
    pend = pcnt
    for s in (1, 2, 4, 8, 16):
        pend = pend + jnp.where(erow >= s, pltpu.roll(pend, s, axis=0), 0)
    pstart = (pend - pcnt).astype(F32)

    ri = ri_ref[...].astype(F32)
    etok = lax.broadcasted_iota(I32, (N_EXPERTS, tm), 0).astype(F32)
    pstart_t = jnp.concatenate([pstart] * (tm // LANES), axis=1)
    d1 = jnp.sum(jnp.where(etok == ri[0:1], pstart_t, 0.0), axis=0, keepdims=True) + ri[2:3]
    d2 = jnp.sum(jnp.where(etok == ri[1:2], pstart_t, 0.0), axis=0, keepdims=True) + ri[3:4]
    sub = lax.broadcasted_iota(I32, (SUBLANES, tm), 0)
    dest_ref[...] = jnp.where(sub == 0, d1, d2).astype(I32)

    pend_f = pend.astype(F32)
    pend_b = jnp.concatenate([pend_f] * (BLOCK_LANES // LANES), axis=1)
    blk_lane = lax.broadcasted_iota(I32, (1, BLOCK_LANES), 1)
    le = jnp.where(pend_b <= (blk_lane * MOE_BLOCK).astype(F32), 1.0, 0.0)
    be = jnp.minimum(jnp.sum(le, axis=0, keepdims=True), float(N_EXPERTS - 1))
    total = pend_b[N_EXPERTS - 1:N_EXPERTS, :]
    be = jnp.where(blk_lane == N_BLOCKS, total * (1.0 / MOE_BLOCK), be)
    be_ref[...] = jnp.broadcast_to(be, (SUBLANES, BLOCK_LANES)).astype(I32)
    seg_ref[0:N_EXPERTS, :] = pend - pcnt + cnt
    seg_ref[N_EXPERTS:2 * N_EXPERTS, :] = pend


def _dest(ri_t, cnt):
    tm = DEST_TM
    fixed = lambda i: (0, 0)
    tok = lambda i: (0, i)
    return pl.pallas_call(
        _dest_kernel,
        grid=(T // tm,),
        in_specs=[pl.BlockSpec((SUBLANES, tm), tok), pl.BlockSpec((N_EXPERTS, LANES), fixed)],
        out_specs=[pl.BlockSpec((SUBLANES, tm), tok),
                   pl.BlockSpec((SUBLANES, BLOCK_LANES), fixed),
                   pl.BlockSpec((2 * N_EXPERTS, LANES), fixed)],
        out_shape=[jax.ShapeDtypeStruct((SUBLANES, T), I32),
                   jax.ShapeDtypeStruct((SUBLANES, BLOCK_LANES), I32),
                   jax.ShapeDtypeStruct((2 * N_EXPERTS, LANES), I32)],
        compiler_params=_params(("arbitrary",)),
        name="dest",
    )(ri_t, cnt)


GROUP = 2
INV_LEN = (-(-N_BLOCKS // GROUP) * GROUP + GROUP - 1) * MOE_BLOCK
DUMP_ROWS = 4 * MOE_BLOCK


PAD_STEP = 8


def _invert_permutation(d1_ref, d2_ref, vend_ref, pend_ref, inv_ref, pos_end):
    def pad_range(lo, hi):
        def body(k, carry):
            p0 = lo + k * PAD_STEP
            for j in range(PAD_STEP):
                inv_ref[p0 + j] = N_ROUTED + ((p0 + j) & (DUMP_ROWS - 1))
            return carry

        lax.fori_loop(0, (hi - lo + (PAD_STEP - 1)) // PAD_STEP, body, 0)

    def pad_expert(e, carry):
        pad_range(vend_ref[e], pend_ref[e])
        return carry

    lax.fori_loop(0, N_EXPERTS, pad_expert, 0)
    pad_range(pend_ref[N_EXPERTS - 1], pos_end)

    def place(t, carry):
        inv_ref[d1_ref[t]] = 2 * t
        inv_ref[d2_ref[t]] = 2 * t + 1
        return carry

    lax.fori_loop(0, T, place, 0, unroll=16)


BLOCK_TILE_ROWS = MOE_BLOCK * ROW_TILE
Y_ROWS = (N_ROUTED + DUMP_ROWS) * ROW_TILE


def _moe_kernel(meta_ref, d1_ref, d2_ref, vend_ref, pend_ref, h_ref, wg_hbm, wu_hbm, wd_hbm, y_ref,
                *scratch):
    xbufs, ybufs = scratch[:GROUP], scratch[GROUP:2 * GROUP]
    wg_s, wu_s, wd_s, wg_b, wu_b, wd_b, inv_ref, sem_x, sem_y, sem_w = scratch[2 * GROUP:]
    n_used = meta_ref[N_BLOCKS]
    n_groups = (n_used + (GROUP - 1)) // GROUP
    half = D_MODEL // 2

    def issue_gather(b, v):
        for r in range(MOE_BLOCK):
            tok = (inv_ref[b * MOE_BLOCK + r] >> 1) & (T - 1)
            src = h_ref.at[pl.ds(pl.multiple_of(tok * ROW_TILE, ROW_TILE), ROW_TILE), :]
            pltpu.make_async_copy(src, xbufs[v].at[pl.ds(r * ROW_TILE, ROW_TILE), :],
                                  sem_x.at[v]).start()

    def wait_gather(v):
        pltpu.make_async_copy(h_ref.at[pl.ds(0, BLOCK_TILE_ROWS), :], xbufs[v],
                              sem_x.at[v]).wait()

    def issue_scatter(b, v):
        for r in range(MOE_BLOCK):
            slot = inv_ref[b * MOE_BLOCK + r]
            dst = y_ref.at[pl.ds(pl.multiple_of(slot * ROW_TILE, ROW_TILE), ROW_TILE), :]
            pltpu.make_async_copy(ybufs[v].at[pl.ds(r * ROW_TILE, ROW_TILE), :], dst,
                                  sem_y.at[v]).start()

    def wait_scatter(v):
        pltpu.make_async_copy(ybufs[v], y_ref.at[pl.ds(0, BLOCK_TILE_ROWS), :],
                              sem_y.at[v]).wait()

    def weight_copies(e):
        return (pltpu.make_async_copy(wg_hbm.at[e], wg_s, sem_w),
                pltpu.make_async_copy(wu_hbm.at[e], wu_s, sem_w),
                pltpu.make_async_copy(wd_hbm.at[e], wd_s, sem_w))

    def load_expert(b):
        e = meta_ref[b]
        prev = meta_ref[jnp.maximum(b - 1, 0)]
        first = jnp.logical_and(b < n_used, jnp.logical_or(b == 0, e != prev))

        @pl.when(first)
        def _():
            for c in weight_copies(e):
                c.wait()
            wg_b[...] = wg_s[...].astype(BF16)
            wu_b[...] = wu_s[...].astype(BF16)
            wd_b[...] = wd_s[...].astype(BF16)
            nxt = lax.while_loop(
                lambda j: jnp.logical_and(j < n_used, meta_ref[jnp.minimum(j, N_BLOCKS - 1)] == e),
                lambda j: j + 1, b + 1)

            @pl.when(nxt < n_used)
            def _():
                for c in weight_copies(meta_ref[nxt]):
                    c.start(priority=1)

    def compute(v):
        lo, hi = _unpack_pairs_f32(_load_row_tiles(xbufs[v], MOE_BLOCK))
        lo, hi = lo.astype(BF16), hi.astype(BF16)
        g = (jnp.dot(lo, wg_b[0:half, :], preferred_element_type=F32)
             + jnp.dot(hi, wg_b[half:, :], preferred_element_type=F32))
        u = (jnp.dot(lo, wu_b[0:half, :], preferred_element_type=F32)
             + jnp.dot(hi, wu_b[half:, :], preferred_element_type=F32))
        a = g * (1.0 / (1.0 + jnp.exp(-g))) * u
        y = jnp.dot(a.astype(BF16), wd_b[...], preferred_element_type=F32)
        _store_row_tiles(ybufs[v], _pack_bf16_pairs(y))

    def block(b, v, scatter_prev, wait_y):
        wait_gather(v)
        if wait_y:
            wait_scatter(v)
        load_expert(b)
        issue_gather(b + GROUP - 1, (v + GROUP - 1) % GROUP)
        if scatter_prev:
            issue_scatter(b - 1, (v + GROUP - 1) % GROUP)
        compute(v)

    for c in weight_copies(meta_ref[0]):
        c.start()
    zeros_buf = ybufs[GROUP - 1]
    zeros_buf[...] = jnp.zeros_like(zeros_buf)
    dump_fills = [
        pltpu.make_async_copy(
            zeros_buf, y_ref.at[pl.ds((N_ROUTED + k * MOE_BLOCK) * ROW_TILE, BLOCK_TILE_ROWS), :],
            sem_y.at[GROUP - 1])
        for k in range(DUMP_ROWS // MOE_BLOCK)]
    for c in dump_fills:
        c.start()
    _invert_permutation(d1_ref, d2_ref, vend_ref, pend_ref, inv_ref,
                        (n_groups * GROUP + GROUP - 1) * MOE_BLOCK)
    for c in dump_fills:
        c.wait()
    for v in range(GROUP - 1):
        issue_gather(v, v)
    for v in range(GROUP):
        block(v, v, scatter_prev=v > 0, wait_y=False)

    def group(gi, carry):
        for v in range(GROUP):
            block(gi * GROUP + v, v, scatter_prev=True, wait_y=True)
        return carry

    lax.fori_loop(1, n_groups, group, 0)

    last = n_groups * GROUP
    issue_scatter(last - 1, GROUP - 1)
    for v in range(GROUP - 1):
        wait_gather(v)
    for v in range(GROUP):
        wait_scatter(v)


def _moe(meta, d1, d2, vend, pend, h_tiles, w_gate, w_up, w_down):
    any_spec = pl.BlockSpec(memory_space=pl.ANY)
    xy = pltpu.VMEM((BLOCK_TILE_ROWS, LANES), I32)
    return pl.pallas_call(
        _moe_kernel,
        grid_spec=pltpu.PrefetchScalarGridSpec(
            num_scalar_prefetch=5,
            grid=(1,),
            in_specs=[any_spec, any_spec, any_spec, any_spec],
            out_specs=any_spec,
            scratch_shapes=[
                *([xy] * (2 * GROUP)),
                pltpu.VMEM((D_MODEL, D_EXPERT), F32),
                pltpu.VMEM((D_MODEL, D_EXPERT), F32),
                pltpu.VMEM((D_EXPERT, D_MODEL), F32),
                pltpu.VMEM((D_MODEL, D_EXPERT), BF16),
                pltpu.VMEM((D_MODEL, D_EXPERT), BF16),
                pltpu.VMEM((D_EXPERT, D_MODEL), BF16),
                pltpu.SMEM((INV_LEN,), I32),
                pltpu.SemaphoreType.DMA((GROUP,)),
                pltpu.SemaphoreType.DMA((GROUP,)),
                pltpu.SemaphoreType.DMA(()),
            ],
        ),
        out_shape=jax.ShapeDtypeStruct((Y_ROWS, LANES), I32),
        compiler_params=_params(("arbitrary",)),
        name="moe",
    )(meta, d1, d2, vend, pend, h_tiles, w_gate, w_up, w_down)


COMB_TM = 512


def _combine_kernel(x1_ref, rw_ref, gf_ref, y_ref, o_ref):
    tm = COMB_TM
    w = rw_ref[...]
    w0, w1 = w[:, 0:1], w[:, 1:2]
    lo0, hi0 = _unpack_pairs_f32(_load_row_tiles(y_ref, tm, 0, 2 * ROW_TILE))
    lo1, hi1 = _unpack_pairs_f32(_load_row_tiles(y_ref, tm, ROW_TILE, 2 * ROW_TILE))
    moe = jnp.concatenate([lo0 * w0 + lo1 * w1, hi0 * w0 + hi1 * w1], axis=1)
    o_ref[...] = _rms(x1_ref[...] + moe, gf_ref[...])


def _combine(x1, rw, gf, y_tiles):
    tm = COMB_TM
    row = lambda i: (i, 0)
    return pl.pallas_call(
        _combine_kernel,
        grid=(T // tm,),
        in_specs=[
            pl.BlockSpec((tm, D_MODEL), row),
            pl.BlockSpec((tm, LANES), row),
            pl.BlockSpec((1, D_MODEL), lambda i: (0, 0)),
            pl.BlockSpec((tm * 2 * ROW_TILE, LANES), row),
        ],
        out_specs=pl.BlockSpec((tm, D_MODEL), row),
        out_shape=jax.ShapeDtypeStruct((T, D_MODEL), F32),
        compiler_params=_params(("arbitrary",)),
        name="combine",
    )(x1, rw, gf, y_tiles)


def _layer(x2, norm1, w_in, rel_bias, conv_w, g_out_attn, g_out_conv, w_out, norm2,
           w_rg, b_rg, w_re, b_re, w_gate, w_up, w_down):
    conv_w8 = jnp.zeros((SUBLANES, D_CONV), F32).at[0:3].set(conv_w)
    qkv, c_n = _inproj(x2, norm1[None, :], w_in.astype(BF16), conv_w8, g_out_conv[None, :])
    a_n = _attention(qkv, _bias_table(rel_bias), g_out_attn[None, :])

    w_r = jnp.zeros((D_MODEL, LANES), F32)
    g0 = GROUP_LOGIT_ROW
    w_r = w_r.at[:, 0:N_EXPERTS].set(w_re).at[:, g0:g0 + N_GROUPS].set(w_rg)
    b_r = jnp.zeros((1, LANES), F32)
    b_r = b_r.at[0, 0:N_EXPERTS].set(b_re).at[0, g0:g0 + N_GROUPS].set(b_rg)
    x1, hp, ri_t, rw, cnt = _outproj(a_n, c_n, x2, w_out.astype(BF16), norm2[None, :],
                                     w_r.astype(BF16), b_r)

    dest_t, be, seg = _dest(ri_t, cnt)
    y_tiles = _moe(be[0], dest_t[0], dest_t[1], seg[0:N_EXPERTS, 0],
                   seg[N_EXPERTS:2 * N_EXPERTS, 0], hp, w_gate, w_up, w_down)
    return x1, rw, y_tiles


def kernel(x, norm1, w_in, rel_bias, conv_w, g_out_attn, g_out_conv, w_out, norm2,
           w_router_group, b_router_group, w_router_expert, b_router_expert,
           w_gate, w_up, w_down, norm_final):
    assert x.shape == (BATCH, SEQ, D_MODEL) and norm1.shape[0] == 1
    x2 = x.reshape(T, D_MODEL)
    x1, rw, y_tiles = _layer(
        x2, norm1[0], w_in[0], rel_bias[0], conv_w[0], g_out_attn[0], g_out_conv[0],
        w_out[0], norm2[0], w_router_group[0], b_router_group[0], w_router_expert[0],
        b_router_expert[0], w_gate[0], w_up[0], w_down[0])
    out = _combine(x1, rw, norm_final[None, :], y_tiles)
    return out.reshape(BATCH, SEQ, D_MODEL)
```

```python
import functools

import jax
import jax.numpy as jnp
from jax import lax
from jax.experimental import pallas as pl
from jax.experimental.pallas import tpu as pltpu

F32 = jnp.float32
BF16 = jnp.bfloat16
I32 = jnp.int32

D_MODEL = 2048
BATCH = 4
SEQ = 4096
T = BATCH * SEQ
CHUNK = 64
LEFT_CHUNKS = 8
BAND = LEFT_CHUNKS + 1
KEYS = BAND * CHUNK
D_ATTN = 1024
D_CONV = 1024
HEAD_DIM = 64
N_HEADS = 16
N_PAIRS = N_HEADS // 2
REL_CLIP = 256
D_IN_PROJ = 6 * 1024
N_GROUPS = 4
EPG = 8
N_EXPERTS = 32
D_EXPERT = 512
MOE_BLOCK = 256
MOE_BLOCK_SHIFT = MOE_BLOCK.bit_length() - 1
assert MOE_BLOCK == 1 << MOE_BLOCK_SHIFT
N_ROUTED = 2 * T
N_BLOCKS = N_ROUTED // MOE_BLOCK + N_EXPERTS
EPS = 1e-6
NEG_INF = -1e30
LANES = 128
VMEM_LIMIT = 52 * 1024 * 1024

ATTN_TILE = 512
ATTN_TILES_PER_SEQ = SEQ // ATTN_TILE


def _params(sem):
    return pltpu.CompilerParams(dimension_semantics=sem, vmem_limit_bytes=VMEM_LIMIT)


def _rms(x, g):
    ms = jnp.mean(x * x, axis=-1, keepdims=True)
    return x * lax.rsqrt(ms + EPS) * g


INPROJ_TM = 512
INPROJ_TN = 1536


D_QKV = 3 * D_ATTN
SUBLANES = 8


def _short_conv(b, z, z_before, w, g):
    n = z.shape[0]
    ze = jnp.concatenate([z_before, z], axis=0)
    z1 = pltpu.roll(ze, 1, axis=0)[SUBLANES:SUBLANES + n]
    z2 = pltpu.roll(ze, 2, axis=0)[SUBLANES:SUBLANES + n]
    y = w[0:1] * z2 + w[1:2] * z1 + w[2:3] * z
    return _rms(b * y, g).astype(BF16)


def _inproj_kernel(x_ref, g_ref, w_ref, cw_ref, gc_ref, qkv_ref, cn_ref, ztail_ref):
    i = pl.program_id(0)

    @pl.when(i == 0)
    def _():
        ztail_ref[...] = jnp.zeros_like(ztail_ref)

    hn = _rms(x_ref[...], g_ref[...]).astype(BF16)
    group = lambda k: jnp.dot(hn, w_ref[:, D_QKV + k * D_CONV:D_QKV + (k + 1) * D_CONV],
                              preferred_element_type=F32)
    b, z = group(0), group(1) * group(2)
    seq_start = (i % (SEQ // INPROJ_TM)) == 0
    z_before = jnp.where(seq_start, 0.0, ztail_ref[...])
    ztail_ref[...] = z[INPROJ_TM - SUBLANES:]
    cn_ref[...] = _short_conv(b, z, z_before, cw_ref[...], gc_ref[...])
    for j in range(D_QKV // INPROJ_TN):
        cols = slice(j * INPROJ_TN, (j + 1) * INPROJ_TN)
        qkv_ref[:, cols] = jnp.dot(hn, w_ref[:, cols],
                                   preferred_element_type=F32).astype(BF16)


def _inproj(x2, g, w_bf, conv_w, g_conv):
    tm = INPROJ_TM
    fixed = lambda i: (0, 0)
    row = lambda i: (i, 0)
    return pl.pallas_call(
        _inproj_kernel,
        grid=(T // tm,),
        in_specs=[
            pl.BlockSpec((tm, D_MODEL), row),
            pl.BlockSpec((1, D_MODEL), fixed),
            pl.BlockSpec((D_MODEL, D_IN_PROJ), fixed, pipeline_mode=pl.Buffered(1)),
            pl.BlockSpec((SUBLANES, D_CONV), fixed),
            pl.BlockSpec((1, D_CONV), fixed),
        ],
        out_specs=[pl.BlockSpec((tm, D_QKV), row), pl.BlockSpec((tm, D_CONV), row)],
        out_shape=[jax.ShapeDtypeStruct((T, D_QKV), BF16),
                   jax.ShapeDtypeStruct((T, D_CONV), BF16)],
        scratch_shapes=[pltpu.VMEM((SUBLANES, D_CONV), F32)],
        compiler_params=_params(("arbitrary",)),
        name="inproj",
    )(x2, g, w_bf, conv_w, g_conv)


N_QUADS = N_HEADS // 4
QUAD = 4 * HEAD_DIM
BIAS_W = 640
BIAS_WIN = 512
VAL_CHUNKS = 4
P_ROWS = KEYS + (VAL_CHUNKS - 1) * CHUNK
P_BUFS = 2 * VAL_CHUNKS


def _bias_kernel(b_ref, o_ref):
    first_head = lax.broadcasted_iota(I32, (KEYS, LANES), 1) < HEAD_DIM
    for hq in range(N_QUADS):
        for col in range(2):
            h0 = 4 * hq + 2 * col
            pieces = []
            for hh in range(2):
                x = jnp.broadcast_to(b_ref[h0 + hh:h0 + hh + 1, :], (KEYS, BIAS_W))
                rolled = pltpu.roll(x, 0, 1, stride=1, stride_axis=0)
                pieces.append(rolled[:, BIAS_WIN:BIAS_WIN + LANES])
            o_ref[hq, :, col * LANES:(col + 1) * LANES] = jnp.where(
                first_head, pieces[0], pieces[1])


def _bias_table(rel_bias):
    r = rel_bias.astype(F32)
    edge = jnp.broadcast_to(r[:, 2 * REL_CLIP:], (N_HEADS, KEYS - REL_CLIP + 1))
    b = jnp.concatenate([r[:, REL_CLIP:2 * REL_CLIP], edge,
                         r[:, REL_CLIP - (BIAS_W - KEYS - 1):REL_CLIP]], axis=1)
    odd = jnp.concatenate([b[:, BIAS_W - HEAD_DIM:], b[:, :BIAS_W - HEAD_DIM]], axis=1)
    is_odd = (jnp.arange(N_HEADS) % 2 == 1)[:, None]
    bvec = jnp.where(is_odd, odd, b)
    return pl.pallas_call(
        _bias_kernel,
        out_shape=jax.ShapeDtypeStruct((N_QUADS, KEYS, QUAD), F32),
        compiler_params=pltpu.CompilerParams(vmem_limit_bytes=VMEM_LIMIT),
        name="biastab",
    )(bvec)


def _attn_kernel(q_ref, kp_ref, kc_ref, vp_ref, vc_ref, bias_ref, g_ref, o_ref,
                 st_ref, p_ref, a_ref):
    i = pl.program_id(1)

    def window(prev_ref, cur_ref, r0, n, cols):
        parts = []
        if r0 < ATTN_TILE:
            parts.append(prev_ref[r0:min(ATTN_TILE, r0 + n), cols])
        if r0 + n > ATTN_TILE:
            parts.append(cur_ref[max(r0, ATTN_TILE) - ATTN_TILE:r0 + n - ATTN_TILE, cols])
        return parts[0] if len(parts) == 1 else jnp.concatenate(parts, axis=0)

    head_of_lane = lax.broadcasted_iota(I32, (CHUNK, QUAD), 1) // HEAD_DIM
    first_head = lax.broadcasted_iota(I32, (CHUNK, LANES), 1) < HEAD_DIM
    key_row = lax.broadcasted_iota(I32, (KEYS, QUAD), 0)
    n_rb = KEYS // CHUNK

    all_quads = tuple(range(N_QUADS))
    all_pairs = tuple(range(N_PAIRS))

    def scores(r0, buf, masked, quads=all_quads):
        for hq in quads:
            c0 = hq * QUAD
            q4 = q_ref[pl.ds(r0, CHUNK), c0:c0 + QUAD] * (HEAD_DIM ** -0.5)
            zero = jnp.zeros_like(q4)
            qbd = jnp.concatenate(
                [jnp.where(head_of_lane == h, q4, zero) for h in range(4)], axis=0)
            k4 = window(kp_ref, kc_ref, r0, KEYS, slice(c0, c0 + QUAD))
            st = lax.dot_general(k4, qbd, (((1,), (1,)), ((), ())),
                                 preferred_element_type=F32)
            st = st + bias_ref[hq]
            if masked:
                st = jnp.where(key_row + r0 >= ATTN_TILE, st, NEG_INF)
            st_ref[buf, hq] = st

    def exps(buf, pbuf, off, quads=all_quads):
        for hq in quads:
            m = st_ref[buf, hq, 0:CHUNK, :]
            for rb in range(1, n_rb):
                m = jnp.maximum(m, st_ref[buf, hq, rb * CHUNK:(rb + 1) * CHUNK, :])
            m = jnp.max(m, axis=0, keepdims=True)
            for rb in range(n_rb):
                rows = slice(rb * CHUNK, (rb + 1) * CHUNK)
                prow = slice(off + rb * CHUNK, off + (rb + 1) * CHUNK)
                p_ref[pbuf, hq, prow, :] = jnp.exp(st_ref[buf, hq, rows, :] - m).astype(BF16)

    def values(r0, pbufs, pairs=all_pairs):
        nq = len(pbufs)
        span = KEYS + (nq - 1) * CHUNK
        ones = jnp.ones((span, LANES), BF16)
        for hp in pairs:
            c0 = hp * LANES
            half = (hp % 2) * LANES
            pt = jnp.concatenate(
                [p_ref[b, hp // 2, 0:span, half:half + LANES] for b in pbufs], axis=1)
            vext = jnp.concatenate(
                [window(vp_ref, vc_ref, r0, span, slice(c0, c0 + LANES)), ones], axis=1)
            oe = lax.dot_general(pt, vext, (((0,), (0,)), ((), ())),
                                 preferred_element_type=F32)
            o = oe[:, :LANES] * (1.0 / oe[:, LANES:])
            for j in range(nq):
                oj = o[2 * j * CHUNK:2 * (j + 1) * CHUNK]
                a_ref[pl.ds(r0 + j * CHUNK, CHUNK), c0:c0 + LANES] = jnp.where(
                    first_head, oj[0:CHUNK], oj[CHUNK:2 * CHUNK])

    n_chunks = ATTN_TILE // CHUNK

    def group_bufs(c0):
        return tuple((c0 + j) % P_BUFS for j in range(VAL_CHUNKS))

    def tile(masked):
        for b in range(P_BUFS):
            off = (b % VAL_CHUNKS) * CHUNK
            for z in (slice(0, off), slice(off + KEYS, P_ROWS)):
                if z.stop > z.start:
                    p_ref[b, :, z, :] = jnp.zeros((N_QUADS, z.stop - z.start, QUAD), BF16)
        scores(0, 0, masked)
        for ci in range(n_chunks):
            c0 = (ci // VAL_CHUNKS - 1) * VAL_CHUNKS
            k = ci % VAL_CHUNKS
            for hq in all_quads:
                if ci + 1 < n_chunks:
                    scores((ci + 1) * CHUNK, (ci + 1) % 2, masked, (hq,))
                if c0 >= 0 and hq % 2 == 0:
                    values(c0 * CHUNK, group_bufs(c0), (2 * k + hq // 2,))
                exps(ci % 2, ci % P_BUFS, k * CHUNK, (hq,))
        last = n_chunks - VAL_CHUNKS
        values(last * CHUNK, group_bufs(last))

    pl.when(i == 0)(functools.partial(tile, True))
    pl.when(i > 0)(functools.partial(tile, False))

    o_ref[...] = _rms(a_ref[...], g_ref[...]).astype(BF16)


def _attention(proj, bias_t, g):
    n = ATTN_TILES_PER_SEQ
    cur = lambda col: (lambda b, i: (b * n + i, col))
    prev = lambda col: (lambda b, i: (b * n + jnp.maximum(i - 1, 0), col))
    blk = (ATTN_TILE, D_ATTN)
    return pl.pallas_call(
        _attn_kernel,
        grid=(BATCH, n),
        in_specs=[
            pl.BlockSpec(blk, cur(0)),
            pl.BlockSpec(blk, prev(1)),
            pl.BlockSpec(blk, cur(1)),
            pl.BlockSpec(blk, prev(2)),
            pl.BlockSpec(blk, cur(2)),
            pl.BlockSpec((N_QUADS, KEYS, QUAD), lambda b, i: (0, 0, 0)),
            pl.BlockSpec((1, D_ATTN), lambda b, i: (0, 0)),
        ],
        out_specs=pl.BlockSpec(blk, lambda b, i: (b * n + i, 0)),
        out_shape=jax.ShapeDtypeStruct((T, D_ATTN), BF16),
        scratch_shapes=[
            pltpu.VMEM((2, N_QUADS, KEYS, QUAD), F32),
            pltpu.VMEM((P_BUFS, N_QUADS, P_ROWS, QUAD), BF16),
            pltpu.VMEM((ATTN_TILE, D_ATTN), F32),
        ],
        compiler_params=_params(("arbitrary", "arbitrary")),
        name="attn",
    )(proj, proj, proj, proj, proj, bias_t, g)


OUT_TM = 512
OUT_SUB = 2


def _pack_bf16_pairs(h):
    half = D_MODEL // 2
    return pltpu.pack_elementwise([h[:, :half], h[:, half:]], packed_dtype=BF16)


def _unpack_pairs_f32(u):
    lo = pltpu.unpack_elementwise(u, index=0, packed_dtype=BF16, unpacked_dtype=F32)
    hi = pltpu.unpack_elementwise(u, index=1, packed_dtype=BF16, unpacked_dtype=F32)
    return lo, hi


ROW_TILE = 8


def _store_row_tiles(ref, packed):
    m = packed.shape[0]
    for s in range(ROW_TILE):
        ref[pl.ds(s, m, stride=ROW_TILE), :] = packed[:, s * LANES:(s + 1) * LANES]


def _load_row_tiles(ref, m, first=0, stride=ROW_TILE):
    return jnp.concatenate(
        [ref[pl.ds(first + s, m, stride=stride), :] for s in range(ROW_TILE)], axis=1)


GROUP_LOGIT_ROW = N_EXPERTS


def _route_cols(lg, base_ref):
    tm = lg.shape[0]
    lt = lg.T
    sub = lax.broadcasted_iota(I32, (SUBLANES, tm), 0)
    sub_f = sub.astype(F32)
    big = jnp.float32(SUBLANES)
    ninf = jnp.float32(-jnp.inf)

    gl = lt[GROUP_LOGIT_ROW:GROUP_LOGIT_ROW + SUBLANES, :]
    gvalid = sub < N_GROUPS
    gmax = jnp.max(jnp.where(gvalid, gl, ninf), axis=0, keepdims=True)
    gsum = jnp.sum(jnp.where(gvalid, jnp.exp(gl - gmax), 0.0), axis=0, keepdims=True)
    p_g = 1.0 / gsum
    g_sel = jnp.min(jnp.where(jnp.logical_and(gvalid, gl == gmax), sub_f, big),
                    axis=0, keepdims=True)
    el = lt[0:EPG, :]
    for g in range(1, N_GROUPS):
        el = jnp.where(g_sel == g, lt[g * EPG:(g + 1) * EPG, :], el)
    v1 = jnp.max(el, axis=0, keepdims=True)
    i1 = jnp.min(jnp.where(el == v1, sub_f, big), axis=0, keepdims=True)
    rest = sub_f != i1
    v2 = jnp.max(jnp.where(rest, el, ninf), axis=0, keepdims=True)
    i2 = jnp.min(jnp.where(jnp.logical_and(rest, el == v2), sub_f, big),
                 axis=0, keepdims=True)
    t = jnp.exp(v2 - v1)
    w1 = p_g * (1.0 / (1.0 + t))
    w2 = p_g * (t / (1.0 + t))
    e1 = g_sel * EPG + i1
    e2 = g_sel * EPG + i2

    erow = lax.broadcasted_iota(I32, (N_EXPERTS, tm), 0).astype(F32)
    hit1 = erow == e1
    hit2 = erow == e2
    onehot = jnp.where(jnp.logical_or(hit1, hit2), 1.0, 0.0)
    r = lax.broadcasted_iota(I32, (tm, tm), 0)
    c = lax.broadcasted_iota(I32, (tm, tm), 1)
    tri = jnp.where(r < c, 1.0, 0.0).astype(BF16)
    base = base_ref[...]
    before = (jnp.dot(onehot.astype(BF16), tri, preferred_element_type=F32)
              + jnp.concatenate([base] * (tm // LANES), axis=1))
    r1 = jnp.sum(jnp.where(hit1, before, 0.0), axis=0, keepdims=True)
    r2 = jnp.sum(jnp.where(hit2, before, 0.0), axis=0, keepdims=True)
    base_ref[...] = base + jnp.broadcast_to(
        jnp.sum(onehot, axis=1, keepdims=True), (N_EXPERTS, LANES))

    ri_t = jnp.where(sub == 0, e1, jnp.where(sub == 1, e2, jnp.where(sub == 2, r1, r2)))
    w_t = jnp.where(sub == 0, w1, jnp.where(sub == 1, w2, 0.0))
    w_t = jnp.concatenate([w_t, jnp.zeros((LANES - SUBLANES, tm), F32)], axis=0)
    return ri_t.astype(I32), w_t.T


def _outproj_kernel(a_ref, cn_ref, x_ref, wo_ref, g2_ref, wr_ref, br_ref,
                    x1_ref, hp_ref, ri_ref, rw_ref, cnt_ref, base_ref):
    @pl.when(pl.program_id(0) == 0)
    def _():
        base_ref[...] = jnp.zeros_like(base_ref)

    sub = OUT_TM // OUT_SUB
    rows_of = lambda s: slice(s * sub, (s + 1) * sub)

    def project(s):
        rows = rows_of(s)
        acc = jnp.dot(a_ref[rows, :], wo_ref[0:D_ATTN, :], preferred_element_type=F32)
        acc = acc + jnp.dot(cn_ref[rows, :], wo_ref[D_ATTN:, :], preferred_element_type=F32)
        x1_ref[rows, :] = x_ref[rows, :] + acc

    def route(s):
        rows = rows_of(s)
        h2 = _rms(x1_ref[rows, :], g2_ref[...])
        logits = jnp.dot(h2.astype(BF16), wr_ref[...],
                         preferred_element_type=F32) + br_ref[...]
        _store_row_tiles(hp_ref.at[pl.ds(s * sub * ROW_TILE, sub * ROW_TILE), :],
                         _pack_bf16_pairs(h2))
        ri_t, rw = _route_cols(logits, base_ref)
        ri_ref[:, rows] = ri_t
        rw_ref[rows, :] = rw

    for s in range(OUT_SUB):
        project(s)
        if s > 0:
            route(s - 1)
    route(OUT_SUB - 1)
    cnt_ref[...] = base_ref[...]


def _outproj(a_n, c_n, x2, wo_bf, g2, wr_bf, br):
    tm = OUT_TM
    row = lambda i: (i, 0)
    fixed = lambda i: (0, 0)
    return pl.pallas_call(
        _outproj_kernel,
        grid=(T // tm,),
        in_specs=[
            pl.BlockSpec((tm, D_ATTN), row),
            pl.BlockSpec((tm, D_CONV), row),
            pl.BlockSpec((tm, D_MODEL), row),
            pl.BlockSpec((D_MODEL, D_MODEL), fixed),
            pl.BlockSpec((1, D_MODEL), fixed),
            pl.BlockSpec((D_MODEL, LANES), fixed),
            pl.BlockSpec((1, LANES), fixed),
        ],
        out_specs=[
            pl.BlockSpec((tm, D_MODEL), row),
            pl.BlockSpec((tm * ROW_TILE, LANES), row),
            pl.BlockSpec((SUBLANES, tm), lambda i: (0, i)),
            pl.BlockSpec((tm, LANES), row),
            pl.BlockSpec((N_EXPERTS, LANES), fixed),
        ],
        out_shape=[
            jax.ShapeDtypeStruct((T, D_MODEL), F32),
            jax.ShapeDtypeStruct((T * ROW_TILE, LANES), I32),
            jax.ShapeDtypeStruct((SUBLANES, T), I32),
            jax.ShapeDtypeStruct((T, LANES), F32),
            jax.ShapeDtypeStruct((N_EXPERTS, LANES), F32),
        ],
        scratch_shapes=[pltpu.VMEM((N_EXPERTS, LANES), F32)],
        compiler_params=_params(("arbitrary",)),
        name="outproj",
    )(a_n, c_n, x2, wo_bf, g2, wr_bf, br)


BLOCK_LANES = 384
DEST_TM = 2048


def _dest_kernel(ri_ref, cnt_ref, dest_ref, be_ref, seg_ref):
    tm = DEST_TM
    erow = lax.broadcasted_iota(I32, (N_EXPERTS, LANES), 0)
    cnt = cnt_ref[...].astype(I32)
    pcnt = ((cnt + (MOE_BLOCK - 1)) >> MOE_BLOCK_SHIFT) << MOE_BLOCK_SHIFT
    pend = pcnt
    for s in (1, 2, 4, 8, 16):
        pend = pend + jnp.where(erow >= s, pltpu.roll(pend, s, axis=0), 0)
    pstart = (pend - pcnt).astype(F32)

    ri = ri_ref[...].astype(F32)
    etok = lax.broadcasted_iota(I32, (N_EXPERTS, tm), 0).astype(F32)
    pstart_t = jnp.concatenate([pstart] * (tm // LANES), axis=1)
    d1 = jnp.sum(jnp.where(etok == ri[0:1], pstart_t, 0.0), axis=0, keepdims=True) + ri[2:3]
    d2 = jnp.sum(jnp.where(etok == ri[1:2], pstart_t, 0.0), axis=0, keepdims=True) + ri[3:4]
    sub = lax.broadcasted_iota(I32, (SUBLANES, tm), 0)
    dest_ref[...] = jnp.where(sub == 0, d1, d2).astype(I32)

    pend_f = pend.astype(F32)
    pend_b = jnp.concatenate([pend_f] * (BLOCK_LANES // LANES), axis=1)
    blk_lane = lax.broadcasted_iota(I32, (1, BLOCK_LANES), 1)
    le = jnp.where(pend_b <= (blk_lane * MOE_BLOCK).astype(F32), 1.0, 0.0)
    be = jnp.minimum(jnp.sum(le, axis=0, keepdims=True), float(N_EXPERTS - 1))
    total = pend_b[N_EXPERTS - 1:N_EXPERTS, :]
    be = jnp.where(blk_lane == N_BLOCKS, total * (1.0 / MOE_BLOCK), be)
    be_ref[...] = jnp.broadcast_to(be, (SUBLANES, BLOCK_LANES)).astype(I32)
    seg_ref[0:N_EXPERTS, :] = pend - pcnt + cnt
    seg_ref[N_EXPERTS:2 * N_EXPERTS, :] = pend


def _dest(ri_t, cnt):
    tm = DEST_TM
    fixed = lambda i: (0, 0)
    tok = lambda i: (0, i)
    return pl.pallas_call(
        _dest_kernel,
        grid=(T // tm,),
        in_specs=[pl.BlockSpec((SUBLANES, tm), tok), pl.BlockSpec((N_EXPERTS, LANES), fixed)],
        out_specs=[pl.BlockSpec((SUBLANES, tm), tok),
                   pl.BlockSpec((SUBLANES, BLOCK_LANES), fixed),
                   pl.BlockSpec((2 * N_EXPERTS, LANES), fixed)],
        out_shape=[jax.ShapeDtypeStruct((SUBLANES, T), I32),
                   jax.ShapeDtypeStruct((SUBLANES, BLOCK_LANES), I32),
                   jax.ShapeDtypeStruct((2 * N_EXPERTS, LANES), I32)],
        compiler_params=_params(("arbitrary",)),
        name="dest",
    )(ri_t, cnt)


GROUP = 3
INV_LEN = (-(-N_BLOCKS // GROUP) * GROUP + GROUP - 1) * MOE_BLOCK
DUMP_ROWS = 4 * MOE_BLOCK


PAD_STEP = 8


def _invert_permutation(d1_ref, d2_ref, vend_ref, pend_ref, inv_ref, pos_end):
    def pad_range(lo, hi):
        def body(k, carry):
            p0 = lo + k * PAD_STEP
            for j in range(PAD_STEP):
                inv_ref[p0 + j] = N_ROUTED + ((p0 + j) & (DUMP_ROWS - 1))
            return carry

        lax.fori_loop(0, (hi - lo + (PAD_STEP - 1)) // PAD_STEP, body, 0)

    def pad_expert(e, carry):
        pad_range(vend_ref[e], pend_ref[e])
        return carry

    lax.fori_loop(0, N_EXPERTS, pad_expert, 0)
    pad_range(pend_ref[N_EXPERTS - 1], pos_end)

    def place(t, carry):
        inv_ref[d1_ref[t]] = 2 * t
        inv_ref[d2_ref[t]] = 2 * t + 1
        return carry

    lax.fori_loop(0, T, place, 0, unroll=16)


BLOCK_TILE_ROWS = MOE_BLOCK * ROW_TILE
Y_ROWS = (N_ROUTED + DUMP_ROWS) * ROW_TILE


def _moe_kernel(meta_ref, d1_ref, d2_ref, vend_ref, pend_ref, h_ref, wg_hbm, wu_hbm, wd_hbm, y_ref,
                *scratch):
    xbufs, ybufs = scratch[:GROUP], scratch[GROUP:2 * GROUP]
    wg_s, wu_s, wd_s, wg_b, wu_b, wd_b, inv_ref, sem_x, sem_y, sem_w = scratch[2 * GROUP:]
    n_used = meta_ref[N_BLOCKS]
    n_groups = (n_used + (GROUP - 1)) // GROUP
    half = D_MODEL // 2

    def issue_gather(b, v):
        for r in range(MOE_BLOCK):
            tok = (inv_ref[b * MOE_BLOCK + r] >> 1) & (T - 1)
            src = h_ref.at[pl.ds(pl.multiple_of(tok * ROW_TILE, ROW_TILE), ROW_TILE), :]
            pltpu.make_async_copy(src, xbufs[v].at[pl.ds(r * ROW_TILE, ROW_TILE), :],
                                  sem_x.at[v]).start()

    def wait_gather(v):
        pltpu.make_async_copy(h_ref.at[pl.ds(0, BLOCK_TILE_ROWS), :], xbufs[v],
                              sem_x.at[v]).wait()

    def issue_scatter(b, v):
        for r in range(MOE_BLOCK):
            slot = inv_ref[b * MOE_BLOCK + r]
            dst = y_ref.at[pl.ds(pl.multiple_of(slot * ROW_TILE, ROW_TILE), ROW_TILE), :]
            pltpu.make_async_copy(ybufs[v].at[pl.ds(r * ROW_TILE, ROW_TILE), :], dst,
                                  sem_y.at[v]).start()

    def wait_scatter(v):
        pltpu.make_async_copy(ybufs[v], y_ref.at[pl.ds(0, BLOCK_TILE_ROWS), :],
                              sem_y.at[v]).wait()

    def weight_copies(e):
        return (pltpu.make_async_copy(wg_hbm.at[e], wg_s, sem_w),
                pltpu.make_async_copy(wu_hbm.at[e], wu_s, sem_w),
                pltpu.make_async_copy(wd_hbm.at[e], wd_s, sem_w))

    def load_expert(b):
        e = meta_ref[b]
        prev = meta_ref[jnp.maximum(b - 1, 0)]
        first = jnp.logical_and(b < n_used, jnp.logical_or(b == 0, e != prev))

        @pl.when(first)
        def _():
            for c in weight_copies(e):
                c.wait()
            wg_b[...] = wg_s[...].astype(BF16)
            wu_b[...] = wu_s[...].astype(BF16)
            wd_b[...] = wd_s[...].astype(BF16)
            nxt = lax.while_loop(
                lambda j: jnp.logical_and(j < n_used, meta_ref[jnp.minimum(j, N_BLOCKS - 1)] == e),
                lambda j: j + 1, b + 1)

            @pl.when(nxt < n_used)
            def _():
                for c in weight_copies(meta_ref[nxt]):
                    c.start(priority=1)

    def compute(v):
        lo, hi = _unpack_pairs_f32(_load_row_tiles(xbufs[v], MOE_BLOCK))
        lo, hi = lo.astype(BF16), hi.astype(BF16)
        g = (jnp.dot(lo, wg_b[0:half, :], preferred_element_type=F32)
             + jnp.dot(hi, wg_b[half:, :], preferred_element_type=F32))
        u = (jnp.dot(lo, wu_b[0:half, :], preferred_element_type=F32)
             + jnp.dot(hi, wu_b[half:, :], preferred_element_type=F32))
        a = g * (1.0 / (1.0 + jnp.exp(-g))) * u
        y = jnp.dot(a.astype(BF16), wd_b[...], preferred_element_type=F32)
        _store_row_tiles(ybufs[v], _pack_bf16_pairs(y))

    def block(b, v, scatter_prev, wait_y):
        wait_gather(v)
        if wait_y:
            wait_scatter(v)
        load_expert(b)
        issue_gather(b + GROUP - 1, (v + GROUP - 1) % GROUP)
        if scatter_prev:
            issue_scatter(b - 1, (v + GROUP - 1) % GROUP)
        compute(v)

    for c in weight_copies(meta_ref[0]):
        c.start()
    zeros_buf = ybufs[GROUP - 1]
    zeros_buf[...] = jnp.zeros_like(zeros_buf)
    dump_fills = [
        pltpu.make_async_copy(
            zeros_buf, y_ref.at[pl.ds((N_ROUTED + k * MOE_BLOCK) * ROW_TILE, BLOCK_TILE_ROWS), :],
            sem_y.at[GROUP - 1])
        for k in range(DUMP_ROWS // MOE_BLOCK)]
    for c in dump_fills:
        c.start()
    _invert_permutation(d1_ref, d2_ref, vend_ref, pend_ref, inv_ref,
                        (n_groups * GROUP + GROUP - 1) * MOE_BLOCK)
    for c in dump_fills:
        c.wait()
    for v in range(GROUP - 1):
        issue_gather(v, v)
    for v in range(GROUP):
        block(v, v, scatter_prev=v > 0, wait_y=False)

    def group(gi, carry):
        for v in range(GROUP):
            block(gi * GROUP + v, v, scatter_prev=True, wait_y=True)
        return carry

    lax.fori_loop(1, n_groups, group, 0)

    last = n_groups * GROUP
    issue_scatter(last - 1, GROUP - 1)
    for v in range(GROUP - 1):
        wait_gather(v)
    for v in range(GROUP):
        wait_scatter(v)


def _moe(meta, d1, d2, vend, pend, h_tiles, w_gate, w_up, w_down):
    any_spec = pl.BlockSpec(memory_space=pl.ANY)
    xy = pltpu.VMEM((BLOCK_TILE_ROWS, LANES), I32)
    return pl.pallas_call(
        _moe_kernel,
        grid_spec=pltpu.PrefetchScalarGridSpec(
            num_scalar_prefetch=5,
            grid=(1,),
            in_specs=[any_spec, any_spec, any_spec, any_spec],
            out_specs=any_spec,
            scratch_shapes=[
                *([xy] * (2 * GROUP)),
                pltpu.VMEM((D_MODEL, D_EXPERT), F32),
                pltpu.VMEM((D_MODEL, D_EXPERT), F32),
                pltpu.VMEM((D_EXPERT, D_MODEL), F32),
                pltpu.VMEM((D_MODEL, D_EXPERT), BF16),
                pltpu.VMEM((D_MODEL, D_EXPERT), BF16),
                pltpu.VMEM((D_EXPERT, D_MODEL), BF16),
                pltpu.SMEM((INV_LEN,), I32),
                pltpu.SemaphoreType.DMA((GROUP,)),
                pltpu.SemaphoreType.DMA((GROUP,)),
                pltpu.SemaphoreType.DMA(()),
            ],
        ),
        out_shape=jax.ShapeDtypeStruct((Y_ROWS, LANES), I32),
        compiler_params=_params(("arbitrary",)),
        name="moe",
    )(meta, d1, d2, vend, pend, h_tiles, w_gate, w_up, w_down)


COMB_TM = 512


COMB_BUFS = 3
COMB_AHEAD = COMB_BUFS - 1


def _combine_kernel(x1_hbm, rw_ref, gf_ref, y_hbm, o_ref, xbuf, ybuf, sem_x, sem_y):
    tm = COMB_TM
    i = pl.program_id(0)
    n = pl.num_programs(0)
    y_rows = tm * 2 * ROW_TILE

    def copies(t, slot):
        return (pltpu.make_async_copy(x1_hbm.at[pl.ds(t * tm, tm), :], xbuf.at[slot],
                                      sem_x.at[slot]),
                pltpu.make_async_copy(y_hbm.at[pl.ds(t * y_rows, y_rows), :], ybuf.at[slot],
                                      sem_y.at[slot]))

    @pl.when(i == 0)
    def _():
        for t in range(COMB_AHEAD):
            for c in copies(t, t):
                c.start()

    @pl.when(i + COMB_AHEAD < n)
    def _():
        for c in copies(i + COMB_AHEAD, (i + COMB_AHEAD) % COMB_BUFS):
            c.start()

    slot = i % COMB_BUFS
    for c in copies(i, slot):
        c.wait()
    y_ref = ybuf.at[slot]
    w = rw_ref[...]
    w0, w1 = w[:, 0:1], w[:, 1:2]
    lo0, hi0 = _unpack_pairs_f32(_load_row_tiles(y_ref, tm, 0, 2 * ROW_TILE))
    lo1, hi1 = _unpack_pairs_f32(_load_row_tiles(y_ref, tm, ROW_TILE, 2 * ROW_TILE))
    moe = jnp.concatenate([lo0 * w0 + lo1 * w1, hi0 * w0 + hi1 * w1], axis=1)
    o_ref[...] = _rms(xbuf[slot] + moe, gf_ref[...])


def _combine(x1, rw, gf, y_tiles):
    tm = COMB_TM
    row = lambda i: (i, 0)
    any_spec = pl.BlockSpec(memory_space=pl.ANY)
    return pl.pallas_call(
        _combine_kernel,
        grid=(T // tm,),
        in_specs=[
            any_spec,
            pl.BlockSpec((tm, LANES), row),
            pl.BlockSpec((1, D_MODEL), lambda i: (0, 0)),
            any_spec,
        ],
        out_specs=pl.BlockSpec((tm, D_MODEL), row),
        out_shape=jax.ShapeDtypeStruct((T, D_MODEL), F32),
        scratch_shapes=[
            pltpu.VMEM((COMB_BUFS, tm, D_MODEL), F32),
            pltpu.VMEM((COMB_BUFS, tm * 2 * ROW_TILE, LANES), I32),
            pltpu.SemaphoreType.DMA((COMB_BUFS,)),
            pltpu.SemaphoreType.DMA((COMB_BUFS,)),
        ],
        compiler_params=_params(("arbitrary",)),
        name="combine",
    )(x1, rw, gf, y_tiles)


def _layer(x2, norm1, w_in, rel_bias, conv_w, g_out_attn, g_out_conv, w_out, norm2,
           w_rg, b_rg, w_re, b_re, w_gate, w_up, w_down):
    conv_w8 = jnp.zeros((SUBLANES, D_CONV), F32).at[0:3].set(conv_w)
    qkv, c_n = _inproj(x2, norm1[None, :], w_in.astype(BF16), conv_w8, g_out_conv[None, :])
    a_n = _attention(qkv, _bias_table(rel_bias), g_out_attn[None, :])

    w_r = jnp.zeros((D_MODEL, LANES), F32)
    g0 = GROUP_LOGIT_ROW
    w_r = w_r.at[:, 0:N_EXPERTS].set(w_re).at[:, g0:g0 + N_GROUPS].set(w_rg)
    b_r = jnp.zeros((1, LANES), F32)
    b_r = b_r.at[0, 0:N_EXPERTS].set(b_re).at[0, g0:g0 + N_GROUPS].set(b_rg)
    x1, hp, ri_t, rw, cnt = _outproj(a_n, c_n, x2, w_out.astype(BF16), norm2[None, :],
                                     w_r.astype(BF16), b_r)

    dest_t, be, seg = _dest(ri_t, cnt)
    y_tiles = _moe(be[0], dest_t[0], dest_t[1], seg[0:N_EXPERTS, 0],
                   seg[N_EXPERTS:2 * N_EXPERTS, 0], hp, w_gate, w_up, w_down)
    return x1, rw, y_tiles


def kernel(x, norm1, w_in, rel_bias, conv_w, g_out_attn, g_out_conv, w_out, norm2,
           w_router_group, b_router_group, w_router_expert, b_router_expert,
           w_gate, w_up, w_down, norm_final):
    assert x.shape == (BATCH, SEQ, D_MODEL) and norm1.shape[0] == 1
    x2 = x.reshape(T, D_MODEL)
    x1, rw, y_tiles = _layer(
        x2, norm1[0], w_in[0], rel_bias[0], conv_w[0], g_out_attn[0], g_out_conv[0],
        w_out[0], norm2[0], w_router_group[0], b_router_group[0], w_router_expert[0],
        b_router_expert[0], w_gate[0], w_up[0], w_down[0])
    out = _combine(x1, rw, norm_final[None, :], y_tiles)
    return out.reshape(BATCH, SEQ, D_MODEL)
```

```python
import functools

import jax
import jax.numpy as jnp
from jax import lax
from jax.experimental import pallas as pl
from jax.experimental.pallas import tpu as pltpu

F32 = jnp.float32
BF16 = jnp.bfloat16
I32 = jnp.int32

D_MODEL = 2048
BATCH = 4
SEQ = 4096
T = BATCH * SEQ
CHUNK = 64
LEFT_CHUNKS = 8
BAND = LEFT_CHUNKS + 1
KEYS = BAND * CHUNK
D_ATTN = 1024
D_CONV = 1024
HEAD_DIM = 64
N_HEADS = 16
N_PAIRS = N_HEADS // 2
REL_CLIP = 256
D_IN_PROJ = 6 * 1024
N_GROUPS = 4
EPG = 8
N_EXPERTS = 32
D_EXPERT = 512
MOE_BLOCK = 256
MOE_BLOCK_SHIFT = MOE_BLOCK.bit_length() - 1
assert MOE_BLOCK == 1 << MOE_BLOCK_SHIFT
N_ROUTED = 2 * T
N_BLOCKS = N_ROUTED // MOE_BLOCK + N_EXPERTS
EPS = 1e-6
NEG_INF = -1e30
LANES = 128
VMEM_LIMIT = 52 * 1024 * 1024

ATTN_TILE = 512
ATTN_TILES_PER_SEQ = SEQ // ATTN_TILE


def _params(sem):
    return pltpu.CompilerParams(dimension_semantics=sem, vmem_limit_bytes=VMEM_LIMIT)


def _rms(x, g):
    ms = jnp.mean(x * x, axis=-1, keepdims=True)
    return x * lax.rsqrt(ms + EPS) * g


INPROJ_TM = 512
INPROJ_TN = 1536


D_QKV = 3 * D_ATTN
SUBLANES = 8


def _short_conv(b, z, z_before, w, g):
    n = z.shape[0]
    ze = jnp.concatenate([z_before, z], axis=0)
    z1 = pltpu.roll(ze, 1, axis=0)[SUBLANES:SUBLANES + n]
    z2 = pltpu.roll(ze, 2, axis=0)[SUBLANES:SUBLANES + n]
    y = w[0:1] * z2 + w[1:2] * z1 + w[2:3] * z
    return _rms(b * y, g).astype(BF16)


def _inproj_kernel(x_ref, g_ref, w_ref, cw_ref, gc_ref, qkv_ref, cn_ref, ztail_ref):
    i = pl.program_id(0)

    @pl.when(i == 0)
    def _():
        ztail_ref[...] = jnp.zeros_like(ztail_ref)

    hn = _rms(x_ref[...], g_ref[...]).astype(BF16)
    group = lambda k: jnp.dot(hn, w_ref[:, D_QKV + k * D_CONV:D_QKV + (k + 1) * D_CONV],
                              preferred_element_type=F32)
    b, z = group(0), group(1) * group(2)
    seq_start = (i % (SEQ // INPROJ_TM)) == 0
    z_before = jnp.where(seq_start, 0.0, ztail_ref[...])
    ztail_ref[...] = z[INPROJ_TM - SUBLANES:]
    cn_ref[...] = _short_conv(b, z, z_before, cw_ref[...], gc_ref[...])
    for j in range(D_QKV // INPROJ_TN):
        cols = slice(j * INPROJ_TN, (j + 1) * INPROJ_TN)
        qkv_ref[:, cols] = jnp.dot(hn, w_ref[:, cols],
                                   preferred_element_type=F32).astype(BF16)


def _inproj(x2, g, w_bf, conv_w, g_conv):
    tm = INPROJ_TM
    fixed = lambda i: (0, 0)
    row = lambda i: (i, 0)
    return pl.pallas_call(
        _inproj_kernel,
        grid=(T // tm,),
        in_specs=[
            pl.BlockSpec((tm, D_MODEL), row),
            pl.BlockSpec((1, D_MODEL), fixed),
            pl.BlockSpec((D_MODEL, D_IN_PROJ), fixed, pipeline_mode=pl.Buffered(1)),
            pl.BlockSpec((SUBLANES, D_CONV), fixed),
            pl.BlockSpec((1, D_CONV), fixed),
        ],
        out_specs=[pl.BlockSpec((tm, D_QKV), row), pl.BlockSpec((tm, D_CONV), row)],
        out_shape=[jax.ShapeDtypeStruct((T, D_QKV), BF16),
                   jax.ShapeDtypeStruct((T, D_CONV), BF16)],
        scratch_shapes=[pltpu.VMEM((SUBLANES, D_CONV), F32)],
        compiler_params=_params(("arbitrary",)),
        name="inproj",
    )(x2, g, w_bf, conv_w, g_conv)


N_QUADS = N_HEADS // 4
QUAD = 4 * HEAD_DIM
BIAS_W = 640
BIAS_WIN = 512
VAL_CHUNKS = 4
P_ROWS = KEYS + (VAL_CHUNKS - 1) * CHUNK
P_BUFS = 2 * VAL_CHUNKS


def _bias_kernel(b_ref, o_ref):
    first_head = lax.broadcasted_iota(I32, (KEYS, LANES), 1) < HEAD_DIM
    for hq in range(N_QUADS):
        for col in range(2):
            h0 = 4 * hq + 2 * col
            pieces = []
            for hh in range(2):
                x = jnp.broadcast_to(b_ref[h0 + hh:h0 + hh + 1, :], (KEYS, BIAS_W))
                rolled = pltpu.roll(x, 0, 1, stride=1, stride_axis=0)
                pieces.append(rolled[:, BIAS_WIN:BIAS_WIN + LANES])
            o_ref[hq, :, col * LANES:(col + 1) * LANES] = jnp.where(
                first_head, pieces[0], pieces[1])


def _bias_table(rel_bias):
    r = rel_bias.astype(F32)
    edge = jnp.broadcast_to(r[:, 2 * REL_CLIP:], (N_HEADS, KEYS - REL_CLIP + 1))
    b = jnp.concatenate([r[:, REL_CLIP:2 * REL_CLIP], edge,
                         r[:, REL_CLIP - (BIAS_W - KEYS - 1):REL_CLIP]], axis=1)
    odd = jnp.concatenate([b[:, BIAS_W - HEAD_DIM:], b[:, :BIAS_W - HEAD_DIM]], axis=1)
    is_odd = (jnp.arange(N_HEADS) % 2 == 1)[:, None]
    bvec = jnp.where(is_odd, odd, b)
    return pl.pallas_call(
        _bias_kernel,
        out_shape=jax.ShapeDtypeStruct((N_QUADS, KEYS, QUAD), F32),
        compiler_params=pltpu.CompilerParams(vmem_limit_bytes=VMEM_LIMIT),
        name="biastab",
    )(bvec)


def _attn_kernel(q_ref, kp_ref, kc_ref, vp_ref, vc_ref, bias_ref, g_ref, o_ref,
                 st_ref, p_ref, a_ref):
    i = pl.program_id(1)

    def window(prev_ref, cur_ref, r0, n, cols):
        parts = []
        if r0 < ATTN_TILE:
            parts.append(prev_ref[r0:min(ATTN_TILE, r0 + n), cols])
        if r0 + n > ATTN_TILE:
            parts.append(cur_ref[max(r0, ATTN_TILE) - ATTN_TILE:r0 + n - ATTN_TILE, cols])
        return parts[0] if len(parts) == 1 else jnp.concatenate(parts, axis=0)

    head_of_lane = lax.broadcasted_iota(I32, (CHUNK, QUAD), 1) // HEAD_DIM
    first_head = lax.broadcasted_iota(I32, (CHUNK, LANES), 1) < HEAD_DIM
    key_row = lax.broadcasted_iota(I32, (KEYS, QUAD), 0)
    n_rb = KEYS // CHUNK

    all_quads = tuple(range(N_QUADS))
    all_pairs = tuple(range(N_PAIRS))

    def scores(r0, buf, masked, quads=all_quads):
        for hq in quads:
            c0 = hq * QUAD
            q4 = q_ref[pl.ds(r0, CHUNK), c0:c0 + QUAD] * (HEAD_DIM ** -0.5)
            zero = jnp.zeros_like(q4)
            qbd = jnp.concatenate(
                [jnp.where(head_of_lane == h, q4, zero) for h in range(4)], axis=0)
            k4 = window(kp_ref, kc_ref, r0, KEYS, slice(c0, c0 + QUAD))
            st = lax.dot_general(k4, qbd, (((1,), (1,)), ((), ())),
                                 preferred_element_type=F32)
            st = st + bias_ref[hq]
            if masked:
                st = jnp.where(key_row + r0 >= ATTN_TILE, st, NEG_INF)
            st_ref[buf, hq] = st

    def exps(buf, pbuf, off, quads=all_quads):
        for hq in quads:
            m = st_ref[buf, hq, 0:CHUNK, :]
            for rb in range(1, n_rb):
                m = jnp.maximum(m, st_ref[buf, hq, rb * CHUNK:(rb + 1) * CHUNK, :])
            m = jnp.max(m, axis=0, keepdims=True)
            for rb in range(n_rb):
                rows = slice(rb * CHUNK, (rb + 1) * CHUNK)
                prow = slice(off + rb * CHUNK, off + (rb + 1) * CHUNK)
                p_ref[pbuf, hq, prow, :] = jnp.exp(st_ref[buf, hq, rows, :] - m).astype(BF16)

    def values(r0, pbufs, pairs=all_pairs):
        nq = len(pbufs)
        span = KEYS + (nq - 1) * CHUNK
        ones = jnp.ones((span, LANES), BF16)
        for hp in pairs:
            c0 = hp * LANES
            half = (hp % 2) * LANES
            pt = jnp.concatenate(
                [p_ref[b, hp // 2, 0:span, half:half + LANES] for b in pbufs], axis=1)
            vext = jnp.concatenate(
                [window(vp_ref, vc_ref, r0, span, slice(c0, c0 + LANES)), ones], axis=1)
            oe = lax.dot_general(pt, vext, (((0,), (0,)), ((), ())),
                                 preferred_element_type=F32)
            o = oe[:, :LANES] * (1.0 / oe[:, LANES:])
            for j in range(nq):
                oj = o[2 * j * CHUNK:2 * (j + 1) * CHUNK]
                a_ref[pl.ds(r0 + j * CHUNK, CHUNK), c0:c0 + LANES] = jnp.where(
                    first_head, oj[0:CHUNK], oj[CHUNK:2 * CHUNK])

    n_chunks = ATTN_TILE // CHUNK

    def group_bufs(c0):
        return tuple((c0 + j) % P_BUFS for j in range(VAL_CHUNKS))

    def tile(masked):
        for b in range(P_BUFS):
            off = (b % VAL_CHUNKS) * CHUNK
            for z in (slice(0, off), slice(off + KEYS, P_ROWS)):
                if z.stop > z.start:
                    p_ref[b, :, z, :] = jnp.zeros((N_QUADS, z.stop - z.start, QUAD), BF16)
        scores(0, 0, masked)
        for ci in range(n_chunks):
            c0 = (ci // VAL_CHUNKS - 1) * VAL_CHUNKS
            k = ci % VAL_CHUNKS
            for hq in all_quads:
                if ci + 1 < n_chunks:
                    scores((ci + 1) * CHUNK, (ci + 1) % 2, masked, (hq,))
                if c0 >= 0 and hq % 2 == 0:
                    values(c0 * CHUNK, group_bufs(c0), (2 * k + hq // 2,))
                exps(ci % 2, ci % P_BUFS, k * CHUNK, (hq,))
        last = n_chunks - VAL_CHUNKS
        values(last * CHUNK, group_bufs(last))

    pl.when(i == 0)(functools.partial(tile, True))
    pl.when(i > 0)(functools.partial(tile, False))

    o_ref[...] = _rms(a_ref[...], g_ref[...]).astype(BF16)


def _attention(proj, bias_t, g):
    n = ATTN_TILES_PER_SEQ
    cur = lambda col: (lambda b, i: (b * n + i, col))
    prev = lambda col: (lambda b, i: (b * n + jnp.maximum(i - 1, 0), col))
    blk = (ATTN_TILE, D_ATTN)
    return pl.pallas_call(
        _attn_kernel,
        grid=(BATCH, n),
        in_specs=[
            pl.BlockSpec(blk, cur(0)),
            pl.BlockSpec(blk, prev(1)),
            pl.BlockSpec(blk, cur(1)),
            pl.BlockSpec(blk, prev(2)),
            pl.BlockSpec(blk, cur(2)),
            pl.BlockSpec((N_QUADS, KEYS, QUAD), lambda b, i: (0, 0, 0)),
            pl.BlockSpec((1, D_ATTN), lambda b, i: (0, 0)),
        ],
        out_specs=pl.BlockSpec(blk, lambda b, i: (b * n + i, 0)),
        out_shape=jax.ShapeDtypeStruct((T, D_ATTN), BF16),
        scratch_shapes=[
            pltpu.VMEM((2, N_QUADS, KEYS, QUAD), F32),
            pltpu.VMEM((P_BUFS, N_QUADS, P_ROWS, QUAD), BF16),
            pltpu.VMEM((ATTN_TILE, D_ATTN), F32),
        ],
        compiler_params=_params(("arbitrary", "arbitrary")),
        name="attn",
    )(proj, proj, proj, proj, proj, bias_t, g)


OUT_TM = 512
OUT_SUB = 2


def _pack_bf16_pairs(h):
    half = D_MODEL // 2
    return pltpu.pack_elementwise([h[:, :half], h[:, half:]], packed_dtype=BF16)


def _unpack_pairs_f32(u):
    lo = pltpu.unpack_elementwise(u, index=0, packed_dtype=BF16, unpacked_dtype=F32)
    hi = pltpu.unpack_elementwise(u, index=1, packed_dtype=BF16, unpacked_dtype=F32)
    return lo, hi


ROW_TILE = 8


def _store_row_tiles(ref, packed):
    m = packed.shape[0]
    for s in range(ROW_TILE):
        ref[pl.ds(s, m, stride=ROW_TILE), :] = packed[:, s * LANES:(s + 1) * LANES]


def _load_row_tiles(ref, m, first=0, stride=ROW_TILE):
    return jnp.concatenate(
        [ref[pl.ds(first + s, m, stride=stride), :] for s in range(ROW_TILE)], axis=1)


GROUP_LOGIT_ROW = N_EXPERTS


def _route_cols(lg, base_ref):
    tm = lg.shape[0]
    lt = lg.T
    sub = lax.broadcasted_iota(I32, (SUBLANES, tm), 0)
    sub_f = sub.astype(F32)
    big = jnp.float32(SUBLANES)
    ninf = jnp.float32(-jnp.inf)

    gl = lt[GROUP_LOGIT_ROW:GROUP_LOGIT_ROW + SUBLANES, :]
    gvalid = sub < N_GROUPS
    gmax = jnp.max(jnp.where(gvalid, gl, ninf), axis=0, keepdims=True)
    gsum = jnp.sum(jnp.where(gvalid, jnp.exp(gl - gmax), 0.0), axis=0, keepdims=True)
    p_g = 1.0 / gsum
    g_sel = jnp.min(jnp.where(jnp.logical_and(gvalid, gl == gmax), sub_f, big),
                    axis=0, keepdims=True)
    el = lt[0:EPG, :]
    for g in range(1, N_GROUPS):
        el = jnp.where(g_sel == g, lt[g * EPG:(g + 1) * EPG, :], el)
    v1 = jnp.max(el, axis=0, keepdims=True)
    i1 = jnp.min(jnp.where(el == v1, sub_f, big), axis=0, keepdims=True)
    rest = sub_f != i1
    v2 = jnp.max(jnp.where(rest, el, ninf), axis=0, keepdims=True)
    i2 = jnp.min(jnp.where(jnp.logical_and(rest, el == v2), sub_f, big),
                 axis=0, keepdims=True)
    t = jnp.exp(v2 - v1)
    w1 = p_g * (1.0 / (1.0 + t))
    w2 = p_g * (t / (1.0 + t))
    e1 = g_sel * EPG + i1
    e2 = g_sel * EPG + i2

    erow = lax.broadcasted_iota(I32, (N_EXPERTS, tm), 0).astype(F32)
    hit1 = erow == e1
    hit2 = erow == e2
    onehot = jnp.where(jnp.logical_or(hit1, hit2), 1.0, 0.0)
    r = lax.broadcasted_iota(I32, (tm, tm), 0)
    c = lax.broadcasted_iota(I32, (tm, tm), 1)
    tri = jnp.where(r < c, 1.0, 0.0).astype(BF16)
    base = base_ref[...]
    before = (jnp.dot(onehot.astype(BF16), tri, preferred_element_type=F32)
              + jnp.concatenate([base] * (tm // LANES), axis=1))
    r1 = jnp.sum(jnp.where(hit1, before, 0.0), axis=0, keepdims=True)
    r2 = jnp.sum(jnp.where(hit2, before, 0.0), axis=0, keepdims=True)
    base_ref[...] = base + jnp.broadcast_to(
        jnp.sum(onehot, axis=1, keepdims=True), (N_EXPERTS, LANES))

    ri_t = jnp.where(sub == 0, e1, jnp.where(sub == 1, e2, jnp.where(sub == 2, r1, r2)))
    w_t = jnp.where(sub == 0, w1, jnp.where(sub == 1, w2, 0.0))
    w_t = jnp.concatenate([w_t, jnp.zeros((LANES - SUBLANES, tm), F32)], axis=0)
    return ri_t.astype(I32), w_t.T


def _outproj_kernel(a_ref, cn_ref, x_ref, wo_ref, g2_ref, wr_ref, br_ref,
                    x1_ref, hp_ref, ri_ref, rw_ref, cnt_ref, base_ref):
    @pl.when(pl.program_id(0) == 0)
    def _():
        base_ref[...] = jnp.zeros_like(base_ref)

    sub = OUT_TM // OUT_SUB
    rows_of = lambda s: slice(s * sub, (s + 1) * sub)

    def project(s):
        rows = rows_of(s)
        acc = jnp.dot(a_ref[rows, :], wo_ref[0:D_ATTN, :], preferred_element_type=F32)
        acc = acc + jnp.dot(cn_ref[rows, :], wo_ref[D_ATTN:, :], preferred_element_type=F32)
        x1_ref[rows, :] = x_ref[rows, :] + acc

    def route(s):
        rows = rows_of(s)
        h2 = _rms(x1_ref[rows, :], g2_ref[...])
        logits = jnp.dot(h2.astype(BF16), wr_ref[...],
                         preferred_element_type=F32) + br_ref[...]
        _store_row_tiles(hp_ref.at[pl.ds(s * sub * ROW_TILE, sub * ROW_TILE), :],
                         _pack_bf16_pairs(h2))
        ri_t, rw = _route_cols(logits, base_ref)
        ri_ref[:, rows] = ri_t
        rw_ref[rows, :] = rw

    for s in range(OUT_SUB):
        project(s)
        if s > 0:
            route(s - 1)
    route(OUT_SUB - 1)
    cnt_ref[...] = base_ref[...]


def _outproj(a_n, c_n, x2, wo_bf, g2, wr_bf, br):
    tm = OUT_TM
    row = lambda i: (i, 0)
    fixed = lambda i: (0, 0)
    return pl.pallas_call(
        _outproj_kernel,
        grid=(T // tm,),
        in_specs=[
            pl.BlockSpec((tm, D_ATTN), row),
            pl.BlockSpec((tm, D_CONV), row),
            pl.BlockSpec((tm, D_MODEL), row),
            pl.BlockSpec((D_MODEL, D_MODEL), fixed),
            pl.BlockSpec((1, D_MODEL), fixed),
            pl.BlockSpec((D_MODEL, LANES), fixed),
            pl.BlockSpec((1, LANES), fixed),
        ],
        out_specs=[
            pl.BlockSpec((tm, D_MODEL), row),
            pl.BlockSpec((tm * ROW_TILE, LANES), row),
            pl.BlockSpec((SUBLANES, tm), lambda i: (0, i)),
            pl.BlockSpec((tm, LANES), row),
            pl.BlockSpec((N_EXPERTS, LANES), fixed),
        ],
        out_shape=[
            jax.ShapeDtypeStruct((T, D_MODEL), F32),
            jax.ShapeDtypeStruct((T * ROW_TILE, LANES), I32),
            jax.ShapeDtypeStruct((SUBLANES, T), I32),
            jax.ShapeDtypeStruct((T, LANES), F32),
            jax.ShapeDtypeStruct((N_EXPERTS, LANES), F32),
        ],
        scratch_shapes=[pltpu.VMEM((N_EXPERTS, LANES), F32)],
        compiler_params=_params(("arbitrary",)),
        name="outproj",
    )(a_n, c_n, x2, wo_bf, g2, wr_bf, br)


BLOCK_LANES = 384
DEST_TM = 2048


def _dest_kernel(ri_ref, cnt_ref, dest_ref, be_ref, seg_ref):
    tm = DEST_TM
    erow = lax.broadcasted_iota(I32, (N_EXPERTS, LANES), 0)
    cnt = cnt_ref[...].astype(I32)
    pcnt = ((cnt + (MOE_BLOCK - 1)) >> MOE_BLOCK_SHIFT) << MOE_BLOCK_SHIFT
    pend = pcnt
    for s in (1, 2, 4, 8, 16):
        pend = pend + jnp.where(erow >= s, pltpu.roll(pend, s, axis=0), 0)
    pstart = (pend - pcnt).astype(F32)

    ri = ri_ref[...].astype(F32)
    etok = lax.broadcasted_iota(I32, (N_EXPERTS, tm), 0).astype(F32)
    pstart_t = jnp.concatenate([pstart] * (tm // LANES), axis=1)
    d1 = jnp.sum(jnp.where(etok == ri[0:1], pstart_t, 0.0), axis=0, keepdims=True) + ri[2:3]
    d2 = jnp.sum(jnp.where(etok == ri[1:2], pstart_t, 0.0), axis=0, keepdims=True) + ri[3:4]
    sub = lax.broadcasted_iota(I32, (SUBLANES, tm), 0)
    dest_ref[...] = jnp.where(sub == 0, d1, d2).astype(I32)

    pend_f = pend.astype(F32)
    pend_b = jnp.concatenate([pend_f] * (BLOCK_LANES // LANES), axis=1)
    blk_lane = lax.broadcasted_iota(I32, (1, BLOCK_LANES), 1)
    le = jnp.where(pend_b <= (blk_lane * MOE_BLOCK).astype(F32), 1.0, 0.0)
    be = jnp.minimum(jnp.sum(le, axis=0, keepdims=True), float(N_EXPERTS - 1))
    total = pend_b[N_EXPERTS - 1:N_EXPERTS, :]
    be = jnp.where(blk_lane == N_BLOCKS, total * (1.0 / MOE_BLOCK), be)
    be_ref[...] = jnp.broadcast_to(be, (SUBLANES, BLOCK_LANES)).astype(I32)
    seg_ref[0:N_EXPERTS, :] = pend - pcnt + cnt
    seg_ref[N_EXPERTS:2 * N_EXPERTS, :] = pend


def _dest(ri_t, cnt):
    tm = DEST_TM
    fixed = lambda i: (0, 0)
    tok = lambda i: (0, i)
    return pl.pallas_call(
        _dest_kernel,
        grid=(T // tm,),
        in_specs=[pl.BlockSpec((SUBLANES, tm), tok), pl.BlockSpec((N_EXPERTS, LANES), fixed)],
        out_specs=[pl.BlockSpec((SUBLANES, tm), tok),
                   pl.BlockSpec((SUBLANES, BLOCK_LANES), fixed),
                   pl.BlockSpec((2 * N_EXPERTS, LANES), fixed)],
        out_shape=[jax.ShapeDtypeStruct((SUBLANES, T), I32),
                   jax.ShapeDtypeStruct((SUBLANES, BLOCK_LANES), I32),
                   jax.ShapeDtypeStruct((2 * N_EXPERTS, LANES), I32)],
        compiler_params=_params(("arbitrary",)),
        name="dest",
    )(ri_t, cnt)


GROUP = 3
INV_LEN = (-(-N_BLOCKS // GROUP) * GROUP + GROUP - 1) * MOE_BLOCK
DUMP_ROWS = 4 * MOE_BLOCK


PAD_STEP = 8


def _invert_permutation(d1_ref, d2_ref, vend_ref, pend_ref, inv_ref, pos_end):
    def pad_range(lo, hi):
        def body(k, carry):
            p0 = lo + k * PAD_STEP
            for j in range(PAD_STEP):
                inv_ref[p0 + j] = N_ROUTED + ((p0 + j) & (DUMP_ROWS - 1))
            return carry

        lax.fori_loop(0, (hi - lo + (PAD_STEP - 1)) // PAD_STEP, body, 0)

    def pad_expert(e, carry):
        pad_range(vend_ref[e], pend_ref[e])
        return carry

    lax.fori_loop(0, N_EXPERTS, pad_expert, 0)
    pad_range(pend_ref[N_EXPERTS - 1], pos_end)

    def place(t, carry):
        inv_ref[d1_ref[t]] = 2 * t
        inv_ref[d2_ref[t]] = 2 * t + 1
        return carry

    lax.fori_loop(0, T, place, 0, unroll=16)


BLOCK_TILE_ROWS = MOE_BLOCK * ROW_TILE
Y_ROWS = (N_ROUTED + DUMP_ROWS) * ROW_TILE


def _moe_kernel(meta_ref, d1_ref, d2_ref, vend_ref, pend_ref, h_ref, wg_hbm, wu_hbm, wd_hbm, y_ref,
                *scratch):
    xbufs, ybufs = scratch[:GROUP], scratch[GROUP:2 * GROUP]
    wg_s, wu_s, wd_s, wg_b, wu_b, wd_b, inv_ref, sem_x, sem_y, sem_w = scratch[2 * GROUP:]
    n_used = meta_ref[N_BLOCKS]
    n_groups = (n_used + (GROUP - 1)) // GROUP
    half = D_MODEL // 2

    def issue_gather(b, v):
        for r in range(MOE_BLOCK):
            tok = (inv_ref[b * MOE_BLOCK + r] >> 1) & (T - 1)
            src = h_ref.at[pl.ds(pl.multiple_of(tok * ROW_TILE, ROW_TILE), ROW_TILE), :]
            pltpu.make_async_copy(src, xbufs[v].at[pl.ds(r * ROW_TILE, ROW_TILE), :],
                                  sem_x.at[v]).start()

    def wait_gather(v):
        pltpu.make_async_copy(h_ref.at[pl.ds(0, BLOCK_TILE_ROWS), :], xbufs[v],
                              sem_x.at[v]).wait()

    def issue_scatter(b, v):
        for r in range(MOE_BLOCK):
            slot = inv_ref[b * MOE_BLOCK + r]
            dst = y_ref.at[pl.ds(pl.multiple_of(slot * ROW_TILE, ROW_TILE), ROW_TILE), :]
            pltpu.make_async_copy(ybufs[v].at[pl.ds(r * ROW_TILE, ROW_TILE), :], dst,
                                  sem_y.at[v]).start()

    def wait_scatter(v):
        pltpu.make_async_copy(ybufs[v], y_ref.at[pl.ds(0, BLOCK_TILE_ROWS), :],
                              sem_y.at[v]).wait()

    def weight_copies(e):
        return (pltpu.make_async_copy(wg_hbm.at[e], wg_s, sem_w),
                pltpu.make_async_copy(wu_hbm.at[e], wu_s, sem_w),
                pltpu.make_async_copy(wd_hbm.at[e], wd_s, sem_w))

    def load_expert(b):
        e = meta_ref[b]
        prev = meta_ref[jnp.maximum(b - 1, 0)]
        first = jnp.logical_and(b < n_used, jnp.logical_or(b == 0, e != prev))

        @pl.when(first)
        def _():
            for c in weight_copies(e):
                c.wait()
            wg_b[...] = wg_s[...].astype(BF16)
            wu_b[...] = wu_s[...].astype(BF16)
            wd_b[...] = wd_s[...].astype(BF16)
            nxt = lax.while_loop(
                lambda j: jnp.logical_and(j < n_used, meta_ref[jnp.minimum(j, N_BLOCKS - 1)] == e),
                lambda j: j + 1, b + 1)

            @pl.when(nxt < n_used)
            def _():
                for c in weight_copies(meta_ref[nxt]):
                    c.start(priority=1)

    def compute(v):
        lo, hi = _unpack_pairs_f32(_load_row_tiles(xbufs[v], MOE_BLOCK))
        lo, hi = lo.astype(BF16), hi.astype(BF16)
        g = (jnp.dot(lo, wg_b[0:half, :], preferred_element_type=F32)
             + jnp.dot(hi, wg_b[half:, :], preferred_element_type=F32))
        u = (jnp.dot(lo, wu_b[0:half, :], preferred_element_type=F32)
             + jnp.dot(hi, wu_b[half:, :], preferred_element_type=F32))
        a = g * (1.0 / (1.0 + jnp.exp(-g))) * u
        y = jnp.dot(a.astype(BF16), wd_b[...], preferred_element_type=F32)
        _store_row_tiles(ybufs[v], _pack_bf16_pairs(y))

    def block(b, v, scatter_prev, wait_y):
        wait_gather(v)
        if wait_y:
            wait_scatter(v)
        load_expert(b)
        issue_gather(b + GROUP - 1, (v + GROUP - 1) % GROUP)
        if scatter_prev:
            issue_scatter(b - 1, (v + GROUP - 1) % GROUP)
        compute(v)

    for c in weight_copies(meta_ref[0]):
        c.start()
    zeros_buf = ybufs[GROUP - 1]
    zeros_buf[...] = jnp.zeros_like(zeros_buf)
    dump_fills = [
        pltpu.make_async_copy(
            zeros_buf, y_ref.at[pl.ds((N_ROUTED + k * MOE_BLOCK) * ROW_TILE, BLOCK_TILE_ROWS), :],
            sem_y.at[GROUP - 1])
        for k in range(DUMP_ROWS // MOE_BLOCK)]
    for c in dump_fills:
        c.start()
    _invert_permutation(d1_ref, d2_ref, vend_ref, pend_ref, inv_ref,
                        (n_groups * GROUP + GROUP - 1) * MOE_BLOCK)
    for c in dump_fills:
        c.wait()
    for v in range(GROUP - 1):
        issue_gather(v, v)
    for v in range(GROUP):
        block(v, v, scatter_prev=v > 0, wait_y=False)

    def group(gi, carry):
        for v in range(GROUP):
            block(gi * GROUP + v, v, scatter_prev=True, wait_y=True)
        return carry

    lax.fori_loop(1, n_groups, group, 0)

    last = n_groups * GROUP
    issue_scatter(last - 1, GROUP - 1)
    for v in range(GROUP - 1):
        wait_gather(v)
    for v in range(GROUP):
        wait_scatter(v)


def _moe(meta, d1, d2, vend, pend, h_tiles, w_gate, w_up, w_down):
    any_spec = pl.BlockSpec(memory_space=pl.ANY)
    xy = pltpu.VMEM((BLOCK_TILE_ROWS, LANES), I32)
    return pl.pallas_call(
        _moe_kernel,
        grid_spec=pltpu.PrefetchScalarGridSpec(
            num_scalar_prefetch=5,
            grid=(1,),
            in_specs=[any_spec, any_spec, any_spec, any_spec],
            out_specs=any_spec,
            scratch_shapes=[
                *([xy] * (2 * GROUP)),
                pltpu.VMEM((D_MODEL, D_EXPERT), F32),
                pltpu.VMEM((D_MODEL, D_EXPERT), F32),
                pltpu.VMEM((D_EXPERT, D_MODEL), F32),
                pltpu.VMEM((D_MODEL, D_EXPERT), BF16),
                pltpu.VMEM((D_MODEL, D_EXPERT), BF16),
                pltpu.VMEM((D_EXPERT, D_MODEL), BF16),
                pltpu.SMEM((INV_LEN,), I32),
                pltpu.SemaphoreType.DMA((GROUP,)),
                pltpu.SemaphoreType.DMA((GROUP,)),
                pltpu.SemaphoreType.DMA(()),
            ],
        ),
        out_shape=jax.ShapeDtypeStruct((Y_ROWS, LANES), I32),
        compiler_params=_params(("arbitrary",)),
        name="moe",
    )(meta, d1, d2, vend, pend, h_tiles, w_gate, w_up, w_down)


COMB_TM = 512


COMB_BUFS = 4
COMB_AHEAD = COMB_BUFS - 1


def _combine_kernel(x1_hbm, rw_ref, gf_ref, y_hbm, o_ref, xbuf, ybuf, sem_x, sem_y):
    tm = COMB_TM
    i = pl.program_id(0)
    n = pl.num_programs(0)
    y_rows = tm * 2 * ROW_TILE

    def copies(t, slot):
        return (pltpu.make_async_copy(x1_hbm.at[pl.ds(t * tm, tm), :], xbuf.at[slot],
                                      sem_x.at[slot]),
                pltpu.make_async_copy(y_hbm.at[pl.ds(t * y_rows, y_rows), :], ybuf.at[slot],
                                      sem_y.at[slot]))

    @pl.when(i == 0)
    def _():
        for t in range(COMB_AHEAD):
            for c in copies(t, t):
                c.start()

    @pl.when(i + COMB_AHEAD < n)
    def _():
        for c in copies(i + COMB_AHEAD, (i + COMB_AHEAD) % COMB_BUFS):
            c.start()

    slot = i % COMB_BUFS
    for c in copies(i, slot):
        c.wait()
    y_ref = ybuf.at[slot]
    w = rw_ref[...]
    w0, w1 = w[:, 0:1], w[:, 1:2]
    lo0, hi0 = _unpack_pairs_f32(_load_row_tiles(y_ref, tm, 0, 2 * ROW_TILE))
    lo1, hi1 = _unpack_pairs_f32(_load_row_tiles(y_ref, tm, ROW_TILE, 2 * ROW_TILE))
    moe = jnp.concatenate([lo0 * w0 + lo1 * w1, hi0 * w0 + hi1 * w1], axis=1)
    o_ref[...] = _rms(xbuf[slot] + moe, gf_ref[...])


def _combine(x1, rw, gf, y_tiles):
    tm = COMB_TM
    row = lambda i: (i, 0)
    any_spec = pl.BlockSpec(memory_space=pl.ANY)
    return pl.pallas_call(
        _combine_kernel,
        grid=(T // tm,),
        in_specs=[
            any_spec,
            pl.BlockSpec((tm, LANES), row),
            pl.BlockSpec((1, D_MODEL), lambda i: (0, 0)),
            any_spec,
        ],
        out_specs=pl.BlockSpec((tm, D_MODEL), row),
        out_shape=jax.ShapeDtypeStruct((T, D_MODEL), F32),
        scratch_shapes=[
            pltpu.VMEM((COMB_BUFS, tm, D_MODEL), F32),
            pltpu.VMEM((COMB_BUFS, tm * 2 * ROW_TILE, LANES), I32),
            pltpu.SemaphoreType.DMA((COMB_BUFS,)),
            pltpu.SemaphoreType.DMA((COMB_BUFS,)),
        ],
        compiler_params=_params(("arbitrary",)),
        name="combine",
    )(x1, rw, gf, y_tiles)


def _layer(x2, norm1, w_in, rel_bias, conv_w, g_out_attn, g_out_conv, w_out, norm2,
           w_rg, b_rg, w_re, b_re, w_gate, w_up, w_down):
    conv_w8 = jnp.zeros((SUBLANES, D_CONV), F32).at[0:3].set(conv_w)
    qkv, c_n = _inproj(x2, norm1[None, :], w_in.astype(BF16), conv_w8, g_out_conv[None, :])
    a_n = _attention(qkv, _bias_table(rel_bias), g_out_attn[None, :])

    w_r = jnp.zeros((D_MODEL, LANES), F32)
    g0 = GROUP_LOGIT_ROW
    w_r = w_r.at[:, 0:N_EXPERTS].set(w_re).at[:, g0:g0 + N_GROUPS].set(w_rg)
    b_r = jnp.zeros((1, LANES), F32)
    b_r = b_r.at[0, 0:N_EXPERTS].set(b_re).at[0, g0:g0 + N_GROUPS].set(b_rg)
    x1, hp, ri_t, rw, cnt = _outproj(a_n, c_n, x2, w_out.astype(BF16), norm2[None, :],
                                     w_r.astype(BF16), b_r)

    dest_t, be, seg = _dest(ri_t, cnt)
    y_tiles = _moe(be[0], dest_t[0], dest_t[1], seg[0:N_EXPERTS, 0],
                   seg[N_EXPERTS:2 * N_EXPERTS, 0], hp, w_gate, w_up, w_down)
    return x1, rw, y_tiles


def kernel(x, norm1, w_in, rel_bias, conv_w, g_out_attn, g_out_conv, w_out, norm2,
           w_router_group, b_router_group, w_router_expert, b_router_expert,
           w_gate, w_up, w_down, norm_final):
    assert x.shape == (BATCH, SEQ, D_MODEL) and norm1.shape[0] == 1
    x2 = x.reshape(T, D_MODEL)
    x1, rw, y_tiles = _layer(
        x2, norm1[0], w_in[0], rel_bias[0], conv_w[0], g_out_attn[0], g_out_conv[0],
        w_out[0], norm2[0], w_router_group[0], b_router_group[0], w_router_expert[0],
        b_router_expert[0], w_gate[0], w_up[0], w_down[0])
    out = _combine(x1, rw, norm_final[None, :], y_tiles)
    return out.reshape(BATCH, SEQ, D_MODEL)
```
